```python
import jax, jax.numpy as jnp
from jax import lax
import numpy as np

D_MODEL = 1024
BATCH = 16
SEQ = 256
DEPTH = 1
DEC_BATCH = 8
DEC_SEQ = 1024
PAST_LEN = 512

GRID_W = 64
NA_HEADS = 8
HEAD_DIM = 64
NA_WIDTH = NA_HEADS * HEAD_DIM
WIN_ROWS = 8
WIN_COLS = 16
POOL_WINDOWS = (2, 4, 8, 16)
N_POOL_GROUPS = 4
POOL_GROUP_DIM = 128
POOL_WIDTH = N_POOL_GROUPS * POOL_GROUP_DIM
POOL_OUT_DIM = D_MODEL // N_POOL_GROUPS
N_EXPERTS = 16
CAPACITY_FACTOR = 2
D_EXPERT = 2752
N_MOD = 6
Q_BLOCK = 128
IN_WIDTH = 3 * NA_WIDTH + POOL_WIDTH + 2 * D_MODEL
EPS = 1e-6
NEG_INF = -1e30

kernel_name = "hybrid_na_pool_ecmoe_diffusion_step"


def rms_norm(x, g):
    x32 = x.astype(jnp.float32)
    y = x32 * lax.rsqrt(jnp.mean(x32 * x32, axis=-1, keepdims=True) + EPS)
    return (y * g.astype(jnp.float32)).astype(x.dtype)


def ada_mod(cond, w_ada, b_ada):
    m = jax.nn.silu(cond) @ w_ada + b_ada
    return jnp.split(m[:, None, :], N_MOD, axis=-1)


def mixer_projections(h, w_in, q_g, k_g):
    B, L, _ = h.shape
    p = h @ w_in
    cuts = [NA_WIDTH, 2 * NA_WIDTH, 3 * NA_WIDTH, 3 * NA_WIDTH + POOL_WIDTH,
            3 * NA_WIDTH + POOL_WIDTH + D_MODEL]
    q, k, v, u, ga, gb = jnp.split(p, cuts, axis=-1)
    q = rms_norm(q.reshape(B, L, NA_HEADS, HEAD_DIM), q_g)
    k = rms_norm(k.reshape(B, L, NA_HEADS, HEAD_DIM), k_g)
    v = v.reshape(B, L, NA_HEADS, HEAD_DIM)
    return q, k, v, u, ga, gb


def context_attention(q, k, v):
    B, L, H, dh = q.shape
    nblk = L // Q_BLOCK
    qb = q.reshape(B, nblk, Q_BLOCK, H, dh).transpose(1, 0, 2, 3, 4)

    def one_block(qblk):
        s = jnp.einsum('bqhd,bkhd->bhqk', qblk, k).astype(jnp.float32) * (HEAD_DIM ** -0.5)
        p = jax.nn.softmax(s, axis=-1)
        return jnp.einsum('bhqk,bkhd->bqhd', p.astype(v.dtype), v)

    o = lax.map(one_block, qb)
    return o.transpose(1, 0, 2, 3, 4).reshape(B, L, H * dh)


def latent_neighbourhood_attention(q, k, v, k_ctx, v_ctx, rpb):
    B, L, H, dh = q.shape
    rows = L // GRID_W
    kr = min(WIN_ROWS, rows)
    r = jnp.arange(rows)
    row_start = jnp.clip(r - kr // 2, 0, rows - kr)
    row_idx = row_start[:, None] + jnp.arange(kr)[None, :]
    qg = q.reshape(B, rows, GRID_W, H, dh)
    kg = k.reshape(B, rows, GRID_W, H, dh)[:, row_idx]
    vg = v.reshape(B, rows, GRID_W, H, dh)[:, row_idx]
    col = jnp.arange(GRID_W)
    col_start = jnp.clip(col - WIN_COLS // 2, 0, GRID_W - WIN_COLS)
    in_win = (col[None, :] >= col_start[:, None]) & (col[None, :] < col_start[:, None] + WIN_COLS)
    ro_idx = row_idx - r[:, None] + (WIN_ROWS - 1)
    co_idx = jnp.clip(col[None, :] - col[:, None], -(WIN_COLS - 1), WIN_COLS - 1) + (WIN_COLS - 1)
    bias = rpb[:, ro_idx[:, None, :, None], co_idx[None, :, None, :]].astype(jnp.float32)
    scale = HEAD_DIM ** -0.5
    s_loc = jnp.einsum('brqhd,brkwhd->bhrqkw', qg, kg).astype(jnp.float32) * scale + bias[None]
    s_loc = jnp.where(in_win[:, None, :], s_loc, NEG_INF)
    s_ctx = jnp.einsum('brqhd,bchd->bhrqc', qg, k_ctx).astype(jnp.float32) * scale
    n_loc = kr * GRID_W
    s = jnp.concatenate([s_loc.reshape(B, H, rows, GRID_W, n_loc), s_ctx], axis=-1)
    p = jax.nn.softmax(s, axis=-1)
    p_loc = p[..., :n_loc].reshape(B, H, rows, GRID_W, kr, GRID_W).astype(v.dtype)
    p_ctx = p[..., n_loc:].astype(v.dtype)
    o = (jnp.einsum('bhrqkw,brkwhd->brqhd', p_loc, vg)
         + jnp.einsum('bhrqc,bchd->brqhd', p_ctx, v_ctx))
    return o.reshape(B, L, H * dh)


def multiscale_pool(u, w_pool, pool_scale):
    B, L, _ = u.shape
    u32 = u.astype(jnp.float32).reshape(B, L, N_POOL_GROUPS, POOL_GROUP_DIM)
    cs = jnp.concatenate([jnp.zeros((B, 1, N_POOL_GROUPS, POOL_GROUP_DIM), jnp.float32),
                          jnp.cumsum(u32, axis=1)], axis=1)
    t = jnp.arange(L)[:, None]
    w = jnp.array(POOL_WINDOWS, jnp.int32)[None, :]
    lo = jnp.clip(t - w // 2, 0, L)
    hi = jnp.clip(t + w - w // 2, 0, L)
    gid = jnp.arange(N_POOL_GROUPS)[None, :]
    sums = cs[:, hi, gid] - cs[:, lo, gid]
    cnt = (hi - lo).astype(jnp.float32)[None, :, :, None]
    pooled = (sums / cnt - u32).astype(u.dtype)
    out = jnp.einsum('blgc,gcd->blgd', pooled, w_pool).reshape(B, L, D_MODEL)
    return out * pool_scale


def expert_choice_ffn(h, w_router, w_gate, w_up, w_down):
    B, L, D = h.shape
    n = B * L
    xt = h.reshape(n, D)
    aff = jax.nn.softmax((xt @ w_router).astype(jnp.float32), axis=-1)
    cap = max(1, (CAPACITY_FACTOR * n) // N_EXPERTS)
    g, idx = lax.top_k(aff.T, cap)
    xe = xt[idx]
    a = jnp.einsum('ecd,edf->ecf', xe, w_gate)
    b = jnp.einsum('ecd,edf->ecf', xe, w_up)
    ye = jnp.einsum('ecf,efd->ecd', jax.nn.silu(a) * b, w_down) * g[..., None].astype(h.dtype)
    out = jnp.zeros((n, D), h.dtype).at[idx.reshape(-1)].add(ye.reshape(-1, D))
    return out.reshape(B, L, D)


def trunk_layer(x, cond, p, k_ctx=None, v_ctx=None):
    sh1, sc1, g1, sh2, sc2, g2 = ada_mod(cond, p['w_ada'], p['b_ada'])
    h = rms_norm(x, p['norm1']) * (1 + sc1) + sh1
    q, k, v, u, ga, gb = mixer_projections(h, p['w_in'], p['q_g'], p['k_g'])
    if k_ctx is None:
        att = context_attention(q, k, v)
    else:
        att = latent_neighbourhood_attention(q, k, v, k_ctx, v_ctx, p['rpb'])
    o_a = att @ p['w_att_proj']
    o_b = multiscale_pool(u, p['w_pool'], p['pool_scale'])
    merged = jax.nn.sigmoid(ga) * o_a + jax.nn.sigmoid(gb) * o_b
    x = x + g1 * (merged @ p['w_out'])
    h = rms_norm(x, p['norm2']) * (1 + sc2) + sh2
    x = x + g2 * expert_choice_ffn(h, p['w_router'], p['w_gate_e'], p['w_up_e'], p['w_down_e'])
    return x, k, v


def setup_inputs(seed: int = 0) -> dict:
    key = jax.random.key(seed)
    ks = jax.random.split(key, 24)

    def nrm(k, shape, s):
        return jax.random.normal(k, shape, jnp.float32) * s

    return {
        'x_prompt': nrm(ks[0], (BATCH, SEQ, D_MODEL), 1.0),
        'x_sample': nrm(ks[1], (DEC_BATCH, DEC_SEQ, D_MODEL), 1.0),
        'cache_k': nrm(ks[2], (DEC_BATCH, DEPTH, PAST_LEN, NA_HEADS, HEAD_DIM), 1.0),
        'cache_v': nrm(ks[3], (DEC_BATCH, DEPTH, PAST_LEN, NA_HEADS, HEAD_DIM), 1.0),
        'c': nrm(ks[4], (DEC_BATCH, D_MODEL), 1.0),
        'c_ctx': nrm(ks[5], (D_MODEL,), 1.0),
        'norm1_g': 1.0 + nrm(ks[6], (DEPTH, D_MODEL), 0.02),
        'norm2_g': 1.0 + nrm(ks[7], (DEPTH, D_MODEL), 0.02),
        'w_ada': nrm(ks[8], (DEPTH, D_MODEL, N_MOD * D_MODEL), 0.5 * D_MODEL ** -0.5),
        'b_ada': nrm(ks[9], (DEPTH, N_MOD * D_MODEL), 0.02),
        'w_in': nrm(ks[10], (DEPTH, D_MODEL, IN_WIDTH), D_MODEL ** -0.5),
        'q_norm_g': 1.0 + nrm(ks[11], (DEPTH, HEAD_DIM), 0.02),
        'k_norm_g': 1.0 + nrm(ks[12], (DEPTH, HEAD_DIM), 0.02),
        'rpb': nrm(ks[13], (DEPTH, NA_HEADS, 2 * WIN_ROWS - 1, 2 * WIN_COLS - 1), 0.1),
        'w_att_proj': nrm(ks[14], (DEPTH, NA_WIDTH, D_MODEL), NA_WIDTH ** -0.5),
        'w_pool': nrm(ks[15], (DEPTH, N_POOL_GROUPS, POOL_GROUP_DIM, POOL_OUT_DIM), POOL_GROUP_DIM ** -0.5),
        'pool_scale': 1.0 + nrm(ks[16], (DEPTH, D_MODEL), 0.02),
        'w_out': nrm(ks[17], (DEPTH, D_MODEL, D_MODEL), D_MODEL ** -0.5),
        'w_router': nrm(ks[18], (DEPTH, D_MODEL, N_EXPERTS), D_MODEL ** -0.5),
        'w_gate_e': nrm(ks[19], (DEPTH, N_EXPERTS, D_MODEL, D_EXPERT), D_MODEL ** -0.5),
        'w_up_e': nrm(ks[20], (DEPTH, N_EXPERTS, D_MODEL, D_EXPERT), D_MODEL ** -0.5),
        'w_down_e': nrm(ks[21], (DEPTH, N_EXPERTS, D_EXPERT, D_MODEL), D_EXPERT ** -0.5),
    }


def reference(x_prompt, x_sample, cache_k, cache_v, c, c_ctx, norm1_g, norm2_g, w_ada, b_ada,
              w_in, q_norm_g, k_norm_g, rpb, w_att_proj, w_pool, pool_scale, w_out,
              w_router, w_gate_e, w_up_e, w_down_e):
    params = []
    for layer in range(DEPTH):
        params.append({
            'norm1': norm1_g[layer], 'norm2': norm2_g[layer],
            'w_ada': w_ada[layer], 'b_ada': b_ada[layer], 'w_in': w_in[layer],
            'q_g': q_norm_g[layer], 'k_g': k_norm_g[layer], 'rpb': rpb[layer],
            'w_att_proj': w_att_proj[layer], 'w_pool': w_pool[layer],
            'pool_scale': pool_scale[layer], 'w_out': w_out[layer],
            'w_router': w_router[layer], 'w_gate_e': w_gate_e[layer],
            'w_up_e': w_up_e[layer], 'w_down_e': w_down_e[layer],
        })

    y_prompt = x_prompt
    k_states, v_states = [], []
    for layer in range(DEPTH):
        y_prompt, k_l, v_l = trunk_layer(y_prompt, c_ctx[None, :], params[layer])
        k_states.append(k_l)
        v_states.append(v_l)
    state_k_ctx = jnp.stack(k_states, axis=1)
    state_v_ctx = jnp.stack(v_states, axis=1)

    y_sample = x_sample
    for layer in range(DEPTH):
        y_sample, _, _ = trunk_layer(y_sample, c, params[layer], cache_k[:, layer], cache_v[:, layer])

    return (y_prompt, y_sample, state_k_ctx, state_v_ctx)
```

```python
import functools

import jax
import jax.numpy as jnp
from jax import lax
from jax.experimental import pallas as pl
from jax.experimental.pallas import tpu as pltpu

F32 = jnp.float32
BF16 = jnp.bfloat16
I32 = jnp.int32

D_MODEL = 1024
GRID_W = 64
NA_HEADS = 8
HEAD_DIM = 64
NA_WIDTH = NA_HEADS * HEAD_DIM
WIN_ROWS = 8
WIN_COLS = 16
POOL_WINDOWS = (2, 4, 8, 16)
POOL_GROUP_DIM = 128
POOL_WIDTH = 512
POOL_OUT_DIM = 256
N_EXPERTS = 16
CAPACITY_FACTOR = 2
D_EXPERT = 2752
N_MOD = 6
EPS = 1e-6
NEG_INF = -1e30

TOKEN_BLOCK = 256
EXPERT_F_BLOCK = 256
EXPERT_M_BLOCK = 512
Z_CHUNK = 256
ROW_ALIGN = 8
VMEM_LIMIT = 56 * 1024 * 1024


def _dot(a, b):
    return jnp.dot(a, b, preferred_element_type=F32)


def _dot_nt(a, b):
    return lax.dot_general(a, b, (((1,), (1,)), ((), ())), preferred_element_type=F32)


def _dot3(a, b):
    a_hi = a.astype(BF16)
    a_lo = (a - a_hi.astype(F32)).astype(BF16)
    b_hi = b.astype(BF16)
    b_lo = (b - b_hi.astype(F32)).astype(BF16)
    return _dot(a_hi, b_hi) + _dot(a_hi, b_lo) + _dot(a_lo, b_hi)


def _sigmoid(x):
    return 1.0 / (1.0 + jnp.exp(-x))


def _rms(x, g):
    return x * lax.rsqrt(jnp.mean(x * x, axis=-1, keepdims=True) + EPS) * g


def _params(sem):
    return pltpu.CompilerParams(dimension_semantics=sem, vmem_limit_bytes=VMEM_LIMIT)


def _ada_kernel(cond_ref, w_ref, b_ref, o_ref):
    c = cond_ref[...]
    o_ref[...] = _dot3(c * _sigmoid(c), w_ref[...]) + b_ref[...]


def _ada(cond, w_ada, b_ada):
    rows, d = cond.shape
    n_out = w_ada.shape[1]
    bn = 512
    return pl.pallas_call(
        _ada_kernel,
        grid=(n_out // bn,),
        in_specs=[pl.BlockSpec((rows, d), lambda i: (0, 0)),
                  pl.BlockSpec((d, bn), lambda i: (0, i)),
                  pl.BlockSpec((1, bn), lambda i: (0, i))],
        out_specs=pl.BlockSpec((rows, bn), lambda i: (0, i)),
        out_shape=jax.ShapeDtypeStruct((rows, n_out), F32),
        compiler_params=_params(("arbitrary",)),
        name="ada",
    )(cond, w_ada, b_ada)


_IN_CUTS = (0, 512, 1024, 1536, 2048, 3072, 4096)


def _in_kernel(x_ref, mod_ref, g_ref, w_ref, q_ref, k_ref, v_ref, u_ref, ga_ref, gb_ref):
    y = _rms(x_ref[...], g_ref[...])
    h = (y * (1.0 + mod_ref[0, 1:2, :]) + mod_ref[0, 0:1, :]).astype(BF16)
    outs = (q_ref, k_ref, v_ref, u_ref, ga_ref, gb_ref)
    for o_ref, lo, hi in zip(outs, _IN_CUTS[:-1], _IN_CUTS[1:]):
        o_ref[...] = _dot(h, w_ref[:, lo:hi])


def _in_proj(x2d, mods, norm_g, w_in_bf16, cond_of_block):
    n, d = x2d.shape
    widths = [hi - lo for lo, hi in zip(_IN_CUTS[:-1], _IN_CUTS[1:])]
    return pl.pallas_call(
        _in_kernel,
        grid=(n // TOKEN_BLOCK,),
        in_specs=[pl.BlockSpec((TOKEN_BLOCK, d), lambda i: (i, 0)),
                  pl.BlockSpec((1, N_MOD, d), lambda i: (cond_of_block(i), 0, 0)),
                  pl.BlockSpec((1, d), lambda i: (0, 0)),
                  pl.BlockSpec(w_in_bf16.shape, lambda i: (0, 0))],
        out_specs=[pl.BlockSpec((TOKEN_BLOCK, w), lambda i: (i, 0)) for w in widths],
        out_shape=[jax.ShapeDtypeStruct((n, w), F32) for w in widths],
        compiler_params=_params(("arbitrary",)),
        name="in_proj",
    )(x2d, mods, norm_g, w_in_bf16)


def _attn_ctx_kernel(q_ref, k_ref, v_ref, qg_ref, kg_ref, att_ref, kn_ref):
    scale = HEAD_DIM ** -0.5
    for h in range(NA_HEADS):
        sl = slice(h * HEAD_DIM, (h + 1) * HEAD_DIM)
        qn = _rms(q_ref[0, :, sl], qg_ref[...]) * scale
        kn = _rms(k_ref[0, :, sl], kg_ref[...])
        kn_ref[0, :, sl] = kn
        s = _dot_nt(qn.astype(BF16), kn.astype(BF16))
        e = jnp.exp(s - jnp.max(s, axis=-1, keepdims=True))
        p = e / jnp.sum(e, axis=-1, keepdims=True)
        att_ref[0, :, sl] = _dot(p.astype(BF16), v_ref[0, :, sl].astype(BF16))


def _attn_ctx(q, k, v, q_g, k_g):
    b, l, w = q.shape
    blk = pl.BlockSpec((1, l, w), lambda i: (i, 0, 0))
    gspec = pl.BlockSpec((1, HEAD_DIM), lambda i: (0, 0))
    return pl.pallas_call(
        _attn_ctx_kernel,
        grid=(b,),
        in_specs=[blk, blk, blk, gspec, gspec],
        out_specs=[blk, blk],
        out_shape=[jax.ShapeDtypeStruct((b, l, w), F32)] * 2,
        compiler_params=_params(("arbitrary",)),
        name="attn_ctx",
    )(q, k, v, q_g, k_g)


def _attn_lat_kernel(q_ref, k_ref, v_ref, ck_ref, cv_ref, bias_ref, qg_ref, kg_ref, att_ref,
                     kn_s, vb_s, ckb_s, cvb_s, *, rows, kr):
    r = pl.program_id(1)

    @pl.when(r == 0)
    def _():
        for h in range(NA_HEADS):
            sl = slice(h * HEAD_DIM, (h + 1) * HEAD_DIM)
            kn_s[:, sl] = _rms(k_ref[0, :, sl], kg_ref[...]).astype(BF16)
        vb_s[...] = v_ref[0].astype(BF16)
        ckb_s[...] = ck_ref[0].astype(BF16)
        cvb_s[...] = cv_ref[0].astype(BF16)

    row_start = jnp.clip(r - kr // 2, 0, rows - kr)
    key_rows = pl.ds(pl.multiple_of(row_start * GRID_W, GRID_W), kr * GRID_W)
    scale = HEAD_DIM ** -0.5
    for h in range(NA_HEADS):
        sl = slice(h * HEAD_DIM, (h + 1) * HEAD_DIM)
        qn = (_rms(q_ref[0, :, sl], qg_ref[...]) * scale).astype(BF16)
        s_loc = _dot_nt(qn, kn_s[key_rows, sl]) + bias_ref[h, r]
        s_ctx = _dot_nt(qn, ckb_s[:, sl])
        m = jnp.maximum(jnp.max(s_loc, axis=-1, keepdims=True), jnp.max(s_ctx, axis=-1, keepdims=True))
        e_loc = jnp.exp(s_loc - m)
        e_ctx = jnp.exp(s_ctx - m)
        denom = jnp.sum(e_loc, axis=-1, keepdims=True) + jnp.sum(e_ctx, axis=-1, keepdims=True)
        o = (_dot((e_loc / denom).astype(BF16), vb_s[key_rows, sl])
             + _dot((e_ctx / denom).astype(BF16), cvb_s[:, sl]))
        att_ref[0, :, sl] = o


def _attn_lat(q, k, v, ck, cv, bias, q_g, k_g):
    b, l, w = q.shape
    lc = ck.shape[1]
    rows = l // GRID_W
    kr = min(WIN_ROWS, rows)
    seq = pl.BlockSpec((1, l, w), lambda i, r: (i, 0, 0))
    ctx = pl.BlockSpec((1, lc, w), lambda i, r: (i, 0, 0))
    rowblk = pl.BlockSpec((1, GRID_W, w), lambda i, r: (i, r, 0))
    gspec = pl.BlockSpec((1, HEAD_DIM), lambda i, r: (0, 0))
    return pl.pallas_call(
        functools.partial(_attn_lat_kernel, rows=rows, kr=kr),
        grid=(b, rows),
        in_specs=[rowblk, seq, seq, ctx, ctx,
                  pl.BlockSpec(bias.shape, lambda i, r: (0, 0, 0, 0)), gspec, gspec],
        out_specs=rowblk,
        out_shape=jax.ShapeDtypeStruct((b, l, w), F32),
        scratch_shapes=[pltpu.VMEM((l, w), BF16), pltpu.VMEM((l, w), BF16),
                        pltpu.VMEM((lc, w), BF16), pltpu.VMEM((lc, w), BF16)],
        compiler_params=_params(("arbitrary", "arbitrary")),
        name="attn_lat",
    )(q, k, v, ck, cv, bias, q_g, k_g)


def _window_bias(rpb, rows):
    kr = min(WIN_ROWS, rows)
    r = jnp.arange(rows)
    row_start = jnp.clip(r - kr // 2, 0, rows - kr)
    row_idx = row_start[:, None] + jnp.arange(kr)[None, :]
    col = jnp.arange(GRID_W)
    col_start = jnp.clip(col - WIN_COLS // 2, 0, GRID_W - WIN_COLS)
    in_win = (col[None, :] >= col_start[:, None]) & (col[None, :] < col_start[:, None] + WIN_COLS)
    ro_idx = row_idx - r[:, None] + (WIN_ROWS - 1)
    co_idx = jnp.clip(col[None, :] - col[:, None], -(WIN_COLS - 1), WIN_COLS - 1) + (WIN_COLS - 1)
    bias = rpb[:, ro_idx[:, None, :, None], co_idx[None, :, None, :]].astype(F32)
    bias = jnp.where(in_win[None, None, :, None, :], bias, NEG_INF)
    return bias.reshape(rpb.shape[0], rows, GRID_W, kr * GRID_W)


def _pool_kernel(u_ref, o_ref):
    l = u_ref.shape[1]
    t = lax.broadcasted_iota(I32, (l, POOL_GROUP_DIM), 0)
    for g, w in enumerate(POOL_WINDOWS):
        sl = slice(g * POOL_GROUP_DIM, (g + 1) * POOL_GROUP_DIM)
        x = u_ref[0, :, sl]
        acc = x
        for k in range(-(w // 2), w - w // 2):
            if k == 0:
                continue
            shifted = pltpu.roll(x, (-k) % l, 0)
            acc = acc + jnp.where((t + k >= 0) & (t + k < l), shifted, 0.0)
        cnt = (jnp.minimum(t + (w - w // 2), l) - jnp.maximum(t - w // 2, 0)).astype(F32)
        o_ref[0, :, sl] = acc / cnt - x


def _pool(u):
    b, l, w = u.shape
    blk = pl.BlockSpec((1, l, w), lambda i: (i, 0, 0))
    return pl.pallas_call(
        _pool_kernel, grid=(b,), in_specs=[blk], out_specs=blk,
        out_shape=jax.ShapeDtypeStruct((b, l, w), F32),
        compiler_params=_params(("arbitrary",)),
        name="pool",
    )(u)


def _merge_kernel(x_ref, att_ref, pl_ref, ga_ref, gb_ref, mod_ref, watt_ref, wpool_ref, ps_ref, wout_ref,
                  g2_ref, wr_ref, x1_ref, h2_ref, aff_ref):
    o_a = _dot(att_ref[...].astype(BF16), watt_ref[...])
    o_b = jnp.concatenate(
        [_dot(pl_ref[:, g * POOL_GROUP_DIM:(g + 1) * POOL_GROUP_DIM].astype(BF16), wpool_ref[g])
         for g in range(len(POOL_WINDOWS))], axis=-1) * ps_ref[...]
    merged = _sigmoid(ga_ref[...]) * o_a + _sigmoid(gb_ref[...]) * o_b
    x1 = x_ref[...] + mod_ref[0, 2:3, :] * _dot(merged.astype(BF16), wout_ref[...])
    x1_ref[...] = x1
    h2 = _rms(x1, g2_ref[...]) * (1.0 + mod_ref[0, 4:5, :]) + mod_ref[0, 3:4, :]
    h2_ref[...] = h2
    logits = _dot3(h2, wr_ref[...])
    e = jnp.exp(logits - jnp.max(logits, axis=-1, keepdims=True))
    aff_ref[...] = e / jnp.sum(e, axis=-1, keepdims=True)


def _merge(x2d, att, pooled, ga, gb, mods, w_att, w_pool, pool_scale, w_out, norm2_g, w_router, cond_of_block):
    n, d = x2d.shape
    tb = TOKEN_BLOCK
    row = lambda w: pl.BlockSpec((tb, w), lambda i: (i, 0))
    full = lambda a: pl.BlockSpec(a.shape, lambda i: (0,) * a.ndim)
    return pl.pallas_call(
        _merge_kernel,
        grid=(n // tb,),
        in_specs=[row(d), row(NA_WIDTH), row(POOL_WIDTH), row(d), row(d),
                  pl.BlockSpec((1, N_MOD, d), lambda i: (cond_of_block(i), 0, 0)),
                  full(w_att), full(w_pool), full(pool_scale), full(w_out), full(norm2_g), full(w_router)],
        out_specs=[row(d), row(d), row(N_EXPERTS)],
        out_shape=[jax.ShapeDtypeStruct((n, d), F32), jax.ShapeDtypeStruct((n, d), F32),
                   jax.ShapeDtypeStruct((n, N_EXPERTS), F32)],
        compiler_params=_params(("arbitrary",)),
        name="merge",
    )(x2d, att, pooled, ga, gb, mods, w_att, w_pool, pool_scale, w_out, norm2_g, w_router)


def _prefix_incl(mask_f32, out_ref):
    e, n = mask_f32.shape
    i = lax.broadcasted_iota(I32, (128, 128), 0)
    j = lax.broadcasted_iota(I32, (128, 128), 1)
    tri = jnp.where(i <= j, 1.0, 0.0).astype(BF16)
    carry = jnp.zeros((e, 1), F32)
    for c in range(n // 128):
        inc = _dot(mask_f32[:, c * 128:(c + 1) * 128].astype(BF16), tri) + carry
        out_ref[:, c * 128:(c + 1) * 128] = inc.astype(I32)
        carry = inc[:, 127:128]


def _route_kernel(aff_ref, sel_ref, rinc_ref, tmp_ref, *, cap):
    bits = pltpu.bitcast(aff_ref[...], I32)

    def bit_step(i, thr):
        cand = thr | jnp.left_shift(jnp.int32(1), 30 - i)
        cnt = jnp.sum(jnp.where(bits >= cand, 1, 0), axis=1, keepdims=True)
        return jnp.where(cnt >= cap, cand, thr)

    thr = lax.fori_loop(0, 31, bit_step, jnp.zeros((bits.shape[0], 1), I32))
    gt = bits > thr
    eq = bits == thr
    need = cap - jnp.sum(jnp.where(gt, 1, 0), axis=1, keepdims=True)
    _prefix_incl(jnp.where(eq, 1.0, 0.0), tmp_ref)
    sel = gt | (eq & (tmp_ref[...] <= need))
    sel_ref[...] = jnp.where(sel, 1, 0)
    _prefix_incl(jnp.where(sel, 1.0, 0.0), rinc_ref)


def _route(aff_t, cap):
    e, n = aff_t.shape
    full = pl.BlockSpec((e, n), lambda: (0, 0))
    return pl.pallas_call(
        functools.partial(_route_kernel, cap=cap),
        in_specs=[full], out_specs=[full, full],
        out_shape=[jax.ShapeDtypeStruct((e, n), I32)] * 2,
        scratch_shapes=[pltpu.VMEM((e, n), I32)],
        compiler_params=pltpu.CompilerParams(vmem_limit_bytes=VMEM_LIMIT),
        name="route",
    )(aff_t)


def _moe_kernel(gidx_ref, qpos_ref, hc_ref, hl_ref, gate_ref, wg_ref, wu_ref, wd_ref, z_ref,
                xf_ref, xb_ref, acc_ref, sem_g, sem_s, *, cap_c, cap_l, n_f):
    e = pl.program_id(0)
    j = pl.program_id(1)
    m_tot = cap_c + cap_l
    base = e * m_tot
    tf = wg_ref.shape[2]
    z_pad = z_ref.shape[0] - N_EXPERTS * m_tot

    @pl.when((e == 0) & (j == 0))
    def _():
        acc_ref[0:z_pad, :] = jnp.zeros((z_pad, acc_ref.shape[1]), F32)
        cp = pltpu.make_async_copy(acc_ref.at[pl.ds(0, z_pad)], z_ref.at[pl.ds(N_EXPERTS * m_tot, z_pad)], sem_s)
        cp.start()
        cp.wait()

    @pl.when(j == 0)
    def _():
        def issue_c(s, c):
            t = gidx_ref[base + s]
            pltpu.make_async_copy(hc_ref.at[pl.ds(t, 1)], xf_ref.at[pl.ds(s, 1)], sem_g).start()
            return c

        def issue_l(s, c):
            t = gidx_ref[base + cap_c + s]
            pltpu.make_async_copy(hl_ref.at[pl.ds(t, 1)], xf_ref.at[pl.ds(cap_c + s, 1)], sem_g).start()
            return c

        lax.fori_loop(0, cap_c, issue_c, 0)
        lax.fori_loop(0, cap_l, issue_l, 0)
        pltpu.make_async_copy(hl_ref.at[pl.ds(0, m_tot)], xf_ref, sem_g).wait()
        xb_ref[...] = xf_ref[...].astype(BF16)
        acc_ref[...] = jnp.zeros_like(acc_ref)

    valid = D_EXPERT - j * tf
    cmask = lax.broadcasted_iota(I32, (1, tf), 1) < valid
    rmask = lax.broadcasted_iota(I32, (tf, 1), 0) < valid
    wg = wg_ref[0].astype(BF16)
    wu = wu_ref[0].astype(BF16)
    wd = jnp.where(rmask, wd_ref[0], 0.0).astype(BF16)
    for m in range(m_tot // EXPERT_M_BLOCK):
        rs = slice(m * EXPERT_M_BLOCK, (m + 1) * EXPERT_M_BLOCK)
        xs = xb_ref[rs, :]
        a = _dot(xs, wg)
        b = _dot(xs, wu)
        hm = jnp.where(cmask, a * _sigmoid(a) * b, 0.0).astype(BF16)
        acc_ref[rs, :] += _dot(hm, wd)

    @pl.when(j == n_f - 1)
    def _():
        acc_ref[...] = acc_ref[...] * gate_ref[0]

        def scatter(s, c):
            q = qpos_ref[base + s]
            pltpu.make_async_copy(acc_ref.at[pl.ds(s, 1)], z_ref.at[pl.ds(q, 1)], sem_s).start()
            return c

        lax.fori_loop(0, m_tot, scatter, 0)
        pltpu.make_async_copy(acc_ref, z_ref.at[pl.ds(0, m_tot)], sem_s).wait()


def _moe(gidx, qpos, h_ctx, h_lat, gate, w_gate, w_up, w_down, cap_c, cap_l, z_rows):
    d = h_ctx.shape[1]
    m_tot = cap_c + cap_l
    tf = EXPERT_F_BLOCK
    n_f = pl.cdiv(D_EXPERT, tf)
    grid_spec = pltpu.PrefetchScalarGridSpec(
        num_scalar_prefetch=2,
        grid=(N_EXPERTS, n_f),
        in_specs=[pl.BlockSpec(memory_space=pl.ANY),
                  pl.BlockSpec(memory_space=pl.ANY),
                  pl.BlockSpec((1, m_tot, 1), lambda e, j, *_: (e, 0, 0)),
                  pl.BlockSpec((1, d, tf), lambda e, j, *_: (e, 0, j)),
                  pl.BlockSpec((1, d, tf), lambda e, j, *_: (e, 0, j)),
                  pl.BlockSpec((1, tf, d), lambda e, j, *_: (e, j, 0))],
        out_specs=pl.BlockSpec(memory_space=pl.ANY),
        scratch_shapes=[pltpu.VMEM((m_tot, d), F32), pltpu.VMEM((m_tot, d), BF16), pltpu.VMEM((m_tot, d), F32),
                        pltpu.SemaphoreType.DMA, pltpu.SemaphoreType.DMA],
    )
    return pl.pallas_call(
        functools.partial(_moe_kernel, cap_c=cap_c, cap_l=cap_l, n_f=n_f),
        grid_spec=grid_spec,
        out_shape=jax.ShapeDtypeStruct((z_rows, d), F32),
        compiler_params=_params(("arbitrary", "arbitrary")),
        name="moe",
    )(gidx, qpos, h_ctx, h_lat, gate, w_gate, w_up, w_down)


def _combine_kernel(w0_ref, nc_ref, z_ref, x1_ref, mod_ref, qt_ref, kt_ref, o_ref, zbuf, sem, acc_ref):
    blk = pl.program_id(0)
    w0 = w0_ref[blk]
    qt = qt_ref[...]
    kt = kt_ref[...]
    acc_ref[...] = jnp.zeros_like(acc_ref)

    def chunk(c, carry):
        start = pl.multiple_of(w0 + c * Z_CHUNK, ROW_ALIGN)
        cp = pltpu.make_async_copy(z_ref.at[pl.ds(start, Z_CHUNK)], zbuf, sem)
        cp.start()
        cp.wait()
        r = start + lax.broadcasted_iota(I32, (qt.shape[0], Z_CHUNK), 1)
        onehot = jnp.where((r >= qt) & (r < qt + kt), 1.0, 0.0).astype(BF16)
        acc_ref[...] += _dot(onehot, zbuf[...].astype(BF16))
        return carry

    lax.fori_loop(0, nc_ref[blk], chunk, 0)
    o_ref[...] = x1_ref[...] + mod_ref[0, 5:6, :] * acc_ref[...]


def _combine(w0, nc, z, x1, mods, qt, kt, cond_of_block):
    n, d = x1.shape
    tb = TOKEN_BLOCK
    grid_spec = pltpu.PrefetchScalarGridSpec(
        num_scalar_prefetch=2,
        grid=(n // tb,),
        in_specs=[pl.BlockSpec(memory_space=pl.ANY),
                  pl.BlockSpec((tb, d), lambda i, *_: (i, 0)),
                  pl.BlockSpec((1, N_MOD, d), lambda i, *_: (cond_of_block(i), 0, 0)),
                  pl.BlockSpec((tb, 1), lambda i, *_: (i, 0)),
                  pl.BlockSpec((tb, 1), lambda i, *_: (i, 0))],
        out_specs=pl.BlockSpec((tb, d), lambda i, *_: (i, 0)),
        scratch_shapes=[pltpu.VMEM((Z_CHUNK, d), F32), pltpu.SemaphoreType.DMA, pltpu.VMEM((tb, d), F32)],
    )
    return pl.pallas_call(
        _combine_kernel,
        grid_spec=grid_spec,
        out_shape=jax.ShapeDtypeStruct((n, d), F32),
        compiler_params=_params(("arbitrary",)),
        name="combine",
    )(w0, nc, z, x1, mods, qt, kt)


def _routing_tables(aff, cap, z_off):
    n = aff.shape[0]
    aff_t = aff.T
    sel, rinc = _route(aff_t, cap)
    slots = jnp.arange(1, cap + 1, dtype=I32)
    idx = jax.vmap(lambda r: jnp.searchsorted(r, slots, side="left"))(rinc).astype(I32)
    gate = jnp.take_along_axis(aff_t, idx, axis=1)
    k_tok = jnp.sum(sel, axis=0)
    q_tok = jnp.cumsum(k_tok) - k_tok + z_off
    before = jnp.cumsum(sel, axis=0) - sel
    qpos = jnp.take_along_axis(q_tok[None, :] + before, idx, axis=1)
    nblk = n // TOKEN_BLOCK
    q_blk = q_tok[::TOKEN_BLOCK]
    end_blk = q_blk + jnp.sum(k_tok.reshape(nblk, TOKEN_BLOCK), axis=1)
    w0 = (q_blk // ROW_ALIGN) * ROW_ALIGN
    nc = (end_blk - w0 + Z_CHUNK - 1) // Z_CHUNK
    return idx, gate, qpos.astype(I32), w0.astype(I32), nc.astype(I32), q_tok.astype(I32)[:, None], k_tok.astype(I32)[:, None]


def kernel(x_prompt, x_sample, cache_k, cache_v, c, c_ctx, norm1_g, norm2_g, w_ada, b_ada, w_in, q_norm_g, k_norm_g,
           rpb, w_att_proj, w_pool, pool_scale, w_out, w_router, w_gate_e, w_up_e, w_down_e):
    assert w_ada.shape[0] == 1, "single trunk layer"
    nb, ls, d = x_prompt.shape
    db, ll, _ = x_sample.shape
    n_c, n_l = nb * ls, db * ll
    lc = cache_k.shape[2]

    cond = jnp.concatenate([c_ctx[None, :], c, jnp.zeros((16 - 1 - db, d), F32)], axis=0)
    mods = _ada(cond, w_ada[0], b_ada).reshape(16, N_MOD, d)

    w_in_b = w_in[0].astype(BF16)
    w_att_b = w_att_proj[0].astype(BF16)
    w_pool_b = w_pool[0].astype(BF16)
    w_out_b = w_out[0].astype(BF16)

    cond_ctx = lambda i: 0
    lat_blocks = ll // TOKEN_BLOCK
    cond_lat = lambda i: 1 + i // lat_blocks

    xc = x_prompt.reshape(n_c, d)
    xl = x_sample.reshape(n_l, d)
    qc, kc, vc, uc, gac, gbc = _in_proj(xc, mods, norm1_g, w_in_b, cond_ctx)
    ql, kl, vl, ul, gal, gbl = _in_proj(xl, mods, norm1_g, w_in_b, cond_lat)

    seq_c = lambda a: a.reshape(nb, ls, a.shape[-1])
    seq_l = lambda a: a.reshape(db, ll, a.shape[-1])
    att_c, kn_c = _attn_ctx(seq_c(qc), seq_c(kc), seq_c(vc), q_norm_g, k_norm_g)
    bias = _window_bias(rpb[0], ll // GRID_W)
    att_l = _attn_lat(seq_l(ql), seq_l(kl), seq_l(vl), cache_k[:, 0].reshape(db, lc, NA_WIDTH),
                      cache_v[:, 0].reshape(db, lc, NA_WIDTH), bias, q_norm_g, k_norm_g)
    pool_c = _pool(seq_c(uc)).reshape(n_c, POOL_WIDTH)
    pool_l = _pool(seq_l(ul)).reshape(n_l, POOL_WIDTH)

    x1c, h2c, affc = _merge(xc, att_c.reshape(n_c, NA_WIDTH), pool_c, gac, gbc, mods, w_att_b, w_pool_b, pool_scale,
                            w_out_b, norm2_g, w_router[0], cond_ctx)
    x1l, h2l, affl = _merge(xl, att_l.reshape(n_l, NA_WIDTH), pool_l, gal, gbl, mods, w_att_b, w_pool_b, pool_scale,
                            w_out_b, norm2_g, w_router[0], cond_lat)

    cap_c = max(1, (CAPACITY_FACTOR * n_c) // N_EXPERTS)
    cap_l = max(1, (CAPACITY_FACTOR * n_l) // N_EXPERTS)
    z_valid = N_EXPERTS * (cap_c + cap_l)
    z_rows = z_valid + Z_CHUNK + ROW_ALIGN
    idx_c, gate_c, qpos_c, w0_c, nc_c, qt_c, kt_c = _routing_tables(affc, cap_c, 0)
    idx_l, gate_l, qpos_l, w0_l, nc_l, qt_l, kt_l = _routing_tables(affl, cap_l, N_EXPERTS * cap_c)
    gidx = jnp.concatenate([idx_c, idx_l], axis=1).reshape(-1)
    qpos = jnp.concatenate([qpos_c, qpos_l], axis=1).reshape(-1)
    gate = jnp.concatenate([gate_c, gate_l], axis=1)[:, :, None]

    z = _moe(gidx, qpos, h2c, h2l, gate, w_gate_e[0], w_up_e[0], w_down_e[0], cap_c, cap_l, z_rows)

    y_c = _combine(w0_c, nc_c, z, x1c, mods, qt_c, kt_c, cond_ctx)
    y_l = _combine(w0_l, nc_l, z, x1l, mods, qt_l, kt_l, cond_lat)

    state_k = kn_c.reshape(nb, 1, ls, NA_HEADS, HEAD_DIM)
    state_v = vc.reshape(nb, 1, ls, NA_HEADS, HEAD_DIM)
    return (y_c.reshape(nb, ls, d), y_l.reshape(db, ll, d), state_k, state_v)
```

```python
import functools

import jax
import jax.numpy as jnp
from jax import lax
from jax.experimental import pallas as pl
from jax.experimental.pallas import tpu as pltpu

F32 = jnp.float32
BF16 = jnp.bfloat16
I32 = jnp.int32

D_MODEL = 1024
GRID_W = 64
NA_HEADS = 8
HEAD_DIM = 64
NA_WIDTH = NA_HEADS * HEAD_DIM
WIN_ROWS = 8
WIN_COLS = 16
POOL_WINDOWS = (2, 4, 8, 16)
POOL_GROUP_DIM = 128
POOL_WIDTH = 512
POOL_OUT_DIM = 256
N_EXPERTS = 16
CAPACITY_FACTOR = 2
D_EXPERT = 2752
N_MOD = 6
EPS = 1e-6
NEG_INF = -1e30

TOKEN_BLOCK = 256
EXPERT_F_BLOCK = 256
EXPERT_M_BLOCK = 512
Z_CHUNK = 256
ROW_ALIGN = 8
VMEM_LIMIT = 56 * 1024 * 1024


def _dot(a, b):
    return jnp.dot(a, b, preferred_element_type=F32)


def _dot_nt(a, b):
    return lax.dot_general(a, b, (((1,), (1,)), ((), ())), preferred_element_type=F32)


def _dot3(a, b):
    a_hi = a.astype(BF16)
    a_lo = (a - a_hi.astype(F32)).astype(BF16)
    b_hi = b.astype(BF16)
    b_lo = (b - b_hi.astype(F32)).astype(BF16)
    return _dot(a_hi, b_hi) + _dot(a_hi, b_lo) + _dot(a_lo, b_hi)


def _sigmoid(x):
    return 1.0 / (1.0 + jnp.exp(-x))


def _rms(x, g):
    return x * lax.rsqrt(jnp.mean(x * x, axis=-1, keepdims=True) + EPS) * g


def _params(sem):
    return pltpu.CompilerParams(dimension_semantics=sem, vmem_limit_bytes=VMEM_LIMIT)


def _ada_kernel(cond_ref, w_ref, b_ref, o_ref):
    c = cond_ref[...]
    o_ref[...] = _dot3(c * _sigmoid(c), w_ref[...]) + b_ref[...]


def _ada(cond, w_ada, b_ada):
    rows, d = cond.shape
    n_out = w_ada.shape[1]
    bn = 512
    return pl.pallas_call(
        _ada_kernel,
        grid=(n_out // bn,),
        in_specs=[pl.BlockSpec((rows, d), lambda i: (0, 0)),
                  pl.BlockSpec((d, bn), lambda i: (0, i)),
                  pl.BlockSpec((1, bn), lambda i: (0, i))],
        out_specs=pl.BlockSpec((rows, bn), lambda i: (0, i)),
        out_shape=jax.ShapeDtypeStruct((rows, n_out), F32),
        compiler_params=_params(("arbitrary",)),
        name="ada",
    )(cond, w_ada, b_ada)


_IN_CUTS = (0, 512, 1024, 1536, 2048, 3072, 4096)


def _in_kernel(x_ref, mod_ref, g_ref, w_ref, q_ref, k_ref, v_ref, u_ref, ga_ref, gb_ref):
    y = _rms(x_ref[...], g_ref[...])
    h = (y * (1.0 + mod_ref[0, 1:2, :]) + mod_ref[0, 0:1, :]).astype(BF16)
    outs = (q_ref, k_ref, v_ref, u_ref, ga_ref, gb_ref)
    for o_ref, lo, hi in zip(outs, _IN_CUTS[:-1], _IN_CUTS[1:]):
        o_ref[...] = _dot(h, w_ref[:, lo:hi])


def _in_proj(x2d, mods, norm_g, w_in_bf16, cond_of_block):
    n, d = x2d.shape
    widths = [hi - lo for lo, hi in zip(_IN_CUTS[:-1], _IN_CUTS[1:])]
    return pl.pallas_call(
        _in_kernel,
        grid=(n // TOKEN_BLOCK,),
        in_specs=[pl.BlockSpec((TOKEN_BLOCK, d), lambda i: (i, 0)),
                  pl.BlockSpec((1, N_MOD, d), lambda i: (cond_of_block(i), 0, 0)),
                  pl.BlockSpec((1, d), lambda i: (0, 0)),
                  pl.BlockSpec(w_in_bf16.shape, lambda i: (0, 0))],
        out_specs=[pl.BlockSpec((TOKEN_BLOCK, w), lambda i: (i, 0)) for w in widths],
        out_shape=[jax.ShapeDtypeStruct((n, w), F32) for w in widths],
        compiler_params=_params(("arbitrary",)),
        name="in_proj",
    )(x2d, mods, norm_g, w_in_bf16)


def _attn_ctx_kernel(q_ref, k_ref, v_ref, qg_ref, kg_ref, att_ref, kn_ref):
    scale = HEAD_DIM ** -0.5
    for h in range(NA_HEADS):
        sl = slice(h * HEAD_DIM, (h + 1) * HEAD_DIM)
        qn = _rms(q_ref[0, :, sl], qg_ref[...]) * scale
        kn = _rms(k_ref[0, :, sl], kg_ref[...])
        kn_ref[0, :, sl] = kn
        s = _dot_nt(qn.astype(BF16), kn.astype(BF16))
        e = jnp.exp(s - jnp.max(s, axis=-1, keepdims=True))
        p = e / jnp.sum(e, axis=-1, keepdims=True)
        att_ref[0, :, sl] = _dot(p.astype(BF16), v_ref[0, :, sl].astype(BF16))


def _attn_ctx(q, k, v, q_g, k_g):
    b, l, w = q.shape
    blk = pl.BlockSpec((1, l, w), lambda i: (i, 0, 0))
    gspec = pl.BlockSpec((1, HEAD_DIM), lambda i: (0, 0))
    return pl.pallas_call(
        _attn_ctx_kernel,
        grid=(b,),
        in_specs=[blk, blk, blk, gspec, gspec],
        out_specs=[blk, blk],
        out_shape=[jax.ShapeDtypeStruct((b, l, w), F32)] * 2,
        compiler_params=_params(("arbitrary",)),
        name="attn_ctx",
    )(q, k, v, q_g, k_g)


def _attn_lat_kernel(q_ref, k_ref, v_ref, ck_ref, cv_ref, bias_ref, qg_ref, kg_ref, att_ref,
                     kn_s, vb_s, ckb_s, cvb_s, *, rows, kr):
    r = pl.program_id(1)

    @pl.when(r == 0)
    def _():
        for h in range(NA_HEADS):
            sl = slice(h * HEAD_DIM, (h + 1) * HEAD_DIM)
            kn_s[:, sl] = _rms(k_ref[0, :, sl], kg_ref[...]).astype(BF16)
        vb_s[...] = v_ref[0].astype(BF16)
        ckb_s[...] = ck_ref[0].astype(BF16)
        cvb_s[...] = cv_ref[0].astype(BF16)

    row_start = jnp.clip(r - kr // 2, 0, rows - kr)
    key_rows = pl.ds(pl.multiple_of(row_start * GRID_W, GRID_W), kr * GRID_W)
    scale = HEAD_DIM ** -0.5
    for h in range(NA_HEADS):
        sl = slice(h * HEAD_DIM, (h + 1) * HEAD_DIM)
        qn = (_rms(q_ref[0, :, sl], qg_ref[...]) * scale).astype(BF16)
        s_loc = _dot_nt(qn, kn_s[key_rows, sl]) + bias_ref[h, r]
        s_ctx = _dot_nt(qn, ckb_s[:, sl])
        m = jnp.maximum(jnp.max(s_loc, axis=-1, keepdims=True), jnp.max(s_ctx, axis=-1, keepdims=True))
        e_loc = jnp.exp(s_loc - m)
        e_ctx = jnp.exp(s_ctx - m)
        denom = jnp.sum(e_loc, axis=-1, keepdims=True) + jnp.sum(e_ctx, axis=-1, keepdims=True)
        o = (_dot((e_loc / denom).astype(BF16), vb_s[key_rows, sl])
             + _dot((e_ctx / denom).astype(BF16), cvb_s[:, sl]))
        att_ref[0, :, sl] = o


def _attn_lat(q, k, v, ck, cv, bias, q_g, k_g):
    b, l, w = q.shape
    lc = ck.shape[1]
    rows = l // GRID_W
    kr = min(WIN_ROWS, rows)
    seq = pl.BlockSpec((1, l, w), lambda i, r: (i, 0, 0))
    ctx = pl.BlockSpec((1, lc, w), lambda i, r: (i, 0, 0))
    rowblk = pl.BlockSpec((1, GRID_W, w), lambda i, r: (i, r, 0))
    gspec = pl.BlockSpec((1, HEAD_DIM), lambda i, r: (0, 0))
    return pl.pallas_call(
        functools.partial(_attn_lat_kernel, rows=rows, kr=kr),
        grid=(b, rows),
        in_specs=[rowblk, seq, seq, ctx, ctx,
                  pl.BlockSpec(bias.shape, lambda i, r: (0, 0, 0, 0)), gspec, gspec],
        out_specs=rowblk,
        out_shape=jax.ShapeDtypeStruct((b, l, w), F32),
        scratch_shapes=[pltpu.VMEM((l, w), BF16), pltpu.VMEM((l, w), BF16),
                        pltpu.VMEM((lc, w), BF16), pltpu.VMEM((lc, w), BF16)],
        compiler_params=_params(("arbitrary", "arbitrary")),
        name="attn_lat",
    )(q, k, v, ck, cv, bias, q_g, k_g)


def _bias_kernel(rpb_ref, o_ref, *, rows, kr):
    h = pl.program_id(0)
    n_ro, n_co = 2 * WIN_ROWS - 1, 2 * WIN_COLS - 1
    wq = lax.broadcasted_iota(I32, (GRID_W, GRID_W), 0)
    wk = lax.broadcasted_iota(I32, (GRID_W, GRID_W), 1)
    co = jnp.clip(wk - wq, -(WIN_COLS - 1), WIN_COLS - 1) + (WIN_COLS - 1)
    col_start = jnp.clip(wq - WIN_COLS // 2, 0, GRID_W - WIN_COLS)
    in_win = (wk >= col_start) & (wk < col_start + WIN_COLS)
    co = jnp.where(in_win, co, -1)
    tables = {}
    for r in range(rows):
        row_start = min(max(r - kr // 2, 0), rows - kr)
        for j in range(kr):
            ro = row_start + j - r + (WIN_ROWS - 1)
            if ro not in tables:
                t = jnp.full((GRID_W, GRID_W), NEG_INF, F32)
                for c in range(n_co):
                    t = jnp.where(co == c, rpb_ref[(h * n_ro + ro) * n_co + c], t)
                tables[ro] = t
            o_ref[0, r, :, j * GRID_W:(j + 1) * GRID_W] = tables[ro]


def _window_bias(rpb, rows):
    kr = min(WIN_ROWS, rows)
    nh = rpb.shape[0]
    return pl.pallas_call(
        functools.partial(_bias_kernel, rows=rows, kr=kr),
        grid=(nh,),
        in_specs=[pl.BlockSpec(memory_space=pltpu.SMEM)],
        out_specs=pl.BlockSpec((1, rows, GRID_W, kr * GRID_W), lambda h: (h, 0, 0, 0)),
        out_shape=jax.ShapeDtypeStruct((nh, rows, GRID_W, kr * GRID_W), F32),
        compiler_params=_params(("arbitrary",)),
        name="window_bias",
    )(rpb.reshape(-1))


def _pool_kernel(u_ref, o_ref):
    l = u_ref.shape[1]
    t = lax.broadcasted_iota(I32, (l, POOL_GROUP_DIM), 0)
    for g, w in enumerate(POOL_WINDOWS):
        sl = slice(g * POOL_GROUP_DIM, (g + 1) * POOL_GROUP_DIM)
        x = u_ref[0, :, sl]
        acc = x
        for k in range(-(w // 2), w - w // 2):
            if k == 0:
                continue
            shifted = pltpu.roll(x, (-k) % l, 0)
            acc = acc + jnp.where((t + k >= 0) & (t + k < l), shifted, 0.0)
        cnt = (jnp.minimum(t + (w - w // 2), l) - jnp.maximum(t - w // 2, 0)).astype(F32)
        o_ref[0, :, sl] = acc / cnt - x


def _pool(u):
    b, l, w = u.shape
    blk = pl.BlockSpec((1, l, w), lambda i: (i, 0, 0))
    return pl.pallas_call(
        _pool_kernel, grid=(b,), in_specs=[blk], out_specs=blk,
        out_shape=jax.ShapeDtypeStruct((b, l, w), F32),
        compiler_params=_params(("arbitrary",)),
        name="pool",
    )(u)


def _merge_kernel(x_ref, att_ref, pl_ref, ga_ref, gb_ref, mod_ref, watt_ref, wpool_ref, ps_ref, wout_ref,
                  g2_ref, wr_ref, x1_ref, h2_ref, aff_ref):
    o_a = _dot(att_ref[...].astype(BF16), watt_ref[...])
    o_b = jnp.concatenate(
        [_dot(pl_ref[:, g * POOL_GROUP_DIM:(g + 1) * POOL_GROUP_DIM].astype(BF16), wpool_ref[g])
         for g in range(len(POOL_WINDOWS))], axis=-1) * ps_ref[...]
    merged = _sigmoid(ga_ref[...]) * o_a + _sigmoid(gb_ref[...]) * o_b
    x1 = x_ref[...] + mod_ref[0, 2:3, :] * _dot(merged.astype(BF16), wout_ref[...])
    x1_ref[...] = x1
    h2 = _rms(x1, g2_ref[...]) * (1.0 + mod_ref[0, 4:5, :]) + mod_ref[0, 3:4, :]
    h2_ref[...] = h2
    logits = _dot3(h2, wr_ref[...])
    e = jnp.exp(logits - jnp.max(logits, axis=-1, keepdims=True))
    aff_ref[...] = e / jnp.sum(e, axis=-1, keepdims=True)


def _merge(x2d, att, pooled, ga, gb, mods, w_att, w_pool, pool_scale, w_out, norm2_g, w_router, cond_of_block):
    n, d = x2d.shape
    tb = TOKEN_BLOCK
    row = lambda w: pl.BlockSpec((tb, w), lambda i: (i, 0))
    full = lambda a: pl.BlockSpec(a.shape, lambda i: (0,) * a.ndim)
    return pl.pallas_call(
        _merge_kernel,
        grid=(n // tb,),
        in_specs=[row(d), row(NA_WIDTH), row(POOL_WIDTH), row(d), row(d),
                  pl.BlockSpec((1, N_MOD, d), lambda i: (cond_of_block(i), 0, 0)),
                  full(w_att), full(w_pool), full(pool_scale), full(w_out), full(norm2_g), full(w_router)],
        out_specs=[row(d), row(d), row(N_EXPERTS)],
        out_shape=[jax.ShapeDtypeStruct((n, d), F32), jax.ShapeDtypeStruct((n, d), F32),
                   jax.ShapeDtypeStruct((n, N_EXPERTS), F32)],
        compiler_params=_params(("arbitrary",)),
        name="merge",
    )(x2d, att, pooled, ga, gb, mods, w_att, w_pool, pool_scale, w_out, norm2_g, w_router)


def _prefix_incl(mask_f32, out_ref):
    e, n = mask_f32.shape
    i = lax.broadcasted_iota(I32, (128, 128), 0)
    j = lax.broadcasted_iota(I32, (128, 128), 1)
    tri = jnp.where(i <= j, 1.0, 0.0).astype(BF16)
    carry = jnp.zeros((e, 1), F32)
    for c in range(n // 128):
        inc = _dot(mask_f32[:, c * 128:(c + 1) * 128].astype(BF16), tri) + carry
        out_ref[:, c * 128:(c + 1) * 128] = inc.astype(I32)
        carry = inc[:, 127:128]


def _route_kernel(aff_ref, sel_ref, rinc_ref, tmp_ref, *, cap):
    aff = aff_ref[...]

    def bit_step(i, thr):
        cand = thr | jnp.left_shift(jnp.int32(1), 30 - i)
        cnt = jnp.sum(jnp.where(aff >= pltpu.bitcast(cand, F32), 1, 0), axis=1, keepdims=True)
        return jnp.where(cnt >= cap, cand, thr)

    thr = lax.fori_loop(0, 31, bit_step, jnp.zeros((aff.shape[0], 1), I32))
    gt = aff >= pltpu.bitcast(thr + 1, F32)
    eq = (aff >= pltpu.bitcast(thr, F32)) & jnp.logical_not(gt)
    need = cap - jnp.sum(jnp.where(gt, 1, 0), axis=1, keepdims=True)
    _prefix_incl(jnp.where(eq, 1.0, 0.0), tmp_ref)
    sel = gt | (eq & (tmp_ref[...] <= need))
    sel_ref[...] = jnp.where(sel, 1, 0)
    _prefix_incl(jnp.where(sel, 1.0, 0.0), rinc_ref)


def _route(aff_t, cap):
    e, n = aff_t.shape
    full = pl.BlockSpec((e, n), lambda: (0, 0))
    return pl.pallas_call(
        functools.partial(_route_kernel, cap=cap),
        in_specs=[full], out_specs=[full, full],
        out_shape=[jax.ShapeDtypeStruct((e, n), I32)] * 2,
        scratch_shapes=[pltpu.VMEM((e, n), I32)],
        compiler_params=pltpu.CompilerParams(vmem_limit_bytes=VMEM_LIMIT),
        name="route",
    )(aff_t)


def _moe_kernel(gidx_ref, qpos_ref, hc_ref, hl_ref, gate_ref, wg_ref, wu_ref, wd_ref, z_ref,
                xf_ref, xb_ref, acc_ref, sem_g, sem_s, *, cap_c, cap_l, n_f):
    e = pl.program_id(0)
    j = pl.program_id(1)
    m_tot = cap_c + cap_l
    base = e * m_tot
    tf = wg_ref.shape[1]
    z_pad = z_ref.shape[0] - N_EXPERTS * m_tot

    @pl.when((e == 0) & (j == 0))
    def _():
        acc_ref[0:z_pad, :] = jnp.zeros((z_pad, acc_ref.shape[1]), F32)
        cp = pltpu.make_async_copy(acc_ref.at[pl.ds(0, z_pad)], z_ref.at[pl.ds(N_EXPERTS * m_tot, z_pad)], sem_s)
        cp.start()
        cp.wait()

    @pl.when(j == 0)
    def _():
        def issue_c(s, c):
            t = gidx_ref[base + s]
            pltpu.make_async_copy(hc_ref.at[pl.ds(t, 1)], xf_ref.at[pl.ds(s, 1)], sem_g).start()
            return c

        def issue_l(s, c):
            t = gidx_ref[base + cap_c + s]
            pltpu.make_async_copy(hl_ref.at[pl.ds(t, 1)], xf_ref.at[pl.ds(cap_c + s, 1)], sem_g).start()
            return c

        lax.fori_loop(0, cap_c, issue_c, 0)
        lax.fori_loop(0, cap_l, issue_l, 0)
        pltpu.make_async_copy(hl_ref.at[pl.ds(0, m_tot)], xf_ref, sem_g).wait()
        xb_ref[...] = xf_ref[...].astype(BF16)
        acc_ref[...] = jnp.zeros_like(acc_ref)

    valid = D_EXPERT - j * tf
    cmask = lax.broadcasted_iota(I32, (1, tf), 1) < valid
    rmask = lax.broadcasted_iota(I32, (tf, 1), 0) < valid
    wg = wg_ref[0].astype(BF16)
    wu = wu_ref[0].astype(BF16)
    wd = jnp.where(rmask, wd_ref[0], 0.0).astype(BF16)
    for m in range(m_tot // EXPERT_M_BLOCK):
        rs = slice(m * EXPERT_M_BLOCK, (m + 1) * EXPERT_M_BLOCK)
        xs = xb_ref[rs, :]
        a = _dot_nt(xs, wg)
        b = _dot_nt(xs, wu)
        hm = jnp.where(cmask, a * _sigmoid(a) * b, 0.0).astype(BF16)
        acc_ref[rs, :] += _dot(hm, wd)

    @pl.when(j == n_f - 1)
    def _():
        acc_ref[...] = acc_ref[...] * gate_ref[0]

        def scatter(s, c):
            q = qpos_ref[base + s]
            pltpu.make_async_copy(acc_ref.at[pl.ds(s, 1)], z_ref.at[pl.ds(q, 1)], sem_s).start()
            return c

        lax.fori_loop(0, m_tot, scatter, 0)
        pltpu.make_async_copy(acc_ref, z_ref.at[pl.ds(0, m_tot)], sem_s).wait()


def _moe(gidx, qpos, h_ctx, h_lat, gate, w_gate, w_up, w_down, cap_c, cap_l, z_rows):
    d = h_ctx.shape[1]
    m_tot = cap_c + cap_l
    tf = EXPERT_F_BLOCK
    n_f = pl.cdiv(D_EXPERT, tf)
    grid_spec = pltpu.PrefetchScalarGridSpec(
        num_scalar_prefetch=2,
        grid=(N_EXPERTS, n_f),
        in_specs=[pl.BlockSpec(memory_space=pl.ANY),
                  pl.BlockSpec(memory_space=pl.ANY),
                  pl.BlockSpec((1, m_tot, 1), lambda e, j, *_: (e, 0, 0)),
                  pl.BlockSpec((1, tf, d), lambda e, j, *_: (e, j, 0)),
                  pl.BlockSpec((1, tf, d), lambda e, j, *_: (e, j, 0)),
                  pl.BlockSpec((1, tf, d), lambda e, j, *_: (e, j, 0))],
        out_specs=pl.BlockSpec(memory_space=pl.ANY),
        scratch_shapes=[pltpu.VMEM((m_tot, d), F32), pltpu.VMEM((m_tot, d), BF16), pltpu.VMEM((m_tot, d), F32),
                        pltpu.SemaphoreType.DMA, pltpu.SemaphoreType.DMA],
    )
    return pl.pallas_call(
        functools.partial(_moe_kernel, cap_c=cap_c, cap_l=cap_l, n_f=n_f),
        grid_spec=grid_spec,
        out_shape=jax.ShapeDtypeStruct((z_rows, d), F32),
        compiler_params=_params(("arbitrary", "arbitrary")),
        name="moe",
    )(gidx, qpos, h_ctx, h_lat, gate, w_gate, w_up, w_down)


def _combine_kernel(w0_ref, nc_ref, z_ref, x1_ref, mod_ref, qt_ref, kt_ref, o_ref, zbuf, sem, acc_ref):
    blk = pl.program_id(0)
    w0 = w0_ref[blk]
    qt = qt_ref[...]
    kt = kt_ref[...]
    acc_ref[...] = jnp.zeros_like(acc_ref)

    def chunk(c, carry):
        start = pl.multiple_of(w0 + c * Z_CHUNK, ROW_ALIGN)
        cp = pltpu.make_async_copy(z_ref.at[pl.ds(start, Z_CHUNK)], zbuf, sem)
        cp.start()
        cp.wait()
        r = start + lax.broadcasted_iota(I32, (qt.shape[0], Z_CHUNK), 1)
        onehot = jnp.where((r >= qt) & (r < qt + kt), 1.0, 0.0).astype(BF16)
        acc_ref[...] += _dot(onehot, zbuf[...].astype(BF16))
        return carry

    lax.fori_loop(0, nc_ref[blk], chunk, 0)
    o_ref[...] = x1_ref[...] + mod_ref[0, 5:6, :] * acc_ref[...]


def _combine(w0, nc, z, x1, mods, qt, kt, cond_of_block):
    n, d = x1.shape
    tb = TOKEN_BLOCK
    grid_spec = pltpu.PrefetchScalarGridSpec(
        num_scalar_prefetch=2,
        grid=(n // tb,),
        in_specs=[pl.BlockSpec(memory_space=pl.ANY),
                  pl.BlockSpec((tb, d), lambda i, *_: (i, 0)),
                  pl.BlockSpec((1, N_MOD, d), lambda i, *_: (cond_of_block(i), 0, 0)),
                  pl.BlockSpec((tb, 1), lambda i, *_: (i, 0)),
                  pl.BlockSpec((tb, 1), lambda i, *_: (i, 0))],
        out_specs=pl.BlockSpec((tb, d), lambda i, *_: (i, 0)),
        scratch_shapes=[pltpu.VMEM((Z_CHUNK, d), F32), pltpu.SemaphoreType.DMA, pltpu.VMEM((tb, d), F32)],
    )
    return pl.pallas_call(
        _combine_kernel,
        grid_spec=grid_spec,
        out_shape=jax.ShapeDtypeStruct((n, d), F32),
        compiler_params=_params(("arbitrary",)),
        name="combine",
    )(w0, nc, z, x1, mods, qt, kt)


def _routing_tables(aff, cap, z_off):
    n = aff.shape[0]
    aff_t = aff.T
    sel, rinc = _route(aff_t, cap)
    slots = jnp.arange(cap, dtype=I32)
    idx = jnp.sum((rinc[:, None, :] <= slots[None, :, None]).astype(I32), axis=-1)
    gate = jnp.take_along_axis(aff_t, idx, axis=1)
    k_tok = jnp.sum(sel, axis=0)
    q_tok = jnp.cumsum(k_tok) - k_tok + z_off
    before = jnp.cumsum(sel, axis=0) - sel
    qpos = jnp.take_along_axis(q_tok[None, :] + before, idx, axis=1)
    nblk = n // TOKEN_BLOCK
    q_blk = q_tok[::TOKEN_BLOCK]
    end_blk = q_blk + jnp.sum(k_tok.reshape(nblk, TOKEN_BLOCK), axis=1)
    w0 = (q_blk // ROW_ALIGN) * ROW_ALIGN
    nc = (end_blk - w0 + Z_CHUNK - 1) // Z_CHUNK
    return idx, gate, qpos.astype(I32), w0.astype(I32), nc.astype(I32), q_tok.astype(I32)[:, None], k_tok.astype(I32)[:, None]


def kernel(x_prompt, x_sample, cache_k, cache_v, c, c_ctx, norm1_g, norm2_g, w_ada, b_ada, w_in, q_norm_g, k_norm_g,
           rpb, w_att_proj, w_pool, pool_scale, w_out, w_router, w_gate_e, w_up_e, w_down_e):
    assert w_ada.shape[0] == 1, "single trunk layer"
    nb, ls, d = x_prompt.shape
    db, ll, _ = x_sample.shape
    n_c, n_l = nb * ls, db * ll
    lc = cache_k.shape[2]

    cond = jnp.concatenate([c_ctx[None, :], c, jnp.zeros((16 - 1 - db, d), F32)], axis=0)
    mods = _ada(cond, w_ada[0], b_ada).reshape(16, N_MOD, d)

    w_in_b = w_in[0].astype(BF16)
    w_att_b = w_att_proj[0].astype(BF16)
    w_pool_b = w_pool[0].astype(BF16)
    w_out_b = w_out[0].astype(BF16)

    cond_ctx = lambda i: 0
    lat_blocks = ll // TOKEN_BLOCK
    cond_lat = lambda i: 1 + i // lat_blocks

    xc = x_prompt.reshape(n_c, d)
    xl = x_sample.reshape(n_l, d)
    qc, kc, vc, uc, gac, gbc = _in_proj(xc, mods, norm1_g, w_in_b, cond_ctx)
    ql, kl, vl, ul, gal, gbl = _in_proj(xl, mods, norm1_g, w_in_b, cond_lat)

    seq_c = lambda a: a.reshape(nb, ls, a.shape[-1])
    seq_l = lambda a: a.reshape(db, ll, a.shape[-1])
    att_c, kn_c = _attn_ctx(seq_c(qc), seq_c(kc), seq_c(vc), q_norm_g, k_norm_g)
    bias = _window_bias(rpb[0], ll // GRID_W)
    att_l = _attn_lat(seq_l(ql), seq_l(kl), seq_l(vl), cache_k[:, 0].reshape(db, lc, NA_WIDTH),
                      cache_v[:, 0].reshape(db, lc, NA_WIDTH), bias, q_norm_g, k_norm_g)
    pool_c = _pool(seq_c(uc)).reshape(n_c, POOL_WIDTH)
    pool_l = _pool(seq_l(ul)).reshape(n_l, POOL_WIDTH)

    x1c, h2c, affc = _merge(xc, att_c.reshape(n_c, NA_WIDTH), pool_c, gac, gbc, mods, w_att_b, w_pool_b, pool_scale,
                            w_out_b, norm2_g, w_router[0], cond_ctx)
    x1l, h2l, affl = _merge(xl, att_l.reshape(n_l, NA_WIDTH), pool_l, gal, gbl, mods, w_att_b, w_pool_b, pool_scale,
                            w_out_b, norm2_g, w_router[0], cond_lat)

    cap_c = max(1, (CAPACITY_FACTOR * n_c) // N_EXPERTS)
    cap_l = max(1, (CAPACITY_FACTOR * n_l) // N_EXPERTS)
    z_valid = N_EXPERTS * (cap_c + cap_l)
    z_rows = z_valid + Z_CHUNK + ROW_ALIGN
    idx_c, gate_c, qpos_c, w0_c, nc_c, qt_c, kt_c = _routing_tables(affc, cap_c, 0)
    idx_l, gate_l, qpos_l, w0_l, nc_l, qt_l, kt_l = _routing_tables(affl, cap_l, N_EXPERTS * cap_c)
    gidx = jnp.concatenate([idx_c, idx_l], axis=1).reshape(-1)
    qpos = jnp.concatenate([qpos_c, qpos_l], axis=1).reshape(-1)
    gate = jnp.concatenate([gate_c, gate_l], axis=1)[:, :, None]

    z = _moe(gidx, qpos, h2c, h2l, gate, jnp.swapaxes(w_gate_e[0], 1, 2), jnp.swapaxes(w_up_e[0], 1, 2), w_down_e[0],
             cap_c, cap_l, z_rows)

    y_c = _combine(w0_c, nc_c, z, x1c, mods, qt_c, kt_c, cond_ctx)
    y_l = _combine(w0_l, nc_l, z, x1l, mods, qt_l, kt_l, cond_lat)

    state_k = kn_c.reshape(nb, 1, ls, NA_HEADS, HEAD_DIM)
    state_v = vc.reshape(nb, 1, ls, NA_HEADS, HEAD_DIM)
    return (y_c.reshape(nb, ls, d), y_l.reshape(db, ll, d), state_k, state_v)
```

```python
import functools

import jax
import jax.numpy as jnp
from jax import lax
from jax.experimental import pallas as pl
from jax.experimental.pallas import tpu as pltpu

F32 = jnp.float32
BF16 = jnp.bfloat16
I32 = jnp.int32

D_MODEL = 1024
GRID_W = 64
NA_HEADS = 8
HEAD_DIM = 64
NA_WIDTH = NA_HEADS * HEAD_DIM
WIN_ROWS = 8
WIN_COLS = 16
POOL_WINDOWS = (2, 4, 8, 16)
POOL_GROUP_DIM = 128
POOL_WIDTH = 512
POOL_OUT_DIM = 256
N_EXPERTS = 16
CAPACITY_FACTOR = 2
D_EXPERT = 2752
N_MOD = 6
EPS = 1e-6
NEG_INF = -1e30

TOKEN_BLOCK = 256
EXPERT_F_BLOCK = 256
EXPERT_M_BLOCK = 512
Z_CHUNK = 256
ROW_ALIGN = 8
VMEM_LIMIT = 56 * 1024 * 1024


def _dot(a, b):
    return jnp.dot(a, b, preferred_element_type=F32)


def _dot_nt(a, b):
    return lax.dot_general(a, b, (((1,), (1,)), ((), ())), preferred_element_type=F32)


def _dot3(a, b):
    a_hi = a.astype(BF16)
    a_lo = (a - a_hi.astype(F32)).astype(BF16)
    b_hi = b.astype(BF16)
    b_lo = (b - b_hi.astype(F32)).astype(BF16)
    return _dot(a_hi, b_hi) + _dot(a_hi, b_lo) + _dot(a_lo, b_hi)


def _sigmoid(x):
    return 1.0 / (1.0 + jnp.exp(-x))


def _rms(x, g):
    return x * lax.rsqrt(jnp.mean(x * x, axis=-1, keepdims=True) + EPS) * g


def _params(sem):
    return pltpu.CompilerParams(dimension_semantics=sem, vmem_limit_bytes=VMEM_LIMIT)


def _ada_kernel(cond_ref, w_ref, b_ref, o_ref):
    c = cond_ref[...]
    o_ref[...] = _dot3(c * _sigmoid(c), w_ref[...]) + b_ref[...]


def _ada(cond, w_ada, b_ada):
    rows, d = cond.shape
    n_out = w_ada.shape[1]
    bn = 512
    return pl.pallas_call(
        _ada_kernel,
        grid=(n_out // bn,),
        in_specs=[pl.BlockSpec((rows, d), lambda i: (0, 0)),
                  pl.BlockSpec((d, bn), lambda i: (0, i)),
                  pl.BlockSpec((1, bn), lambda i: (0, i))],
        out_specs=pl.BlockSpec((rows, bn), lambda i: (0, i)),
        out_shape=jax.ShapeDtypeStruct((rows, n_out), F32),
        compiler_params=_params(("arbitrary",)),
        name="ada",
    )(cond, w_ada, b_ada)


_IN_CUTS = (0, 512, 1024, 1536, 2048, 3072, 4096)


def _in_kernel(x_ref, mod_ref, g_ref, w_ref, q_ref, k_ref, v_ref, u_ref, ga_ref, gb_ref):
    y = _rms(x_ref[...], g_ref[...])
    h = (y * (1.0 + mod_ref[0, 1:2, :]) + mod_ref[0, 0:1, :]).astype(BF16)
    outs = (q_ref, k_ref, v_ref, u_ref, ga_ref, gb_ref)
    for o_ref, lo, hi in zip(outs, _IN_CUTS[:-1], _IN_CUTS[1:]):
        o_ref[...] = _dot(h, w_ref[:, lo:hi])


def _in_proj(x2d, mods, norm_g, w_in_bf16, cond_of_block):
    n, d = x2d.shape
    widths = [hi - lo for lo, hi in zip(_IN_CUTS[:-1], _IN_CUTS[1:])]
    return pl.pallas_call(
        _in_kernel,
        grid=(n // TOKEN_BLOCK,),
        in_specs=[pl.BlockSpec((TOKEN_BLOCK, d), lambda i: (i, 0)),
                  pl.BlockSpec((1, N_MOD, d), lambda i: (cond_of_block(i), 0, 0)),
                  pl.BlockSpec((1, d), lambda i: (0, 0)),
                  pl.BlockSpec(w_in_bf16.shape, lambda i: (0, 0))],
        out_specs=[pl.BlockSpec((TOKEN_BLOCK, w), lambda i: (i, 0)) for w in widths],
        out_shape=[jax.ShapeDtypeStruct((n, w), F32) for w in widths],
        compiler_params=_params(("arbitrary",)),
        name="in_proj",
    )(x2d, mods, norm_g, w_in_bf16)


def _attn_ctx_kernel(q_ref, k_ref, v_ref, qg_ref, kg_ref, att_ref, kn_ref):
    scale = HEAD_DIM ** -0.5
    for h in range(NA_HEADS):
        sl = slice(h * HEAD_DIM, (h + 1) * HEAD_DIM)
        qn = _rms(q_ref[0, :, sl], qg_ref[...]) * scale
        kn = _rms(k_ref[0, :, sl], kg_ref[...])
        kn_ref[0, :, sl] = kn
        s = _dot_nt(qn.astype(BF16), kn.astype(BF16))
        e = jnp.exp(s - jnp.max(s, axis=-1, keepdims=True))
        p = e / jnp.sum(e, axis=-1, keepdims=True)
        att_ref[0, :, sl] = _dot(p.astype(BF16), v_ref[0, :, sl].astype(BF16))


def _attn_ctx(q, k, v, q_g, k_g):
    b, l, w = q.shape
    blk = pl.BlockSpec((1, l, w), lambda i: (i, 0, 0))
    gspec = pl.BlockSpec((1, HEAD_DIM), lambda i: (0, 0))
    return pl.pallas_call(
        _attn_ctx_kernel,
        grid=(b,),
        in_specs=[blk, blk, blk, gspec, gspec],
        out_specs=[blk, blk],
        out_shape=[jax.ShapeDtypeStruct((b, l, w), F32)] * 2,
        compiler_params=_params(("arbitrary",)),
        name="attn_ctx",
    )(q, k, v, q_g, k_g)


def _attn_lat_kernel(q_ref, k_ref, v_ref, ck_ref, cv_ref, bias_ref, qg_ref, kg_ref, att_ref,
                     kn_s, vb_s, ckb_s, cvb_s, *, rows, kr):
    r = pl.program_id(1)

    @pl.when(r == 0)
    def _():
        for h in range(NA_HEADS):
            sl = slice(h * HEAD_DIM, (h + 1) * HEAD_DIM)
            kn_s[:, sl] = _rms(k_ref[0, :, sl], kg_ref[...]).astype(BF16)
        vb_s[...] = v_ref[0].astype(BF16)
        ckb_s[...] = ck_ref[0].astype(BF16)
        cvb_s[...] = cv_ref[0].astype(BF16)

    row_start = jnp.clip(r - kr // 2, 0, rows - kr)
    key_rows = pl.ds(pl.multiple_of(row_start * GRID_W, GRID_W), kr * GRID_W)
    scale = HEAD_DIM ** -0.5
    for h in range(NA_HEADS):
        sl = slice(h * HEAD_DIM, (h + 1) * HEAD_DIM)
        qn = (_rms(q_ref[0, :, sl], qg_ref[...]) * scale).astype(BF16)
        s_loc = _dot_nt(qn, kn_s[key_rows, sl]) + bias_ref[h, r]
        s_ctx = _dot_nt(qn, ckb_s[:, sl])
        m = jnp.maximum(jnp.max(s_loc, axis=-1, keepdims=True), jnp.max(s_ctx, axis=-1, keepdims=True))
        e_loc = jnp.exp(s_loc - m)
        e_ctx = jnp.exp(s_ctx - m)
        denom = jnp.sum(e_loc, axis=-1, keepdims=True) + jnp.sum(e_ctx, axis=-1, keepdims=True)
        o = (_dot((e_loc / denom).astype(BF16), vb_s[key_rows, sl])
             + _dot((e_ctx / denom).astype(BF16), cvb_s[:, sl]))
        att_ref[0, :, sl] = o


def _attn_lat(q, k, v, ck, cv, bias, q_g, k_g):
    b, l, w = q.shape
    lc = ck.shape[1]
    rows = l // GRID_W
    kr = min(WIN_ROWS, rows)
    seq = pl.BlockSpec((1, l, w), lambda i, r: (i, 0, 0))
    ctx = pl.BlockSpec((1, lc, w), lambda i, r: (i, 0, 0))
    rowblk = pl.BlockSpec((1, GRID_W, w), lambda i, r: (i, r, 0))
    gspec = pl.BlockSpec((1, HEAD_DIM), lambda i, r: (0, 0))
    return pl.pallas_call(
        functools.partial(_attn_lat_kernel, rows=rows, kr=kr),
        grid=(b, rows),
        in_specs=[rowblk, seq, seq, ctx, ctx,
                  pl.BlockSpec(bias.shape, lambda i, r: (0, 0, 0, 0)), gspec, gspec],
        out_specs=rowblk,
        out_shape=jax.ShapeDtypeStruct((b, l, w), F32),
        scratch_shapes=[pltpu.VMEM((l, w), BF16), pltpu.VMEM((l, w), BF16),
                        pltpu.VMEM((lc, w), BF16), pltpu.VMEM((lc, w), BF16)],
        compiler_params=_params(("arbitrary", "arbitrary")),
        name="attn_lat",
    )(q, k, v, ck, cv, bias, q_g, k_g)


def _bias_kernel(rpb_ref, o_ref, *, rows, kr):
    h = pl.program_id(0)
    n_ro, n_co = 2 * WIN_ROWS - 1, 2 * WIN_COLS - 1
    wq = lax.broadcasted_iota(I32, (GRID_W, GRID_W), 0)
    wk = lax.broadcasted_iota(I32, (GRID_W, GRID_W), 1)
    co = jnp.clip(wk - wq, -(WIN_COLS - 1), WIN_COLS - 1) + (WIN_COLS - 1)
    col_start = jnp.clip(wq - WIN_COLS // 2, 0, GRID_W - WIN_COLS)
    in_win = (wk >= col_start) & (wk < col_start + WIN_COLS)
    co = jnp.where(in_win, co, -1)
    tables = {}
    for r in range(rows):
        row_start = min(max(r - kr // 2, 0), rows - kr)
        for j in range(kr):
            ro = row_start + j - r + (WIN_ROWS - 1)
            if ro not in tables:
                t = jnp.full((GRID_W, GRID_W), NEG_INF, F32)
                for c in range(n_co):
                    t = jnp.where(co == c, rpb_ref[(h * n_ro + ro) * n_co + c], t)
                tables[ro] = t
            o_ref[0, r, :, j * GRID_W:(j + 1) * GRID_W] = tables[ro]


def _window_bias(rpb, rows):
    kr = min(WIN_ROWS, rows)
    nh = rpb.shape[0]
    return pl.pallas_call(
        functools.partial(_bias_kernel, rows=rows, kr=kr),
        grid=(nh,),
        in_specs=[pl.BlockSpec(memory_space=pltpu.SMEM)],
        out_specs=pl.BlockSpec((1, rows, GRID_W, kr * GRID_W), lambda h: (h, 0, 0, 0)),
        out_shape=jax.ShapeDtypeStruct((nh, rows, GRID_W, kr * GRID_W), F32),
        compiler_params=_params(("arbitrary",)),
        name="window_bias",
    )(rpb.reshape(-1))


def _pool_kernel(u_ref, o_ref):
    l = u_ref.shape[1]
    t = lax.broadcasted_iota(I32, (l, POOL_GROUP_DIM), 0)
    for g, w in enumerate(POOL_WINDOWS):
        sl = slice(g * POOL_GROUP_DIM, (g + 1) * POOL_GROUP_DIM)
        x = u_ref[0, :, sl]
        acc = x
        for k in range(-(w // 2), w - w // 2):
            if k == 0:
                continue
            shifted = pltpu.roll(x, (-k) % l, 0)
            acc = acc + jnp.where((t + k >= 0) & (t + k < l), shifted, 0.0)
        cnt = (jnp.minimum(t + (w - w // 2), l) - jnp.maximum(t - w // 2, 0)).astype(F32)
        o_ref[0, :, sl] = acc / cnt - x


def _pool(u):
    b, l, w = u.shape
    blk = pl.BlockSpec((1, l, w), lambda i: (i, 0, 0))
    return pl.pallas_call(
        _pool_kernel, grid=(b,), in_specs=[blk], out_specs=blk,
        out_shape=jax.ShapeDtypeStruct((b, l, w), F32),
        compiler_params=_params(("arbitrary",)),
        name="pool",
    )(u)


def _merge_kernel(xc_ref, xl_ref, attc_ref, attl_ref, plc_ref, pll_ref, gac_ref, gal_ref, gbc_ref, gbl_ref,
                  mod_ref, watt_ref, wpool_ref, ps_ref, wout_ref, g2_ref, wr_ref, x1_ref, h2_ref, aff_ref, *, ctx_blocks):
    is_ctx = pl.program_id(0) < ctx_blocks
    pick = lambda c_ref, l_ref: jnp.where(is_ctx, c_ref[...], l_ref[...])
    o_a = _dot(pick(attc_ref, attl_ref).astype(BF16), watt_ref[...])
    pooled = pick(plc_ref, pll_ref)
    o_b = jnp.concatenate(
        [_dot(pooled[:, g * POOL_GROUP_DIM:(g + 1) * POOL_GROUP_DIM].astype(BF16), wpool_ref[g])
         for g in range(len(POOL_WINDOWS))], axis=-1) * ps_ref[...]
    merged = _sigmoid(pick(gac_ref, gal_ref)) * o_a + _sigmoid(pick(gbc_ref, gbl_ref)) * o_b
    x1 = pick(xc_ref, xl_ref) + mod_ref[0, 2:3, :] * _dot(merged.astype(BF16), wout_ref[...])
    x1_ref[...] = x1
    h2 = _rms(x1, g2_ref[...]) * (1.0 + mod_ref[0, 4:5, :]) + mod_ref[0, 3:4, :]
    h2_ref[...] = h2
    logits = _dot3(h2, wr_ref[...])
    e = jnp.exp(logits - jnp.max(logits, axis=-1, keepdims=True))
    aff_ref[...] = e / jnp.sum(e, axis=-1, keepdims=True)


def _merge(ctx, lat, mods, w_att, w_pool, pool_scale, w_out, norm2_g, w_router, lat_blocks_per_seq):
    n_c, d = ctx[0].shape
    n = n_c + lat[0].shape[0]
    tb = TOKEN_BLOCK
    cb = n_c // tb
    crow = lambda w: pl.BlockSpec((tb, w), lambda i: (jnp.minimum(i, cb - 1), 0))
    lrow = lambda w: pl.BlockSpec((tb, w), lambda i: (jnp.maximum(i - cb, 0), 0))
    orow = lambda w: pl.BlockSpec((tb, w), lambda i: (i, 0))
    full = lambda a: pl.BlockSpec(a.shape, lambda i: (0,) * a.ndim)
    cond = lambda i: jnp.where(i < cb, 0, 1 + (i - cb) // lat_blocks_per_seq)
    pairs, specs = [], []
    for a_c, a_l in zip(ctx, lat):
        pairs += [a_c, a_l]
        specs += [crow(a_c.shape[1]), lrow(a_l.shape[1])]
    return pl.pallas_call(
        functools.partial(_merge_kernel, ctx_blocks=cb),
        grid=(n // tb,),
        in_specs=specs + [pl.BlockSpec((1, N_MOD, d), lambda i: (cond(i), 0, 0)),
                          full(w_att), full(w_pool), full(pool_scale), full(w_out), full(norm2_g), full(w_router)],
        out_specs=[orow(d), orow(d), orow(N_EXPERTS)],
        out_shape=[jax.ShapeDtypeStruct((n, d), F32), jax.ShapeDtypeStruct((n, d), F32),
                   jax.ShapeDtypeStruct((n, N_EXPERTS), F32)],
        compiler_params=_params(("arbitrary",)),
        name="merge",
    )(*pairs, mods, w_att, w_pool, pool_scale, w_out, norm2_g, w_router)


def _prefix_incl(mask_f32, out_ref):
    e, n = mask_f32.shape
    i = lax.broadcasted_iota(I32, (128, 128), 0)
    j = lax.broadcasted_iota(I32, (128, 128), 1)
    tri = jnp.where(i <= j, 1.0, 0.0).astype(BF16)
    carry = jnp.zeros((e, 1), F32)
    for c in range(n // 128):
        inc = _dot(mask_f32[:, c * 128:(c + 1) * 128].astype(BF16), tri) + carry
        out_ref[:, c * 128:(c + 1) * 128] = inc.astype(I32)
        carry = inc[:, 127:128]


def _route_kernel(aff_ref, sel_ref, rinc_ref, tmp_ref, *, cap):
    aff = aff_ref[...]

    def bit_step(i, thr):
        cand = thr | jnp.left_shift(jnp.int32(1), 30 - i)
        cnt = jnp.sum(jnp.where(aff >= pltpu.bitcast(cand, F32), 1, 0), axis=1, keepdims=True)
        return jnp.where(cnt >= cap, cand, thr)

    thr = lax.fori_loop(0, 31, bit_step, jnp.zeros((aff.shape[0], 1), I32))
    gt = aff >= pltpu.bitcast(thr + 1, F32)
    eq = (aff >= pltpu.bitcast(thr, F32)) & jnp.logical_not(gt)
    need = cap - jnp.sum(jnp.where(gt, 1, 0), axis=1, keepdims=True)
    _prefix_incl(jnp.where(eq, 1.0, 0.0), tmp_ref)
    sel = gt | (eq & (tmp_ref[...] <= need))
    sel_ref[...] = jnp.where(sel, 1, 0)
    _prefix_incl(jnp.where(sel, 1.0, 0.0), rinc_ref)


def _route(aff_t, cap):
    e, n = aff_t.shape
    full = pl.BlockSpec((e, n), lambda: (0, 0))
    return pl.pallas_call(
        functools.partial(_route_kernel, cap=cap),
        in_specs=[full], out_specs=[full, full],
        out_shape=[jax.ShapeDtypeStruct((e, n), I32)] * 2,
        scratch_shapes=[pltpu.VMEM((e, n), I32)],
        compiler_params=pltpu.CompilerParams(vmem_limit_bytes=VMEM_LIMIT),
        name="route",
    )(aff_t)


def _moe_kernel(gidx_ref, qpos_ref, h_ref, gate_ref, wg_ref, wu_ref, wd_ref, z_ref,
                xf_ref, xb_ref, acc_ref, y_ref, sem_g, sem_s, *, m_tot, n_f, step_rows, z_zero_rows):
    e = pl.program_id(0)
    j = pl.program_id(1)
    n_e = pl.num_programs(0)
    m_iss = step_rows * n_f
    tf = wg_ref.shape[1]
    slot = e % 2

    def gather_row(lst, s, dst_slot):
        t = gidx_ref[lst * m_iss + s]
        return pltpu.make_async_copy(h_ref.at[pl.ds(t, 1)], xf_ref.at[dst_slot, pl.ds(s, 1)], sem_g.at[dst_slot])

    def scatter_row(lst, s):
        q = qpos_ref[lst * m_iss + s]
        return pltpu.make_async_copy(y_ref.at[pl.ds(s, 1)], z_ref.at[pl.ds(q, 1)], sem_s)

    def wait_gathers(dst_slot):
        pltpu.make_async_copy(h_ref.at[pl.ds(0, m_iss)], xf_ref.at[dst_slot], sem_g.at[dst_slot]).wait()

    def wait_scatters():
        pltpu.make_async_copy(y_ref, z_ref.at[pl.ds(0, m_iss)], sem_s).wait()

    @pl.when((e == 0) & (j == 0))
    def _():
        y_ref[...] = jnp.zeros_like(y_ref)
        cp = pltpu.make_async_copy(y_ref.at[pl.ds(0, z_zero_rows)], z_ref.at[pl.ds(n_e * m_tot, z_zero_rows)], sem_s)
        cp.start()
        cp.wait()

        def first(s, c):
            gather_row(0, s, 0).start()
            return c

        lax.fori_loop(0, m_iss, first, 0)

    @pl.when(j == 0)
    def _():
        wait_gathers(slot)
        xb_ref[...] = xf_ref[slot, 0:m_tot, :].astype(BF16)
        acc_ref[...] = jnp.zeros_like(acc_ref)

    nxt = lax.rem(e + 1, n_e)
    for u in range(step_rows):
        s = j * step_rows + u
        gather_row(nxt, s, 1 - slot).start()
        scatter_row(e, s).start()

    valid = D_EXPERT - j * tf
    cmask = lax.broadcasted_iota(I32, (1, tf), 1) < valid
    rmask = lax.broadcasted_iota(I32, (tf, 1), 0) < valid
    wg = wg_ref[0].astype(BF16)
    wu = wu_ref[0].astype(BF16)
    wd = jnp.where(rmask, wd_ref[0], 0.0).astype(BF16)
    for m in range(m_tot // EXPERT_M_BLOCK):
        rs = slice(m * EXPERT_M_BLOCK, (m + 1) * EXPERT_M_BLOCK)
        xs = xb_ref[rs, :]
        a = _dot_nt(xs, wg)
        b = _dot_nt(xs, wu)
        hm = jnp.where(cmask, a * _sigmoid(a) * b, 0.0).astype(BF16)
        acc_ref[rs, :] += _dot(hm, wd)

    @pl.when(j == n_f - 1)
    def _():
        wait_scatters()
        y_ref[0:m_tot, :] = acc_ref[...] * gate_ref[0]

        @pl.when(e == n_e - 1)
        def _():
            def last(s, c):
                scatter_row(n_e, s).start()
                return c

            lax.fori_loop(0, m_iss, last, 0)
            wait_scatters()
            wait_gathers(1 - slot)


def _moe_step_rows(m_tot, n_f):
    return pl.cdiv(pl.cdiv(m_tot, n_f), ROW_ALIGN) * ROW_ALIGN


def _moe(gidx, qpos, h, gate, w_gate, w_up, w_down, m_tot, z_rows, z_zero_rows):
    d = h.shape[1]
    n_e = w_down.shape[0]
    tf = EXPERT_F_BLOCK
    n_f = pl.cdiv(D_EXPERT, tf)
    step_rows = _moe_step_rows(m_tot, n_f)
    m_iss = step_rows * n_f
    grid_spec = pltpu.PrefetchScalarGridSpec(
        num_scalar_prefetch=2,
        grid=(n_e, n_f),
        in_specs=[pl.BlockSpec(memory_space=pl.ANY),
                  pl.BlockSpec((1, m_tot, 1), lambda e, j, *_: (e, 0, 0)),
                  pl.BlockSpec((1, tf, d), lambda e, j, *_: (e, j, 0)),
                  pl.BlockSpec((1, tf, d), lambda e, j, *_: (e, j, 0)),
                  pl.BlockSpec((1, tf, d), lambda e, j, *_: (e, j, 0))],
        out_specs=pl.BlockSpec(memory_space=pl.ANY),
        scratch_shapes=[pltpu.VMEM((2, m_iss, d), F32), pltpu.VMEM((m_tot, d), BF16), pltpu.VMEM((m_tot, d), F32),
                        pltpu.VMEM((m_iss, d), F32), pltpu.SemaphoreType.DMA((2,)), pltpu.SemaphoreType.DMA],
    )
    return pl.pallas_call(
        functools.partial(_moe_kernel, m_tot=m_tot, n_f=n_f, step_rows=step_rows, z_zero_rows=z_zero_rows),
        grid_spec=grid_spec,
        out_shape=jax.ShapeDtypeStruct((z_rows, d), F32),
        compiler_params=_params(("arbitrary", "arbitrary")),
        name="moe",
    )(gidx, qpos, h, gate, w_gate, w_up, w_down)


def _combine_kernel(w0_ref, nc_ref, z_ref, x1_ref, mod_ref, qt_ref, kt_ref, o_ref, zbuf, sem, acc_ref):
    blk = pl.program_id(0)
    w0 = w0_ref[blk]
    qt = qt_ref[...]
    kt = kt_ref[...]
    acc_ref[...] = jnp.zeros_like(acc_ref)

    def chunk(c, carry):
        start = pl.multiple_of(w0 + c * Z_CHUNK, ROW_ALIGN)
        cp = pltpu.make_async_copy(z_ref.at[pl.ds(start, Z_CHUNK)], zbuf, sem)
        cp.start()
        cp.wait()
        r = start + lax.broadcasted_iota(I32, (qt.shape[0], Z_CHUNK), 1)
        onehot = jnp.where((r >= qt) & (r < qt + kt), 1.0, 0.0).astype(BF16)
        acc_ref[...] += _dot(onehot, zbuf[...].astype(BF16))
        return carry

    lax.fori_loop(0, nc_ref[blk], chunk, 0)
    o_ref[...] = x1_ref[...] + mod_ref[0, 5:6, :] * acc_ref[...]


def _combine(w0, nc, z, x1, first_block, mods, qt, kt, cond_of_block):
    n = qt.shape[0]
    d = x1.shape[1]
    tb = TOKEN_BLOCK
    grid_spec = pltpu.PrefetchScalarGridSpec(
        num_scalar_prefetch=2,
        grid=(n // tb,),
        in_specs=[pl.BlockSpec(memory_space=pl.ANY),
                  pl.BlockSpec((tb, d), lambda i, *_: (i + first_block, 0)),
                  pl.BlockSpec((1, N_MOD, d), lambda i, *_: (cond_of_block(i), 0, 0)),
                  pl.BlockSpec((tb, 1), lambda i, *_: (i, 0)),
                  pl.BlockSpec((tb, 1), lambda i, *_: (i, 0))],
        out_specs=pl.BlockSpec((tb, d), lambda i, *_: (i, 0)),
        scratch_shapes=[pltpu.VMEM((Z_CHUNK, d), F32), pltpu.SemaphoreType.DMA, pltpu.VMEM((tb, d), F32)],
    )
    return pl.pallas_call(
        _combine_kernel,
        grid_spec=grid_spec,
        out_shape=jax.ShapeDtypeStruct((n, d), F32),
        compiler_params=_params(("arbitrary",)),
        name="combine",
    )(w0, nc, z, x1, mods, qt, kt)


def _routing_tables(aff, cap, z_off):
    n = aff.shape[0]
    aff_t = aff.T
    sel, rinc = _route(aff_t, cap)
    slots = jnp.arange(cap, dtype=I32)
    idx = jnp.sum((rinc[:, None, :] <= slots[None, :, None]).astype(I32), axis=-1)
    gate = jnp.take_along_axis(aff_t, idx, axis=1)
    k_tok = jnp.sum(sel, axis=0)
    q_tok = jnp.cumsum(k_tok) - k_tok + z_off
    before = jnp.cumsum(sel, axis=0) - sel
    qpos = jnp.take_along_axis(q_tok[None, :] + before, idx, axis=1)
    nblk = n // TOKEN_BLOCK
    q_blk = q_tok[::TOKEN_BLOCK]
    end_blk = q_blk + jnp.sum(k_tok.reshape(nblk, TOKEN_BLOCK), axis=1)
    w0 = (q_blk // ROW_ALIGN) * ROW_ALIGN
    nc = (end_blk - w0 + Z_CHUNK - 1) // Z_CHUNK
    return idx, gate, qpos.astype(I32), w0.astype(I32), nc.astype(I32), q_tok.astype(I32)[:, None], k_tok.astype(I32)[:, None]


def kernel(x_prompt, x_sample, cache_k, cache_v, c, c_ctx, norm1_g, norm2_g, w_ada, b_ada, w_in, q_norm_g, k_norm_g,
           rpb, w_att_proj, w_pool, pool_scale, w_out, w_router, w_gate_e, w_up_e, w_down_e):
    assert w_ada.shape[0] == 1, "single trunk layer"
    nb, ls, d = x_prompt.shape
    db, ll, _ = x_sample.shape
    n_c, n_l = nb * ls, db * ll
    lc = cache_k.shape[2]

    cond = jnp.concatenate([c_ctx[None, :], c, jnp.zeros((16 - 1 - db, d), F32)], axis=0)
    mods = _ada(cond, w_ada[0], b_ada).reshape(16, N_MOD, d)

    w_in_b = w_in[0].astype(BF16)
    w_att_b = w_att_proj[0].astype(BF16)
    w_pool_b = w_pool[0].astype(BF16)
    w_out_b = w_out[0].astype(BF16)

    cond_ctx = lambda i: 0
    lat_blocks = ll // TOKEN_BLOCK
    cond_lat = lambda i: 1 + i // lat_blocks

    xc = x_prompt.reshape(n_c, d)
    xl = x_sample.reshape(n_l, d)
    qc, kc, vc, uc, gac, gbc = _in_proj(xc, mods, norm1_g, w_in_b, cond_ctx)
    ql, kl, vl, ul, gal, gbl = _in_proj(xl, mods, norm1_g, w_in_b, cond_lat)

    seq_c = lambda a: a.reshape(nb, ls, a.shape[-1])
    seq_l = lambda a: a.reshape(db, ll, a.shape[-1])
    att_c, kn_c = _attn_ctx(seq_c(qc), seq_c(kc), seq_c(vc), q_norm_g, k_norm_g)
    bias = _window_bias(rpb[0], ll // GRID_W)
    att_l = _attn_lat(seq_l(ql), seq_l(kl), seq_l(vl), cache_k[:, 0].reshape(db, lc, NA_WIDTH),
                      cache_v[:, 0].reshape(db, lc, NA_WIDTH), bias, q_norm_g, k_norm_g)
    pool_c = _pool(seq_c(uc)).reshape(n_c, POOL_WIDTH)
    pool_l = _pool(seq_l(ul)).reshape(n_l, POOL_WIDTH)

    x1, h2, aff = _merge((xc, att_c.reshape(n_c, NA_WIDTH), pool_c, gac, gbc),
                         (xl, att_l.reshape(n_l, NA_WIDTH), pool_l, gal, gbl),
                         mods, w_att_b, w_pool_b, pool_scale, w_out_b, norm2_g, w_router[0], lat_blocks)

    cap_c = max(1, (CAPACITY_FACTOR * n_c) // N_EXPERTS)
    cap_l = max(1, (CAPACITY_FACTOR * n_l) // N_EXPERTS)
    m_tot = cap_c + cap_l
    m_iss = _moe_step_rows(m_tot, pl.cdiv(D_EXPERT, EXPERT_F_BLOCK)) * pl.cdiv(D_EXPERT, EXPERT_F_BLOCK)
    z_valid = N_EXPERTS * m_tot
    z_zero_rows = Z_CHUNK + ROW_ALIGN
    z_spare = z_valid + z_zero_rows
    z_rows = z_spare + m_iss
    idx_c, gate_c, qpos_c, w0_c, nc_c, qt_c, kt_c = _routing_tables(aff[:n_c], cap_c, 0)
    idx_l, gate_l, qpos_l, w0_l, nc_l, qt_l, kt_l = _routing_tables(aff[n_c:], cap_l, N_EXPERTS * cap_c)
    spare = jnp.broadcast_to(z_spare + jnp.arange(m_iss, dtype=I32), (N_EXPERTS + 1, m_iss))
    gidx = jnp.concatenate([idx_c, idx_l + n_c, jnp.zeros((N_EXPERTS, m_iss - m_tot), I32)], axis=1).reshape(-1)
    qpos = jnp.concatenate([spare[:1], jnp.concatenate([qpos_c, qpos_l, spare[1:, m_tot:]], axis=1)], axis=0).reshape(-1)
    gate = jnp.concatenate([gate_c, gate_l], axis=1)[:, :, None]

    z = _moe(gidx, qpos, h2, gate, jnp.swapaxes(w_gate_e[0], 1, 2), jnp.swapaxes(w_up_e[0], 1, 2), w_down_e[0],
             m_tot, z_rows, z_zero_rows)

    y_c = _combine(w0_c, nc_c, z, x1, 0, mods, qt_c, kt_c, cond_ctx)
    y_l = _combine(w0_l, nc_l, z, x1, n_c // TOKEN_BLOCK, mods, qt_l, kt_l, cond_lat)

    state_k = kn_c.reshape(nb, 1, ls, NA_HEADS, HEAD_DIM)
    state_v = vc.reshape(nb, 1, ls, NA_HEADS, HEAD_DIM)
    return (y_c.reshape(nb, ls, d), y_l.reshape(db, ll, d), state_k, state_v)
```

```python
import functools

import jax
import jax.numpy as jnp
from jax import lax
from jax.experimental import pallas as pl
from jax.experimental.pallas import tpu as pltpu

F32 = jnp.float32
BF16 = jnp.bfloat16
I32 = jnp.int32

D_MODEL = 1024
GRID_W = 64
NA_HEADS = 8
HEAD_DIM = 64
NA_WIDTH = NA_HEADS * HEAD_DIM
WIN_ROWS = 8
WIN_COLS = 16
POOL_WINDOWS = (2, 4, 8, 16)
POOL_GROUP_DIM = 128
POOL_WIDTH = 512
POOL_OUT_DIM = 256
N_EXPERTS = 16
CAPACITY_FACTOR = 2
D_EXPERT = 2752
N_MOD = 6
EPS = 1e-6
NEG_INF = -1e30

TOKEN_BLOCK = 256
LAT_ROW_BLOCK = 4
EXPERT_F_BLOCK = 256
EXPERT_M_BLOCK = 512
Z_CHUNK = 256
ROW_ALIGN = 8
VMEM_LIMIT = 56 * 1024 * 1024


def _dot(a, b):
    return jnp.dot(a, b, preferred_element_type=F32)


def _dot_nt(a, b):
    return lax.dot_general(a, b, (((1,), (1,)), ((), ())), preferred_element_type=F32)


def _dot3(a, b):
    a_hi = a.astype(BF16)
    a_lo = (a - a_hi.astype(F32)).astype(BF16)
    b_hi = b.astype(BF16)
    b_lo = (b - b_hi.astype(F32)).astype(BF16)
    return _dot(a_hi, b_hi) + _dot(a_hi, b_lo) + _dot(a_lo, b_hi)


def _sigmoid(x):
    return 1.0 / (1.0 + jnp.exp(-x))


def _rms(x, g):
    return x * lax.rsqrt(jnp.mean(x * x, axis=-1, keepdims=True) + EPS) * g


def _params(sem):
    return pltpu.CompilerParams(dimension_semantics=sem, vmem_limit_bytes=VMEM_LIMIT)


def _ada_kernel(cond_ref, w_ref, b_ref, o_ref):
    c = cond_ref[...]
    o_ref[...] = _dot3(c * _sigmoid(c), w_ref[...]) + b_ref[...]


def _ada(cond, w_ada, b_ada):
    rows, d = cond.shape
    n_out = w_ada.shape[1]
    bn = 512
    return pl.pallas_call(
        _ada_kernel,
        grid=(n_out // bn,),
        in_specs=[pl.BlockSpec((rows, d), lambda i: (0, 0)),
                  pl.BlockSpec((d, bn), lambda i: (0, i)),
                  pl.BlockSpec((1, bn), lambda i: (0, i))],
        out_specs=pl.BlockSpec((rows, bn), lambda i: (0, i)),
        out_shape=jax.ShapeDtypeStruct((rows, n_out), F32),
        compiler_params=_params(("arbitrary",)),
        name="ada",
    )(cond, w_ada, b_ada)


_IN_CUTS = (0, 512, 1024, 1536, 2048, 3072, 4096)


def _in_kernel(x_ref, mod_ref, g_ref, w_ref, q_ref, k_ref, v_ref, u_ref, ga_ref, gb_ref):
    y = _rms(x_ref[...], g_ref[...])
    h = (y * (1.0 + mod_ref[0, 1:2, :]) + mod_ref[0, 0:1, :]).astype(BF16)
    outs = (q_ref, k_ref, v_ref, u_ref, ga_ref, gb_ref)
    for o_ref, lo, hi in zip(outs, _IN_CUTS[:-1], _IN_CUTS[1:]):
        o_ref[...] = _dot(h, w_ref[:, lo:hi])


def _in_proj(x2d, mods, norm_g, w_in_bf16, cond_of_block):
    n, d = x2d.shape
    widths = [hi - lo for lo, hi in zip(_IN_CUTS[:-1], _IN_CUTS[1:])]
    return pl.pallas_call(
        _in_kernel,
        grid=(n // TOKEN_BLOCK,),
        in_specs=[pl.BlockSpec((TOKEN_BLOCK, d), lambda i: (i, 0)),
                  pl.BlockSpec((1, N_MOD, d), lambda i: (cond_of_block(i), 0, 0)),
                  pl.BlockSpec((1, d), lambda i: (0, 0)),
                  pl.BlockSpec(w_in_bf16.shape, lambda i: (0, 0))],
        out_specs=[pl.BlockSpec((TOKEN_BLOCK, w), lambda i: (i, 0)) for w in widths],
        out_shape=[jax.ShapeDtypeStruct((n, w), F32) for w in widths],
        compiler_params=_params(("arbitrary",)),
        name="in_proj",
    )(x2d, mods, norm_g, w_in_bf16)


def _attn_ctx_kernel(q_ref, k_ref, v_ref, qg_ref, kg_ref, att_ref, kn_ref):
    scale = HEAD_DIM ** -0.5
    for h in range(NA_HEADS):
        sl = slice(h * HEAD_DIM, (h + 1) * HEAD_DIM)
        qn = _rms(q_ref[0, :, sl], qg_ref[...]) * scale
        kn = _rms(k_ref[0, :, sl], kg_ref[...])
        kn_ref[0, :, sl] = kn
        s = _dot_nt(qn.astype(BF16), kn.astype(BF16))
        e = jnp.exp(s - jnp.max(s, axis=-1, keepdims=True))
        p = e / jnp.sum(e, axis=-1, keepdims=True)
        att_ref[0, :, sl] = _dot(p.astype(BF16), v_ref[0, :, sl].astype(BF16))


def _attn_ctx(q, k, v, q_g, k_g):
    b, l, w = q.shape
    blk = pl.BlockSpec((1, l, w), lambda i: (i, 0, 0))
    gspec = pl.BlockSpec((1, HEAD_DIM), lambda i: (0, 0))
    return pl.pallas_call(
        _attn_ctx_kernel,
        grid=(b,),
        in_specs=[blk, blk, blk, gspec, gspec],
        out_specs=[blk, blk],
        out_shape=[jax.ShapeDtypeStruct((b, l, w), F32)] * 2,
        compiler_params=_params(("arbitrary",)),
        name="attn_ctx",
    )(q, k, v, q_g, k_g)


def _lat_windows(rows):
    kr = min(WIN_ROWS, rows)
    rb = min(LAT_ROW_BLOCK, rows)
    wr = min(kr + rb, rows)
    assert rows % rb == 0
    starts = [min(max(i * rb - kr // 2, 0), rows - wr) for i in range(rows // rb)]
    for i, ws in enumerate(starts):
        for r in range(i * rb, (i + 1) * rb):
            rs = min(max(r - kr // 2, 0), rows - kr)
            assert ws <= rs and rs + kr <= ws + wr
    return kr, rb, wr, starts


def _attn_lat_kernel(q_ref, k_ref, v_ref, ck_ref, cv_ref, bias_ref, qg_ref, kg_ref, att_ref,
                     kn_s, vb_s, ckb_s, cvb_s, *, rows, kr, rb, wr):
    i = pl.program_id(1)

    @pl.when(i == 0)
    def _():
        def with_ones(v):
            lane = lax.broadcasted_iota(I32, v.shape, 1)
            return jnp.concatenate([v, jnp.where(lane == 0, 1.0, 0.0)], axis=-1).astype(BF16)

        for h in range(NA_HEADS):
            sl = slice(h * HEAD_DIM, (h + 1) * HEAD_DIM)
            kn_s[h] = _rms(k_ref[0, :, sl], kg_ref[...]).astype(BF16)
            vb_s[h] = with_ones(v_ref[0, :, sl])
            ckb_s[h] = ck_ref[0, :, sl].astype(BF16)
            cvb_s[h] = with_ones(cv_ref[0, :, sl])

    win_start = jnp.clip(i * rb - kr // 2, 0, rows - wr)
    key_rows = pl.ds(pl.multiple_of(win_start * GRID_W, GRID_W), wr * GRID_W)
    scale = HEAD_DIM ** -0.5
    for h in range(NA_HEADS):
        sl = slice(h * HEAD_DIM, (h + 1) * HEAD_DIM)
        qn = (_rms(q_ref[0, :, sl], qg_ref[...]) * scale).astype(BF16)
        s_loc = _dot_nt(qn, kn_s[h, key_rows, :]) + bias_ref[h].reshape(rb * GRID_W, wr * GRID_W)
        s_ctx = _dot_nt(qn, ckb_s[h])
        m = jnp.maximum(jnp.max(s_loc, axis=-1, keepdims=True), jnp.max(s_ctx, axis=-1, keepdims=True))
        o = (_dot(jnp.exp(s_loc - m).astype(BF16), vb_s[h, key_rows, :])
             + _dot(jnp.exp(s_ctx - m).astype(BF16), cvb_s[h]))
        att_ref[0, :, sl] = o[:, :HEAD_DIM] / o[:, HEAD_DIM:HEAD_DIM + 1]


def _attn_lat(q, k, v, ck, cv, bias, q_g, k_g):
    b, l, w = q.shape
    lc = ck.shape[1]
    rows = l // GRID_W
    kr, rb, wr, _ = _lat_windows(rows)
    seq = pl.BlockSpec((1, l, w), lambda i, r: (i, 0, 0))
    ctx = pl.BlockSpec((1, lc, w), lambda i, r: (i, 0, 0))
    rowblk = pl.BlockSpec((1, rb * GRID_W, w), lambda i, r: (i, r, 0))
    gspec = pl.BlockSpec((1, HEAD_DIM), lambda i, r: (0, 0))
    return pl.pallas_call(
        functools.partial(_attn_lat_kernel, rows=rows, kr=kr, rb=rb, wr=wr),
        grid=(b, rows // rb),
        in_specs=[rowblk, seq, seq, ctx, ctx,
                  pl.BlockSpec((NA_HEADS, rb, GRID_W, wr * GRID_W), lambda i, r: (0, r, 0, 0)), gspec, gspec],
        out_specs=rowblk,
        out_shape=jax.ShapeDtypeStruct((b, l, w), F32),
        scratch_shapes=[pltpu.VMEM((NA_HEADS, l, HEAD_DIM), BF16), pltpu.VMEM((NA_HEADS, l, 2 * HEAD_DIM), BF16),
                        pltpu.VMEM((NA_HEADS, lc, HEAD_DIM), BF16), pltpu.VMEM((NA_HEADS, lc, 2 * HEAD_DIM), BF16)],
        compiler_params=_params(("arbitrary", "arbitrary")),
        name="attn_lat",
    )(q, k, v, ck, cv, bias, q_g, k_g)


def _bias_kernel(rpb_ref, o_ref, *, rows):
    kr, rb, wr, starts = _lat_windows(rows)
    h = pl.program_id(0)
    n_ro, n_co = 2 * WIN_ROWS - 1, 2 * WIN_COLS - 1
    wq = lax.broadcasted_iota(I32, (GRID_W, GRID_W), 0)
    wk = lax.broadcasted_iota(I32, (GRID_W, GRID_W), 1)
    co = jnp.clip(wk - wq, -(WIN_COLS - 1), WIN_COLS - 1) + (WIN_COLS - 1)
    col_start = jnp.clip(wq - WIN_COLS // 2, 0, GRID_W - WIN_COLS)
    in_win = (wk >= col_start) & (wk < col_start + WIN_COLS)
    co = jnp.where(in_win, co, -1)
    masked = jnp.full((GRID_W, GRID_W), NEG_INF, F32)
    tables = {}
    for r in range(rows):
        row_start = min(max(r - kr // 2, 0), rows - kr)
        for j in range(wr):
            key_row = starts[r // rb] + j
            blk = masked
            if row_start <= key_row < row_start + kr:
                ro = key_row - r + (WIN_ROWS - 1)
                if ro not in tables:
                    t = masked
                    for c in range(n_co):
                        t = jnp.where(co == c, rpb_ref[(h * n_ro + ro) * n_co + c], t)
                    tables[ro] = t
                blk = tables[ro]
            o_ref[0, r, :, j * GRID_W:(j + 1) * GRID_W] = blk


def _window_bias(rpb, rows):
    _, _, wr, _ = _lat_windows(rows)
    nh = rpb.shape[0]
    return pl.pallas_call(
        functools.partial(_bias_kernel, rows=rows),
        grid=(nh,),
        in_specs=[pl.BlockSpec(memory_space=pltpu.SMEM)],
        out_specs=pl.BlockSpec((1, rows, GRID_W, wr * GRID_W), lambda h: (h, 0, 0, 0)),
        out_shape=jax.ShapeDtypeStruct((nh, rows, GRID_W, wr * GRID_W), F32),
        compiler_params=_params(("arbitrary",)),
        name="window_bias",
    )(rpb.reshape(-1))


def _pool_kernel(u_ref, o_ref):
    l = u_ref.shape[1]
    t = lax.broadcasted_iota(I32, (l, POOL_GROUP_DIM), 0)
    for g, w in enumerate(POOL_WINDOWS):
        sl = slice(g * POOL_GROUP_DIM, (g + 1) * POOL_GROUP_DIM)
        x = u_ref[0, :, sl]
        acc = x
        for k in range(-(w // 2), w - w // 2):
            if k == 0:
                continue
            shifted = pltpu.roll(x, (-k) % l, 0)
            acc = acc + jnp.where((t + k >= 0) & (t + k < l), shifted, 0.0)
        cnt = (jnp.minimum(t + (w - w // 2), l) - jnp.maximum(t - w // 2, 0)).astype(F32)
        o_ref[0, :, sl] = acc / cnt - x


def _pool(u):
    b, l, w = u.shape
    blk = pl.BlockSpec((1, l, w), lambda i: (i, 0, 0))
    return pl.pallas_call(
        _pool_kernel, grid=(b,), in_specs=[blk], out_specs=blk,
        out_shape=jax.ShapeDtypeStruct((b, l, w), F32),
        compiler_params=_params(("arbitrary",)),
        name="pool",
    )(u)


def _merge_kernel(xc_ref, xl_ref, attc_ref, attl_ref, plc_ref, pll_ref, gac_ref, gal_ref, gbc_ref, gbl_ref,
                  mod_ref, watt_ref, wpool_ref, ps_ref, wout_ref, g2_ref, wr_ref, x1_ref, h2_ref, aff_ref, *, ctx_blocks):
    is_ctx = pl.program_id(0) < ctx_blocks
    pick = lambda c_ref, l_ref: jnp.where(is_ctx, c_ref[...], l_ref[...])
    o_a = _dot(pick(attc_ref, attl_ref).astype(BF16), watt_ref[...])
    pooled = pick(plc_ref, pll_ref)
    o_b = jnp.concatenate(
        [_dot(pooled[:, g * POOL_GROUP_DIM:(g + 1) * POOL_GROUP_DIM].astype(BF16), wpool_ref[g])
         for g in range(len(POOL_WINDOWS))], axis=-1) * ps_ref[...]
    merged = _sigmoid(pick(gac_ref, gal_ref)) * o_a + _sigmoid(pick(gbc_ref, gbl_ref)) * o_b
    x1 = pick(xc_ref, xl_ref) + mod_ref[0, 2:3, :] * _dot(merged.astype(BF16), wout_ref[...])
    x1_ref[...] = x1
    h2 = _rms(x1, g2_ref[...]) * (1.0 + mod_ref[0, 4:5, :]) + mod_ref[0, 3:4, :]
    h2_ref[...] = h2
    logits = _dot3(h2, wr_ref[...])
    e = jnp.exp(logits - jnp.max(logits, axis=-1, keepdims=True))
    aff_ref[...] = e / jnp.sum(e, axis=-1, keepdims=True)


def _merge(ctx, lat, mods, w_att, w_pool, pool_scale, w_out, norm2_g, w_router, lat_blocks_per_seq):
    n_c, d = ctx[0].shape
    n = n_c + lat[0].shape[0]
    tb = TOKEN_BLOCK
    cb = n_c // tb
    crow = lambda w: pl.BlockSpec((tb, w), lambda i: (jnp.minimum(i, cb - 1), 0))
    lrow = lambda w: pl.BlockSpec((tb, w), lambda i: (jnp.maximum(i - cb, 0), 0))
    orow = lambda w: pl.BlockSpec((tb, w), lambda i: (i, 0))
    full = lambda a: pl.BlockSpec(a.shape, lambda i: (0,) * a.ndim)
    cond = lambda i: jnp.where(i < cb, 0, 1 + (i - cb) // lat_blocks_per_seq)
    pairs, specs = [], []
    for a_c, a_l in zip(ctx, lat):
        pairs += [a_c, a_l]
        specs += [crow(a_c.shape[1]), lrow(a_l.shape[1])]
    return pl.pallas_call(
        functools.partial(_merge_kernel, ctx_blocks=cb),
        grid=(n // tb,),
        in_specs=specs + [pl.BlockSpec((1, N_MOD, d), lambda i: (cond(i), 0, 0)),
                          full(w_att), full(w_pool), full(pool_scale), full(w_out), full(norm2_g), full(w_router)],
        out_specs=[orow(d), orow(d), orow(N_EXPERTS)],
        out_shape=[jax.ShapeDtypeStruct((n, d), F32), jax.ShapeDtypeStruct((n, d), F32),
                   jax.ShapeDtypeStruct((n, N_EXPERTS), F32)],
        compiler_params=_params(("arbitrary",)),
        name="merge",
    )(*pairs, mods, w_att, w_pool, pool_scale, w_out, norm2_g, w_router)


def _prefix_incl(mask_f32, out_ref):
    e, n = mask_f32.shape
    i = lax.broadcasted_iota(I32, (128, 128), 0)
    j = lax.broadcasted_iota(I32, (128, 128), 1)
    tri = jnp.where(i <= j, 1.0, 0.0).astype(BF16)
    carry = jnp.zeros((e, 1), F32)
    for c in range(n // 128):
        inc = _dot(mask_f32[:, c * 128:(c + 1) * 128].astype(BF16), tri) + carry
        out_ref[:, c * 128:(c + 1) * 128] = inc.astype(I32)
        carry = inc[:, 127:128]


def _route_kernel(aff_ref, sel_ref, rinc_ref, tmp_ref, *, cap):
    aff = aff_ref[...]

    def bit_step(i, thr):
        cand = thr | jnp.left_shift(jnp.int32(1), 30 - i)
        cnt = jnp.sum(jnp.where(aff >= pltpu.bitcast(cand, F32), 1, 0), axis=1, keepdims=True)
        return jnp.where(cnt >= cap, cand, thr)

    thr = lax.fori_loop(0, 31, bit_step, jnp.zeros((aff.shape[0], 1), I32))
    gt = aff >= pltpu.bitcast(thr + 1, F32)
    eq = (aff >= pltpu.bitcast(thr, F32)) & jnp.logical_not(gt)
    need = cap - jnp.sum(jnp.where(gt, 1, 0), axis=1, keepdims=True)
    _prefix_incl(jnp.where(eq, 1.0, 0.0), tmp_ref)
    sel = gt | (eq & (tmp_ref[...] <= need))
    sel_ref[...] = jnp.where(sel, 1, 0)
    _prefix_incl(jnp.where(sel, 1.0, 0.0), rinc_ref)


def _route(aff_t, cap):
    e, n = aff_t.shape
    full = pl.BlockSpec((e, n), lambda: (0, 0))
    return pl.pallas_call(
        functools.partial(_route_kernel, cap=cap),
        in_specs=[full], out_specs=[full, full],
        out_shape=[jax.ShapeDtypeStruct((e, n), I32)] * 2,
        scratch_shapes=[pltpu.VMEM((e, n), I32)],
        compiler_params=pltpu.CompilerParams(vmem_limit_bytes=VMEM_LIMIT),
        name="route",
    )(aff_t)


def _moe_kernel(gidx_ref, qpos_ref, h_ref, gate_ref, wg_ref, wu_ref, wd_ref, z_ref,
                xf_ref, xb_ref, acc_ref, y_ref, sem_g, sem_s, *, m_tot, n_f, step_rows, z_zero_rows):
    e = pl.program_id(0)
    j = pl.program_id(1)
    n_e = pl.num_programs(0)
    m_iss = step_rows * n_f
    tf = wg_ref.shape[1]
    slot = e % 2

    def gather_row(lst, s, dst_slot):
        t = gidx_ref[lst * m_iss + s]
        return pltpu.make_async_copy(h_ref.at[pl.ds(t, 1)], xf_ref.at[dst_slot, pl.ds(s, 1)], sem_g.at[dst_slot])

    def scatter_row(lst, s):
        q = qpos_ref[lst * m_iss + s]
        return pltpu.make_async_copy(y_ref.at[pl.ds(s, 1)], z_ref.at[pl.ds(q, 1)], sem_s)

    def wait_gathers(dst_slot):
        pltpu.make_async_copy(h_ref.at[pl.ds(0, m_iss)], xf_ref.at[dst_slot], sem_g.at[dst_slot]).wait()

    def wait_scatters():
        pltpu.make_async_copy(y_ref, z_ref.at[pl.ds(0, m_iss)], sem_s).wait()

    @pl.when((e == 0) & (j == 0))
    def _():
        y_ref[...] = jnp.zeros_like(y_ref)
        cp = pltpu.make_async_copy(y_ref.at[pl.ds(0, z_zero_rows)], z_ref.at[pl.ds(n_e * m_tot, z_zero_rows)], sem_s)
        cp.start()
        cp.wait()

        def first(s, c):
            gather_row(0, s, 0).start()
            return c

        lax.fori_loop(0, m_iss, first, 0)

    @pl.when(j == 0)
    def _():
        wait_gathers(slot)
        xb_ref[...] = xf_ref[slot, 0:m_tot, :].astype(BF16)
        acc_ref[...] = jnp.zeros_like(acc_ref)

    nxt = lax.rem(e + 1, n_e)
    for u in range(step_rows):
        s = j * step_rows + u
        gather_row(nxt, s, 1 - slot).start()
        scatter_row(e, s).start()

    valid = D_EXPERT - j * tf
    cmask = lax.broadcasted_iota(I32, (1, tf), 1) < valid
    rmask = lax.broadcasted_iota(I32, (tf, 1), 0) < valid
    wg = wg_ref[0].astype(BF16)
    wu = wu_ref[0].astype(BF16)
    wd = jnp.where(rmask, wd_ref[0], 0.0).astype(BF16)
    for m in range(m_tot // EXPERT_M_BLOCK):
        rs = slice(m * EXPERT_M_BLOCK, (m + 1) * EXPERT_M_BLOCK)
        xs = xb_ref[rs, :]
        a = _dot_nt(xs, wg)
        b = _dot_nt(xs, wu)
        hm = jnp.where(cmask, a * _sigmoid(a) * b, 0.0).astype(BF16)
        acc_ref[rs, :] += _dot(hm, wd)

    @pl.when(j == n_f - 1)
    def _():
        wait_scatters()
        y_ref[0:m_tot, :] = acc_ref[...] * gate_ref[0]

        @pl.when(e == n_e - 1)
        def _():
            def last(s, c):
                scatter_row(n_e, s).start()
                return c

            lax.fori_loop(0, m_iss, last, 0)
            wait_scatters()
            wait_gathers(1 - slot)


def _moe_step_rows(m_tot, n_f):
    return pl.cdiv(pl.cdiv(m_tot, n_f), ROW_ALIGN) * ROW_ALIGN


def _moe(gidx, qpos, h, gate, w_gate, w_up, w_down, m_tot, z_rows, z_zero_rows):
    d = h.shape[1]
    n_e = w_down.shape[0]
    tf = EXPERT_F_BLOCK
    n_f = pl.cdiv(D_EXPERT, tf)
    step_rows = _moe_step_rows(m_tot, n_f)
    m_iss = step_rows * n_f
    grid_spec = pltpu.PrefetchScalarGridSpec(
        num_scalar_prefetch=2,
        grid=(n_e, n_f),
        in_specs=[pl.BlockSpec(memory_space=pl.ANY),
                  pl.BlockSpec((1, m_tot, 1), lambda e, j, *_: (e, 0, 0)),
                  pl.BlockSpec((1, tf, d), lambda e, j, *_: (e, j, 0)),
                  pl.BlockSpec((1, tf, d), lambda e, j, *_: (e, j, 0)),
                  pl.BlockSpec((1, tf, d), lambda e, j, *_: (e, j, 0))],
        out_specs=pl.BlockSpec(memory_space=pl.ANY),
        scratch_shapes=[pltpu.VMEM((2, m_iss, d), F32), pltpu.VMEM((m_tot, d), BF16), pltpu.VMEM((m_tot, d), F32),
                        pltpu.VMEM((m_iss, d), F32), pltpu.SemaphoreType.DMA((2,)), pltpu.SemaphoreType.DMA],
    )
    return pl.pallas_call(
        functools.partial(_moe_kernel, m_tot=m_tot, n_f=n_f, step_rows=step_rows, z_zero_rows=z_zero_rows),
        grid_spec=grid_spec,
        out_shape=jax.ShapeDtypeStruct((z_rows, d), F32),
        compiler_params=_params(("arbitrary", "arbitrary")),
        name="moe",
    )(gidx, qpos, h, gate, w_gate, w_up, w_down)


def _combine_kernel(w0_ref, nc_ref, z_ref, x1_ref, mod_ref, qt_ref, kt_ref, o_ref, zbuf, sem, acc_ref):
    blk = pl.program_id(0)
    w0 = w0_ref[blk]
    qt = qt_ref[...]
    kt = kt_ref[...]
    acc_ref[...] = jnp.zeros_like(acc_ref)

    n_chunks = nc_ref[blk]

    def z_copy(c, slot):
        start = pl.multiple_of(w0 + c * Z_CHUNK, ROW_ALIGN)
        return pltpu.make_async_copy(z_ref.at[pl.ds(start, Z_CHUNK)], zbuf.at[slot], sem.at[slot])

    @pl.when(n_chunks > 0)
    def _():
        z_copy(0, 0).start()

    def chunk(c, carry):
        slot = c % 2

        @pl.when(c + 1 < n_chunks)
        def _():
            z_copy(c + 1, 1 - slot).start()

        z_copy(c, slot).wait()
        r = w0 + c * Z_CHUNK + lax.broadcasted_iota(I32, (qt.shape[0], Z_CHUNK), 1)
        onehot = jnp.where((r >= qt) & (r < qt + kt), 1.0, 0.0).astype(BF16)
        acc_ref[...] += _dot(onehot, zbuf[slot].astype(BF16))
        return carry

    lax.fori_loop(0, n_chunks, chunk, 0)
    o_ref[...] = x1_ref[...] + mod_ref[0, 5:6, :] * acc_ref[...]


def _combine(w0, nc, z, x1, first_block, mods, qt, kt, cond_of_block):
    n = qt.shape[0]
    d = x1.shape[1]
    tb = TOKEN_BLOCK
    grid_spec = pltpu.PrefetchScalarGridSpec(
        num_scalar_prefetch=2,
        grid=(n // tb,),
        in_specs=[pl.BlockSpec(memory_space=pl.ANY),
                  pl.BlockSpec((tb, d), lambda i, *_: (i + first_block, 0)),
                  pl.BlockSpec((1, N_MOD, d), lambda i, *_: (cond_of_block(i), 0, 0)),
                  pl.BlockSpec((tb, 1), lambda i, *_: (i, 0)),
                  pl.BlockSpec((tb, 1), lambda i, *_: (i, 0))],
        out_specs=pl.BlockSpec((tb, d), lambda i, *_: (i, 0)),
        scratch_shapes=[pltpu.VMEM((2, Z_CHUNK, d), F32), pltpu.SemaphoreType.DMA((2,)), pltpu.VMEM((tb, d), F32)],
    )
    return pl.pallas_call(
        _combine_kernel,
        grid_spec=grid_spec,
        out_shape=jax.ShapeDtypeStruct((n, d), F32),
        compiler_params=_params(("arbitrary",)),
        name="combine",
    )(w0, nc, z, x1, mods, qt, kt)


def _routing_tables(aff, cap, z_off):
    n = aff.shape[0]
    aff_t = aff.T
    sel, rinc = _route(aff_t, cap)
    slots = jnp.arange(cap, dtype=I32)
    idx = jnp.sum((rinc[:, None, :] <= slots[None, :, None]).astype(I32), axis=-1)
    gate = jnp.take_along_axis(aff_t, idx, axis=1)
    k_tok = jnp.sum(sel, axis=0)
    q_tok = jnp.cumsum(k_tok) - k_tok + z_off
    before = jnp.cumsum(sel, axis=0) - sel
    qpos = jnp.take_along_axis(q_tok[None, :] + before, idx, axis=1)
    nblk = n // TOKEN_BLOCK
    q_blk = q_tok[::TOKEN_BLOCK]
    end_blk = q_blk + jnp.sum(k_tok.reshape(nblk, TOKEN_BLOCK), axis=1)
    w0 = (q_blk // ROW_ALIGN) * ROW_ALIGN
    nc = (end_blk - w0 + Z_CHUNK - 1) // Z_CHUNK
    return idx, gate, qpos.astype(I32), w0.astype(I32), nc.astype(I32), q_tok.astype(I32)[:, None], k_tok.astype(I32)[:, None]


def kernel(x_prompt, x_sample, cache_k, cache_v, c, c_ctx, norm1_g, norm2_g, w_ada, b_ada, w_in, q_norm_g, k_norm_g,
           rpb, w_att_proj, w_pool, pool_scale, w_out, w_router, w_gate_e, w_up_e, w_down_e):
    assert w_ada.shape[0] == 1, "single trunk layer"
    nb, ls, d = x_prompt.shape
    db, ll, _ = x_sample.shape
    n_c, n_l = nb * ls, db * ll
    lc = cache_k.shape[2]

    cond = jnp.concatenate([c_ctx[None, :], c, jnp.zeros((16 - 1 - db, d), F32)], axis=0)
    mods = _ada(cond, w_ada[0], b_ada).reshape(16, N_MOD, d)

    w_in_b = w_in[0].astype(BF16)
    w_att_b = w_att_proj[0].astype(BF16)
    w_pool_b = w_pool[0].astype(BF16)
    w_out_b = w_out[0].astype(BF16)

    cond_ctx = lambda i: 0
    lat_blocks = ll // TOKEN_BLOCK
    cond_lat = lambda i: 1 + i // lat_blocks

    xc = x_prompt.reshape(n_c, d)
    xl = x_sample.reshape(n_l, d)
    qc, kc, vc, uc, gac, gbc = _in_proj(xc, mods, norm1_g, w_in_b, cond_ctx)
    ql, kl, vl, ul, gal, gbl = _in_proj(xl, mods, norm1_g, w_in_b, cond_lat)

    seq_c = lambda a: a.reshape(nb, ls, a.shape[-1])
    seq_l = lambda a: a.reshape(db, ll, a.shape[-1])
    att_c, kn_c = _attn_ctx(seq_c(qc), seq_c(kc), seq_c(vc), q_norm_g, k_norm_g)
    bias = _window_bias(rpb[0], ll // GRID_W)
    att_l = _attn_lat(seq_l(ql), seq_l(kl), seq_l(vl), cache_k[:, 0].reshape(db, lc, NA_WIDTH),
                      cache_v[:, 0].reshape(db, lc, NA_WIDTH), bias, q_norm_g, k_norm_g)
    pool_c = _pool(seq_c(uc)).reshape(n_c, POOL_WIDTH)
    pool_l = _pool(seq_l(ul)).reshape(n_l, POOL_WIDTH)

    x1, h2, aff = _merge((xc, att_c.reshape(n_c, NA_WIDTH), pool_c, gac, gbc),
                         (xl, att_l.reshape(n_l, NA_WIDTH), pool_l, gal, gbl),
                         mods, w_att_b, w_pool_b, pool_scale, w_out_b, norm2_g, w_router[0], lat_blocks)

    cap_c = max(1, (CAPACITY_FACTOR * n_c) // N_EXPERTS)
    cap_l = max(1, (CAPACITY_FACTOR * n_l) // N_EXPERTS)
    m_tot = cap_c + cap_l
    m_iss = _moe_step_rows(m_tot, pl.cdiv(D_EXPERT, EXPERT_F_BLOCK)) * pl.cdiv(D_EXPERT, EXPERT_F_BLOCK)
    z_valid = N_EXPERTS * m_tot
    z_zero_rows = Z_CHUNK + ROW_ALIGN
    z_spare = z_valid + z_zero_rows
    z_rows = z_spare + m_iss
    idx_c, gate_c, qpos_c, w0_c, nc_c, qt_c, kt_c = _routing_tables(aff[:n_c], cap_c, 0)
    idx_l, gate_l, qpos_l, w0_l, nc_l, qt_l, kt_l = _routing_tables(aff[n_c:], cap_l, N_EXPERTS * cap_c)
    spare = jnp.broadcast_to(z_spare + jnp.arange(m_iss, dtype=I32), (N_EXPERTS + 1, m_iss))
    gidx = jnp.concatenate([idx_c, idx_l + n_c, jnp.zeros((N_EXPERTS, m_iss - m_tot), I32)], axis=1).reshape(-1)
    qpos = jnp.concatenate([spare[:1], jnp.concatenate([qpos_c, qpos_l, spare[1:, m_tot:]], axis=1)], axis=0).reshape(-1)
    gate = jnp.concatenate([gate_c, gate_l], axis=1)[:, :, None]

    z = _moe(gidx, qpos, h2, gate, jnp.swapaxes(w_gate_e[0], 1, 2), jnp.swapaxes(w_up_e[0], 1, 2), w_down_e[0],
             m_tot, z_rows, z_zero_rows)

    y_c = _combine(w0_c, nc_c, z, x1, 0, mods, qt_c, kt_c, cond_ctx)
    y_l = _combine(w0_l, nc_l, z, x1, n_c // TOKEN_BLOCK, mods, qt_l, kt_l, cond_lat)

    state_k = kn_c.reshape(nb, 1, ls, NA_HEADS, HEAD_DIM)
    state_v = vc.reshape(nb, 1, ls, NA_HEADS, HEAD_DIM)
    return (y_c.reshape(nb, ls, d), y_l.reshape(db, ll, d), state_k, state_v)
```

```python
import functools

import jax
import jax.numpy as jnp
from jax import lax
from jax.experimental import pallas as pl
from jax.experimental.pallas import tpu as pltpu

F32 = jnp.float32
BF16 = jnp.bfloat16
I32 = jnp.int32

D_MODEL = 1024
GRID_W = 64
NA_HEADS = 8
HEAD_DIM = 64
NA_WIDTH = NA_HEADS * HEAD_DIM
WIN_ROWS = 8
WIN_COLS = 16
POOL_WINDOWS = (2, 4, 8, 16)
POOL_GROUP_DIM = 128
POOL_WIDTH = 512
POOL_OUT_DIM = 256
N_EXPERTS = 16
CAPACITY_FACTOR = 2
D_EXPERT = 2752
N_MOD = 6
EPS = 1e-6
NEG_INF = -1e30

TOKEN_BLOCK = 256
LAT_ROW_BLOCK = 4
EXPERT_F_BLOCK = 256
EXPERT_M_BLOCK = 512
Z_CHUNK = 256
ROW_ALIGN = 8
VMEM_LIMIT = 56 * 1024 * 1024


def _dot(a, b):
    return jnp.dot(a, b, preferred_element_type=F32)


def _dot_nt(a, b):
    return lax.dot_general(a, b, (((1,), (1,)), ((), ())), preferred_element_type=F32)


def _dot3(a, b):
    a_hi = a.astype(BF16)
    a_lo = (a - a_hi.astype(F32)).astype(BF16)
    b_hi = b.astype(BF16)
    b_lo = (b - b_hi.astype(F32)).astype(BF16)
    return _dot(a_hi, b_hi) + _dot(a_hi, b_lo) + _dot(a_lo, b_hi)


def _sigmoid(x):
    return 1.0 / (1.0 + jnp.exp(-x))


def _rms(x, g):
    return x * lax.rsqrt(jnp.mean(x * x, axis=-1, keepdims=True) + EPS) * g


def _params(sem):
    return pltpu.CompilerParams(dimension_semantics=sem, vmem_limit_bytes=VMEM_LIMIT)


def _ada_kernel(cond_ref, w_ref, b_ref, o_ref):
    c = cond_ref[...]
    o_ref[...] = _dot3(c * _sigmoid(c), w_ref[...]) + b_ref[...]


def _ada(cond, w_ada, b_ada):
    rows, d = cond.shape
    n_out = w_ada.shape[1]
    bn = 512
    return pl.pallas_call(
        _ada_kernel,
        grid=(n_out // bn,),
        in_specs=[pl.BlockSpec((rows, d), lambda i: (0, 0)),
                  pl.BlockSpec((d, bn), lambda i: (0, i)),
                  pl.BlockSpec((1, bn), lambda i: (0, i))],
        out_specs=pl.BlockSpec((rows, bn), lambda i: (0, i)),
        out_shape=jax.ShapeDtypeStruct((rows, n_out), F32),
        compiler_params=_params(("arbitrary",)),
        name="ada",
    )(cond, w_ada, b_ada)


_IN_CUTS = (0, 512, 1024, 1536, 2048, 3072, 4096)


def _in_kernel(x_ref, mod_ref, g_ref, w_ref, q_ref, k_ref, v_ref, u_ref, ga_ref, gb_ref):
    y = _rms(x_ref[...], g_ref[...])
    h = (y * (1.0 + mod_ref[0, 1:2, :]) + mod_ref[0, 0:1, :]).astype(BF16)
    outs = (q_ref, k_ref, v_ref, u_ref, ga_ref, gb_ref)
    for o_ref, lo, hi in zip(outs, _IN_CUTS[:-1], _IN_CUTS[1:]):
        o_ref[...] = _dot(h, w_ref[:, lo:hi])


def _in_proj(x2d, mods, norm_g, w_in_bf16, cond_of_block):
    n, d = x2d.shape
    widths = [hi - lo for lo, hi in zip(_IN_CUTS[:-1], _IN_CUTS[1:])]
    return pl.pallas_call(
        _in_kernel,
        grid=(n // TOKEN_BLOCK,),
        in_specs=[pl.BlockSpec((TOKEN_BLOCK, d), lambda i: (i, 0)),
                  pl.BlockSpec((1, N_MOD, d), lambda i: (cond_of_block(i), 0, 0)),
                  pl.BlockSpec((1, d), lambda i: (0, 0)),
                  pl.BlockSpec(w_in_bf16.shape, lambda i: (0, 0))],
        out_specs=[pl.BlockSpec((TOKEN_BLOCK, w), lambda i: (i, 0)) for w in widths],
        out_shape=[jax.ShapeDtypeStruct((n, w), F32) for w in widths],
        compiler_params=_params(("arbitrary",)),
        name="in_proj",
    )(x2d, mods, norm_g, w_in_bf16)


def _head_rms(x, g):
    w = x.shape[1]
    gi = lax.broadcasted_iota(I32, (w, w), 0) // HEAD_DIM
    gj = lax.broadcasted_iota(I32, (w, w), 1) // HEAD_DIM
    avg = jnp.where(gi == gj, 1.0 / HEAD_DIM, 0.0).astype(BF16)
    x2 = x * x
    hi = x2.astype(BF16)
    lo = (x2 - hi.astype(F32)).astype(BF16)
    return x * lax.rsqrt(_dot(hi, avg) + _dot(lo, avg) + EPS) * g


def _with_ones(v):
    lane = lax.broadcasted_iota(I32, v.shape, 1)
    return jnp.concatenate([v, jnp.where(lane == 0, 1.0, 0.0)], axis=-1).astype(BF16)


def _attn_ctx_kernel(q_ref, k_ref, v_ref, qg_ref, kg_ref, att_ref, kn_ref):
    qn = (_head_rms(q_ref[0], qg_ref[...]) * HEAD_DIM ** -0.5).astype(BF16)
    kn = _head_rms(k_ref[0], kg_ref[...])
    kn_ref[0] = kn
    kn = kn.astype(BF16)
    for h in range(NA_HEADS):
        sl = slice(h * HEAD_DIM, (h + 1) * HEAD_DIM)
        s = _dot_nt(qn[:, sl], kn[:, sl])
        e = jnp.exp(s - jnp.max(s, axis=-1, keepdims=True)).astype(BF16)
        o = _dot(e, _with_ones(v_ref[0, :, sl]))
        att_ref[0, :, sl] = o[:, :HEAD_DIM] / o[:, HEAD_DIM:HEAD_DIM + 1]


def _attn_ctx(q, k, v, q_g, k_g):
    b, l, w = q.shape
    blk = pl.BlockSpec((1, l, w), lambda i: (i, 0, 0))
    gspec = pl.BlockSpec((1, w), lambda i: (0, 0))
    return pl.pallas_call(
        _attn_ctx_kernel,
        grid=(b,),
        in_specs=[blk, blk, blk, gspec, gspec],
        out_specs=[blk, blk],
        out_shape=[jax.ShapeDtypeStruct((b, l, w), F32)] * 2,
        compiler_params=_params(("arbitrary",)),
        name="attn_ctx",
    )(q, k, v, q_g, k_g)


def _lat_windows(rows):
    kr = min(WIN_ROWS, rows)
    rb = min(LAT_ROW_BLOCK, rows)
    wr = min(kr + rb, rows)
    assert rows % rb == 0
    starts = [min(max(i * rb - kr // 2, 0), rows - wr) for i in range(rows // rb)]
    for i, ws in enumerate(starts):
        for r in range(i * rb, (i + 1) * rb):
            rs = min(max(r - kr // 2, 0), rows - kr)
            assert ws <= rs and rs + kr <= ws + wr
    return kr, rb, wr, starts


def _attn_lat_kernel(q_ref, k_ref, v_ref, ck_ref, cv_ref, bias_ref, qg_ref, kg_ref, att_ref,
                     kn_s, vb_s, ckb_s, cvb_s, *, rows, kr, rb, wr):
    i = pl.program_id(1)

    @pl.when(i == 0)
    def _():
        kn = _head_rms(k_ref[0], kg_ref[...]).astype(BF16)
        for h in range(NA_HEADS):
            sl = slice(h * HEAD_DIM, (h + 1) * HEAD_DIM)
            kn_s[h] = kn[:, sl]
            vb_s[h] = _with_ones(v_ref[0, :, sl])
            ckb_s[h] = ck_ref[0, :, sl].astype(BF16)
            cvb_s[h] = _with_ones(cv_ref[0, :, sl])

    win_start = jnp.clip(i * rb - kr // 2, 0, rows - wr)
    key_rows = pl.ds(pl.multiple_of(win_start * GRID_W, GRID_W), wr * GRID_W)
    qn_all = (_head_rms(q_ref[0], qg_ref[...]) * HEAD_DIM ** -0.5).astype(BF16)
    for h in range(NA_HEADS):
        sl = slice(h * HEAD_DIM, (h + 1) * HEAD_DIM)
        qn = qn_all[:, sl]
        s_loc = _dot_nt(qn, kn_s[h, key_rows, :]) + bias_ref[h].reshape(rb * GRID_W, wr * GRID_W)
        s_ctx = _dot_nt(qn, ckb_s[h])
        m = jnp.maximum(jnp.max(s_loc, axis=-1, keepdims=True), jnp.max(s_ctx, axis=-1, keepdims=True))
        o = (_dot(jnp.exp(s_loc - m).astype(BF16), vb_s[h, key_rows, :])
             + _dot(jnp.exp(s_ctx - m).astype(BF16), cvb_s[h]))
        att_ref[0, :, sl] = o[:, :HEAD_DIM] / o[:, HEAD_DIM:HEAD_DIM + 1]


def _attn_lat(q, k, v, ck, cv, bias, q_g, k_g):
    b, l, w = q.shape
    lc = ck.shape[1]
    rows = l // GRID_W
    kr, rb, wr, _ = _lat_windows(rows)
    seq = pl.BlockSpec((1, l, w), lambda i, r: (i, 0, 0))
    ctx = pl.BlockSpec((1, lc, w), lambda i, r: (i, 0, 0))
    rowblk = pl.BlockSpec((1, rb * GRID_W, w), lambda i, r: (i, r, 0))
    gspec = pl.BlockSpec((1, w), lambda i, r: (0, 0))
    return pl.pallas_call(
        functools.partial(_attn_lat_kernel, rows=rows, kr=kr, rb=rb, wr=wr),
        grid=(b, rows // rb),
        in_specs=[rowblk, seq, seq, ctx, ctx,
                  pl.BlockSpec((NA_HEADS, rb, GRID_W, wr * GRID_W), lambda i, r: (0, r, 0, 0)), gspec, gspec],
        out_specs=rowblk,
        out_shape=jax.ShapeDtypeStruct((b, l, w), F32),
        scratch_shapes=[pltpu.VMEM((NA_HEADS, l, HEAD_DIM), BF16), pltpu.VMEM((NA_HEADS, l, 2 * HEAD_DIM), BF16),
                        pltpu.VMEM((NA_HEADS, lc, HEAD_DIM), BF16), pltpu.VMEM((NA_HEADS, lc, 2 * HEAD_DIM), BF16)],
        compiler_params=_params(("arbitrary", "arbitrary")),
        name="attn_lat",
    )(q, k, v, ck, cv, bias, q_g, k_g)


def _bias_kernel(rpb_ref, o_ref, *, rows):
    kr, rb, wr, starts = _lat_windows(rows)
    h = pl.program_id(0)
    n_ro, n_co = 2 * WIN_ROWS - 1, 2 * WIN_COLS - 1
    wq = lax.broadcasted_iota(I32, (GRID_W, GRID_W), 0)
    wk = lax.broadcasted_iota(I32, (GRID_W, GRID_W), 1)
    co = jnp.clip(wk - wq, -(WIN_COLS - 1), WIN_COLS - 1) + (WIN_COLS - 1)
    col_start = jnp.clip(wq - WIN_COLS // 2, 0, GRID_W - WIN_COLS)
    in_win = (wk >= col_start) & (wk < col_start + WIN_COLS)
    co = jnp.where(in_win, co, -1)
    masked = jnp.full((GRID_W, GRID_W), NEG_INF, F32)
    tables = {}
    for r in range(rows):
        row_start = min(max(r - kr // 2, 0), rows - kr)
        for j in range(wr):
            key_row = starts[r // rb] + j
            blk = masked
            if row_start <= key_row < row_start + kr:
                ro = key_row - r + (WIN_ROWS - 1)
                if ro not in tables:
                    t = masked
                    for c in range(n_co):
                        t = jnp.where(co == c, rpb_ref[(h * n_ro + ro) * n_co + c], t)
                    tables[ro] = t
                blk = tables[ro]
            o_ref[0, r, :, j * GRID_W:(j + 1) * GRID_W] = blk


def _window_bias(rpb, rows):
    _, _, wr, _ = _lat_windows(rows)
    nh = rpb.shape[0]
    return pl.pallas_call(
        functools.partial(_bias_kernel, rows=rows),
        grid=(nh,),
        in_specs=[pl.BlockSpec(memory_space=pltpu.SMEM)],
        out_specs=pl.BlockSpec((1, rows, GRID_W, wr * GRID_W), lambda h: (h, 0, 0, 0)),
        out_shape=jax.ShapeDtypeStruct((nh, rows, GRID_W, wr * GRID_W), F32),
        compiler_params=_params(("arbitrary",)),
        name="window_bias",
    )(rpb.reshape(-1))


def _pool_kernel(u_ref, o_ref):
    l = u_ref.shape[1]
    t = lax.broadcasted_iota(I32, (l, POOL_GROUP_DIM), 0)
    for g, w in enumerate(POOL_WINDOWS):
        sl = slice(g * POOL_GROUP_DIM, (g + 1) * POOL_GROUP_DIM)
        x = u_ref[0, :, sl]
        acc = x
        for k in range(-(w // 2), w - w // 2):
            if k == 0:
                continue
            shifted = pltpu.roll(x, (-k) % l, 0)
            acc = acc + jnp.where((t + k >= 0) & (t + k < l), shifted, 0.0)
        cnt = (jnp.minimum(t + (w - w // 2), l) - jnp.maximum(t - w // 2, 0)).astype(F32)
        o_ref[0, :, sl] = acc / cnt - x


def _pool(u):
    b, l, w = u.shape
    blk = pl.BlockSpec((1, l, w), lambda i: (i, 0, 0))
    return pl.pallas_call(
        _pool_kernel, grid=(b,), in_specs=[blk], out_specs=blk,
        out_shape=jax.ShapeDtypeStruct((b, l, w), F32),
        compiler_params=_params(("arbitrary",)),
        name="pool",
    )(u)


def _merge_kernel(xc_ref, xl_ref, attc_ref, attl_ref, plc_ref, pll_ref, gac_ref, gal_ref, gbc_ref, gbl_ref,
                  mod_ref, watt_ref, wpool_ref, ps_ref, wout_ref, g2_ref, wr_ref, x1_ref, h2_ref, aff_ref, *, ctx_blocks):
    is_ctx = pl.program_id(0) < ctx_blocks
    pick = lambda c_ref, l_ref: jnp.where(is_ctx, c_ref[...], l_ref[...])
    o_a = _dot(pick(attc_ref, attl_ref).astype(BF16), watt_ref[...])
    pooled = pick(plc_ref, pll_ref)
    o_b = jnp.concatenate(
        [_dot(pooled[:, g * POOL_GROUP_DIM:(g + 1) * POOL_GROUP_DIM].astype(BF16), wpool_ref[g])
         for g in range(len(POOL_WINDOWS))], axis=-1) * ps_ref[...]
    merged = _sigmoid(pick(gac_ref, gal_ref)) * o_a + _sigmoid(pick(gbc_ref, gbl_ref)) * o_b
    x1 = pick(xc_ref, xl_ref) + mod_ref[0, 2:3, :] * _dot(merged.astype(BF16), wout_ref[...])
    x1_ref[...] = x1
    h2 = _rms(x1, g2_ref[...]) * (1.0 + mod_ref[0, 4:5, :]) + mod_ref[0, 3:4, :]
    h2_ref[...] = h2
    logits = _dot3(h2, wr_ref[...])
    e = jnp.exp(logits - jnp.max(logits, axis=-1, keepdims=True))
    aff_ref[...] = e / jnp.sum(e, axis=-1, keepdims=True)


def _merge(ctx, lat, mods, w_att, w_pool, pool_scale, w_out, norm2_g, w_router, lat_blocks_per_seq):
    n_c, d = ctx[0].shape
    n = n_c + lat[0].shape[0]
    tb = TOKEN_BLOCK
    cb = n_c // tb
    crow = lambda w: pl.BlockSpec((tb, w), lambda i: (jnp.minimum(i, cb - 1), 0))
    lrow = lambda w: pl.BlockSpec((tb, w), lambda i: (jnp.maximum(i - cb, 0), 0))
    orow = lambda w: pl.BlockSpec((tb, w), lambda i: (i, 0))
    full = lambda a: pl.BlockSpec(a.shape, lambda i: (0,) * a.ndim)
    cond = lambda i: jnp.where(i < cb, 0, 1 + (i - cb) // lat_blocks_per_seq)
    pairs, specs = [], []
    for a_c, a_l in zip(ctx, lat):
        pairs += [a_c, a_l]
        specs += [crow(a_c.shape[1]), lrow(a_l.shape[1])]
    return pl.pallas_call(
        functools.partial(_merge_kernel, ctx_blocks=cb),
        grid=(n // tb,),
        in_specs=specs + [pl.BlockSpec((1, N_MOD, d), lambda i: (cond(i), 0, 0)),
                          full(w_att), full(w_pool), full(pool_scale), full(w_out), full(norm2_g), full(w_router)],
        out_specs=[orow(d), orow(d), orow(N_EXPERTS)],
        out_shape=[jax.ShapeDtypeStruct((n, d), F32), jax.ShapeDtypeStruct((n, d), F32),
                   jax.ShapeDtypeStruct((n, N_EXPERTS), F32)],
        compiler_params=_params(("arbitrary",)),
        name="merge",
    )(*pairs, mods, w_att, w_pool, pool_scale, w_out, norm2_g, w_router)


def _prefix_incl(mask_f32, out_ref):
    e, n = mask_f32.shape
    i = lax.broadcasted_iota(I32, (128, 128), 0)
    j = lax.broadcasted_iota(I32, (128, 128), 1)
    tri = jnp.where(i <= j, 1.0, 0.0).astype(BF16)
    carry = jnp.zeros((e, 1), F32)
    for c in range(n // 128):
        inc = _dot(mask_f32[:, c * 128:(c + 1) * 128].astype(BF16), tri) + carry
        out_ref[:, c * 128:(c + 1) * 128] = inc.astype(I32)
        carry = inc[:, 127:128]


def _route_kernel(aff_ref, sel_ref, rinc_ref, tmp_ref, *, cap):
    aff = aff_ref[...]

    def bit_step(i, thr):
        cand = thr | jnp.left_shift(jnp.int32(1), 30 - i)
        cnt = jnp.sum(jnp.where(aff >= pltpu.bitcast(cand, F32), 1, 0), axis=1, keepdims=True)
        return jnp.where(cnt >= cap, cand, thr)

    thr = lax.fori_loop(0, 31, bit_step, jnp.zeros((aff.shape[0], 1), I32))
    gt = aff >= pltpu.bitcast(thr + 1, F32)
    eq = (aff >= pltpu.bitcast(thr, F32)) & jnp.logical_not(gt)
    need = cap - jnp.sum(jnp.where(gt, 1, 0), axis=1, keepdims=True)
    _prefix_incl(jnp.where(eq, 1.0, 0.0), tmp_ref)
    sel = gt | (eq & (tmp_ref[...] <= need))
    sel_ref[...] = jnp.where(sel, 1, 0)
    _prefix_incl(jnp.where(sel, 1.0, 0.0), rinc_ref)


def _route(aff_t, cap):
    e, n = aff_t.shape
    full = pl.BlockSpec((e, n), lambda: (0, 0))
    return pl.pallas_call(
        functools.partial(_route_kernel, cap=cap),
        in_specs=[full], out_specs=[full, full],
        out_shape=[jax.ShapeDtypeStruct((e, n), I32)] * 2,
        scratch_shapes=[pltpu.VMEM((e, n), I32)],
        compiler_params=pltpu.CompilerParams(vmem_limit_bytes=VMEM_LIMIT),
        name="route",
    )(aff_t)


def _moe_kernel(gidx_ref, qpos_ref, h_ref, gate_ref, wg_ref, wu_ref, wd_ref, z_ref,
                xf_ref, xb_ref, acc_ref, y_ref, sem_g, sem_s, *, m_tot, n_f, step_rows, z_zero_rows):
    e = pl.program_id(0)
    j = pl.program_id(1)
    n_e = pl.num_programs(0)
    m_iss = step_rows * n_f
    tf = wg_ref.shape[1]
    slot = e % 2

    def gather_row(lst, s, dst_slot):
        t = gidx_ref[lst * m_iss + s]
        return pltpu.make_async_copy(h_ref.at[pl.ds(t, 1)], xf_ref.at[dst_slot, pl.ds(s, 1)], sem_g.at[dst_slot])

    def scatter_row(lst, s):
        q = qpos_ref[lst * m_iss + s]
        return pltpu.make_async_copy(y_ref.at[pl.ds(s, 1)], z_ref.at[pl.ds(q, 1)], sem_s)

    def wait_gathers(dst_slot):
        pltpu.make_async_copy(h_ref.at[pl.ds(0, m_iss)], xf_ref.at[dst_slot], sem_g.at[dst_slot]).wait()

    def wait_scatters():
        pltpu.make_async_copy(y_ref, z_ref.at[pl.ds(0, m_iss)], sem_s).wait()

    @pl.when((e == 0) & (j == 0))
    def _():
        y_ref[...] = jnp.zeros_like(y_ref)
        cp = pltpu.make_async_copy(y_ref.at[pl.ds(0, z_zero_rows)], z_ref.at[pl.ds(n_e * m_tot, z_zero_rows)], sem_s)
        cp.start()
        cp.wait()

        def first(s, c):
            gather_row(0, s, 0).start()
            return c

        lax.fori_loop(0, m_iss, first, 0)

    @pl.when(j == 0)
    def _():
        wait_gathers(slot)
        xb_ref[...] = xf_ref[slot, 0:m_tot, :].astype(BF16)
        acc_ref[...] = jnp.zeros_like(acc_ref)

    nxt = lax.rem(e + 1, n_e)
    for u in range(step_rows):
        s = j * step_rows + u
        gather_row(nxt, s, 1 - slot).start()
        scatter_row(e, s).start()

    valid = D_EXPERT - j * tf
    cmask = lax.broadcasted_iota(I32, (1, tf), 1) < valid
    rmask = lax.broadcasted_iota(I32, (tf, 1), 0) < valid
    wg = wg_ref[0].astype(BF16)
    wu = wu_ref[0].astype(BF16)
    wd = jnp.where(rmask, wd_ref[0], 0.0).astype(BF16)
    for m in range(m_tot // EXPERT_M_BLOCK):
        rs = slice(m * EXPERT_M_BLOCK, (m + 1) * EXPERT_M_BLOCK)
        xs = xb_ref[rs, :]
        a = _dot_nt(xs, wg)
        b = _dot_nt(xs, wu)
        hm = jnp.where(cmask, a * _sigmoid(a) * b, 0.0).astype(BF16)
        acc_ref[rs, :] += _dot(hm, wd)

    @pl.when(j == n_f - 1)
    def _():
        wait_scatters()
        y_ref[0:m_tot, :] = acc_ref[...] * gate_ref[0]

        @pl.when(e == n_e - 1)
        def _():
            def last(s, c):
                scatter_row(n_e, s).start()
                return c

            lax.fori_loop(0, m_iss, last, 0)
            wait_scatters()
            wait_gathers(1 - slot)


def _moe_step_rows(m_tot, n_f):
    return pl.cdiv(pl.cdiv(m_tot, n_f), ROW_ALIGN) * ROW_ALIGN


def _moe(gidx, qpos, h, gate, w_gate, w_up, w_down, m_tot, z_rows, z_zero_rows):
    d = h.shape[1]
    n_e = w_down.shape[0]
    tf = EXPERT_F_BLOCK
    n_f = pl.cdiv(D_EXPERT, tf)
    step_rows = _moe_step_rows(m_tot, n_f)
    m_iss = step_rows * n_f
    grid_spec = pltpu.PrefetchScalarGridSpec(
        num_scalar_prefetch=2,
        grid=(n_e, n_f),
        in_specs=[pl.BlockSpec(memory_space=pl.ANY),
                  pl.BlockSpec((1, m_tot, 1), lambda e, j, *_: (e, 0, 0)),
                  pl.BlockSpec((1, tf, d), lambda e, j, *_: (e, j, 0)),
                  pl.BlockSpec((1, tf, d), lambda e, j, *_: (e, j, 0)),
                  pl.BlockSpec((1, tf, d), lambda e, j, *_: (e, j, 0))],
        out_specs=pl.BlockSpec(memory_space=pl.ANY),
        scratch_shapes=[pltpu.VMEM((2, m_iss, d), F32), pltpu.VMEM((m_tot, d), BF16), pltpu.VMEM((m_tot, d), F32),
                        pltpu.VMEM((m_iss, d), F32), pltpu.SemaphoreType.DMA((2,)), pltpu.SemaphoreType.DMA],
    )
    return pl.pallas_call(
        functools.partial(_moe_kernel, m_tot=m_tot, n_f=n_f, step_rows=step_rows, z_zero_rows=z_zero_rows),
        grid_spec=grid_spec,
        out_shape=jax.ShapeDtypeStruct((z_rows, d), F32),
        compiler_params=_params(("arbitrary", "arbitrary")),
        name="moe",
    )(gidx, qpos, h, gate, w_gate, w_up, w_down)


def _combine_kernel(w0_ref, nc_ref, first_ref, z_ref, x1_ref, mod_ref, qt_ref, kt_ref, o_ref, zbuf, sem, acc_ref):
    blk = pl.program_id(0)
    n_blk = pl.num_programs(0)
    w0 = w0_ref[blk]
    n_chunks = nc_ref[blk]
    first = first_ref[blk]
    qt = qt_ref[...]
    kt = kt_ref[...]
    acc_ref[...] = jnp.zeros_like(acc_ref)

    def z_copy(start, slot):
        rows = pl.ds(pl.multiple_of(start, ROW_ALIGN), Z_CHUNK)
        return pltpu.make_async_copy(z_ref.at[rows], zbuf.at[slot], sem.at[slot])

    @pl.when(blk == 0)
    def _():
        z_copy(w0, first % 2).start()

    def chunk(c, carry):
        slot = (first + c) % 2
        last = c + 1 == n_chunks
        next_start = jnp.where(last, w0_ref[jnp.minimum(blk + 1, n_blk - 1)], w0 + (c + 1) * Z_CHUNK)

        @pl.when(jnp.logical_not(last) | (blk + 1 < n_blk))
        def _():
            z_copy(next_start, 1 - slot).start()

        z_copy(w0 + c * Z_CHUNK, slot).wait()
        r = w0 + c * Z_CHUNK + lax.broadcasted_iota(I32, (qt.shape[0], Z_CHUNK), 1)
        onehot = jnp.where((r >= qt) & (r < qt + kt), 1.0, 0.0).astype(BF16)
        acc_ref[...] += _dot(onehot, zbuf[slot].astype(BF16))
        return carry

    lax.fori_loop(0, n_chunks, chunk, 0)
    o_ref[...] = x1_ref[...] + mod_ref[0, 5:6, :] * acc_ref[...]


def _combine(w0, nc, z, x1, first_block, mods, qt, kt, cond_of_block):
    n = qt.shape[0]
    d = x1.shape[1]
    tb = TOKEN_BLOCK
    grid_spec = pltpu.PrefetchScalarGridSpec(
        num_scalar_prefetch=3,
        grid=(n // tb,),
        in_specs=[pl.BlockSpec(memory_space=pl.ANY),
                  pl.BlockSpec((tb, d), lambda i, *_: (i + first_block, 0)),
                  pl.BlockSpec((1, N_MOD, d), lambda i, *_: (cond_of_block(i), 0, 0)),
                  pl.BlockSpec((tb, 1), lambda i, *_: (i, 0)),
                  pl.BlockSpec((tb, 1), lambda i, *_: (i, 0))],
        out_specs=pl.BlockSpec((tb, d), lambda i, *_: (i, 0)),
        scratch_shapes=[pltpu.VMEM((2, Z_CHUNK, d), F32), pltpu.SemaphoreType.DMA((2,)), pltpu.VMEM((tb, d), F32)],
    )
    return pl.pallas_call(
        _combine_kernel,
        grid_spec=grid_spec,
        out_shape=jax.ShapeDtypeStruct((n, d), F32),
        compiler_params=_params(("arbitrary",)),
        name="combine",
    )(w0, nc, jnp.cumsum(nc) - nc, z, x1, mods, qt, kt)


def _routing_tables(aff, cap, z_off):
    n = aff.shape[0]
    aff_t = aff.T
    sel, rinc = _route(aff_t, cap)
    slots = jnp.arange(cap, dtype=I32)
    chunk = 128
    rinc3 = rinc.reshape(rinc.shape[0], n // chunk, chunk)
    n_full = jnp.sum((rinc3[:, None, :, chunk - 1] <= slots[None, :, None]).astype(I32), axis=-1)
    inside = jnp.take_along_axis(rinc3, jnp.minimum(n_full, n // chunk - 1)[:, :, None], axis=1)
    idx = n_full * chunk + jnp.sum((inside <= slots[None, :, None]).astype(I32), axis=-1)
    gate = jnp.take_along_axis(aff_t, idx, axis=1)
    k_tok = jnp.sum(sel, axis=0)
    q_tok = jnp.cumsum(k_tok) - k_tok + z_off
    before = jnp.cumsum(sel, axis=0) - sel
    qpos = jnp.take_along_axis(q_tok[None, :] + before, idx, axis=1)
    nblk = n // TOKEN_BLOCK
    q_blk = q_tok[::TOKEN_BLOCK]
    end_blk = q_blk + jnp.sum(k_tok.reshape(nblk, TOKEN_BLOCK), axis=1)
    w0 = (q_blk // ROW_ALIGN) * ROW_ALIGN
    nc = jnp.maximum((end_blk - w0 + Z_CHUNK - 1) // Z_CHUNK, 1)
    return idx, gate, qpos.astype(I32), w0.astype(I32), nc.astype(I32), q_tok.astype(I32)[:, None], k_tok.astype(I32)[:, None]


def kernel(x_prompt, x_sample, cache_k, cache_v, c, c_ctx, norm1_g, norm2_g, w_ada, b_ada, w_in, q_norm_g, k_norm_g,
           rpb, w_att_proj, w_pool, pool_scale, w_out, w_router, w_gate_e, w_up_e, w_down_e):
    assert w_ada.shape[0] == 1, "single trunk layer"
    nb, ls, d = x_prompt.shape
    db, ll, _ = x_sample.shape
    n_c, n_l = nb * ls, db * ll
    lc = cache_k.shape[2]

    cond = jnp.concatenate([c_ctx[None, :], c, jnp.zeros((16 - 1 - db, d), F32)], axis=0)
    mods = _ada(cond, w_ada[0], b_ada).reshape(16, N_MOD, d)

    w_in_b = w_in[0].astype(BF16)
    w_att_b = w_att_proj[0].astype(BF16)
    w_pool_b = w_pool[0].astype(BF16)
    w_out_b = w_out[0].astype(BF16)

    cond_ctx = lambda i: 0
    lat_blocks = ll // TOKEN_BLOCK
    cond_lat = lambda i: 1 + i // lat_blocks

    xc = x_prompt.reshape(n_c, d)
    xl = x_sample.reshape(n_l, d)
    qc, kc, vc, uc, gac, gbc = _in_proj(xc, mods, norm1_g, w_in_b, cond_ctx)
    ql, kl, vl, ul, gal, gbl = _in_proj(xl, mods, norm1_g, w_in_b, cond_lat)

    seq_c = lambda a: a.reshape(nb, ls, a.shape[-1])
    seq_l = lambda a: a.reshape(db, ll, a.shape[-1])
    q_gain = jnp.tile(q_norm_g, (1, NA_HEADS))
    k_gain = jnp.tile(k_norm_g, (1, NA_HEADS))
    att_c, kn_c = _attn_ctx(seq_c(qc), seq_c(kc), seq_c(vc), q_gain, k_gain)
    bias = _window_bias(rpb[0], ll // GRID_W)
    att_l = _attn_lat(seq_l(ql), seq_l(kl), seq_l(vl), cache_k[:, 0].reshape(db, lc, NA_WIDTH),
                      cache_v[:, 0].reshape(db, lc, NA_WIDTH), bias, q_gain, k_gain)
    pool_c = _pool(seq_c(uc)).reshape(n_c, POOL_WIDTH)
    pool_l = _pool(seq_l(ul)).reshape(n_l, POOL_WIDTH)

    x1, h2, aff = _merge((xc, att_c.reshape(n_c, NA_WIDTH), pool_c, gac, gbc),
                         (xl, att_l.reshape(n_l, NA_WIDTH), pool_l, gal, gbl),
                         mods, w_att_b, w_pool_b, pool_scale, w_out_b, norm2_g, w_router[0], lat_blocks)

    cap_c = max(1, (CAPACITY_FACTOR * n_c) // N_EXPERTS)
    cap_l = max(1, (CAPACITY_FACTOR * n_l) // N_EXPERTS)
    m_tot = cap_c + cap_l
    m_iss = _moe_step_rows(m_tot, pl.cdiv(D_EXPERT, EXPERT_F_BLOCK)) * pl.cdiv(D_EXPERT, EXPERT_F_BLOCK)
    z_valid = N_EXPERTS * m_tot
    z_zero_rows = Z_CHUNK + ROW_ALIGN
    z_spare = z_valid + z_zero_rows
    z_rows = z_spare + m_iss
    idx_c, gate_c, qpos_c, w0_c, nc_c, qt_c, kt_c = _routing_tables(aff[:n_c], cap_c, 0)
    idx_l, gate_l, qpos_l, w0_l, nc_l, qt_l, kt_l = _routing_tables(aff[n_c:], cap_l, N_EXPERTS * cap_c)
    spare = jnp.broadcast_to(z_spare + jnp.arange(m_iss, dtype=I32), (N_EXPERTS + 1, m_iss))
    gidx = jnp.concatenate([idx_c, idx_l + n_c, jnp.zeros((N_EXPERTS, m_iss - m_tot), I32)], axis=1).reshape(-1)
    qpos = jnp.concatenate([spare[:1], jnp.concatenate([qpos_c, qpos_l, spare[1:, m_tot:]], axis=1)], axis=0).reshape(-1)
    gate = jnp.concatenate([gate_c, gate_l], axis=1)[:, :, None]

    z = _moe(gidx, qpos, h2, gate, jnp.swapaxes(w_gate_e[0], 1, 2), jnp.swapaxes(w_up_e[0], 1, 2), w_down_e[0],
             m_tot, z_rows, z_zero_rows)

    y_c = _combine(w0_c, nc_c, z, x1, 0, mods, qt_c, kt_c, cond_ctx)
    y_l = _combine(w0_l, nc_l, z, x1, n_c // TOKEN_BLOCK, mods, qt_l, kt_l, cond_lat)

    state_k = kn_c.reshape(nb, 1, ls, NA_HEADS, HEAD_DIM)
    state_v = vc.reshape(nb, 1, ls, NA_HEADS, HEAD_DIM)
    return (y_c.reshape(nb, ls, d), y_l.reshape(db, ll, d), state_k, state_v)
```

```python
import functools

import jax
import jax.numpy as jnp
from jax import lax
from jax.experimental import pallas as pl
from jax.experimental.pallas import tpu as pltpu

F32 = jnp.float32
BF16 = jnp.bfloat16
I32 = jnp.int32

D_MODEL = 1024
GRID_W = 64
NA_HEADS = 8
HEAD_DIM = 64
NA_WIDTH = NA_HEADS * HEAD_DIM
WIN_ROWS = 8
WIN_COLS = 16
POOL_WINDOWS = (2, 4, 8, 16)
POOL_GROUP_DIM = 128
POOL_WIDTH = 512
POOL_OUT_DIM = 256
N_EXPERTS = 16
CAPACITY_FACTOR = 2
D_EXPERT = 2752
N_MOD = 6
EPS = 1e-6
NEG_INF = -1e30

TOKEN_BLOCK = 256
LAT_ROW_BLOCK = 4
EXPERT_F_BLOCK = 256
EXPERT_M_BLOCK = 512
Z_CHUNK = 256
ROW_ALIGN = 8
VMEM_LIMIT = 56 * 1024 * 1024


def _dot(a, b):
    return jnp.dot(a, b, preferred_element_type=F32)


def _dot_nt(a, b):
    return lax.dot_general(a, b, (((1,), (1,)), ((), ())), preferred_element_type=F32)


def _dot3(a, b):
    a_hi = a.astype(BF16)
    a_lo = (a - a_hi.astype(F32)).astype(BF16)
    b_hi = b.astype(BF16)
    b_lo = (b - b_hi.astype(F32)).astype(BF16)
    return _dot(a_hi, b_hi) + _dot(a_hi, b_lo) + _dot(a_lo, b_hi)


def _sigmoid(x):
    return 1.0 / (1.0 + jnp.exp(-x))


def _rms(x, g):
    return x * lax.rsqrt(jnp.mean(x * x, axis=-1, keepdims=True) + EPS) * g


def _params(sem):
    return pltpu.CompilerParams(dimension_semantics=sem, vmem_limit_bytes=VMEM_LIMIT)


def _ada_kernel(cond_ref, w_ref, b_ref, o_ref):
    c = cond_ref[...]
    o_ref[...] = _dot3(c * _sigmoid(c), w_ref[...]) + b_ref[...]


def _ada(cond, w_ada, b_ada):
    rows, d = cond.shape
    n_out = w_ada.shape[1]
    bn = 512
    return pl.pallas_call(
        _ada_kernel,
        grid=(n_out // bn,),
        in_specs=[pl.BlockSpec((rows, d), lambda i: (0, 0)),
                  pl.BlockSpec((d, bn), lambda i: (0, i)),
                  pl.BlockSpec((1, bn), lambda i: (0, i))],
        out_specs=pl.BlockSpec((rows, bn), lambda i: (0, i)),
        out_shape=jax.ShapeDtypeStruct((rows, n_out), F32),
        compiler_params=_params(("arbitrary",)),
        name="ada",
    )(cond, w_ada, b_ada)


_IN_CUTS = (0, 512, 1024, 1536, 2048, 3072, 4096)


def _in_kernel(x_ref, mod_ref, g_ref, w_ref, q_ref, k_ref, v_ref, u_ref, ga_ref, gb_ref):
    y = _rms(x_ref[...], g_ref[...])
    h = (y * (1.0 + mod_ref[0, 1:2, :]) + mod_ref[0, 0:1, :]).astype(BF16)
    outs = (q_ref, k_ref, v_ref, u_ref, ga_ref, gb_ref)
    for o_ref, lo, hi in zip(outs, _IN_CUTS[:-1], _IN_CUTS[1:]):
        o_ref[...] = _dot(h, w_ref[:, lo:hi])


def _in_proj(x2d, mods, norm_g, w_in_bf16, cond_of_block):
    n, d = x2d.shape
    widths = [hi - lo for lo, hi in zip(_IN_CUTS[:-1], _IN_CUTS[1:])]
    return pl.pallas_call(
        _in_kernel,
        grid=(n // TOKEN_BLOCK,),
        in_specs=[pl.BlockSpec((TOKEN_BLOCK, d), lambda i: (i, 0)),
                  pl.BlockSpec((1, N_MOD, d), lambda i: (cond_of_block(i), 0, 0)),
                  pl.BlockSpec((1, d), lambda i: (0, 0)),
                  pl.BlockSpec(w_in_bf16.shape, lambda i: (0, 0))],
        out_specs=[pl.BlockSpec((TOKEN_BLOCK, w), lambda i: (i, 0)) for w in widths],
        out_shape=[jax.ShapeDtypeStruct((n, w), F32) for w in widths],
        compiler_params=_params(("arbitrary",)),
        name="in_proj",
    )(x2d, mods, norm_g, w_in_bf16)


def _head_rms(x, g):
    w = x.shape[1]
    gi = lax.broadcasted_iota(I32, (w, w), 0) // HEAD_DIM
    gj = lax.broadcasted_iota(I32, (w, w), 1) // HEAD_DIM
    avg = jnp.where(gi == gj, 1.0 / HEAD_DIM, 0.0).astype(BF16)
    x2 = x * x
    hi = x2.astype(BF16)
    lo = (x2 - hi.astype(F32)).astype(BF16)
    return x * lax.rsqrt(_dot(hi, avg) + _dot(lo, avg) + EPS) * g


def _with_ones(v):
    lane = lax.broadcasted_iota(I32, v.shape, 1)
    return jnp.concatenate([v, jnp.where(lane == 0, 1.0, 0.0)], axis=-1).astype(BF16)


def _attn_ctx_kernel(q_ref, k_ref, v_ref, qg_ref, kg_ref, att_ref, kn_ref):
    qn = (_head_rms(q_ref[0], qg_ref[...]) * HEAD_DIM ** -0.5).astype(BF16)
    kn = _head_rms(k_ref[0], kg_ref[...])
    kn_ref[0] = kn
    kn = kn.astype(BF16)
    for h in range(NA_HEADS):
        sl = slice(h * HEAD_DIM, (h + 1) * HEAD_DIM)
        s = _dot_nt(qn[:, sl], kn[:, sl])
        e = jnp.exp(s - jnp.max(s, axis=-1, keepdims=True)).astype(BF16)
        o = _dot(e, _with_ones(v_ref[0, :, sl]))
        att_ref[0, :, sl] = (o[:, :HEAD_DIM] / o[:, HEAD_DIM:HEAD_DIM + 1]).astype(att_ref.dtype)


def _attn_ctx(q, k, v, q_g, k_g):
    b, l, w = q.shape
    blk = pl.BlockSpec((1, l, w), lambda i: (i, 0, 0))
    gspec = pl.BlockSpec((1, w), lambda i: (0, 0))
    return pl.pallas_call(
        _attn_ctx_kernel,
        grid=(b,),
        in_specs=[blk, blk, blk, gspec, gspec],
        out_specs=[blk, blk],
        out_shape=[jax.ShapeDtypeStruct((b, l, w), BF16), jax.ShapeDtypeStruct((b, l, w), F32)],
        compiler_params=_params(("arbitrary",)),
        name="attn_ctx",
    )(q, k, v, q_g, k_g)


def _lat_windows(rows):
    kr = min(WIN_ROWS, rows)
    rb = min(LAT_ROW_BLOCK, rows)
    wr = min(kr + rb, rows)
    assert rows % rb == 0
    starts = [min(max(i * rb - kr // 2, 0), rows - wr) for i in range(rows // rb)]
    for i, ws in enumerate(starts):
        for r in range(i * rb, (i + 1) * rb):
            rs = min(max(r - kr // 2, 0), rows - kr)
            assert ws <= rs and rs + kr <= ws + wr
    return kr, rb, wr, starts


def _attn_lat_kernel(q_ref, k_ref, v_ref, ck_ref, cv_ref, bias_ref, qg_ref, kg_ref, att_ref,
                     kn_s, vb_s, ckb_s, cvb_s, *, rows, kr, rb, wr):
    i = pl.program_id(1)

    @pl.when(i == 0)
    def _():
        kn = _head_rms(k_ref[0], kg_ref[...]).astype(BF16)
        for h in range(NA_HEADS):
            sl = slice(h * HEAD_DIM, (h + 1) * HEAD_DIM)
            kn_s[h] = kn[:, sl]
            vb_s[h] = _with_ones(v_ref[0, :, sl])
            ckb_s[h] = ck_ref[0, :, sl].astype(BF16)
            cvb_s[h] = _with_ones(cv_ref[0, :, sl])

    win_start = jnp.clip(i * rb - kr // 2, 0, rows - wr)
    key_rows = pl.ds(pl.multiple_of(win_start * GRID_W, GRID_W), wr * GRID_W)
    qn_all = (_head_rms(q_ref[0], qg_ref[...]) * HEAD_DIM ** -0.5).astype(BF16)
    for h in range(NA_HEADS):
        sl = slice(h * HEAD_DIM, (h + 1) * HEAD_DIM)
        qn = qn_all[:, sl]
        s_loc = _dot_nt(qn, kn_s[h, key_rows, :]) + bias_ref[h].reshape(rb * GRID_W, wr * GRID_W)
        s_ctx = _dot_nt(qn, ckb_s[h])
        m = jnp.maximum(jnp.max(s_loc, axis=-1, keepdims=True), jnp.max(s_ctx, axis=-1, keepdims=True))
        o = (_dot(jnp.exp(s_loc - m).astype(BF16), vb_s[h, key_rows, :])
             + _dot(jnp.exp(s_ctx - m).astype(BF16), cvb_s[h]))
        att_ref[0, :, sl] = (o[:, :HEAD_DIM] / o[:, HEAD_DIM:HEAD_DIM + 1]).astype(att_ref.dtype)


def _attn_lat(q, k, v, ck, cv, bias, q_g, k_g):
    b, l, w = q.shape
    lc = ck.shape[1]
    rows = l // GRID_W
    kr, rb, wr, _ = _lat_windows(rows)
    seq = pl.BlockSpec((1, l, w), lambda i, r: (i, 0, 0))
    ctx = pl.BlockSpec((1, lc, w), lambda i, r: (i, 0, 0))
    rowblk = pl.BlockSpec((1, rb * GRID_W, w), lambda i, r: (i, r, 0))
    gspec = pl.BlockSpec((1, w), lambda i, r: (0, 0))
    return pl.pallas_call(
        functools.partial(_attn_lat_kernel, rows=rows, kr=kr, rb=rb, wr=wr),
        grid=(b, rows // rb),
        in_specs=[rowblk, seq, seq, ctx, ctx,
                  pl.BlockSpec((NA_HEADS, rb, GRID_W, wr * GRID_W), lambda i, r: (0, r, 0, 0)), gspec, gspec],
        out_specs=rowblk,
        out_shape=jax.ShapeDtypeStruct((b, l, w), BF16),
        scratch_shapes=[pltpu.VMEM((NA_HEADS, l, HEAD_DIM), BF16), pltpu.VMEM((NA_HEADS, l, 2 * HEAD_DIM), BF16),
                        pltpu.VMEM((NA_HEADS, lc, HEAD_DIM), BF16), pltpu.VMEM((NA_HEADS, lc, 2 * HEAD_DIM), BF16)],
        compiler_params=_params(("arbitrary", "arbitrary")),
        name="attn_lat",
    )(q, k, v, ck, cv, bias, q_g, k_g)


def _bias_kernel(rpb_ref, o_ref, *, rows):
    kr, rb, wr, starts = _lat_windows(rows)
    h = pl.program_id(0)
    n_ro, n_co = 2 * WIN_ROWS - 1, 2 * WIN_COLS - 1
    wq = lax.broadcasted_iota(I32, (GRID_W, GRID_W), 0)
    wk = lax.broadcasted_iota(I32, (GRID_W, GRID_W), 1)
    co = jnp.clip(wk - wq, -(WIN_COLS - 1), WIN_COLS - 1) + (WIN_COLS - 1)
    col_start = jnp.clip(wq - WIN_COLS // 2, 0, GRID_W - WIN_COLS)
    in_win = (wk >= col_start) & (wk < col_start + WIN_COLS)
    co = jnp.where(in_win, co, -1)
    masked = jnp.full((GRID_W, GRID_W), NEG_INF, F32)
    tables = {}
    for r in range(rows):
        row_start = min(max(r - kr // 2, 0), rows - kr)
        for j in range(wr):
            key_row = starts[r // rb] + j
            blk = masked
            if row_start <= key_row < row_start + kr:
                ro = key_row - r + (WIN_ROWS - 1)
                if ro not in tables:
                    t = masked
                    for c in range(n_co):
                        t = jnp.where(co == c, rpb_ref[(h * n_ro + ro) * n_co + c], t)
                    tables[ro] = t
                blk = tables[ro]
            o_ref[0, r, :, j * GRID_W:(j + 1) * GRID_W] = blk


def _window_bias(rpb, rows):
    _, _, wr, _ = _lat_windows(rows)
    nh = rpb.shape[0]
    return pl.pallas_call(
        functools.partial(_bias_kernel, rows=rows),
        grid=(nh,),
        in_specs=[pl.BlockSpec(memory_space=pltpu.SMEM)],
        out_specs=pl.BlockSpec((1, rows, GRID_W, wr * GRID_W), lambda h: (h, 0, 0, 0)),
        out_shape=jax.ShapeDtypeStruct((nh, rows, GRID_W, wr * GRID_W), F32),
        compiler_params=_params(("arbitrary",)),
        name="window_bias",
    )(rpb.reshape(-1))


def _pool_kernel(u_ref, o_ref):
    l = u_ref.shape[1]
    t = lax.broadcasted_iota(I32, (l, POOL_GROUP_DIM), 0)
    for g, w in enumerate(POOL_WINDOWS):
        sl = slice(g * POOL_GROUP_DIM, (g + 1) * POOL_GROUP_DIM)
        x = u_ref[0, :, sl]
        acc = x
        for k in range(-(w // 2), w - w // 2):
            if k == 0:
                continue
            shifted = pltpu.roll(x, (-k) % l, 0)
            acc = acc + jnp.where((t + k >= 0) & (t + k < l), shifted, 0.0)
        cnt = (jnp.minimum(t + (w - w // 2), l) - jnp.maximum(t - w // 2, 0)).astype(F32)
        o_ref[0, :, sl] = (acc / cnt - x).astype(o_ref.dtype)


def _pool(u):
    b, l, w = u.shape
    blk = pl.BlockSpec((1, l, w), lambda i: (i, 0, 0))
    return pl.pallas_call(
        _pool_kernel, grid=(b,), in_specs=[blk], out_specs=blk,
        out_shape=jax.ShapeDtypeStruct((b, l, w), BF16),
        compiler_params=_params(("arbitrary",)),
        name="pool",
    )(u)


def _merge_kernel(xc_ref, xl_ref, attc_ref, attl_ref, plc_ref, pll_ref, gac_ref, gal_ref, gbc_ref, gbl_ref,
                  mod_ref, watt_ref, wpool_ref, ps_ref, wout_ref, g2_ref, wr_ref, x1_ref, h2_ref, aff_ref, *, ctx_blocks):
    is_ctx = pl.program_id(0) < ctx_blocks
    pick = lambda c_ref, l_ref: jnp.where(is_ctx, c_ref[...], l_ref[...])
    o_a = _dot(pick(attc_ref, attl_ref), watt_ref[...])
    pooled = pick(plc_ref, pll_ref)
    o_b = jnp.concatenate(
        [_dot(pooled[:, g * POOL_GROUP_DIM:(g + 1) * POOL_GROUP_DIM], wpool_ref[g])
         for g in range(len(POOL_WINDOWS))], axis=-1) * ps_ref[...]
    merged = _sigmoid(pick(gac_ref, gal_ref)) * o_a + _sigmoid(pick(gbc_ref, gbl_ref)) * o_b
    x1 = pick(xc_ref, xl_ref) + mod_ref[0, 2:3, :] * _dot(merged.astype(BF16), wout_ref[...])
    x1_ref[...] = x1
    h2 = _rms(x1, g2_ref[...]) * (1.0 + mod_ref[0, 4:5, :]) + mod_ref[0, 3:4, :]
    h2_ref[...] = h2
    logits = _dot3(h2, wr_ref[...])
    e = jnp.exp(logits - jnp.max(logits, axis=-1, keepdims=True))
    aff_ref[...] = e / jnp.sum(e, axis=-1, keepdims=True)


def _merge(ctx, lat, mods, w_att, w_pool, pool_scale, w_out, norm2_g, w_router, lat_blocks_per_seq):
    n_c, d = ctx[0].shape
    n = n_c + lat[0].shape[0]
    tb = TOKEN_BLOCK
    cb = n_c // tb
    crow = lambda w: pl.BlockSpec((tb, w), lambda i: (jnp.minimum(i, cb - 1), 0))
    lrow = lambda w: pl.BlockSpec((tb, w), lambda i: (jnp.maximum(i - cb, 0), 0))
    orow = lambda w: pl.BlockSpec((tb, w), lambda i: (i, 0))
    full = lambda a: pl.BlockSpec(a.shape, lambda i: (0,) * a.ndim)
    cond = lambda i: jnp.where(i < cb, 0, 1 + (i - cb) // lat_blocks_per_seq)
    pairs, specs = [], []
    for a_c, a_l in zip(ctx, lat):
        pairs += [a_c, a_l]
        specs += [crow(a_c.shape[1]), lrow(a_l.shape[1])]
    return pl.pallas_call(
        functools.partial(_merge_kernel, ctx_blocks=cb),
        grid=(n // tb,),
        in_specs=specs + [pl.BlockSpec((1, N_MOD, d), lambda i: (cond(i), 0, 0)),
                          full(w_att), full(w_pool), full(pool_scale), full(w_out), full(norm2_g), full(w_router)],
        out_specs=[orow(d), orow(d), orow(N_EXPERTS)],
        out_shape=[jax.ShapeDtypeStruct((n, d), F32), jax.ShapeDtypeStruct((n, d), F32),
                   jax.ShapeDtypeStruct((n, N_EXPERTS), F32)],
        compiler_params=_params(("arbitrary",)),
        name="merge",
    )(*pairs, mods, w_att, w_pool, pool_scale, w_out, norm2_g, w_router)


def _prefix_incl(mask_f32, out_ref):
    e, n = mask_f32.shape
    i = lax.broadcasted_iota(I32, (128, 128), 0)
    j = lax.broadcasted_iota(I32, (128, 128), 1)
    tri = jnp.where(i <= j, 1.0, 0.0).astype(BF16)
    carry = jnp.zeros((e, 1), F32)
    for c in range(n // 128):
        inc = _dot(mask_f32[:, c * 128:(c + 1) * 128].astype(BF16), tri) + carry
        out_ref[:, c * 128:(c + 1) * 128] = inc.astype(I32)
        carry = inc[:, 127:128]


def _route_kernel(aff_ref, sel_ref, rinc_ref, tmp_ref, *, cap):
    aff = aff_ref[...]

    def bit_step(i, thr):
        cand = thr | jnp.left_shift(jnp.int32(1), 30 - i)
        cnt = jnp.sum(jnp.where(aff >= pltpu.bitcast(cand, F32), 1, 0), axis=1, keepdims=True)
        return jnp.where(cnt >= cap, cand, thr)

    thr = lax.fori_loop(0, 31, bit_step, jnp.zeros((aff.shape[0], 1), I32))
    gt = aff >= pltpu.bitcast(thr + 1, F32)
    eq = (aff >= pltpu.bitcast(thr, F32)) & jnp.logical_not(gt)
    need = cap - jnp.sum(jnp.where(gt, 1, 0), axis=1, keepdims=True)
    _prefix_incl(jnp.where(eq, 1.0, 0.0), tmp_ref)
    sel = gt | (eq & (tmp_ref[...] <= need))
    sel_ref[...] = jnp.where(sel, 1, 0)
    _prefix_incl(jnp.where(sel, 1.0, 0.0), rinc_ref)


def _route(aff_t, cap):
    e, n = aff_t.shape
    full = pl.BlockSpec((e, n), lambda: (0, 0))
    return pl.pallas_call(
        functools.partial(_route_kernel, cap=cap),
        in_specs=[full], out_specs=[full, full],
        out_shape=[jax.ShapeDtypeStruct((e, n), I32)] * 2,
        scratch_shapes=[pltpu.VMEM((e, n), I32)],
        compiler_params=pltpu.CompilerParams(vmem_limit_bytes=VMEM_LIMIT),
        name="route",
    )(aff_t)


def _moe_kernel(gidx_ref, qpos_ref, h_ref, gate_ref, wg_ref, wu_ref, wd_ref, z_ref,
                xf_ref, xb_ref, acc_ref, y_ref, sem_g, sem_s, *, m_tot, n_f, step_rows, z_zero_rows):
    e = pl.program_id(0)
    j = pl.program_id(1)
    n_e = pl.num_programs(0)
    m_iss = step_rows * n_f
    tf = wg_ref.shape[1]
    slot = e % 2

    def gather_row(lst, s, dst_slot):
        t = gidx_ref[lst * m_iss + s]
        return pltpu.make_async_copy(h_ref.at[pl.ds(t, 1)], xf_ref.at[dst_slot, pl.ds(s, 1)], sem_g.at[dst_slot])

    def scatter_row(lst, s):
        q = qpos_ref[lst * m_iss + s]
        return pltpu.make_async_copy(y_ref.at[pl.ds(s, 1)], z_ref.at[pl.ds(q, 1)], sem_s)

    def wait_gathers(dst_slot):
        pltpu.make_async_copy(h_ref.at[pl.ds(0, m_iss)], xf_ref.at[dst_slot], sem_g.at[dst_slot]).wait()

    def wait_scatters():
        pltpu.make_async_copy(y_ref, z_ref.at[pl.ds(0, m_iss)], sem_s).wait()

    @pl.when((e == 0) & (j == 0))
    def _():
        y_ref[...] = jnp.zeros_like(y_ref)
        cp = pltpu.make_async_copy(y_ref.at[pl.ds(0, z_zero_rows)], z_ref.at[pl.ds(n_e * m_tot, z_zero_rows)], sem_s)
        cp.start()
        cp.wait()

        def first(s, c):
            gather_row(0, s, 0).start()
            return c

        lax.fori_loop(0, m_iss, first, 0)

    @pl.when(j == 0)
    def _():
        wait_gathers(slot)
        xb_ref[...] = xf_ref[slot, 0:m_tot, :].astype(BF16)
        acc_ref[...] = jnp.zeros_like(acc_ref)

    nxt = lax.rem(e + 1, n_e)
    for u in range(step_rows):
        s = j * step_rows + u
        gather_row(nxt, s, 1 - slot).start()
        scatter_row(e, s).start()

    valid = D_EXPERT - j * tf
    cmask = lax.broadcasted_iota(I32, (1, tf), 1) < valid
    rmask = lax.broadcasted_iota(I32, (tf, 1), 0) < valid
    wg = wg_ref[0].astype(BF16)
    wu = wu_ref[0].astype(BF16)
    wd = jnp.where(rmask, wd_ref[0], 0.0).astype(BF16)
    for m in range(m_tot // EXPERT_M_BLOCK):
        rs = slice(m * EXPERT_M_BLOCK, (m + 1) * EXPERT_M_BLOCK)
        xs = xb_ref[rs, :]
        a = _dot_nt(xs, wg)
        b = _dot_nt(xs, wu)
        hm = jnp.where(cmask, a * _sigmoid(a) * b, 0.0).astype(BF16)
        acc_ref[rs, :] += _dot(hm, wd)

    @pl.when(j == n_f - 1)
    def _():
        wait_scatters()
        y_ref[0:m_tot, :] = acc_ref[...] * gate_ref[0]

        @pl.when(e == n_e - 1)
        def _():
            def last(s, c):
                scatter_row(n_e, s).start()
                return c

            lax.fori_loop(0, m_iss, last, 0)
            wait_scatters()
            wait_gathers(1 - slot)


def _moe_step_rows(m_tot, n_f):
    return pl.cdiv(pl.cdiv(m_tot, n_f), ROW_ALIGN) * ROW_ALIGN


def _moe(gidx, qpos, h, gate, w_gate, w_up, w_down, m_tot, z_rows, z_zero_rows):
    d = h.shape[1]
    n_e = w_down.shape[0]
    tf = EXPERT_F_BLOCK
    n_f = pl.cdiv(D_EXPERT, tf)
    step_rows = _moe_step_rows(m_tot, n_f)
    m_iss = step_rows * n_f
    grid_spec = pltpu.PrefetchScalarGridSpec(
        num_scalar_prefetch=2,
        grid=(n_e, n_f),
        in_specs=[pl.BlockSpec(memory_space=pl.ANY),
                  pl.BlockSpec((1, m_tot, 1), lambda e, j, *_: (e, 0, 0)),
                  pl.BlockSpec((1, tf, d), lambda e, j, *_: (e, j, 0)),
                  pl.BlockSpec((1, tf, d), lambda e, j, *_: (e, j, 0)),
                  pl.BlockSpec((1, tf, d), lambda e, j, *_: (e, j, 0))],
        out_specs=pl.BlockSpec(memory_space=pl.ANY),
        scratch_shapes=[pltpu.VMEM((2, m_iss, d), F32), pltpu.VMEM((m_tot, d), BF16), pltpu.VMEM((m_tot, d), F32),
                        pltpu.VMEM((m_iss, d), F32), pltpu.SemaphoreType.DMA((2,)), pltpu.SemaphoreType.DMA],
    )
    return pl.pallas_call(
        functools.partial(_moe_kernel, m_tot=m_tot, n_f=n_f, step_rows=step_rows, z_zero_rows=z_zero_rows),
        grid_spec=grid_spec,
        out_shape=jax.ShapeDtypeStruct((z_rows, d), F32),
        compiler_params=_params(("arbitrary", "arbitrary")),
        name="moe",
    )(gidx, qpos, h, gate, w_gate, w_up, w_down)


def _combine_kernel(w0_ref, nc_ref, first_ref, z_ref, x1_ref, mod_ref, qt_ref, kt_ref, o_ref, zbuf, sem, acc_ref):
    blk = pl.program_id(0)
    n_blk = pl.num_programs(0)
    w0 = w0_ref[blk]
    n_chunks = nc_ref[blk]
    first = first_ref[blk]
    qt = qt_ref[...]
    kt = kt_ref[...]
    acc_ref[...] = jnp.zeros_like(acc_ref)

    def z_copy(start, slot):
        rows = pl.ds(pl.multiple_of(start, ROW_ALIGN), Z_CHUNK)
        return pltpu.make_async_copy(z_ref.at[rows], zbuf.at[slot], sem.at[slot])

    @pl.when(blk == 0)
    def _():
        z_copy(w0, first % 2).start()

    def chunk(c, carry):
        slot = (first + c) % 2
        last = c + 1 == n_chunks
        next_start = jnp.where(last, w0_ref[jnp.minimum(blk + 1, n_blk - 1)], w0 + (c + 1) * Z_CHUNK)

        @pl.when(jnp.logical_not(last) | (blk + 1 < n_blk))
        def _():
            z_copy(next_start, 1 - slot).start()

        z_copy(w0 + c * Z_CHUNK, slot).wait()
        r = w0 + c * Z_CHUNK + lax.broadcasted_iota(I32, (qt.shape[0], Z_CHUNK), 1)
        onehot = jnp.where((r >= qt) & (r < qt + kt), 1.0, 0.0).astype(BF16)
        acc_ref[...] += _dot(onehot, zbuf[slot].astype(BF16))
        return carry

    lax.fori_loop(0, n_chunks, chunk, 0)
    o_ref[...] = x1_ref[...] + mod_ref[0, 5:6, :] * acc_ref[...]


def _combine(w0, nc, z, x1, first_block, mods, qt, kt, cond_of_block):
    n = qt.shape[0]
    d = x1.shape[1]
    tb = TOKEN_BLOCK
    grid_spec = pltpu.PrefetchScalarGridSpec(
        num_scalar_prefetch=3,
        grid=(n // tb,),
        in_specs=[pl.BlockSpec(memory_space=pl.ANY),
                  pl.BlockSpec((tb, d), lambda i, *_: (i + first_block, 0)),
                  pl.BlockSpec((1, N_MOD, d), lambda i, *_: (cond_of_block(i), 0, 0)),
                  pl.BlockSpec((tb, 1), lambda i, *_: (i, 0)),
                  pl.BlockSpec((tb, 1), lambda i, *_: (i, 0))],
        out_specs=pl.BlockSpec((tb, d), lambda i, *_: (i, 0)),
        scratch_shapes=[pltpu.VMEM((2, Z_CHUNK, d), F32), pltpu.SemaphoreType.DMA((2,)), pltpu.VMEM((tb, d), F32)],
    )
    return pl.pallas_call(
        _combine_kernel,
        grid_spec=grid_spec,
        out_shape=jax.ShapeDtypeStruct((n, d), F32),
        compiler_params=_params(("arbitrary",)),
        name="combine",
    )(w0, nc, jnp.cumsum(nc) - nc, z, x1, mods, qt, kt)


def _routing_tables(aff, cap, z_off):
    n = aff.shape[0]
    aff_t = aff.T
    sel, rinc = _route(aff_t, cap)
    slots = jnp.arange(cap, dtype=I32)
    chunk = 128
    rinc3 = rinc.reshape(rinc.shape[0], n // chunk, chunk)
    n_full = jnp.sum((rinc3[:, None, :, chunk - 1] <= slots[None, :, None]).astype(I32), axis=-1)
    pick = (n_full[:, :, None] == jnp.arange(n // chunk, dtype=I32)[None, None, :]).astype(F32)
    inside = jnp.einsum("eca,eab->ecb", pick, rinc3.astype(F32), precision=lax.Precision.HIGHEST)
    idx = n_full * chunk + jnp.sum((inside <= slots[None, :, None].astype(F32)).astype(I32), axis=-1)
    gate = jnp.take_along_axis(aff_t, idx, axis=1)
    k_tok = jnp.sum(sel, axis=0)
    q_tok = jnp.cumsum(k_tok) - k_tok + z_off
    before = jnp.cumsum(sel, axis=0) - sel
    qpos = jnp.take_along_axis(q_tok[None, :] + before, idx, axis=1)
    nblk = n // TOKEN_BLOCK
    q_blk = q_tok[::TOKEN_BLOCK]
    end_blk = q_blk + jnp.sum(k_tok.reshape(nblk, TOKEN_BLOCK), axis=1)
    w0 = (q_blk // ROW_ALIGN) * ROW_ALIGN
    nc = jnp.maximum((end_blk - w0 + Z_CHUNK - 1) // Z_CHUNK, 1)
    return idx, gate, qpos.astype(I32), w0.astype(I32), nc.astype(I32), q_tok.astype(I32)[:, None], k_tok.astype(I32)[:, None]


def kernel(x_prompt, x_sample, cache_k, cache_v, c, c_ctx, norm1_g, norm2_g, w_ada, b_ada, w_in, q_norm_g, k_norm_g,
           rpb, w_att_proj, w_pool, pool_scale, w_out, w_router, w_gate_e, w_up_e, w_down_e):
    assert w_ada.shape[0] == 1, "single trunk layer"
    nb, ls, d = x_prompt.shape
    db, ll, _ = x_sample.shape
    n_c, n_l = nb * ls, db * ll
    lc = cache_k.shape[2]

    cond = jnp.concatenate([c_ctx[None, :], c, jnp.zeros((16 - 1 - db, d), F32)], axis=0)
    mods = _ada(cond, w_ada[0], b_ada).reshape(16, N_MOD, d)

    w_in_b = w_in[0].astype(BF16)
    w_att_b = w_att_proj[0].astype(BF16)
    w_pool_b = w_pool[0].astype(BF16)
    w_out_b = w_out[0].astype(BF16)

    cond_ctx = lambda i: 0
    lat_blocks = ll // TOKEN_BLOCK
    cond_lat = lambda i: 1 + i // lat_blocks

    xc = x_prompt.reshape(n_c, d)
    xl = x_sample.reshape(n_l, d)
    qc, kc, vc, uc, gac, gbc = _in_proj(xc, mods, norm1_g, w_in_b, cond_ctx)
    ql, kl, vl, ul, gal, gbl = _in_proj(xl, mods, norm1_g, w_in_b, cond_lat)

    seq_c = lambda a: a.reshape(nb, ls, a.shape[-1])
    seq_l = lambda a: a.reshape(db, ll, a.shape[-1])
    q_gain = jnp.tile(q_norm_g, (1, NA_HEADS))
    k_gain = jnp.tile(k_norm_g, (1, NA_HEADS))
    att_c, kn_c = _attn_ctx(seq_c(qc), seq_c(kc), seq_c(vc), q_gain, k_gain)
    bias = _window_bias(rpb[0], ll // GRID_W)
    att_l = _attn_lat(seq_l(ql), seq_l(kl), seq_l(vl), cache_k[:, 0].reshape(db, lc, NA_WIDTH),
                      cache_v[:, 0].reshape(db, lc, NA_WIDTH), bias, q_gain, k_gain)
    pool_c = _pool(seq_c(uc)).reshape(n_c, POOL_WIDTH)
    pool_l = _pool(seq_l(ul)).reshape(n_l, POOL_WIDTH)

    x1, h2, aff = _merge((xc, att_c.reshape(n_c, NA_WIDTH), pool_c, gac, gbc),
                         (xl, att_l.reshape(n_l, NA_WIDTH), pool_l, gal, gbl),
                         mods, w_att_b, w_pool_b, pool_scale, w_out_b, norm2_g, w_router[0], lat_blocks)

    cap_c = max(1, (CAPACITY_FACTOR * n_c) // N_EXPERTS)
    cap_l = max(1, (CAPACITY_FACTOR * n_l) // N_EXPERTS)
    m_tot = cap_c + cap_l
    m_iss = _moe_step_rows(m_tot, pl.cdiv(D_EXPERT, EXPERT_F_BLOCK)) * pl.cdiv(D_EXPERT, EXPERT_F_BLOCK)
    z_valid = N_EXPERTS * m_tot
    z_zero_rows = Z_CHUNK + ROW_ALIGN
    z_spare = z_valid + z_zero_rows
    z_rows = z_spare + m_iss
    idx_c, gate_c, qpos_c, w0_c, nc_c, qt_c, kt_c = _routing_tables(aff[:n_c], cap_c, 0)
    idx_l, gate_l, qpos_l, w0_l, nc_l, qt_l, kt_l = _routing_tables(aff[n_c:], cap_l, N_EXPERTS * cap_c)
    spare = jnp.broadcast_to(z_spare + jnp.arange(m_iss, dtype=I32), (N_EXPERTS + 1, m_iss))
    gidx = jnp.concatenate([idx_c, idx_l + n_c, jnp.zeros((N_EXPERTS, m_iss - m_tot), I32)], axis=1).reshape(-1)
    qpos = jnp.concatenate([spare[:1], jnp.concatenate([qpos_c, qpos_l, spare[1:, m_tot:]], axis=1)], axis=0).reshape(-1)
    gate = jnp.concatenate([gate_c, gate_l], axis=1)[:, :, None]

    z = _moe(gidx, qpos, h2, gate, jnp.swapaxes(w_gate_e[0], 1, 2), jnp.swapaxes(w_up_e[0], 1, 2), w_down_e[0],
             m_tot, z_rows, z_zero_rows)

    y_c = _combine(w0_c, nc_c, z, x1, 0, mods, qt_c, kt_c, cond_ctx)
    y_l = _combine(w0_l, nc_l, z, x1, n_c // TOKEN_BLOCK, mods, qt_l, kt_l, cond_lat)

    state_k = kn_c.reshape(nb, 1, ls, NA_HEADS, HEAD_DIM)
    state_v = vc.reshape(nb, 1, ls, NA_HEADS, HEAD_DIM)
    return (y_c.reshape(nb, ls, d), y_l.reshape(db, ll, d), state_k, state_v)
```

```python
import functools

import jax
import jax.numpy as jnp
from jax import lax
from jax.experimental import pallas as pl
from jax.experimental.pallas import tpu as pltpu

F32 = jnp.float32
BF16 = jnp.bfloat16
I32 = jnp.int32

D_MODEL = 1024
GRID_W = 64
NA_HEADS = 8
HEAD_DIM = 64
NA_WIDTH = NA_HEADS * HEAD_DIM
WIN_ROWS = 8
WIN_COLS = 16
POOL_WINDOWS = (2, 4, 8, 16)
POOL_GROUP_DIM = 128
POOL_WIDTH = 512
POOL_OUT_DIM = 256
N_EXPERTS = 16
CAPACITY_FACTOR = 2
D_EXPERT = 2752
N_MOD = 6
EPS = 1e-6
NEG_INF = -1e30

TOKEN_BLOCK = 256
LAT_ROW_BLOCK = 4
EXPERT_F_BLOCK = 256
EXPERT_M_BLOCK = 768
Z_CHUNK = 256
ROW_ALIGN = 8
VMEM_LIMIT = 56 * 1024 * 1024


def _dot(a, b):
    return jnp.dot(a, b, preferred_element_type=F32)


def _dot_nt(a, b):
    return lax.dot_general(a, b, (((1,), (1,)), ((), ())), preferred_element_type=F32)


def _dot3(a, b):
    a_hi = a.astype(BF16)
    a_lo = (a - a_hi.astype(F32)).astype(BF16)
    b_hi = b.astype(BF16)
    b_lo = (b - b_hi.astype(F32)).astype(BF16)
    return _dot(a_hi, b_hi) + _dot(a_hi, b_lo) + _dot(a_lo, b_hi)


def _sigmoid(x):
    return 0.5 * jnp.tanh(0.5 * x) + 0.5


def _rms(x, g):
    return x * lax.rsqrt(jnp.mean(x * x, axis=-1, keepdims=True) + EPS) * g


def _params(sem):
    return pltpu.CompilerParams(dimension_semantics=sem, vmem_limit_bytes=VMEM_LIMIT)


def _ada_kernel(cond_ref, w_ref, b_ref, o_ref):
    c = cond_ref[...]
    o_ref[...] = _dot3(c * _sigmoid(c), w_ref[...]) + b_ref[...]


def _ada(cond, w_ada, b_ada):
    rows, d = cond.shape
    n_out = w_ada.shape[1]
    bn = 512
    return pl.pallas_call(
        _ada_kernel,
        grid=(n_out // bn,),
        in_specs=[pl.BlockSpec((rows, d), lambda i: (0, 0)),
                  pl.BlockSpec((d, bn), lambda i: (0, i)),
                  pl.BlockSpec((1, bn), lambda i: (0, i))],
        out_specs=pl.BlockSpec((rows, bn), lambda i: (0, i)),
        out_shape=jax.ShapeDtypeStruct((rows, n_out), F32),
        compiler_params=_params(("arbitrary",)),
        name="ada",
    )(cond, w_ada, b_ada)


_IN_CUTS = (0, 512, 1024, 1536, 2048, 3072, 4096)


HEAD_AVG_WIDTH = 256


def _head_avg_matrix():
    head = jnp.arange(HEAD_AVG_WIDTH, dtype=I32) // HEAD_DIM
    return jnp.where(head[:, None] == head[None, :], 1.0 / HEAD_DIM, 0.0).astype(BF16)


def _head_rms(x, g, avg, two_term):
    x2 = x * x
    means = []
    for c in range(0, x.shape[1], HEAD_AVG_WIDTH):
        blk = x2[:, c:c + HEAD_AVG_WIDTH]
        hi = blk.astype(BF16)
        ms = _dot(hi, avg)
        if two_term:
            ms = ms + _dot((blk - hi.astype(F32)).astype(BF16), avg)
        means.append(ms)
    return x * lax.rsqrt(jnp.concatenate(means, axis=-1) + EPS) * g


def _in_kernel(x_ref, mod_ref, g_ref, w_ref, qg_ref, kg_ref, avg_ref, q_ref, k_ref, v_ref, u_ref, ga_ref, gb_ref):
    y = _rms(x_ref[...], g_ref[...])
    h = (y * (1.0 + mod_ref[0, 1:2, :]) + mod_ref[0, 0:1, :]).astype(BF16)
    part = lambda i: _dot(h, w_ref[:, _IN_CUTS[i]:_IN_CUTS[i + 1]])
    q_ref[...] = (_head_rms(part(0), qg_ref[...], avg_ref[...], False) * HEAD_DIM ** -0.5).astype(q_ref.dtype)
    k_ref[...] = _head_rms(part(1), kg_ref[...], avg_ref[...], True).astype(k_ref.dtype)
    v_ref[...] = part(2).astype(v_ref.dtype)
    u_ref[...] = part(3)
    ga_ref[...] = part(4)
    gb_ref[...] = part(5)


def _in_proj(x2d, mods, norm_g, w_in_bf16, q_gain, k_gain, cond_of_block, kv_dtype):
    n, d = x2d.shape
    widths = [hi - lo for lo, hi in zip(_IN_CUTS[:-1], _IN_CUTS[1:])]
    dtypes = [BF16, kv_dtype, kv_dtype, F32, F32, F32]
    full = lambda a: pl.BlockSpec(a.shape, lambda i: (0,) * a.ndim)
    return pl.pallas_call(
        _in_kernel,
        grid=(n // TOKEN_BLOCK,),
        in_specs=[pl.BlockSpec((TOKEN_BLOCK, d), lambda i: (i, 0)),
                  pl.BlockSpec((1, N_MOD, d), lambda i: (cond_of_block(i), 0, 0)),
                  full(norm_g), full(w_in_bf16), full(q_gain), full(k_gain),
                  pl.BlockSpec((HEAD_AVG_WIDTH, HEAD_AVG_WIDTH), lambda i: (0, 0))],
        out_specs=[pl.BlockSpec((TOKEN_BLOCK, w), lambda i: (i, 0)) for w in widths],
        out_shape=[jax.ShapeDtypeStruct((n, w), t) for w, t in zip(widths, dtypes)],
        compiler_params=_params(("arbitrary",)),
        name="in_proj",
    )(x2d, mods, norm_g, w_in_bf16, q_gain, k_gain, _head_avg_matrix())


def _with_ones(v):
    lane = lax.broadcasted_iota(I32, v.shape, 1)
    return jnp.concatenate([v.astype(BF16), jnp.where(lane == 0, 1.0, 0.0).astype(BF16)], axis=-1)


def _attn_ctx_kernel(q_ref, k_ref, v_ref, att_ref):
    qn = q_ref[0]
    kn = k_ref[0].astype(BF16)
    for h in range(NA_HEADS):
        sl = slice(h * HEAD_DIM, (h + 1) * HEAD_DIM)
        s = _dot_nt(qn[:, sl], kn[:, sl])
        e = jnp.exp(s - jnp.max(s, axis=-1, keepdims=True)).astype(BF16)
        o = _dot(e, _with_ones(v_ref[0, :, sl]))
        att_ref[0, :, sl] = (o[:, :HEAD_DIM] / o[:, HEAD_DIM:HEAD_DIM + 1]).astype(att_ref.dtype)


def _attn_ctx(q, k, v):
    b, l, w = q.shape
    blk = pl.BlockSpec((1, l, w), lambda i: (i, 0, 0))
    return pl.pallas_call(
        _attn_ctx_kernel,
        grid=(b,),
        in_specs=[blk, blk, blk],
        out_specs=blk,
        out_shape=jax.ShapeDtypeStruct((b, l, w), BF16),
        compiler_params=_params(("arbitrary",)),
        name="attn_ctx",
    )(q, k, v)


def _lat_windows(rows):
    kr = min(WIN_ROWS, rows)
    rb = min(LAT_ROW_BLOCK, rows)
    wr = min(kr + rb, rows)
    assert rows % rb == 0
    starts = [min(max(i * rb - kr // 2, 0), rows - wr) for i in range(rows // rb)]
    for i, ws in enumerate(starts):
        for r in range(i * rb, (i + 1) * rb):
            rs = min(max(r - kr // 2, 0), rows - kr)
            assert ws <= rs and rs + kr <= ws + wr
    return kr, rb, wr, starts


def _attn_lat_kernel(q_ref, k_ref, v_ref, ck_ref, cv_ref, bias_ref, att_ref,
                     kn_s, vb_s, ckb_s, cvb_s, *, rows, kr, rb, wr):
    i = pl.program_id(1)

    @pl.when(i == 0)
    def _():
        for h in range(NA_HEADS):
            sl = slice(h * HEAD_DIM, (h + 1) * HEAD_DIM)
            kn_s[h] = k_ref[0, :, sl]
            vb_s[h] = _with_ones(v_ref[0, :, sl])
            ckb_s[h] = ck_ref[0, :, sl].astype(BF16)
            cvb_s[h] = _with_ones(cv_ref[0, :, sl])

    win_start = jnp.clip(i * rb - kr // 2, 0, rows - wr)
    key_rows = pl.ds(pl.multiple_of(win_start * GRID_W, GRID_W), wr * GRID_W)
    qn_all = q_ref[0]
    for h in range(NA_HEADS):
        sl = slice(h * HEAD_DIM, (h + 1) * HEAD_DIM)
        qn = qn_all[:, sl]
        s_loc = _dot_nt(qn, kn_s[h, key_rows, :]) + bias_ref[h].reshape(rb * GRID_W, wr * GRID_W)
        s_ctx = _dot_nt(qn, ckb_s[h])
        m = jnp.maximum(jnp.max(s_loc, axis=-1, keepdims=True), jnp.max(s_ctx, axis=-1, keepdims=True))
        o = (_dot(jnp.exp(s_loc - m).astype(BF16), vb_s[h, key_rows, :])
             + _dot(jnp.exp(s_ctx - m).astype(BF16), cvb_s[h]))
        att_ref[0, :, sl] = (o[:, :HEAD_DIM] / o[:, HEAD_DIM:HEAD_DIM + 1]).astype(att_ref.dtype)


def _attn_lat(q, k, v, ck, cv, bias):
    b, l, w = q.shape
    lc = ck.shape[1]
    rows = l // GRID_W
    kr, rb, wr, _ = _lat_windows(rows)
    seq = pl.BlockSpec((1, l, w), lambda i, r: (i, 0, 0))
    ctx = pl.BlockSpec((1, lc, w), lambda i, r: (i, 0, 0))
    rowblk = pl.BlockSpec((1, rb * GRID_W, w), lambda i, r: (i, r, 0))
    return pl.pallas_call(
        functools.partial(_attn_lat_kernel, rows=rows, kr=kr, rb=rb, wr=wr),
        grid=(b, rows // rb),
        in_specs=[rowblk, seq, seq, ctx, ctx,
                  pl.BlockSpec((NA_HEADS, rb, GRID_W, wr * GRID_W), lambda i, r: (0, r, 0, 0))],
        out_specs=rowblk,
        out_shape=jax.ShapeDtypeStruct((b, l, w), BF16),
        scratch_shapes=[pltpu.VMEM((NA_HEADS, l, HEAD_DIM), BF16), pltpu.VMEM((NA_HEADS, l, 2 * HEAD_DIM), BF16),
                        pltpu.VMEM((NA_HEADS, lc, HEAD_DIM), BF16), pltpu.VMEM((NA_HEADS, lc, 2 * HEAD_DIM), BF16)],
        compiler_params=_params(("arbitrary", "arbitrary")),
        name="attn_lat",
    )(q, k, v, ck, cv, bias)


def _bias_kernel(rpb_ref, o_ref, *, rows):
    kr, rb, wr, starts = _lat_windows(rows)
    h = pl.program_id(0)
    n_ro, n_co = 2 * WIN_ROWS - 1, 2 * WIN_COLS - 1
    wq = lax.broadcasted_iota(I32, (GRID_W, GRID_W), 0)
    wk = lax.broadcasted_iota(I32, (GRID_W, GRID_W), 1)
    co = jnp.clip(wk - wq, -(WIN_COLS - 1), WIN_COLS - 1) + (WIN_COLS - 1)
    col_start = jnp.clip(wq - WIN_COLS // 2, 0, GRID_W - WIN_COLS)
    in_win = (wk >= col_start) & (wk < col_start + WIN_COLS)
    co = jnp.where(in_win, co, -1)
    masked = jnp.full((GRID_W, GRID_W), NEG_INF, F32)
    tables = {}
    for r in range(rows):
        row_start = min(max(r - kr // 2, 0), rows - kr)
        for j in range(wr):
            key_row = starts[r // rb] + j
            blk = masked
            if row_start <= key_row < row_start + kr:
                ro = key_row - r + (WIN_ROWS - 1)
                if ro not in tables:
                    t = masked
                    for c in range(n_co):
                        t = jnp.where(co == c, rpb_ref[(h * n_ro + ro) * n_co + c], t)
                    tables[ro] = t
                blk = tables[ro]
            o_ref[0, r, :, j * GRID_W:(j + 1) * GRID_W] = blk


def _window_bias(rpb, rows):
    _, _, wr, _ = _lat_windows(rows)
    nh = rpb.shape[0]
    return pl.pallas_call(
        functools.partial(_bias_kernel, rows=rows),
        grid=(nh,),
        in_specs=[pl.BlockSpec(memory_space=pltpu.SMEM)],
        out_specs=pl.BlockSpec((1, rows, GRID_W, wr * GRID_W), lambda h: (h, 0, 0, 0)),
        out_shape=jax.ShapeDtypeStruct((nh, rows, GRID_W, wr * GRID_W), F32),
        compiler_params=_params(("arbitrary",)),
        name="window_bias",
    )(rpb.reshape(-1))


def _pool_kernel(u_ref, o_ref):
    l = u_ref.shape[1]
    t = lax.broadcasted_iota(I32, (l, POOL_GROUP_DIM), 0)
    for g, w in enumerate(POOL_WINDOWS):
        sl = slice(g * POOL_GROUP_DIM, (g + 1) * POOL_GROUP_DIM)
        x = u_ref[0, :, sl]
        acc = x
        for k in range(-(w // 2), w - w // 2):
            if k == 0:
                continue
            shifted = pltpu.roll(x, (-k) % l, 0)
            acc = acc + jnp.where((t + k >= 0) & (t + k < l), shifted, 0.0)
        cnt = (jnp.minimum(t + (w - w // 2), l) - jnp.maximum(t - w // 2, 0)).astype(F32)
        o_ref[0, :, sl] = (acc / cnt - x).astype(o_ref.dtype)


def _pool(u):
    b, l, w = u.shape
    blk = pl.BlockSpec((1, l, w), lambda i: (i, 0, 0))
    return pl.pallas_call(
        _pool_kernel, grid=(b,), in_specs=[blk], out_specs=blk,
        out_shape=jax.ShapeDtypeStruct((b, l, w), BF16),
        compiler_params=_params(("arbitrary",)),
        name="pool",
    )(u)


def _merge_kernel(xc_ref, xl_ref, attc_ref, attl_ref, plc_ref, pll_ref, gac_ref, gal_ref, gbc_ref, gbl_ref,
                  mod_ref, watt_ref, wpool_ref, ps_ref, wout_ref, g2_ref, wr_ref, x1_ref, h2_ref, aff_ref, *, ctx_blocks):
    is_ctx = pl.program_id(0) < ctx_blocks
    pick = lambda c_ref, l_ref: jnp.where(is_ctx, c_ref[...], l_ref[...])
    o_a = _dot(pick(attc_ref, attl_ref), watt_ref[...])
    pooled = pick(plc_ref, pll_ref)
    o_b = jnp.concatenate(
        [_dot(pooled[:, g * POOL_GROUP_DIM:(g + 1) * POOL_GROUP_DIM], wpool_ref[g])
         for g in range(len(POOL_WINDOWS))], axis=-1) * ps_ref[...]
    merged = _sigmoid(pick(gac_ref, gal_ref)) * o_a + _sigmoid(pick(gbc_ref, gbl_ref)) * o_b
    x1 = pick(xc_ref, xl_ref) + mod_ref[0, 2:3, :] * _dot(merged.astype(BF16), wout_ref[...])
    x1_ref[...] = x1
    h2 = _rms(x1, g2_ref[...]) * (1.0 + mod_ref[0, 4:5, :]) + mod_ref[0, 3:4, :]
    h2_ref[...] = h2
    logits = _dot3(h2, wr_ref[...])
    e = jnp.exp(logits - jnp.max(logits, axis=-1, keepdims=True))
    aff_ref[...] = e / jnp.sum(e, axis=-1, keepdims=True)


def _merge(ctx, lat, mods, w_att, w_pool, pool_scale, w_out, norm2_g, w_router, lat_blocks_per_seq):
    n_c, d = ctx[0].shape
    n = n_c + lat[0].shape[0]
    tb = TOKEN_BLOCK
    cb = n_c // tb
    crow = lambda w: pl.BlockSpec((tb, w), lambda i: (jnp.minimum(i, cb - 1), 0))
    lrow = lambda w: pl.BlockSpec((tb, w), lambda i: (jnp.maximum(i - cb, 0), 0))
    orow = lambda w: pl.BlockSpec((tb, w), lambda i: (i, 0))
    full = lambda a: pl.BlockSpec(a.shape, lambda i: (0,) * a.ndim)
    cond = lambda i: jnp.where(i < cb, 0, 1 + (i - cb) // lat_blocks_per_seq)
    pairs, specs = [], []
    for a_c, a_l in zip(ctx, lat):
        pairs += [a_c, a_l]
        specs += [crow(a_c.shape[1]), lrow(a_l.shape[1])]
    return pl.pallas_call(
        functools.partial(_merge_kernel, ctx_blocks=cb),
        grid=(n // tb,),
        in_specs=specs + [pl.BlockSpec((1, N_MOD, d), lambda i: (cond(i), 0, 0)),
                          full(w_att), full(w_pool), full(pool_scale), full(w_out), full(norm2_g), full(w_router)],
        out_specs=[orow(d), orow(d), orow(N_EXPERTS)],
        out_shape=[jax.ShapeDtypeStruct((n, d), F32), jax.ShapeDtypeStruct((n, d), F32),
                   jax.ShapeDtypeStruct((n, N_EXPERTS), F32)],
        compiler_params=_params(("arbitrary",)),
        name="merge",
    )(*pairs, mods, w_att, w_pool, pool_scale, w_out, norm2_g, w_router)


def _prefix_incl(mask_f32, out_ref):
    e, n = mask_f32.shape
    i = lax.broadcasted_iota(I32, (128, 128), 0)
    j = lax.broadcasted_iota(I32, (128, 128), 1)
    tri = jnp.where(i <= j, 1.0, 0.0).astype(BF16)
    carry = jnp.zeros((e, 1), F32)
    for c in range(n // 128):
        inc = _dot(mask_f32[:, c * 128:(c + 1) * 128].astype(BF16), tri) + carry
        out_ref[:, c * 128:(c + 1) * 128] = inc.astype(I32)
        carry = inc[:, 127:128]


def _route_kernel(aff_ref, sel_ref, rinc_ref, tmp_ref, *, cap):
    aff = aff_ref[...]

    def bit_step(i, thr):
        cand = thr | jnp.left_shift(jnp.int32(1), 30 - i)
        cnt = jnp.sum(jnp.where(aff >= pltpu.bitcast(cand, F32), 1, 0), axis=1, keepdims=True)
        return jnp.where(cnt >= cap, cand, thr)

    thr = lax.fori_loop(0, 31, bit_step, jnp.zeros((aff.shape[0], 1), I32))
    gt = aff >= pltpu.bitcast(thr + 1, F32)
    eq = (aff >= pltpu.bitcast(thr, F32)) & jnp.logical_not(gt)
    need = cap - jnp.sum(jnp.where(gt, 1, 0), axis=1, keepdims=True)
    _prefix_incl(jnp.where(eq, 1.0, 0.0), tmp_ref)
    sel = gt | (eq & (tmp_ref[...] <= need))
    sel_ref[...] = jnp.where(sel, 1, 0)
    _prefix_incl(jnp.where(sel, 1.0, 0.0), rinc_ref)


def _route(aff_t, cap):
    e, n = aff_t.shape
    full = pl.BlockSpec((e, n), lambda: (0, 0))
    return pl.pallas_call(
        functools.partial(_route_kernel, cap=cap),
        in_specs=[full], out_specs=[full, full],
        out_shape=[jax.ShapeDtypeStruct((e, n), I32)] * 2,
        scratch_shapes=[pltpu.VMEM((e, n), I32)],
        compiler_params=pltpu.CompilerParams(vmem_limit_bytes=VMEM_LIMIT),
        name="route",
    )(aff_t)


def _moe_kernel(gidx_ref, qpos_ref, h_ref, gate_ref, wg_ref, wu_ref, wd_ref, z_ref,
                xf_ref, xb_ref, acc_ref, y_ref, sem_g, sem_s, *, m_tot, n_f, step_rows, z_zero_rows):
    e = pl.program_id(0)
    j = pl.program_id(1)
    n_e = pl.num_programs(0)
    m_iss = step_rows * n_f
    tf = wg_ref.shape[1]
    slot = e % 2

    def gather_row(lst, s, dst_slot):
        t = gidx_ref[lst * m_iss + s]
        return pltpu.make_async_copy(h_ref.at[pl.ds(t, 1)], xf_ref.at[dst_slot, pl.ds(s, 1)], sem_g.at[dst_slot])

    def scatter_row(lst, s):
        q = qpos_ref[lst * m_iss + s]
        return pltpu.make_async_copy(y_ref.at[pl.ds(s, 1)], z_ref.at[pl.ds(q, 1)], sem_s)

    def wait_gathers(dst_slot):
        pltpu.make_async_copy(h_ref.at[pl.ds(0, m_iss)], xf_ref.at[dst_slot], sem_g.at[dst_slot]).wait()

    def wait_scatters():
        pltpu.make_async_copy(y_ref, z_ref.at[pl.ds(0, m_iss)], sem_s).wait()

    @pl.when((e == 0) & (j == 0))
    def _():
        y_ref[...] = jnp.zeros_like(y_ref)
        cp = pltpu.make_async_copy(y_ref.at[pl.ds(0, z_zero_rows)], z_ref.at[pl.ds(n_e * m_tot, z_zero_rows)], sem_s)
        cp.start()
        cp.wait()

        def first(s, c):
            gather_row(0, s, 0).start()
            return c

        lax.fori_loop(0, m_iss, first, 0)

    @pl.when(j == 0)
    def _():
        wait_gathers(slot)
        xb_ref[...] = xf_ref[slot, 0:m_tot, :].astype(BF16)
        acc_ref[...] = jnp.zeros_like(acc_ref)

    nxt = lax.rem(e + 1, n_e)
    for u in range(step_rows):
        s = j * step_rows + u
        gather_row(nxt, s, 1 - slot).start()
        scatter_row(e, s).start()

    valid = D_EXPERT - j * tf
    cmask = lax.broadcasted_iota(I32, (1, tf), 1) < valid
    rmask = lax.broadcasted_iota(I32, (tf, 1), 0) < valid
    wg = wg_ref[0].astype(BF16)
    wu = wu_ref[0].astype(BF16)
    wd = jnp.where(rmask, wd_ref[0], 0.0).astype(BF16)
    for m in range(m_tot // EXPERT_M_BLOCK):
        rs = slice(m * EXPERT_M_BLOCK, (m + 1) * EXPERT_M_BLOCK)
        xs = xb_ref[rs, :]
        a = _dot_nt(xs, wg)
        b = _dot_nt(xs, wu)
        hm = jnp.where(cmask, a * _sigmoid(a) * b, 0.0).astype(BF16)
        acc_ref[rs, :] += _dot(hm, wd)

    @pl.when(j == n_f - 1)
    def _():
        wait_scatters()
        y_ref[0:m_tot, :] = acc_ref[...] * gate_ref[0]

        @pl.when(e == n_e - 1)
        def _():
            def last(s, c):
                scatter_row(n_e, s).start()
                return c

            lax.fori_loop(0, m_iss, last, 0)
            wait_scatters()
            wait_gathers(1 - slot)


def _moe_step_rows(m_tot, n_f):
    return pl.cdiv(pl.cdiv(m_tot, n_f), ROW_ALIGN) * ROW_ALIGN


def _moe(gidx, qpos, h, gate, w_gate, w_up, w_down, m_tot, z_rows, z_zero_rows):
    d = h.shape[1]
    n_e = w_down.shape[0]
    tf = EXPERT_F_BLOCK
    n_f = pl.cdiv(D_EXPERT, tf)
    step_rows = _moe_step_rows(m_tot, n_f)
    m_iss = step_rows * n_f
    grid_spec = pltpu.PrefetchScalarGridSpec(
        num_scalar_prefetch=2,
        grid=(n_e, n_f),
        in_specs=[pl.BlockSpec(memory_space=pl.ANY),
                  pl.BlockSpec((1, m_tot, 1), lambda e, j, *_: (e, 0, 0)),
                  pl.BlockSpec((1, tf, d), lambda e, j, *_: (e, j, 0)),
                  pl.BlockSpec((1, tf, d), lambda e, j, *_: (e, j, 0)),
                  pl.BlockSpec((1, tf, d), lambda e, j, *_: (e, j, 0))],
        out_specs=pl.BlockSpec(memory_space=pl.ANY),
        scratch_shapes=[pltpu.VMEM((2, m_iss, d), F32), pltpu.VMEM((m_tot, d), BF16), pltpu.VMEM((m_tot, d), F32),
                        pltpu.VMEM((m_iss, d), F32), pltpu.SemaphoreType.DMA((2,)), pltpu.SemaphoreType.DMA],
    )
    return pl.pallas_call(
        functools.partial(_moe_kernel, m_tot=m_tot, n_f=n_f, step_rows=step_rows, z_zero_rows=z_zero_rows),
        grid_spec=grid_spec,
        out_shape=jax.ShapeDtypeStruct((z_rows, d), F32),
        compiler_params=_params(("arbitrary", "arbitrary")),
        name="moe",
    )(gidx, qpos, h, gate, w_gate, w_up, w_down)


def _combine_kernel(w0_ref, nc_ref, first_ref, z_ref, x1_ref, mod_ref, qt_ref, kt_ref, o_ref, zbuf, sem, acc_ref):
    blk = pl.program_id(0)
    n_blk = pl.num_programs(0)
    w0 = w0_ref[blk]
    n_chunks = nc_ref[blk]
    first = first_ref[blk]
    qt = qt_ref[...]
    kt = kt_ref[...]
    acc_ref[...] = jnp.zeros_like(acc_ref)

    def z_copy(start, slot):
        rows = pl.ds(pl.multiple_of(start, ROW_ALIGN), Z_CHUNK)
        return pltpu.make_async_copy(z_ref.at[rows], zbuf.at[slot], sem.at[slot])

    @pl.when(blk == 0)
    def _():
        z_copy(w0, first % 2).start()

    def chunk(c, carry):
        slot = (first + c) % 2
        last = c + 1 == n_chunks
        next_start = jnp.where(last, w0_ref[jnp.minimum(blk + 1, n_blk - 1)], w0 + (c + 1) * Z_CHUNK)

        @pl.when(jnp.logical_not(last) | (blk + 1 < n_blk))
        def _():
            z_copy(next_start, 1 - slot).start()

        z_copy(w0 + c * Z_CHUNK, slot).wait()
        r = w0 + c * Z_CHUNK + lax.broadcasted_iota(I32, (qt.shape[0], Z_CHUNK), 1)
        onehot = jnp.where((r >= qt) & (r < qt + kt), 1.0, 0.0).astype(BF16)
        acc_ref[...] += _dot(onehot, zbuf[slot].astype(BF16))
        return carry

    lax.fori_loop(0, n_chunks, chunk, 0)
    o_ref[...] = x1_ref[...] + mod_ref[0, 5:6, :] * acc_ref[...]


def _combine(w0, nc, z, x1, first_block, mods, qt, kt, cond_of_block):
    n = qt.shape[0]
    d = x1.shape[1]
    tb = TOKEN_BLOCK
    grid_spec = pltpu.PrefetchScalarGridSpec(
        num_scalar_prefetch=3,
        grid=(n // tb,),
        in_specs=[pl.BlockSpec(memory_space=pl.ANY),
                  pl.BlockSpec((tb, d), lambda i, *_: (i + first_block, 0)),
                  pl.BlockSpec((1, N_MOD, d), lambda i, *_: (cond_of_block(i), 0, 0)),
                  pl.BlockSpec((tb, 1), lambda i, *_: (i, 0)),
                  pl.BlockSpec((tb, 1), lambda i, *_: (i, 0))],
        out_specs=pl.BlockSpec((tb, d), lambda i, *_: (i, 0)),
        scratch_shapes=[pltpu.VMEM((2, Z_CHUNK, d), F32), pltpu.SemaphoreType.DMA((2,)), pltpu.VMEM((tb, d), F32)],
    )
    return pl.pallas_call(
        _combine_kernel,
        grid_spec=grid_spec,
        out_shape=jax.ShapeDtypeStruct((n, d), F32),
        compiler_params=_params(("arbitrary",)),
        name="combine",
    )(w0, nc, jnp.cumsum(nc) - nc, z, x1, mods, qt, kt)


def _routing_tables(aff, cap, z_off):
    n = aff.shape[0]
    aff_t = aff.T
    sel, rinc = _route(aff_t, cap)
    slots = jnp.arange(cap, dtype=I32)
    chunk = 128
    rinc3 = rinc.reshape(rinc.shape[0], n // chunk, chunk)
    n_full = jnp.sum((rinc3[:, None, :, chunk - 1] <= slots[None, :, None]).astype(I32), axis=-1)
    pick = (n_full[:, :, None] == jnp.arange(n // chunk, dtype=I32)[None, None, :]).astype(F32)
    inside = jnp.einsum("eca,eab->ecb", pick, rinc3.astype(F32), precision=lax.Precision.HIGHEST)
    idx = n_full * chunk + jnp.sum((inside <= slots[None, :, None].astype(F32)).astype(I32), axis=-1)
    gate = jnp.take_along_axis(aff_t, idx, axis=1)
    k_tok = jnp.sum(sel, axis=0)
    q_tok = jnp.cumsum(k_tok) - k_tok + z_off
    before = jnp.cumsum(sel, axis=0) - sel
    qpos = jnp.take_along_axis(q_tok[None, :] + before, idx, axis=1)
    nblk = n // TOKEN_BLOCK
    q_blk = q_tok[::TOKEN_BLOCK]
    end_blk = q_blk + jnp.sum(k_tok.reshape(nblk, TOKEN_BLOCK), axis=1)
    w0 = (q_blk // ROW_ALIGN) * ROW_ALIGN
    nc = jnp.maximum((end_blk - w0 + Z_CHUNK - 1) // Z_CHUNK, 1)
    return idx, gate, qpos.astype(I32), w0.astype(I32), nc.astype(I32), q_tok.astype(I32)[:, None], k_tok.astype(I32)[:, None]


def kernel(x_prompt, x_sample, cache_k, cache_v, c, c_ctx, norm1_g, norm2_g, w_ada, b_ada, w_in, q_norm_g, k_norm_g,
           rpb, w_att_proj, w_pool, pool_scale, w_out, w_router, w_gate_e, w_up_e, w_down_e):
    assert w_ada.shape[0] == 1, "single trunk layer"
    nb, ls, d = x_prompt.shape
    db, ll, _ = x_sample.shape
    n_c, n_l = nb * ls, db * ll
    lc = cache_k.shape[2]

    cond = jnp.concatenate([c_ctx[None, :], c, jnp.zeros((16 - 1 - db, d), F32)], axis=0)
    mods = _ada(cond, w_ada[0], b_ada).reshape(16, N_MOD, d)

    w_in_b = w_in[0].astype(BF16)
    w_att_b = w_att_proj[0].astype(BF16)
    w_pool_b = w_pool[0].astype(BF16)
    w_out_b = w_out[0].astype(BF16)

    cond_ctx = lambda i: 0
    lat_blocks = ll // TOKEN_BLOCK
    cond_lat = lambda i: 1 + i // lat_blocks

    xc = x_prompt.reshape(n_c, d)
    xl = x_sample.reshape(n_l, d)
    q_gain = jnp.tile(q_norm_g, (1, NA_HEADS))
    k_gain = jnp.tile(k_norm_g, (1, NA_HEADS))
    qc, kc, vc, uc, gac, gbc = _in_proj(xc, mods, norm1_g, w_in_b, q_gain, k_gain, cond_ctx, F32)
    ql, kl, vl, ul, gal, gbl = _in_proj(xl, mods, norm1_g, w_in_b, q_gain, k_gain, cond_lat, BF16)

    seq_c = lambda a: a.reshape(nb, ls, a.shape[-1])
    seq_l = lambda a: a.reshape(db, ll, a.shape[-1])
    att_c = _attn_ctx(seq_c(qc), seq_c(kc), seq_c(vc))
    bias = _window_bias(rpb[0], ll // GRID_W)
    att_l = _attn_lat(seq_l(ql), seq_l(kl), seq_l(vl), cache_k[:, 0].reshape(db, lc, NA_WIDTH),
                      cache_v[:, 0].reshape(db, lc, NA_WIDTH), bias)
    pool_c = _pool(seq_c(uc)).reshape(n_c, POOL_WIDTH)
    pool_l = _pool(seq_l(ul)).reshape(n_l, POOL_WIDTH)

    x1, h2, aff = _merge((xc, att_c.reshape(n_c, NA_WIDTH), pool_c, gac, gbc),
                         (xl, att_l.reshape(n_l, NA_WIDTH), pool_l, gal, gbl),
                         mods, w_att_b, w_pool_b, pool_scale, w_out_b, norm2_g, w_router[0], lat_blocks)

    cap_c = max(1, (CAPACITY_FACTOR * n_c) // N_EXPERTS)
    cap_l = max(1, (CAPACITY_FACTOR * n_l) // N_EXPERTS)
    m_tot = cap_c + cap_l
    m_iss = _moe_step_rows(m_tot, pl.cdiv(D_EXPERT, EXPERT_F_BLOCK)) * pl.cdiv(D_EXPERT, EXPERT_F_BLOCK)
    z_valid = N_EXPERTS * m_tot
    z_zero_rows = Z_CHUNK + ROW_ALIGN
    z_spare = z_valid + z_zero_rows
    z_rows = z_spare + m_iss
    idx_c, gate_c, qpos_c, w0_c, nc_c, qt_c, kt_c = _routing_tables(aff[:n_c], cap_c, 0)
    idx_l, gate_l, qpos_l, w0_l, nc_l, qt_l, kt_l = _routing_tables(aff[n_c:], cap_l, N_EXPERTS * cap_c)
    spare = jnp.broadcast_to(z_spare + jnp.arange(m_iss, dtype=I32), (N_EXPERTS + 1, m_iss))
    gidx = jnp.concatenate([idx_c, idx_l + n_c, jnp.zeros((N_EXPERTS, m_iss - m_tot), I32)], axis=1).reshape(-1)
    qpos = jnp.concatenate([spare[:1], jnp.concatenate([qpos_c, qpos_l, spare[1:, m_tot:]], axis=1)], axis=0).reshape(-1)
    gate = jnp.concatenate([gate_c, gate_l], axis=1)[:, :, None]

    z = _moe(gidx, qpos, h2, gate, jnp.swapaxes(w_gate_e[0], 1, 2), jnp.swapaxes(w_up_e[0], 1, 2), w_down_e[0],
             m_tot, z_rows, z_zero_rows)

    y_c = _combine(w0_c, nc_c, z, x1, 0, mods, qt_c, kt_c, cond_ctx)
    y_l = _combine(w0_l, nc_l, z, x1, n_c // TOKEN_BLOCK, mods, qt_l, kt_l, cond_lat)

    state_k = kc.reshape(nb, 1, ls, NA_HEADS, HEAD_DIM)
    state_v = vc.reshape(nb, 1, ls, NA_HEADS, HEAD_DIM)
    return (y_c.reshape(nb, ls, d), y_l.reshape(db, ll, d), state_k, state_v)
```

```python
import functools

import jax
import jax.numpy as jnp
from jax import lax
from jax.experimental import pallas as pl
from jax.experimental.pallas import tpu as pltpu

F32 = jnp.float32
BF16 = jnp.bfloat16
I32 = jnp.int32

D_MODEL = 1024
GRID_W = 64
NA_HEADS = 8
HEAD_DIM = 64
NA_WIDTH = NA_HEADS * HEAD_DIM
WIN_ROWS = 8
WIN_COLS = 16
POOL_WINDOWS = (2, 4, 8, 16)
POOL_GROUP_DIM = 128
POOL_WIDTH = 512
POOL_OUT_DIM = 256
N_EXPERTS = 16
CAPACITY_FACTOR = 2
D_EXPERT = 2752
N_MOD = 6
EPS = 1e-6
NEG_INF = -1e30

TOKEN_BLOCK = 256
LAT_ROW_BLOCK = 4
EXPERT_F_BLOCK = 256
EXPERT_M_BLOCK = 768
Z_CHUNK = 256
ROW_ALIGN = 8
VMEM_LIMIT = 56 * 1024 * 1024


def _dot(a, b):
    return jnp.dot(a, b, preferred_element_type=F32)


def _dot_nt(a, b):
    return lax.dot_general(a, b, (((1,), (1,)), ((), ())), preferred_element_type=F32)


def _dot3(a, b):
    a_hi = a.astype(BF16)
    a_lo = (a - a_hi.astype(F32)).astype(BF16)
    b_hi = b.astype(BF16)
    b_lo = (b - b_hi.astype(F32)).astype(BF16)
    return _dot(a_hi, b_hi) + _dot(a_hi, b_lo) + _dot(a_lo, b_hi)


def _sigmoid(x):
    return 0.5 * jnp.tanh(0.5 * x) + 0.5


def _rms(x, g):
    return x * lax.rsqrt(jnp.mean(x * x, axis=-1, keepdims=True) + EPS) * g


def _params(sem):
    return pltpu.CompilerParams(dimension_semantics=sem, vmem_limit_bytes=VMEM_LIMIT)


def _ada_kernel(cond_ref, w_ref, b_ref, o_ref):
    c = cond_ref[...]
    o_ref[...] = _dot3(c * _sigmoid(c), w_ref[...]) + b_ref[...]


def _ada(cond, w_ada, b_ada):
    rows, d = cond.shape
    n_out = w_ada.shape[1]
    bn = 512
    return pl.pallas_call(
        _ada_kernel,
        grid=(n_out // bn,),
        in_specs=[pl.BlockSpec((rows, d), lambda i: (0, 0)),
                  pl.BlockSpec((d, bn), lambda i: (0, i)),
                  pl.BlockSpec((1, bn), lambda i: (0, i))],
        out_specs=pl.BlockSpec((rows, bn), lambda i: (0, i)),
        out_shape=jax.ShapeDtypeStruct((rows, n_out), F32),
        compiler_params=_params(("arbitrary",)),
        name="ada",
    )(cond, w_ada, b_ada)


_IN_CUTS = (0, 512, 1024, 1536, 2048, 3072, 4096)


HEAD_AVG_WIDTH = 256


def _head_avg_matrix():
    head = jnp.arange(HEAD_AVG_WIDTH, dtype=I32) // HEAD_DIM
    return jnp.where(head[:, None] == head[None, :], 1.0 / HEAD_DIM, 0.0).astype(BF16)


def _head_rms(x, g, avg, two_term):
    x2 = x * x
    means = []
    for c in range(0, x.shape[1], HEAD_AVG_WIDTH):
        blk = x2[:, c:c + HEAD_AVG_WIDTH]
        hi = blk.astype(BF16)
        ms = _dot(hi, avg)
        if two_term:
            ms = ms + _dot((blk - hi.astype(F32)).astype(BF16), avg)
        means.append(ms)
    return x * lax.rsqrt(jnp.concatenate(means, axis=-1) + EPS) * g


def _in_kernel(x_ref, mod_ref, g_ref, w_ref, qg_ref, kg_ref, avg_ref, q_ref, k_ref, v_ref, u_ref, ga_ref, gb_ref):
    y = _rms(x_ref[...], g_ref[...])
    h = (y * (1.0 + mod_ref[0, 1:2, :]) + mod_ref[0, 0:1, :]).astype(BF16)
    part = lambda i: _dot(h, w_ref[:, _IN_CUTS[i]:_IN_CUTS[i + 1]])
    q_ref[...] = (_head_rms(part(0), qg_ref[...], avg_ref[...], False) * HEAD_DIM ** -0.5).astype(q_ref.dtype)
    k_ref[...] = _head_rms(part(1), kg_ref[...], avg_ref[...], True).astype(k_ref.dtype)
    v_ref[...] = part(2).astype(v_ref.dtype)
    u_ref[...] = part(3)
    ga_ref[...] = _sigmoid(part(4)).astype(ga_ref.dtype)
    gb_ref[...] = _sigmoid(part(5)).astype(gb_ref.dtype)


def _in_proj(x2d, mods, norm_g, w_in_bf16, q_gain, k_gain, cond_of_block, kv_dtype):
    n, d = x2d.shape
    widths = [hi - lo for lo, hi in zip(_IN_CUTS[:-1], _IN_CUTS[1:])]
    dtypes = [BF16, kv_dtype, kv_dtype, F32, BF16, BF16]
    full = lambda a: pl.BlockSpec(a.shape, lambda i: (0,) * a.ndim)
    return pl.pallas_call(
        _in_kernel,
        grid=(n // TOKEN_BLOCK,),
        in_specs=[pl.BlockSpec((TOKEN_BLOCK, d), lambda i: (i, 0)),
                  pl.BlockSpec((1, N_MOD, d), lambda i: (cond_of_block(i), 0, 0)),
                  full(norm_g), full(w_in_bf16), full(q_gain), full(k_gain),
                  pl.BlockSpec((HEAD_AVG_WIDTH, HEAD_AVG_WIDTH), lambda i: (0, 0))],
        out_specs=[pl.BlockSpec((TOKEN_BLOCK, w), lambda i: (i, 0)) for w in widths],
        out_shape=[jax.ShapeDtypeStruct((n, w), t) for w, t in zip(widths, dtypes)],
        compiler_params=_params(("arbitrary",)),
        name="in_proj",
    )(x2d, mods, norm_g, w_in_bf16, q_gain, k_gain, _head_avg_matrix())


def _with_ones(v):
    lane = lax.broadcasted_iota(I32, v.shape, 1)
    return jnp.concatenate([v.astype(BF16), jnp.where(lane == 0, 1.0, 0.0).astype(BF16)], axis=-1)


def _attn_ctx_kernel(q_ref, k_ref, v_ref, att_ref):
    qn = q_ref[0]
    kn = k_ref[0].astype(BF16)
    for h in range(NA_HEADS):
        sl = slice(h * HEAD_DIM, (h + 1) * HEAD_DIM)
        s = _dot_nt(qn[:, sl], kn[:, sl])
        e = jnp.exp(s - jnp.max(s, axis=-1, keepdims=True)).astype(BF16)
        o = _dot(e, _with_ones(v_ref[0, :, sl]))
        att_ref[0, :, sl] = (o[:, :HEAD_DIM] / o[:, HEAD_DIM:HEAD_DIM + 1]).astype(att_ref.dtype)


def _attn_ctx(q, k, v):
    b, l, w = q.shape
    blk = pl.BlockSpec((1, l, w), lambda i: (i, 0, 0))
    return pl.pallas_call(
        _attn_ctx_kernel,
        grid=(b,),
        in_specs=[blk, blk, blk],
        out_specs=blk,
        out_shape=jax.ShapeDtypeStruct((b, l, w), BF16),
        compiler_params=_params(("arbitrary",)),
        name="attn_ctx",
    )(q, k, v)


def _lat_windows(rows):
    kr = min(WIN_ROWS, rows)
    rb = min(LAT_ROW_BLOCK, rows)
    wr = min(kr + rb, rows)
    assert rows % rb == 0
    starts = [min(max(i * rb - kr // 2, 0), rows - wr) for i in range(rows // rb)]
    for i, ws in enumerate(starts):
        for r in range(i * rb, (i + 1) * rb):
            rs = min(max(r - kr // 2, 0), rows - kr)
            assert ws <= rs and rs + kr <= ws + wr
    return kr, rb, wr, starts


def _attn_lat_kernel(q_ref, k_ref, v_ref, ckt_ref, cvt_ref, bias_ref, att_ref,
                     kn_s, vb_s, ckb_s, cvb_s, *, rows, kr, rb, wr):
    i = pl.program_id(1)

    @pl.when(i == 0)
    def _():
        lc = ckt_ref.shape[3]
        ones_row = jnp.where(lax.broadcasted_iota(I32, (HEAD_DIM, lc), 0) == 0, 1.0, 0.0)
        for h in range(NA_HEADS):
            sl = slice(h * HEAD_DIM, (h + 1) * HEAD_DIM)
            kn_s[h] = k_ref[0, :, sl]
            vb_s[h] = _with_ones(v_ref[0, :, sl])
            ckb_s[h] = ckt_ref[0, h].astype(BF16)
            cvb_s[h] = jnp.concatenate([cvt_ref[0, h], ones_row], axis=0).astype(BF16)

    win_start = jnp.clip(i * rb - kr // 2, 0, rows - wr)
    key_rows = pl.ds(pl.multiple_of(win_start * GRID_W, GRID_W), wr * GRID_W)
    qn_all = q_ref[0]
    for h in range(NA_HEADS):
        sl = slice(h * HEAD_DIM, (h + 1) * HEAD_DIM)
        qn = qn_all[:, sl]
        bias = bias_ref[h, pl.ds(i * rb, rb)].reshape(rb * GRID_W, wr * GRID_W)
        s_loc = _dot_nt(qn, kn_s[h, key_rows, :]) + bias
        s_ctx = _dot(qn, ckb_s[h])
        m = jnp.maximum(jnp.max(s_loc, axis=-1, keepdims=True), jnp.max(s_ctx, axis=-1, keepdims=True))
        o = (_dot(jnp.exp(s_loc - m).astype(BF16), vb_s[h, key_rows, :])
             + _dot_nt(jnp.exp(s_ctx - m).astype(BF16), cvb_s[h]))
        att_ref[0, :, sl] = (o[:, :HEAD_DIM] / o[:, HEAD_DIM:HEAD_DIM + 1]).astype(att_ref.dtype)


def _attn_lat(q, k, v, ckt, cvt, bias):
    b, l, w = q.shape
    lc = ckt.shape[3]
    rows = l // GRID_W
    kr, rb, wr, _ = _lat_windows(rows)
    seq = pl.BlockSpec((1, l, w), lambda i, r: (i, 0, 0))
    ctx = pl.BlockSpec((1, NA_HEADS, HEAD_DIM, lc), lambda i, r: (i, 0, 0, 0))
    rowblk = pl.BlockSpec((1, rb * GRID_W, w), lambda i, r: (i, r, 0))
    return pl.pallas_call(
        functools.partial(_attn_lat_kernel, rows=rows, kr=kr, rb=rb, wr=wr),
        grid=(b, rows // rb),
        in_specs=[rowblk, seq, seq, ctx, ctx,
                  pl.BlockSpec(bias.shape, lambda i, r: (0, 0, 0, 0), pipeline_mode=pl.Buffered(1))],
        out_specs=rowblk,
        out_shape=jax.ShapeDtypeStruct((b, l, w), BF16),
        scratch_shapes=[pltpu.VMEM((NA_HEADS, l, HEAD_DIM), BF16), pltpu.VMEM((NA_HEADS, l, 2 * HEAD_DIM), BF16),
                        pltpu.VMEM((NA_HEADS, HEAD_DIM, lc), BF16), pltpu.VMEM((NA_HEADS, 2 * HEAD_DIM, lc), BF16)],
        compiler_params=_params(("arbitrary", "arbitrary")),
        name="attn_lat",
    )(q, k, v, ckt, cvt, bias)


def _bias_kernel(rpb_ref, o_ref, *, rows):
    kr, rb, wr, starts = _lat_windows(rows)
    h = pl.program_id(0)
    n_ro, n_co = 2 * WIN_ROWS - 1, 2 * WIN_COLS - 1
    wq = lax.broadcasted_iota(I32, (GRID_W, GRID_W), 0)
    wk = lax.broadcasted_iota(I32, (GRID_W, GRID_W), 1)
    co = jnp.clip(wk - wq, -(WIN_COLS - 1), WIN_COLS - 1) + (WIN_COLS - 1)
    col_start = jnp.clip(wq - WIN_COLS // 2, 0, GRID_W - WIN_COLS)
    in_win = (wk >= col_start) & (wk < col_start + WIN_COLS)
    co = jnp.where(in_win, co, -1)
    masked = jnp.full((GRID_W, GRID_W), NEG_INF, F32)
    tables = {}
    for r in range(rows):
        row_start = min(max(r - kr // 2, 0), rows - kr)
        for j in range(wr):
            key_row = starts[r // rb] + j
            blk = masked
            if row_start <= key_row < row_start + kr:
                ro = key_row - r + (WIN_ROWS - 1)
                if ro not in tables:
                    t = masked
                    for c in range(n_co):
                        t = jnp.where(co == c, rpb_ref[(h * n_ro + ro) * n_co + c], t)
                    tables[ro] = t
                blk = tables[ro]
            o_ref[0, r, :, j * GRID_W:(j + 1) * GRID_W] = blk


def _window_bias(rpb, rows):
    _, _, wr, _ = _lat_windows(rows)
    nh = rpb.shape[0]
    return pl.pallas_call(
        functools.partial(_bias_kernel, rows=rows),
        grid=(nh,),
        in_specs=[pl.BlockSpec(memory_space=pltpu.SMEM)],
        out_specs=pl.BlockSpec((1, rows, GRID_W, wr * GRID_W), lambda h: (h, 0, 0, 0)),
        out_shape=jax.ShapeDtypeStruct((nh, rows, GRID_W, wr * GRID_W), F32),
        compiler_params=_params(("arbitrary",)),
        name="window_bias",
    )(rpb.reshape(-1))


def _pool_kernel(u_ref, o_ref):
    l = u_ref.shape[1]
    t = lax.broadcasted_iota(I32, (l, POOL_GROUP_DIM), 0)
    for g, w in enumerate(POOL_WINDOWS):
        sl = slice(g * POOL_GROUP_DIM, (g + 1) * POOL_GROUP_DIM)
        x = u_ref[0, :, sl]
        acc = x
        for k in range(-(w // 2), w - w // 2):
            if k == 0:
                continue
            shifted = pltpu.roll(x, (-k) % l, 0)
            acc = acc + jnp.where((t + k >= 0) & (t + k < l), shifted, 0.0)
        cnt = (jnp.minimum(t + (w - w // 2), l) - jnp.maximum(t - w // 2, 0)).astype(F32)
        o_ref[0, :, sl] = (acc / cnt - x).astype(o_ref.dtype)


def _pool(u):
    b, l, w = u.shape
    blk = pl.BlockSpec((1, l, w), lambda i: (i, 0, 0))
    return pl.pallas_call(
        _pool_kernel, grid=(b,), in_specs=[blk], out_specs=blk,
        out_shape=jax.ShapeDtypeStruct((b, l, w), BF16),
        compiler_params=_params(("arbitrary",)),
        name="pool",
    )(u)


def _merge_kernel(xc_ref, xl_ref, attc_ref, attl_ref, plc_ref, pll_ref, gac_ref, gal_ref, gbc_ref, gbl_ref,
                  mod_ref, watt_ref, wpool_ref, ps_ref, wout_ref, g2_ref, wr_ref, x1_ref, h2_ref, aff_ref, *, ctx_blocks):
    is_ctx = pl.program_id(0) < ctx_blocks
    pick = lambda c_ref, l_ref: jnp.where(is_ctx, c_ref[...], l_ref[...])
    o_a = _dot(pick(attc_ref, attl_ref), watt_ref[...])
    pooled = pick(plc_ref, pll_ref)
    o_b = jnp.concatenate(
        [_dot(pooled[:, g * POOL_GROUP_DIM:(g + 1) * POOL_GROUP_DIM], wpool_ref[g])
         for g in range(len(POOL_WINDOWS))], axis=-1) * ps_ref[...]
    merged = pick(gac_ref, gal_ref) * o_a + pick(gbc_ref, gbl_ref) * o_b
    x1 = pick(xc_ref, xl_ref) + mod_ref[0, 2:3, :] * _dot(merged.astype(BF16), wout_ref[...])
    x1_ref[...] = x1
    h2 = _rms(x1, g2_ref[...]) * (1.0 + mod_ref[0, 4:5, :]) + mod_ref[0, 3:4, :]
    h2_ref[...] = h2
    logits = _dot3(h2, wr_ref[...])
    e = jnp.exp(logits - jnp.max(logits, axis=-1, keepdims=True))
    aff_ref[...] = e / jnp.sum(e, axis=-1, keepdims=True)


def _merge(ctx, lat, mods, w_att, w_pool, pool_scale, w_out, norm2_g, w_router, lat_blocks_per_seq):
    n_c, d = ctx[0].shape
    n = n_c + lat[0].shape[0]
    tb = TOKEN_BLOCK
    cb = n_c // tb
    crow = lambda w: pl.BlockSpec((tb, w), lambda i: (jnp.minimum(i, cb - 1), 0))
    lrow = lambda w: pl.BlockSpec((tb, w), lambda i: (jnp.maximum(i - cb, 0), 0))
    orow = lambda w: pl.BlockSpec((tb, w), lambda i: (i, 0))
    full = lambda a: pl.BlockSpec(a.shape, lambda i: (0,) * a.ndim)
    cond = lambda i: jnp.where(i < cb, 0, 1 + (i - cb) // lat_blocks_per_seq)
    pairs, specs = [], []
    for a_c, a_l in zip(ctx, lat):
        pairs += [a_c, a_l]
        specs += [crow(a_c.shape[1]), lrow(a_l.shape[1])]
    return pl.pallas_call(
        functools.partial(_merge_kernel, ctx_blocks=cb),
        grid=(n // tb,),
        in_specs=specs + [pl.BlockSpec((1, N_MOD, d), lambda i: (cond(i), 0, 0)),
                          full(w_att), full(w_pool), full(pool_scale), full(w_out), full(norm2_g), full(w_router)],
        out_specs=[orow(d), orow(d), orow(N_EXPERTS)],
        out_shape=[jax.ShapeDtypeStruct((n, d), F32), jax.ShapeDtypeStruct((n, d), F32),
                   jax.ShapeDtypeStruct((n, N_EXPERTS), F32)],
        compiler_params=_params(("arbitrary",)),
        name="merge",
    )(*pairs, mods, w_att, w_pool, pool_scale, w_out, norm2_g, w_router)


def _prefix_incl(mask_f32, out_ref):
    e, n = mask_f32.shape
    i = lax.broadcasted_iota(I32, (128, 128), 0)
    j = lax.broadcasted_iota(I32, (128, 128), 1)
    tri = jnp.where(i <= j, 1.0, 0.0).astype(BF16)
    carry = jnp.zeros((e, 1), F32)
    for c in range(n // 128):
        inc = _dot(mask_f32[:, c * 128:(c + 1) * 128].astype(BF16), tri) + carry
        out_ref[:, c * 128:(c + 1) * 128] = inc.astype(I32)
        carry = inc[:, 127:128]


def _route_kernel(aff_ref, sel_ref, rinc_ref, tmp_ref, *, cap):
    aff = aff_ref[...]

    def bit_step(i, thr):
        cand = thr | jnp.left_shift(jnp.int32(1), 30 - i)
        cnt = jnp.sum(jnp.where(aff >= pltpu.bitcast(cand, F32), 1, 0), axis=1, keepdims=True)
        return jnp.where(cnt >= cap, cand, thr)

    thr = lax.fori_loop(0, 31, bit_step, jnp.zeros((aff.shape[0], 1), I32))
    gt = aff >= pltpu.bitcast(thr + 1, F32)
    eq = (aff >= pltpu.bitcast(thr, F32)) & jnp.logical_not(gt)
    need = cap - jnp.sum(jnp.where(gt, 1, 0), axis=1, keepdims=True)
    _prefix_incl(jnp.where(eq, 1.0, 0.0), tmp_ref)
    sel = gt | (eq & (tmp_ref[...] <= need))
    sel_ref[...] = jnp.where(sel, 1, 0)
    _prefix_incl(jnp.where(sel, 1.0, 0.0), rinc_ref)


def _route(aff_t, cap):
    e, n = aff_t.shape
    full = pl.BlockSpec((e, n), lambda: (0, 0))
    return pl.pallas_call(
        functools.partial(_route_kernel, cap=cap),
        in_specs=[full], out_specs=[full, full],
        out_shape=[jax.ShapeDtypeStruct((e, n), I32)] * 2,
        scratch_shapes=[pltpu.VMEM((e, n), I32)],
        compiler_params=pltpu.CompilerParams(vmem_limit_bytes=VMEM_LIMIT),
        name="route",
    )(aff_t)


def _moe_kernel(gidx_ref, qpos_ref, h_ref, gate_ref, wg_ref, wu_ref, wd_ref, z_ref,
                xf_ref, xb_ref, acc_ref, y_ref, sem_g, sem_s, *, m_tot, n_f, step_rows, z_zero_rows):
    e = pl.program_id(0)
    j = pl.program_id(1)
    n_e = pl.num_programs(0)
    m_iss = step_rows * n_f
    tf = wg_ref.shape[1]
    slot = e % 2

    def gather_row(lst, s, dst_slot):
        t = gidx_ref[lst * m_iss + s]
        return pltpu.make_async_copy(h_ref.at[pl.ds(t, 1)], xf_ref.at[dst_slot, pl.ds(s, 1)], sem_g.at[dst_slot])

    def scatter_row(lst, s):
        q = qpos_ref[lst * m_iss + s]
        return pltpu.make_async_copy(y_ref.at[pl.ds(s, 1)], z_ref.at[pl.ds(q, 1)], sem_s)

    def wait_gathers(dst_slot):
        pltpu.make_async_copy(h_ref.at[pl.ds(0, m_iss)], xf_ref.at[dst_slot], sem_g.at[dst_slot]).wait()

    def wait_scatters():
        pltpu.make_async_copy(y_ref, z_ref.at[pl.ds(0, m_iss)], sem_s).wait()

    @pl.when((e == 0) & (j == 0))
    def _():
        y_ref[...] = jnp.zeros_like(y_ref)
        cp = pltpu.make_async_copy(y_ref.at[pl.ds(0, z_zero_rows)], z_ref.at[pl.ds(n_e * m_tot, z_zero_rows)], sem_s)
        cp.start()
        cp.wait()

        def first(s, c):
            gather_row(0, s, 0).start()
            return c

        lax.fori_loop(0, m_iss, first, 0)

    @pl.when(j == 0)
    def _():
        wait_gathers(slot)
        xb_ref[...] = xf_ref[slot, 0:m_tot, :].astype(BF16)
        acc_ref[...] = jnp.zeros_like(acc_ref)

    nxt = lax.rem(e + 1, n_e)
    for u in range(step_rows):
        s = j * step_rows + u
        gather_row(nxt, s, 1 - slot).start()
        scatter_row(e, s).start()

    valid = D_EXPERT - j * tf
    cmask = lax.broadcasted_iota(I32, (1, tf), 1) < valid
    rmask = lax.broadcasted_iota(I32, (tf, 1), 0) < valid
    wg = wg_ref[0].astype(BF16)
    wu = wu_ref[0].astype(BF16)
    wd = jnp.where(rmask, wd_ref[0], 0.0).astype(BF16)
    for m in range(m_tot // EXPERT_M_BLOCK):
        rs = slice(m * EXPERT_M_BLOCK, (m + 1) * EXPERT_M_BLOCK)
        xs = xb_ref[rs, :]
        a = _dot_nt(xs, wg)
        b = _dot_nt(xs, wu)
        hm = jnp.where(cmask, a * _sigmoid(a) * b, 0.0).astype(BF16)
        acc_ref[rs, :] += _dot(hm, wd)

    @pl.when(j == n_f - 1)
    def _():
        wait_scatters()
        y_ref[0:m_tot, :] = acc_ref[...] * gate_ref[0]

        @pl.when(e == n_e - 1)
        def _():
            def last(s, c):
                scatter_row(n_e, s).start()
                return c

            lax.fori_loop(0, m_iss, last, 0)
            wait_scatters()
            wait_gathers(1 - slot)


def _moe_step_rows(m_tot, n_f):
    return pl.cdiv(pl.cdiv(m_tot, n_f), ROW_ALIGN) * ROW_ALIGN


def _moe(gidx, qpos, h, gate, w_gate, w_up, w_down, m_tot, z_rows, z_zero_rows):
    d = h.shape[1]
    n_e = w_down.shape[0]
    tf = EXPERT_F_BLOCK
    n_f = pl.cdiv(D_EXPERT, tf)
    step_rows = _moe_step_rows(m_tot, n_f)
    m_iss = step_rows * n_f
    grid_spec = pltpu.PrefetchScalarGridSpec(
        num_scalar_prefetch=2,
        grid=(n_e, n_f),
        in_specs=[pl.BlockSpec(memory_space=pl.ANY),
                  pl.BlockSpec((1, m_tot, 1), lambda e, j, *_: (e, 0, 0)),
                  pl.BlockSpec((1, tf, d), lambda e, j, *_: (e, j, 0)),
                  pl.BlockSpec((1, tf, d), lambda e, j, *_: (e, j, 0)),
                  pl.BlockSpec((1, tf, d), lambda e, j, *_: (e, j, 0))],
        out_specs=pl.BlockSpec(memory_space=pl.ANY),
        scratch_shapes=[pltpu.VMEM((2, m_iss, d), F32), pltpu.VMEM((m_tot, d), BF16), pltpu.VMEM((m_tot, d), F32),
                        pltpu.VMEM((m_iss, d), F32), pltpu.SemaphoreType.DMA((2,)), pltpu.SemaphoreType.DMA],
    )
    return pl.pallas_call(
        functools.partial(_moe_kernel, m_tot=m_tot, n_f=n_f, step_rows=step_rows, z_zero_rows=z_zero_rows),
        grid_spec=grid_spec,
        out_shape=jax.ShapeDtypeStruct((z_rows, d), F32),
        compiler_params=_params(("arbitrary", "arbitrary")),
        name="moe",
    )(gidx, qpos, h, gate, w_gate, w_up, w_down)


def _combine_kernel(w0_ref, nc_ref, first_ref, z_ref, x1_ref, mod_ref, qt_ref, kt_ref, o_ref, zbuf, sem, acc_ref):
    blk = pl.program_id(0)
    n_blk = pl.num_programs(0)
    w0 = w0_ref[blk]
    n_chunks = nc_ref[blk]
    first = first_ref[blk]
    qt = qt_ref[...]
    kt = kt_ref[...]
    acc_ref[...] = jnp.zeros_like(acc_ref)

    def z_copy(start, slot):
        rows = pl.ds(pl.multiple_of(start, ROW_ALIGN), Z_CHUNK)
        return pltpu.make_async_copy(z_ref.at[rows], zbuf.at[slot], sem.at[slot])

    @pl.when(blk == 0)
    def _():
        z_copy(w0, first % 2).start()

    def chunk(c, carry):
        slot = (first + c) % 2
        last = c + 1 == n_chunks
        next_start = jnp.where(last, w0_ref[jnp.minimum(blk + 1, n_blk - 1)], w0 + (c + 1) * Z_CHUNK)

        @pl.when(jnp.logical_not(last) | (blk + 1 < n_blk))
        def _():
            z_copy(next_start, 1 - slot).start()

        z_copy(w0 + c * Z_CHUNK, slot).wait()
        r = w0 + c * Z_CHUNK + lax.broadcasted_iota(I32, (qt.shape[0], Z_CHUNK), 1)
        onehot = jnp.where((r >= qt) & (r < qt + kt), 1.0, 0.0).astype(BF16)
        acc_ref[...] += _dot(onehot, zbuf[slot].astype(BF16))
        return carry

    lax.fori_loop(0, n_chunks, chunk, 0)
    o_ref[...] = x1_ref[...] + mod_ref[0, 5:6, :] * acc_ref[...]


def _combine(w0, nc, z, x1, first_block, mods, qt, kt, cond_of_block):
    n = qt.shape[0]
    d = x1.shape[1]
    tb = TOKEN_BLOCK
    grid_spec = pltpu.PrefetchScalarGridSpec(
        num_scalar_prefetch=3,
        grid=(n // tb,),
        in_specs=[pl.BlockSpec(memory_space=pl.ANY),
                  pl.BlockSpec((tb, d), lambda i, *_: (i + first_block, 0)),
                  pl.BlockSpec((1, N_MOD, d), lambda i, *_: (cond_of_block(i), 0, 0)),
                  pl.BlockSpec((tb, 1), lambda i, *_: (i, 0)),
                  pl.BlockSpec((tb, 1), lambda i, *_: (i, 0))],
        out_specs=pl.BlockSpec((tb, d), lambda i, *_: (i, 0)),
        scratch_shapes=[pltpu.VMEM((2, Z_CHUNK, d), F32), pltpu.SemaphoreType.DMA((2,)), pltpu.VMEM((tb, d), F32)],
    )
    return pl.pallas_call(
        _combine_kernel,
        grid_spec=grid_spec,
        out_shape=jax.ShapeDtypeStruct((n, d), F32),
        compiler_params=_params(("arbitrary",)),
        name="combine",
    )(w0, nc, jnp.cumsum(nc) - nc, z, x1, mods, qt, kt)


def _routing_tables(aff, cap, z_off):
    n = aff.shape[0]
    aff_t = aff.T
    sel, rinc = _route(aff_t, cap)
    slots = jnp.arange(cap, dtype=I32)
    chunk = 128
    rinc3 = rinc.reshape(rinc.shape[0], n // chunk, chunk)
    n_full = jnp.sum((rinc3[:, None, :, chunk - 1] <= slots[None, :, None]).astype(I32), axis=-1)
    pick = (n_full[:, :, None] == jnp.arange(n // chunk, dtype=I32)[None, None, :]).astype(F32)
    inside = jnp.einsum("eca,eab->ecb", pick, rinc3.astype(F32), precision=lax.Precision.HIGHEST)
    idx = n_full * chunk + jnp.sum((inside <= slots[None, :, None].astype(F32)).astype(I32), axis=-1)
    gate = jnp.take_along_axis(aff_t, idx, axis=1)
    k_tok = jnp.sum(sel, axis=0)
    q_tok = jnp.cumsum(k_tok) - k_tok + z_off
    before = jnp.cumsum(sel, axis=0) - sel
    qpos = jnp.take_along_axis(q_tok[None, :] + before, idx, axis=1)
    nblk = n // TOKEN_BLOCK
    q_blk = q_tok[::TOKEN_BLOCK]
    end_blk = q_blk + jnp.sum(k_tok.reshape(nblk, TOKEN_BLOCK), axis=1)
    w0 = (q_blk // ROW_ALIGN) * ROW_ALIGN
    nc = jnp.maximum((end_blk - w0 + Z_CHUNK - 1) // Z_CHUNK, 1)
    return idx, gate, qpos.astype(I32), w0.astype(I32), nc.astype(I32), q_tok.astype(I32)[:, None], k_tok.astype(I32)[:, None]


def kernel(x_prompt, x_sample, cache_k, cache_v, c, c_ctx, norm1_g, norm2_g, w_ada, b_ada, w_in, q_norm_g, k_norm_g,
           rpb, w_att_proj, w_pool, pool_scale, w_out, w_router, w_gate_e, w_up_e, w_down_e):
    assert w_ada.shape[0] == 1, "single trunk layer"
    nb, ls, d = x_prompt.shape
    db, ll, _ = x_sample.shape
    n_c, n_l = nb * ls, db * ll
    lc = cache_k.shape[2]

    cond = jnp.concatenate([c_ctx[None, :], c, jnp.zeros((16 - 1 - db, d), F32)], axis=0)
    mods = _ada(cond, w_ada[0], b_ada).reshape(16, N_MOD, d)

    w_in_b = w_in[0].astype(BF16)
    w_att_b = w_att_proj[0].astype(BF16)
    w_pool_b = w_pool[0].astype(BF16)
    w_out_b = w_out[0].astype(BF16)

    cond_ctx = lambda i: 0
    lat_blocks = ll // TOKEN_BLOCK
    cond_lat = lambda i: 1 + i // lat_blocks

    xc = x_prompt.reshape(n_c, d)
    xl = x_sample.reshape(n_l, d)
    q_gain = jnp.tile(q_norm_g, (1, NA_HEADS))
    k_gain = jnp.tile(k_norm_g, (1, NA_HEADS))
    qc, kc, vc, uc, gac, gbc = _in_proj(xc, mods, norm1_g, w_in_b, q_gain, k_gain, cond_ctx, F32)
    ql, kl, vl, ul, gal, gbl = _in_proj(xl, mods, norm1_g, w_in_b, q_gain, k_gain, cond_lat, BF16)

    seq_c = lambda a: a.reshape(nb, ls, a.shape[-1])
    seq_l = lambda a: a.reshape(db, ll, a.shape[-1])
    att_c = _attn_ctx(seq_c(qc), seq_c(kc), seq_c(vc))
    bias = _window_bias(rpb[0], ll // GRID_W)
    att_l = _attn_lat(seq_l(ql), seq_l(kl), seq_l(vl), jnp.transpose(cache_k[:, 0], (0, 2, 3, 1)),
                      jnp.transpose(cache_v[:, 0], (0, 2, 3, 1)), bias)
    pool_c = _pool(seq_c(uc)).reshape(n_c, POOL_WIDTH)
    pool_l = _pool(seq_l(ul)).reshape(n_l, POOL_WIDTH)

    x1, h2, aff = _merge((xc, att_c.reshape(n_c, NA_WIDTH), pool_c, gac, gbc),
                         (xl, att_l.reshape(n_l, NA_WIDTH), pool_l, gal, gbl),
                         mods, w_att_b, w_pool_b, pool_scale, w_out_b, norm2_g, w_router[0], lat_blocks)

    cap_c = max(1, (CAPACITY_FACTOR * n_c) // N_EXPERTS)
    cap_l = max(1, (CAPACITY_FACTOR * n_l) // N_EXPERTS)
    m_tot = cap_c + cap_l
    m_iss = _moe_step_rows(m_tot, pl.cdiv(D_EXPERT, EXPERT_F_BLOCK)) * pl.cdiv(D_EXPERT, EXPERT_F_BLOCK)
    z_valid = N_EXPERTS * m_tot
    z_zero_rows = Z_CHUNK + ROW_ALIGN
    z_spare = z_valid + z_zero_rows
    z_rows = z_spare + m_iss
    idx_c, gate_c, qpos_c, w0_c, nc_c, qt_c, kt_c = _routing_tables(aff[:n_c], cap_c, 0)
    idx_l, gate_l, qpos_l, w0_l, nc_l, qt_l, kt_l = _routing_tables(aff[n_c:], cap_l, N_EXPERTS * cap_c)
    spare = jnp.broadcast_to(z_spare + jnp.arange(m_iss, dtype=I32), (N_EXPERTS + 1, m_iss))
    gidx = jnp.concatenate([idx_c, idx_l + n_c, jnp.zeros((N_EXPERTS, m_iss - m_tot), I32)], axis=1).reshape(-1)
    qpos = jnp.concatenate([spare[:1], jnp.concatenate([qpos_c, qpos_l, spare[1:, m_tot:]], axis=1)], axis=0).reshape(-1)
    gate = jnp.concatenate([gate_c, gate_l], axis=1)[:, :, None]

    z = _moe(gidx, qpos, h2, gate, jnp.swapaxes(w_gate_e[0], 1, 2), jnp.swapaxes(w_up_e[0], 1, 2), w_down_e[0],
             m_tot, z_rows, z_zero_rows)

    y_c = _combine(w0_c, nc_c, z, x1, 0, mods, qt_c, kt_c, cond_ctx)
    y_l = _combine(w0_l, nc_l, z, x1, n_c // TOKEN_BLOCK, mods, qt_l, kt_l, cond_lat)

    state_k = kc.reshape(nb, 1, ls, NA_HEADS, HEAD_DIM)
    state_v = vc.reshape(nb, 1, ls, NA_HEADS, HEAD_DIM)
    return (y_c.reshape(nb, ls, d), y_l.reshape(db, ll, d), state_k, state_v)
```

```python
import functools

import jax
import jax.numpy as jnp
from jax import lax
from jax.experimental import pallas as pl
from jax.experimental.pallas import tpu as pltpu

F32 = jnp.float32
BF16 = jnp.bfloat16
I32 = jnp.int32

D_MODEL = 1024
GRID_W = 64
NA_HEADS = 8
HEAD_DIM = 64
NA_WIDTH = NA_HEADS * HEAD_DIM
WIN_ROWS = 8
WIN_COLS = 16
POOL_WINDOWS = (2, 4, 8, 16)
POOL_GROUP_DIM = 128
POOL_WIDTH = 512
POOL_OUT_DIM = 256
N_EXPERTS = 16
CAPACITY_FACTOR = 2
D_EXPERT = 2752
N_MOD = 6
EPS = 1e-6
NEG_INF = -1e30

TOKEN_BLOCK = 256
LAT_ROW_BLOCK = 4
EXPERT_F_BLOCK = 256
EXPERT_M_BLOCK = 768
Z_CHUNK = 256
ROW_ALIGN = 8
VMEM_LIMIT = 56 * 1024 * 1024


def _dot(a, b):
    return jnp.dot(a, b, preferred_element_type=F32)


def _dot_nt(a, b):
    return lax.dot_general(a, b, (((1,), (1,)), ((), ())), preferred_element_type=F32)


def _dot3(a, b):
    a_hi = a.astype(BF16)
    a_lo = (a - a_hi.astype(F32)).astype(BF16)
    b_hi = b.astype(BF16)
    b_lo = (b - b_hi.astype(F32)).astype(BF16)
    return _dot(a_hi, b_hi) + _dot(a_hi, b_lo) + _dot(a_lo, b_hi)


def _sigmoid(x):
    return 0.5 * jnp.tanh(0.5 * x) + 0.5


def _rms(x, g):
    return x * lax.rsqrt(jnp.mean(x * x, axis=-1, keepdims=True) + EPS) * g


def _params(sem):
    return pltpu.CompilerParams(dimension_semantics=sem, vmem_limit_bytes=VMEM_LIMIT)


def _ada_kernel(cond_ref, w_ref, b_ref, o_ref):
    c = cond_ref[...]
    o_ref[...] = _dot3(c * _sigmoid(c), w_ref[...]) + b_ref[...]


def _ada(cond, w_ada, b_ada):
    rows, d = cond.shape
    n_out = w_ada.shape[1]
    bn = 512
    return pl.pallas_call(
        _ada_kernel,
        grid=(n_out // bn,),
        in_specs=[pl.BlockSpec((rows, d), lambda i: (0, 0)),
                  pl.BlockSpec((d, bn), lambda i: (0, i)),
                  pl.BlockSpec((1, bn), lambda i: (0, i))],
        out_specs=pl.BlockSpec((rows, bn), lambda i: (0, i)),
        out_shape=jax.ShapeDtypeStruct((rows, n_out), F32),
        compiler_params=_params(("arbitrary",)),
        name="ada",
    )(cond, w_ada, b_ada)


_IN_CUTS = (0, 512, 1024, 1536, 2048, 3072, 4096)


HEAD_AVG_WIDTH = 256


def _head_avg_matrix():
    head = jnp.arange(HEAD_AVG_WIDTH, dtype=I32) // HEAD_DIM
    return jnp.where(head[:, None] == head[None, :], 1.0 / HEAD_DIM, 0.0).astype(BF16)


def _head_rms(x, g, avg, two_term):
    x2 = x * x
    means = []
    for c in range(0, x.shape[1], HEAD_AVG_WIDTH):
        blk = x2[:, c:c + HEAD_AVG_WIDTH]
        hi = blk.astype(BF16)
        ms = _dot(hi, avg)
        if two_term:
            ms = ms + _dot((blk - hi.astype(F32)).astype(BF16), avg)
        means.append(ms)
    return x * lax.rsqrt(jnp.concatenate(means, axis=-1) + EPS) * g


def _in_kernel(x_ref, mod_ref, g_ref, w_ref, qg_ref, kg_ref, avg_ref, q_ref, k_ref, v_ref, u_ref, ga_ref, gb_ref,
               *, kv_channel_major):
    y = _rms(x_ref[...], g_ref[...])
    h = (y * (1.0 + mod_ref[0, 1:2, :]) + mod_ref[0, 0:1, :]).astype(BF16)
    part = lambda i: _dot(h, w_ref[:, _IN_CUTS[i]:_IN_CUTS[i + 1]])
    q_ref[...] = (_head_rms(part(0), qg_ref[...], avg_ref[...], False) * HEAD_DIM ** -0.5).astype(q_ref.dtype)
    kn = _head_rms(part(1), kg_ref[...], avg_ref[...], True)
    v = part(2)
    if kv_channel_major:
        k_ref[0] = kn.T.reshape(k_ref.shape[1:])
        v_ref[0] = v.T.reshape(v_ref.shape[1:])
    else:
        k_ref[...] = kn.astype(k_ref.dtype)
        v_ref[...] = v.astype(v_ref.dtype)
    u_ref[...] = part(3)
    ga_ref[...] = _sigmoid(part(4)).astype(ga_ref.dtype)
    gb_ref[...] = _sigmoid(part(5)).astype(gb_ref.dtype)


def _in_proj(x2d, mods, norm_g, w_in_bf16, q_gain, k_gain, cond_of_block, kv_results):
    n, d = x2d.shape
    tb = TOKEN_BLOCK
    widths = [hi - lo for lo, hi in zip(_IN_CUTS[:-1], _IN_CUTS[1:])]
    dtypes = [BF16, BF16, BF16, F32, BF16, BF16]
    out_specs = [pl.BlockSpec((tb, w), lambda i: (i, 0)) for w in widths]
    out_shape = [jax.ShapeDtypeStruct((n, w), t) for w, t in zip(widths, dtypes)]
    if kv_results:
        for i in (1, 2):
            out_specs[i] = pl.BlockSpec((1, NA_HEADS, HEAD_DIM, tb), lambda i: (i, 0, 0, 0))
            out_shape[i] = jax.ShapeDtypeStruct((n // tb, NA_HEADS, HEAD_DIM, tb), F32)
    full = lambda a: pl.BlockSpec(a.shape, lambda i: (0,) * a.ndim)
    return pl.pallas_call(
        functools.partial(_in_kernel, kv_channel_major=kv_results),
        grid=(n // tb,),
        in_specs=[pl.BlockSpec((tb, d), lambda i: (i, 0)),
                  pl.BlockSpec((1, N_MOD, d), lambda i: (cond_of_block(i), 0, 0)),
                  full(norm_g), full(w_in_bf16), full(q_gain), full(k_gain),
                  pl.BlockSpec((HEAD_AVG_WIDTH, HEAD_AVG_WIDTH), lambda i: (0, 0))],
        out_specs=out_specs,
        out_shape=out_shape,
        compiler_params=_params(("arbitrary",)),
        name="in_proj",
    )(x2d, mods, norm_g, w_in_bf16, q_gain, k_gain, _head_avg_matrix())


def _with_ones(v):
    lane = lax.broadcasted_iota(I32, v.shape, 1)
    return jnp.concatenate([v.astype(BF16), jnp.where(lane == 0, 1.0, 0.0).astype(BF16)], axis=-1)


def _attn_ctx_kernel(q_ref, kt_ref, vt_ref, att_ref):
    qn = q_ref[0]
    l = kt_ref.shape[3]
    ones_row = jnp.where(lax.broadcasted_iota(I32, (HEAD_DIM, l), 0) == 0, 1.0, 0.0)
    for h in range(NA_HEADS):
        sl = slice(h * HEAD_DIM, (h + 1) * HEAD_DIM)
        s = _dot(qn[:, sl], kt_ref[0, h].astype(BF16))
        e = jnp.exp(s - jnp.max(s, axis=-1, keepdims=True)).astype(BF16)
        vt = jnp.concatenate([vt_ref[0, h], ones_row], axis=0).astype(BF16)
        o = _dot_nt(e, vt)
        att_ref[0, :, sl] = (o[:, :HEAD_DIM] / o[:, HEAD_DIM:HEAD_DIM + 1]).astype(att_ref.dtype)


def _attn_ctx(q, kt, vt):
    b, l, w = q.shape
    blk = pl.BlockSpec((1, l, w), lambda i: (i, 0, 0))
    tblk = pl.BlockSpec((1,) + kt.shape[1:], lambda i: (i, 0, 0, 0))
    return pl.pallas_call(
        _attn_ctx_kernel,
        grid=(b,),
        in_specs=[blk, tblk, tblk],
        out_specs=blk,
        out_shape=jax.ShapeDtypeStruct((b, l, w), BF16),
        compiler_params=_params(("arbitrary",)),
        name="attn_ctx",
    )(q, kt, vt)


def _lat_windows(rows):
    kr = min(WIN_ROWS, rows)
    rb = min(LAT_ROW_BLOCK, rows)
    wr = min(kr + rb, rows)
    assert rows % rb == 0
    starts = [min(max(i * rb - kr // 2, 0), rows - wr) for i in range(rows // rb)]
    for i, ws in enumerate(starts):
        for r in range(i * rb, (i + 1) * rb):
            rs = min(max(r - kr // 2, 0), rows - kr)
            assert ws <= rs and rs + kr <= ws + wr
    return kr, rb, wr, starts


def _attn_lat_kernel(q_ref, k_ref, v_ref, ckt_ref, cvt_ref, bias_ref, att_ref,
                     kn_s, vb_s, ckb_s, cvb_s, *, rows, kr, rb, wr):
    i = pl.program_id(1)

    @pl.when(i == 0)
    def _():
        lc = ckt_ref.shape[3]
        ones_row = jnp.where(lax.broadcasted_iota(I32, (HEAD_DIM, lc), 0) == 0, 1.0, 0.0)
        for h in range(NA_HEADS):
            sl = slice(h * HEAD_DIM, (h + 1) * HEAD_DIM)
            kn_s[h] = k_ref[0, :, sl]
            vb_s[h] = _with_ones(v_ref[0, :, sl])
            ckb_s[h] = ckt_ref[0, h].astype(BF16)
            cvb_s[h] = jnp.concatenate([cvt_ref[0, h], ones_row], axis=0).astype(BF16)

    win_start = jnp.clip(i * rb - kr // 2, 0, rows - wr)
    key_rows = pl.ds(pl.multiple_of(win_start * GRID_W, GRID_W), wr * GRID_W)
    qn_all = q_ref[0]
    for h in range(NA_HEADS):
        sl = slice(h * HEAD_DIM, (h + 1) * HEAD_DIM)
        qn = qn_all[:, sl]
        bias = bias_ref[h, pl.ds(i * rb, rb)].reshape(rb * GRID_W, wr * GRID_W)
        s_loc = _dot_nt(qn, kn_s[h, key_rows, :]) + bias
        s_ctx = _dot(qn, ckb_s[h])
        m = jnp.maximum(jnp.max(s_loc, axis=-1, keepdims=True), jnp.max(s_ctx, axis=-1, keepdims=True))
        o = (_dot(jnp.exp(s_loc - m).astype(BF16), vb_s[h, key_rows, :])
             + _dot_nt(jnp.exp(s_ctx - m).astype(BF16), cvb_s[h]))
        att_ref[0, :, sl] = (o[:, :HEAD_DIM] / o[:, HEAD_DIM:HEAD_DIM + 1]).astype(att_ref.dtype)


def _attn_lat(q, k, v, ckt, cvt, bias):
    b, l, w = q.shape
    lc = ckt.shape[3]
    rows = l // GRID_W
    kr, rb, wr, _ = _lat_windows(rows)
    seq = pl.BlockSpec((1, l, w), lambda i, r: (i, 0, 0))
    ctx = pl.BlockSpec((1, NA_HEADS, HEAD_DIM, lc), lambda i, r: (i, 0, 0, 0))
    rowblk = pl.BlockSpec((1, rb * GRID_W, w), lambda i, r: (i, r, 0))
    return pl.pallas_call(
        functools.partial(_attn_lat_kernel, rows=rows, kr=kr, rb=rb, wr=wr),
        grid=(b, rows // rb),
        in_specs=[rowblk, seq, seq, ctx, ctx,
                  pl.BlockSpec(bias.shape, lambda i, r: (0, 0, 0, 0), pipeline_mode=pl.Buffered(1))],
        out_specs=rowblk,
        out_shape=jax.ShapeDtypeStruct((b, l, w), BF16),
        scratch_shapes=[pltpu.VMEM((NA_HEADS, l, HEAD_DIM), BF16), pltpu.VMEM((NA_HEADS, l, 2 * HEAD_DIM), BF16),
                        pltpu.VMEM((NA_HEADS, HEAD_DIM, lc), BF16), pltpu.VMEM((NA_HEADS, 2 * HEAD_DIM, lc), BF16)],
        compiler_params=_params(("arbitrary", "arbitrary")),
        name="attn_lat",
    )(q, k, v, ckt, cvt, bias)


def _bias_kernel(rpb_ref, o_ref, *, rows):
    kr, rb, wr, starts = _lat_windows(rows)
    h = pl.program_id(0)
    n_ro, n_co = 2 * WIN_ROWS - 1, 2 * WIN_COLS - 1
    wq = lax.broadcasted_iota(I32, (GRID_W, GRID_W), 0)
    wk = lax.broadcasted_iota(I32, (GRID_W, GRID_W), 1)
    co = jnp.clip(wk - wq, -(WIN_COLS - 1), WIN_COLS - 1) + (WIN_COLS - 1)
    col_start = jnp.clip(wq - WIN_COLS // 2, 0, GRID_W - WIN_COLS)
    in_win = (wk >= col_start) & (wk < col_start + WIN_COLS)
    co = jnp.where(in_win, co, -1)
    masked = jnp.full((GRID_W, GRID_W), NEG_INF, F32)
    tables = {}
    for r in range(rows):
        row_start = min(max(r - kr // 2, 0), rows - kr)
        for j in range(wr):
            key_row = starts[r // rb] + j
            blk = masked
            if row_start <= key_row < row_start + kr:
                ro = key_row - r + (WIN_ROWS - 1)
                if ro not in tables:
                    t = masked
                    for c in range(n_co):
                        t = jnp.where(co == c, rpb_ref[(h * n_ro + ro) * n_co + c], t)
                    tables[ro] = t
                blk = tables[ro]
            o_ref[0, r, :, j * GRID_W:(j + 1) * GRID_W] = blk


def _window_bias(rpb, rows):
    _, _, wr, _ = _lat_windows(rows)
    nh = rpb.shape[0]
    return pl.pallas_call(
        functools.partial(_bias_kernel, rows=rows),
        grid=(nh,),
        in_specs=[pl.BlockSpec(memory_space=pltpu.SMEM)],
        out_specs=pl.BlockSpec((1, rows, GRID_W, wr * GRID_W), lambda h: (h, 0, 0, 0)),
        out_shape=jax.ShapeDtypeStruct((nh, rows, GRID_W, wr * GRID_W), F32),
        compiler_params=_params(("arbitrary",)),
        name="window_bias",
    )(rpb.reshape(-1))


def _pool_kernel(u_ref, o_ref):
    l = u_ref.shape[1]
    t = lax.broadcasted_iota(I32, (l, POOL_GROUP_DIM), 0)
    for g, w in enumerate(POOL_WINDOWS):
        sl = slice(g * POOL_GROUP_DIM, (g + 1) * POOL_GROUP_DIM)
        x = u_ref[0, :, sl]
        acc = x
        for k in range(-(w // 2), w - w // 2):
            if k == 0:
                continue
            shifted = pltpu.roll(x, (-k) % l, 0)
            acc = acc + jnp.where((t + k >= 0) & (t + k < l), shifted, 0.0)
        cnt = (jnp.minimum(t + (w - w // 2), l) - jnp.maximum(t - w // 2, 0)).astype(F32)
        o_ref[0, :, sl] = (acc / cnt - x).astype(o_ref.dtype)


def _pool(u):
    b, l, w = u.shape
    blk = pl.BlockSpec((1, l, w), lambda i: (i, 0, 0))
    return pl.pallas_call(
        _pool_kernel, grid=(b,), in_specs=[blk], out_specs=blk,
        out_shape=jax.ShapeDtypeStruct((b, l, w), BF16),
        compiler_params=_params(("arbitrary",)),
        name="pool",
    )(u)


def _merge_kernel(xc_ref, xl_ref, attc_ref, attl_ref, plc_ref, pll_ref, gac_ref, gal_ref, gbc_ref, gbl_ref,
                  mod_ref, watt_ref, wpool_ref, ps_ref, wout_ref, g2_ref, wr_ref, x1_ref, h2_ref, aff_ref, *, ctx_blocks):
    is_ctx = pl.program_id(0) < ctx_blocks
    pick = lambda c_ref, l_ref: jnp.where(is_ctx, c_ref[...], l_ref[...])
    o_a = _dot(pick(attc_ref, attl_ref), watt_ref[...])
    pooled = pick(plc_ref, pll_ref)
    o_b = jnp.concatenate(
        [_dot(pooled[:, g * POOL_GROUP_DIM:(g + 1) * POOL_GROUP_DIM], wpool_ref[g])
         for g in range(len(POOL_WINDOWS))], axis=-1) * ps_ref[...]
    merged = pick(gac_ref, gal_ref) * o_a + pick(gbc_ref, gbl_ref) * o_b
    x1 = pick(xc_ref, xl_ref) + mod_ref[0, 2:3, :] * _dot(merged.astype(BF16), wout_ref[...])
    x1_ref[...] = x1
    h2 = _rms(x1, g2_ref[...]) * (1.0 + mod_ref[0, 4:5, :]) + mod_ref[0, 3:4, :]
    h2_ref[...] = h2
    logits = _dot3(h2, wr_ref[...])
    e = jnp.exp(logits - jnp.max(logits, axis=-1, keepdims=True))
    aff_ref[...] = e / jnp.sum(e, axis=-1, keepdims=True)


def _merge(ctx, lat, mods, w_att, w_pool, pool_scale, w_out, norm2_g, w_router, lat_blocks_per_seq):
    n_c, d = ctx[0].shape
    n = n_c + lat[0].shape[0]
    tb = TOKEN_BLOCK
    cb = n_c // tb
    crow = lambda w: pl.BlockSpec((tb, w), lambda i: (jnp.minimum(i, cb - 1), 0))
    lrow = lambda w: pl.BlockSpec((tb, w), lambda i: (jnp.maximum(i - cb, 0), 0))
    orow = lambda w: pl.BlockSpec((tb, w), lambda i: (i, 0))
    full = lambda a: pl.BlockSpec(a.shape, lambda i: (0,) * a.ndim)
    cond = lambda i: jnp.where(i < cb, 0, 1 + (i - cb) // lat_blocks_per_seq)
    pairs, specs = [], []
    for a_c, a_l in zip(ctx, lat):
        pairs += [a_c, a_l]
        specs += [crow(a_c.shape[1]), lrow(a_l.shape[1])]
    return pl.pallas_call(
        functools.partial(_merge_kernel, ctx_blocks=cb),
        grid=(n // tb,),
        in_specs=specs + [pl.BlockSpec((1, N_MOD, d), lambda i: (cond(i), 0, 0)),
                          full(w_att), full(w_pool), full(pool_scale), full(w_out), full(norm2_g), full(w_router)],
        out_specs=[orow(d), orow(d), orow(N_EXPERTS)],
        out_shape=[jax.ShapeDtypeStruct((n, d), F32), jax.ShapeDtypeStruct((n, d), F32),
                   jax.ShapeDtypeStruct((n, N_EXPERTS), F32)],
        compiler_params=_params(("arbitrary",)),
        name="merge",
    )(*pairs, mods, w_att, w_pool, pool_scale, w_out, norm2_g, w_router)


def _prefix_incl(mask_f32, out_ref):
    e, n = mask_f32.shape
    i = lax.broadcasted_iota(I32, (128, 128), 0)
    j = lax.broadcasted_iota(I32, (128, 128), 1)
    tri = jnp.where(i <= j, 1.0, 0.0).astype(BF16)
    carry = jnp.zeros((e, 1), F32)
    for c in range(n // 128):
        inc = _dot(mask_f32[:, c * 128:(c + 1) * 128].astype(BF16), tri) + carry
        out_ref[:, c * 128:(c + 1) * 128] = inc.astype(I32)
        carry = inc[:, 127:128]


def _route_kernel(aff_ref, sel_ref, rinc_ref, tmp_ref, *, cap):
    aff = aff_ref[...]

    def bit_step(i, thr):
        cand = thr | jnp.left_shift(jnp.int32(1), 30 - i)
        cnt = jnp.sum(jnp.where(aff >= pltpu.bitcast(cand, F32), 1, 0), axis=1, keepdims=True)
        return jnp.where(cnt >= cap, cand, thr)

    thr = lax.fori_loop(0, 31, bit_step, jnp.zeros((aff.shape[0], 1), I32))
    gt = aff >= pltpu.bitcast(thr + 1, F32)
    eq = (aff >= pltpu.bitcast(thr, F32)) & jnp.logical_not(gt)
    need = cap - jnp.sum(jnp.where(gt, 1, 0), axis=1, keepdims=True)
    _prefix_incl(jnp.where(eq, 1.0, 0.0), tmp_ref)
    sel = gt | (eq & (tmp_ref[...] <= need))
    sel_ref[...] = jnp.where(sel, 1, 0)
    _prefix_incl(jnp.where(sel, 1.0, 0.0), rinc_ref)


def _route(aff_t, cap):
    e, n = aff_t.shape
    full = pl.BlockSpec((e, n), lambda: (0, 0))
    return pl.pallas_call(
        functools.partial(_route_kernel, cap=cap),
        in_specs=[full], out_specs=[full, full],
        out_shape=[jax.ShapeDtypeStruct((e, n), I32)] * 2,
        scratch_shapes=[pltpu.VMEM((e, n), I32)],
        compiler_params=pltpu.CompilerParams(vmem_limit_bytes=VMEM_LIMIT),
        name="route",
    )(aff_t)


def _moe_kernel(gidx_ref, qpos_ref, h_ref, gate_ref, wg_ref, wu_ref, wd_ref, z_ref,
                xf_ref, xb_ref, acc_ref, y_ref, sem_g, sem_s, *, m_tot, n_f, step_rows, z_zero_rows):
    e = pl.program_id(0)
    j = pl.program_id(1)
    n_e = pl.num_programs(0)
    m_iss = step_rows * n_f
    tf = wg_ref.shape[1]
    slot = e % 2

    def gather_row(lst, s, dst_slot):
        t = gidx_ref[lst * m_iss + s]
        return pltpu.make_async_copy(h_ref.at[pl.ds(t, 1)], xf_ref.at[dst_slot, pl.ds(s, 1)], sem_g.at[dst_slot])

    def scatter_row(lst, s):
        q = qpos_ref[lst * m_iss + s]
        return pltpu.make_async_copy(y_ref.at[pl.ds(s, 1)], z_ref.at[pl.ds(q, 1)], sem_s)

    def wait_gathers(dst_slot):
        pltpu.make_async_copy(h_ref.at[pl.ds(0, m_iss)], xf_ref.at[dst_slot], sem_g.at[dst_slot]).wait()

    def wait_scatters():
        pltpu.make_async_copy(y_ref, z_ref.at[pl.ds(0, m_iss)], sem_s).wait()

    @pl.when((e == 0) & (j == 0))
    def _():
        y_ref[...] = jnp.zeros_like(y_ref)
        cp = pltpu.make_async_copy(y_ref.at[pl.ds(0, z_zero_rows)], z_ref.at[pl.ds(n_e * m_tot, z_zero_rows)], sem_s)
        cp.start()
        cp.wait()

        def first(s, c):
            gather_row(0, s, 0).start()
            return c

        lax.fori_loop(0, m_iss, first, 0)

    @pl.when(j == 0)
    def _():
        wait_gathers(slot)
        xb_ref[...] = xf_ref[slot, 0:m_tot, :].astype(BF16)
        acc_ref[...] = jnp.zeros_like(acc_ref)

    nxt = lax.rem(e + 1, n_e)
    for u in range(step_rows):
        s = j * step_rows + u
        gather_row(nxt, s, 1 - slot).start()
        scatter_row(e, s).start()

    valid = D_EXPERT - j * tf
    cmask = lax.broadcasted_iota(I32, (1, tf), 1) < valid
    rmask = lax.broadcasted_iota(I32, (tf, 1), 0) < valid
    wg = wg_ref[0].astype(BF16)
    wu = wu_ref[0].astype(BF16)
    wd = jnp.where(rmask, wd_ref[0], 0.0).astype(BF16)
    for m in range(m_tot // EXPERT_M_BLOCK):
        rs = slice(m * EXPERT_M_BLOCK, (m + 1) * EXPERT_M_BLOCK)
        xs = xb_ref[rs, :]
        a = _dot_nt(xs, wg)
        b = _dot_nt(xs, wu)
        hm = jnp.where(cmask, a * _sigmoid(a) * b, 0.0).astype(BF16)
        acc_ref[rs, :] += _dot(hm, wd)

    @pl.when(j == n_f - 1)
    def _():
        wait_scatters()
        y_ref[0:m_tot, :] = acc_ref[...] * gate_ref[0]

        @pl.when(e == n_e - 1)
        def _():
            def last(s, c):
                scatter_row(n_e, s).start()
                return c

            lax.fori_loop(0, m_iss, last, 0)
            wait_scatters()
            wait_gathers(1 - slot)


def _moe_step_rows(m_tot, n_f):
    return pl.cdiv(pl.cdiv(m_tot, n_f), ROW_ALIGN) * ROW_ALIGN


def _moe(gidx, qpos, h, gate, w_gate, w_up, w_down, m_tot, z_rows, z_zero_rows):
    d = h.shape[1]
    n_e = w_down.shape[0]
    tf = EXPERT_F_BLOCK
    n_f = pl.cdiv(D_EXPERT, tf)
    step_rows = _moe_step_rows(m_tot, n_f)
    m_iss = step_rows * n_f
    grid_spec = pltpu.PrefetchScalarGridSpec(
        num_scalar_prefetch=2,
        grid=(n_e, n_f),
        in_specs=[pl.BlockSpec(memory_space=pl.ANY),
                  pl.BlockSpec((1, m_tot, 1), lambda e, j, *_: (e, 0, 0)),
                  pl.BlockSpec((1, tf, d), lambda e, j, *_: (e, j, 0)),
                  pl.BlockSpec((1, tf, d), lambda e, j, *_: (e, j, 0)),
                  pl.BlockSpec((1, tf, d), lambda e, j, *_: (e, j, 0))],
        out_specs=pl.BlockSpec(memory_space=pl.ANY),
        scratch_shapes=[pltpu.VMEM((2, m_iss, d), F32), pltpu.VMEM((m_tot, d), BF16), pltpu.VMEM((m_tot, d), F32),
                        pltpu.VMEM((m_iss, d), F32), pltpu.SemaphoreType.DMA((2,)), pltpu.SemaphoreType.DMA],
    )
    return pl.pallas_call(
        functools.partial(_moe_kernel, m_tot=m_tot, n_f=n_f, step_rows=step_rows, z_zero_rows=z_zero_rows),
        grid_spec=grid_spec,
        out_shape=jax.ShapeDtypeStruct((z_rows, d), F32),
        compiler_params=_params(("arbitrary", "arbitrary")),
        name="moe",
    )(gidx, qpos, h, gate, w_gate, w_up, w_down)


def _combine_kernel(w0_ref, nc_ref, first_ref, z_ref, x1_ref, mod_ref, qt_ref, kt_ref, o_ref, zbuf, sem, acc_ref):
    blk = pl.program_id(0)
    n_blk = pl.num_programs(0)
    w0 = w0_ref[blk]
    n_chunks = nc_ref[blk]
    first = first_ref[blk]
    qt = qt_ref[...]
    kt = kt_ref[...]
    acc_ref[...] = jnp.zeros_like(acc_ref)

    def z_copy(start, slot):
        rows = pl.ds(pl.multiple_of(start, ROW_ALIGN), Z_CHUNK)
        return pltpu.make_async_copy(z_ref.at[rows], zbuf.at[slot], sem.at[slot])

    @pl.when(blk == 0)
    def _():
        z_copy(w0, first % 2).start()

    def chunk(c, carry):
        slot = (first + c) % 2
        last = c + 1 == n_chunks
        next_start = jnp.where(last, w0_ref[jnp.minimum(blk + 1, n_blk - 1)], w0 + (c + 1) * Z_CHUNK)

        @pl.when(jnp.logical_not(last) | (blk + 1 < n_blk))
        def _():
            z_copy(next_start, 1 - slot).start()

        z_copy(w0 + c * Z_CHUNK, slot).wait()
        r = w0 + c * Z_CHUNK + lax.broadcasted_iota(I32, (qt.shape[0], Z_CHUNK), 1)
        onehot = jnp.where((r >= qt) & (r < qt + kt), 1.0, 0.0).astype(BF16)
        acc_ref[...] += _dot(onehot, zbuf[slot].astype(BF16))
        return carry

    lax.fori_loop(0, n_chunks, chunk, 0)
    o_ref[...] = x1_ref[...] + mod_ref[0, 5:6, :] * acc_ref[...]


def _combine(w0, nc, z, x1, first_block, mods, qt, kt, cond_of_block):
    n = qt.shape[0]
    d = x1.shape[1]
    tb = TOKEN_BLOCK
    grid_spec = pltpu.PrefetchScalarGridSpec(
        num_scalar_prefetch=3,
        grid=(n // tb,),
        in_specs=[pl.BlockSpec(memory_space=pl.ANY),
                  pl.BlockSpec((tb, d), lambda i, *_: (i + first_block, 0)),
                  pl.BlockSpec((1, N_MOD, d), lambda i, *_: (cond_of_block(i), 0, 0)),
                  pl.BlockSpec((tb, 1), lambda i, *_: (i, 0)),
                  pl.BlockSpec((tb, 1), lambda i, *_: (i, 0))],
        out_specs=pl.BlockSpec((tb, d), lambda i, *_: (i, 0)),
        scratch_shapes=[pltpu.VMEM((2, Z_CHUNK, d), F32), pltpu.SemaphoreType.DMA((2,)), pltpu.VMEM((tb, d), F32)],
    )
    return pl.pallas_call(
        _combine_kernel,
        grid_spec=grid_spec,
        out_shape=jax.ShapeDtypeStruct((n, d), F32),
        compiler_params=_params(("arbitrary",)),
        name="combine",
    )(w0, nc, jnp.cumsum(nc) - nc, z, x1, mods, qt, kt)


def _routing_tables(aff, cap, z_off):
    n = aff.shape[0]
    aff_t = aff.T
    sel, rinc = _route(aff_t, cap)
    slots = jnp.arange(cap, dtype=I32)
    chunk = 128
    rinc3 = rinc.reshape(rinc.shape[0], n // chunk, chunk)
    n_full = jnp.sum((rinc3[:, None, :, chunk - 1] <= slots[None, :, None]).astype(I32), axis=-1)
    pick = (n_full[:, :, None] == jnp.arange(n // chunk, dtype=I32)[None, None, :]).astype(F32)
    inside = jnp.einsum("eca,eab->ecb", pick, rinc3.astype(F32), precision=lax.Precision.HIGHEST)
    idx = n_full * chunk + jnp.sum((inside <= slots[None, :, None].astype(F32)).astype(I32), axis=-1)
    gate = jnp.take_along_axis(aff_t, idx, axis=1)
    k_tok = jnp.sum(sel, axis=0)
    q_tok = jnp.cumsum(k_tok) - k_tok + z_off
    before = jnp.cumsum(sel, axis=0) - sel
    qpos = jnp.take_along_axis(q_tok[None, :] + before, idx, axis=1)
    nblk = n // TOKEN_BLOCK
    q_blk = q_tok[::TOKEN_BLOCK]
    end_blk = q_blk + jnp.sum(k_tok.reshape(nblk, TOKEN_BLOCK), axis=1)
    w0 = (q_blk // ROW_ALIGN) * ROW_ALIGN
    nc = jnp.maximum((end_blk - w0 + Z_CHUNK - 1) // Z_CHUNK, 1)
    return idx, gate, qpos.astype(I32), w0.astype(I32), nc.astype(I32), q_tok.astype(I32)[:, None], k_tok.astype(I32)[:, None]


def kernel(x_prompt, x_sample, cache_k, cache_v, c, c_ctx, norm1_g, norm2_g, w_ada, b_ada, w_in, q_norm_g, k_norm_g,
           rpb, w_att_proj, w_pool, pool_scale, w_out, w_router, w_gate_e, w_up_e, w_down_e):
    assert w_ada.shape[0] == 1, "single trunk layer"
    nb, ls, d = x_prompt.shape
    db, ll, _ = x_sample.shape
    n_c, n_l = nb * ls, db * ll
    assert ls == TOKEN_BLOCK, "context keys / values are written one sequence per projection block"

    cond = jnp.concatenate([c_ctx[None, :], c, jnp.zeros((16 - 1 - db, d), F32)], axis=0)
    mods = _ada(cond, w_ada[0], b_ada).reshape(16, N_MOD, d)

    w_in_b = w_in[0].astype(BF16)
    w_att_b = w_att_proj[0].astype(BF16)
    w_pool_b = w_pool[0].astype(BF16)
    w_out_b = w_out[0].astype(BF16)

    cond_ctx = lambda i: 0
    lat_blocks = ll // TOKEN_BLOCK
    cond_lat = lambda i: 1 + i // lat_blocks

    xc = x_prompt.reshape(n_c, d)
    xl = x_sample.reshape(n_l, d)
    q_gain = jnp.tile(q_norm_g, (1, NA_HEADS))
    k_gain = jnp.tile(k_norm_g, (1, NA_HEADS))
    qc, kc, vc, uc, gac, gbc = _in_proj(xc, mods, norm1_g, w_in_b, q_gain, k_gain, cond_ctx, True)
    ql, kl, vl, ul, gal, gbl = _in_proj(xl, mods, norm1_g, w_in_b, q_gain, k_gain, cond_lat, False)

    seq_c = lambda a: a.reshape(nb, ls, a.shape[-1])
    seq_l = lambda a: a.reshape(db, ll, a.shape[-1])
    att_c = _attn_ctx(seq_c(qc), kc, vc)
    bias = _window_bias(rpb[0], ll // GRID_W)
    att_l = _attn_lat(seq_l(ql), seq_l(kl), seq_l(vl), jnp.transpose(cache_k[:, 0], (0, 2, 3, 1)),
                      jnp.transpose(cache_v[:, 0], (0, 2, 3, 1)), bias)
    pool_c = _pool(seq_c(uc)).reshape(n_c, POOL_WIDTH)
    pool_l = _pool(seq_l(ul)).reshape(n_l, POOL_WIDTH)

    x1, h2, aff = _merge((xc, att_c.reshape(n_c, NA_WIDTH), pool_c, gac, gbc),
                         (xl, att_l.reshape(n_l, NA_WIDTH), pool_l, gal, gbl),
                         mods, w_att_b, w_pool_b, pool_scale, w_out_b, norm2_g, w_router[0], lat_blocks)

    cap_c = max(1, (CAPACITY_FACTOR * n_c) // N_EXPERTS)
    cap_l = max(1, (CAPACITY_FACTOR * n_l) // N_EXPERTS)
    m_tot = cap_c + cap_l
    m_iss = _moe_step_rows(m_tot, pl.cdiv(D_EXPERT, EXPERT_F_BLOCK)) * pl.cdiv(D_EXPERT, EXPERT_F_BLOCK)
    z_valid = N_EXPERTS * m_tot
    z_zero_rows = Z_CHUNK + ROW_ALIGN
    z_spare = z_valid + z_zero_rows
    z_rows = z_spare + m_iss
    idx_c, gate_c, qpos_c, w0_c, nc_c, qt_c, kt_c = _routing_tables(aff[:n_c], cap_c, 0)
    idx_l, gate_l, qpos_l, w0_l, nc_l, qt_l, kt_l = _routing_tables(aff[n_c:], cap_l, N_EXPERTS * cap_c)
    spare = jnp.broadcast_to(z_spare + jnp.arange(m_iss, dtype=I32), (N_EXPERTS + 1, m_iss))
    gidx = jnp.concatenate([idx_c, idx_l + n_c, jnp.zeros((N_EXPERTS, m_iss - m_tot), I32)], axis=1).reshape(-1)
    qpos = jnp.concatenate([spare[:1], jnp.concatenate([qpos_c, qpos_l, spare[1:, m_tot:]], axis=1)], axis=0).reshape(-1)
    gate = jnp.concatenate([gate_c, gate_l], axis=1)[:, :, None]

    z = _moe(gidx, qpos, h2, gate, jnp.swapaxes(w_gate_e[0], 1, 2), jnp.swapaxes(w_up_e[0], 1, 2), w_down_e[0],
             m_tot, z_rows, z_zero_rows)

    y_c = _combine(w0_c, nc_c, z, x1, 0, mods, qt_c, kt_c, cond_ctx)
    y_l = _combine(w0_l, nc_l, z, x1, n_c // TOKEN_BLOCK, mods, qt_l, kt_l, cond_lat)

    state_k = jnp.transpose(kc, (0, 3, 1, 2))[:, None]
    state_v = jnp.transpose(vc, (0, 3, 1, 2))[:, None]
    return (y_c.reshape(nb, ls, d), y_l.reshape(db, ll, d), state_k, state_v)
```

```python
import functools

import jax
import jax.numpy as jnp
from jax import lax
from jax.experimental import pallas as pl
from jax.experimental.pallas import tpu as pltpu

F32 = jnp.float32
BF16 = jnp.bfloat16
I32 = jnp.int32

D_MODEL = 1024
GRID_W = 64
NA_HEADS = 8
HEAD_DIM = 64
NA_WIDTH = NA_HEADS * HEAD_DIM
WIN_ROWS = 8
WIN_COLS = 16
POOL_WINDOWS = (2, 4, 8, 16)
POOL_GROUP_DIM = 128
POOL_WIDTH = 512
POOL_OUT_DIM = 256
N_EXPERTS = 16
CAPACITY_FACTOR = 2
D_EXPERT = 2752
N_MOD = 6
EPS = 1e-6
NEG_INF = -1e30

TOKEN_BLOCK = 256
COMBINE_BLOCK = 512
LAT_ROW_BLOCK = 4
EXPERT_F_BLOCK = 256
EXPERT_M_BLOCK = 768
Z_CHUNK = 256
ROW_ALIGN = 8
VMEM_LIMIT = 56 * 1024 * 1024


def _dot(a, b):
    return jnp.dot(a, b, preferred_element_type=F32)


def _dot_nt(a, b):
    return lax.dot_general(a, b, (((1,), (1,)), ((), ())), preferred_element_type=F32)


def _dot3(a, b):
    a_hi = a.astype(BF16)
    a_lo = (a - a_hi.astype(F32)).astype(BF16)
    b_hi = b.astype(BF16)
    b_lo = (b - b_hi.astype(F32)).astype(BF16)
    return _dot(a_hi, b_hi) + _dot(a_hi, b_lo) + _dot(a_lo, b_hi)


def _sigmoid(x):
    return 0.5 * jnp.tanh(0.5 * x) + 0.5


def _rms(x, g):
    return x * lax.rsqrt(jnp.mean(x * x, axis=-1, keepdims=True) + EPS) * g


def _params(sem):
    return pltpu.CompilerParams(dimension_semantics=sem, vmem_limit_bytes=VMEM_LIMIT)


def _ada_kernel(cond_ref, w_ref, b_ref, o_ref):
    c = cond_ref[...]
    o_ref[...] = _dot3(c * _sigmoid(c), w_ref[...]) + b_ref[...]


def _ada(cond, w_ada, b_ada):
    rows, d = cond.shape
    n_out = w_ada.shape[1]
    bn = 512
    return pl.pallas_call(
        _ada_kernel,
        grid=(n_out // bn,),
        in_specs=[pl.BlockSpec((rows, d), lambda i: (0, 0)),
                  pl.BlockSpec((d, bn), lambda i: (0, i)),
                  pl.BlockSpec((1, bn), lambda i: (0, i))],
        out_specs=pl.BlockSpec((rows, bn), lambda i: (0, i)),
        out_shape=jax.ShapeDtypeStruct((rows, n_out), F32),
        compiler_params=_params(("arbitrary",)),
        name="ada",
    )(cond, w_ada, b_ada)


_IN_CUTS = (0, 512, 1024, 1536, 2048, 3072, 4096)


HEAD_AVG_WIDTH = 256


def _head_avg_matrix():
    head = jnp.arange(HEAD_AVG_WIDTH, dtype=I32) // HEAD_DIM
    return jnp.where(head[:, None] == head[None, :], 1.0 / HEAD_DIM, 0.0).astype(BF16)


def _head_rms(x, g, avg, two_term):
    x2 = x * x
    means = []
    for c in range(0, x.shape[1], HEAD_AVG_WIDTH):
        blk = x2[:, c:c + HEAD_AVG_WIDTH]
        hi = blk.astype(BF16)
        ms = _dot(hi, avg)
        if two_term:
            ms = ms + _dot((blk - hi.astype(F32)).astype(BF16), avg)
        means.append(ms)
    return x * lax.rsqrt(jnp.concatenate(means, axis=-1) + EPS) * g


def _pooled(u):
    l = u.shape[0]
    t = lax.broadcasted_iota(I32, (l, POOL_GROUP_DIM), 0)
    groups = []
    for g, w in enumerate(POOL_WINDOWS):
        x = u[:, g * POOL_GROUP_DIM:(g + 1) * POOL_GROUP_DIM]
        acc = x
        for k in range(-(w // 2), w - w // 2):
            if k == 0:
                continue
            shifted = pltpu.roll(x, (-k) % l, 0)
            acc = acc + jnp.where((t + k >= 0) & (t + k < l), shifted, 0.0)
        cnt = (jnp.minimum(t + (w - w // 2), l) - jnp.maximum(t - w // 2, 0)).astype(F32)
        groups.append(acc / cnt - x)
    return jnp.concatenate(groups, axis=-1)


def _in_kernel(x_ref, mod_ref, g_ref, w_ref, qg_ref, kg_ref, avg_ref, q_ref, k_ref, v_ref, u_ref, ga_ref, gb_ref,
               *, kv_channel_major, pool_here):
    y = _rms(x_ref[...], g_ref[...])
    h = (y * (1.0 + mod_ref[0, 1:2, :]) + mod_ref[0, 0:1, :]).astype(BF16)
    part = lambda i: _dot(h, w_ref[:, _IN_CUTS[i]:_IN_CUTS[i + 1]])
    q_ref[...] = (_head_rms(part(0), qg_ref[...], avg_ref[...], False) * HEAD_DIM ** -0.5).astype(q_ref.dtype)
    kn = _head_rms(part(1), kg_ref[...], avg_ref[...], True)
    v = part(2)
    if kv_channel_major:
        k_ref[0] = kn.T.reshape(k_ref.shape[1:])
        v_ref[0] = v.T.reshape(v_ref.shape[1:])
    else:
        k_ref[...] = kn.astype(k_ref.dtype)
        v_ref[...] = v.astype(v_ref.dtype)
    u = part(3)
    u_ref[...] = _pooled(u).astype(u_ref.dtype) if pool_here else u
    ga_ref[...] = _sigmoid(part(4)).astype(ga_ref.dtype)
    gb_ref[...] = _sigmoid(part(5)).astype(gb_ref.dtype)


def _in_proj(x2d, mods, norm_g, w_in_bf16, q_gain, k_gain, cond_of_block, sequence_blocks):
    n, d = x2d.shape
    tb = TOKEN_BLOCK
    widths = [hi - lo for lo, hi in zip(_IN_CUTS[:-1], _IN_CUTS[1:])]
    dtypes = [BF16, BF16, BF16, BF16 if sequence_blocks else F32, BF16, BF16]
    out_specs = [pl.BlockSpec((tb, w), lambda i: (i, 0)) for w in widths]
    out_shape = [jax.ShapeDtypeStruct((n, w), t) for w, t in zip(widths, dtypes)]
    if sequence_blocks:
        for i in (1, 2):
            out_specs[i] = pl.BlockSpec((1, NA_HEADS, HEAD_DIM, tb), lambda i: (i, 0, 0, 0))
            out_shape[i] = jax.ShapeDtypeStruct((n // tb, NA_HEADS, HEAD_DIM, tb), F32)
    full = lambda a: pl.BlockSpec(a.shape, lambda i: (0,) * a.ndim)
    return pl.pallas_call(
        functools.partial(_in_kernel, kv_channel_major=sequence_blocks, pool_here=sequence_blocks),
        grid=(n // tb,),
        in_specs=[pl.BlockSpec((tb, d), lambda i: (i, 0)),
                  pl.BlockSpec((1, N_MOD, d), lambda i: (cond_of_block(i), 0, 0)),
                  full(norm_g), full(w_in_bf16), full(q_gain), full(k_gain),
                  pl.BlockSpec((HEAD_AVG_WIDTH, HEAD_AVG_WIDTH), lambda i: (0, 0))],
        out_specs=out_specs,
        out_shape=out_shape,
        compiler_params=_params(("arbitrary",)),
        name="in_proj",
    )(x2d, mods, norm_g, w_in_bf16, q_gain, k_gain, _head_avg_matrix())


def _with_ones(v):
    lane = lax.broadcasted_iota(I32, v.shape, 1)
    return jnp.concatenate([v.astype(BF16), jnp.where(lane == 0, 1.0, 0.0).astype(BF16)], axis=-1)


def _attn_ctx_kernel(q_ref, kt_ref, vt_ref, att_ref):
    qn = q_ref[0]
    l = kt_ref.shape[3]
    ones_row = jnp.where(lax.broadcasted_iota(I32, (HEAD_DIM, l), 0) == 0, 1.0, 0.0)
    for h in range(NA_HEADS):
        sl = slice(h * HEAD_DIM, (h + 1) * HEAD_DIM)
        s = _dot(qn[:, sl], kt_ref[0, h].astype(BF16))
        e = jnp.exp(s - jnp.max(s, axis=-1, keepdims=True)).astype(BF16)
        vt = jnp.concatenate([vt_ref[0, h], ones_row], axis=0).astype(BF16)
        o = _dot_nt(e, vt)
        att_ref[0, :, sl] = (o[:, :HEAD_DIM] / o[:, HEAD_DIM:HEAD_DIM + 1]).astype(att_ref.dtype)


def _attn_ctx(q, kt, vt):
    b, l, w = q.shape
    blk = pl.BlockSpec((1, l, w), lambda i: (i, 0, 0))
    tblk = pl.BlockSpec((1,) + kt.shape[1:], lambda i: (i, 0, 0, 0))
    return pl.pallas_call(
        _attn_ctx_kernel,
        grid=(b,),
        in_specs=[blk, tblk, tblk],
        out_specs=blk,
        out_shape=jax.ShapeDtypeStruct((b, l, w), BF16),
        compiler_params=_params(("arbitrary",)),
        name="attn_ctx",
    )(q, kt, vt)


def _lat_windows(rows):
    kr = min(WIN_ROWS, rows)
    rb = min(LAT_ROW_BLOCK, rows)
    wr = min(kr + rb, rows)
    assert rows % rb == 0
    starts = [min(max(i * rb - kr // 2, 0), rows - wr) for i in range(rows // rb)]
    for i, ws in enumerate(starts):
        for r in range(i * rb, (i + 1) * rb):
            rs = min(max(r - kr // 2, 0), rows - kr)
            assert ws <= rs and rs + kr <= ws + wr
    return kr, rb, wr, starts


def _attn_lat_kernel(q_ref, k_ref, v_ref, ckt_ref, cvt_ref, bias_ref, att_ref,
                     kn_s, vb_s, ckb_s, cvb_s, *, rows, kr, rb, wr):
    i = pl.program_id(1)

    @pl.when(i == 0)
    def _():
        lc = ckt_ref.shape[3]
        ones_row = jnp.where(lax.broadcasted_iota(I32, (HEAD_DIM, lc), 0) == 0, 1.0, 0.0)
        for h in range(NA_HEADS):
            sl = slice(h * HEAD_DIM, (h + 1) * HEAD_DIM)
            kn_s[h] = k_ref[0, :, sl]
            vb_s[h] = _with_ones(v_ref[0, :, sl])
            ckb_s[h] = ckt_ref[0, h].astype(BF16)
            cvb_s[h] = jnp.concatenate([cvt_ref[0, h], ones_row], axis=0).astype(BF16)

    win_start = jnp.clip(i * rb - kr // 2, 0, rows - wr)
    key_rows = pl.ds(pl.multiple_of(win_start * GRID_W, GRID_W), wr * GRID_W)
    qn_all = q_ref[0]
    for h in range(NA_HEADS):
        sl = slice(h * HEAD_DIM, (h + 1) * HEAD_DIM)
        qn = qn_all[:, sl]
        bias = bias_ref[h, pl.ds(i * rb, rb)].reshape(rb * GRID_W, wr * GRID_W)
        s_loc = _dot_nt(qn, kn_s[h, key_rows, :]) + bias
        s_ctx = _dot(qn, ckb_s[h])
        m = jnp.maximum(jnp.max(s_loc, axis=-1, keepdims=True), jnp.max(s_ctx, axis=-1, keepdims=True))
        o = (_dot(jnp.exp(s_loc - m).astype(BF16), vb_s[h, key_rows, :])
             + _dot_nt(jnp.exp(s_ctx - m).astype(BF16), cvb_s[h]))
        att_ref[0, :, sl] = (o[:, :HEAD_DIM] / o[:, HEAD_DIM:HEAD_DIM + 1]).astype(att_ref.dtype)


def _attn_lat(q, k, v, ckt, cvt, bias):
    b, l, w = q.shape
    lc = ckt.shape[3]
    rows = l // GRID_W
    kr, rb, wr, _ = _lat_windows(rows)
    seq = pl.BlockSpec((1, l, w), lambda i, r: (i, 0, 0))
    ctx = pl.BlockSpec((1, NA_HEADS, HEAD_DIM, lc), lambda i, r: (i, 0, 0, 0))
    rowblk = pl.BlockSpec((1, rb * GRID_W, w), lambda i, r: (i, r, 0))
    return pl.pallas_call(
        functools.partial(_attn_lat_kernel, rows=rows, kr=kr, rb=rb, wr=wr),
        grid=(b, rows // rb),
        in_specs=[rowblk, seq, seq, ctx, ctx,
                  pl.BlockSpec(bias.shape, lambda i, r: (0, 0, 0, 0), pipeline_mode=pl.Buffered(1))],
        out_specs=rowblk,
        out_shape=jax.ShapeDtypeStruct((b, l, w), BF16),
        scratch_shapes=[pltpu.VMEM((NA_HEADS, l, HEAD_DIM), BF16), pltpu.VMEM((NA_HEADS, l, 2 * HEAD_DIM), BF16),
                        pltpu.VMEM((NA_HEADS, HEAD_DIM, lc), BF16), pltpu.VMEM((NA_HEADS, 2 * HEAD_DIM, lc), BF16)],
        compiler_params=_params(("arbitrary", "arbitrary")),
        name="attn_lat",
    )(q, k, v, ckt, cvt, bias)


def _bias_kernel(rpb_ref, o_ref, *, rows):
    kr, rb, wr, starts = _lat_windows(rows)
    h = pl.program_id(0)
    n_ro, n_co = 2 * WIN_ROWS - 1, 2 * WIN_COLS - 1
    wq = lax.broadcasted_iota(I32, (GRID_W, GRID_W), 0)
    wk = lax.broadcasted_iota(I32, (GRID_W, GRID_W), 1)
    co = jnp.clip(wk - wq, -(WIN_COLS - 1), WIN_COLS - 1) + (WIN_COLS - 1)
    col_start = jnp.clip(wq - WIN_COLS // 2, 0, GRID_W - WIN_COLS)
    in_win = (wk >= col_start) & (wk < col_start + WIN_COLS)
    co = jnp.where(in_win, co, -1)
    masked = jnp.full((GRID_W, GRID_W), NEG_INF, F32)
    tables = {}
    for r in range(rows):
        row_start = min(max(r - kr // 2, 0), rows - kr)
        for j in range(wr):
            key_row = starts[r // rb] + j
            blk = masked
            if row_start <= key_row < row_start + kr:
                ro = key_row - r + (WIN_ROWS - 1)
                if ro not in tables:
                    t = masked
                    for c in range(n_co):
                        t = jnp.where(co == c, rpb_ref[(h * n_ro + ro) * n_co + c], t)
                    tables[ro] = t
                blk = tables[ro]
            o_ref[0, r, :, j * GRID_W:(j + 1) * GRID_W] = blk


def _window_bias(rpb, rows):
    _, _, wr, _ = _lat_windows(rows)
    nh = rpb.shape[0]
    return pl.pallas_call(
        functools.partial(_bias_kernel, rows=rows),
        grid=(nh,),
        in_specs=[pl.BlockSpec(memory_space=pltpu.SMEM)],
        out_specs=pl.BlockSpec((1, rows, GRID_W, wr * GRID_W), lambda h: (h, 0, 0, 0)),
        out_shape=jax.ShapeDtypeStruct((nh, rows, GRID_W, wr * GRID_W), F32),
        compiler_params=_params(("arbitrary",)),
        name="window_bias",
    )(rpb.reshape(-1))


def _pool_kernel(u_ref, o_ref):
    o_ref[0] = _pooled(u_ref[0]).astype(o_ref.dtype)


def _pool(u):
    b, l, w = u.shape
    blk = pl.BlockSpec((1, l, w), lambda i: (i, 0, 0))
    return pl.pallas_call(
        _pool_kernel, grid=(b,), in_specs=[blk], out_specs=blk,
        out_shape=jax.ShapeDtypeStruct((b, l, w), BF16),
        compiler_params=_params(("arbitrary",)),
        name="pool",
    )(u)


def _merge_kernel(xc_ref, xl_ref, attc_ref, attl_ref, plc_ref, pll_ref, gac_ref, gal_ref, gbc_ref, gbl_ref,
                  mod_ref, watt_ref, wpool_ref, ps_ref, wout_ref, g2_ref, wr_ref, x1_ref, h2_ref, aff_ref, *, ctx_blocks):
    is_ctx = pl.program_id(0) < ctx_blocks

    def block(x_ref, att_ref, pl_ref, ga_ref, gb_ref):
        o_a = _dot(att_ref[...], watt_ref[...])
        o_b = jnp.concatenate(
            [_dot(pl_ref[:, g * POOL_GROUP_DIM:(g + 1) * POOL_GROUP_DIM], wpool_ref[g])
             for g in range(len(POOL_WINDOWS))], axis=-1) * ps_ref[...]
        merged = ga_ref[...] * o_a + gb_ref[...] * o_b
        x1 = x_ref[...] + mod_ref[0, 2:3, :] * _dot(merged.astype(BF16), wout_ref[...])
        x1_ref[...] = x1
        h2 = _rms(x1, g2_ref[...]) * (1.0 + mod_ref[0, 4:5, :]) + mod_ref[0, 3:4, :]
        h2_ref[...] = h2
        logits = _dot3(h2, wr_ref[...])
        e = jnp.exp(logits - jnp.max(logits, axis=-1, keepdims=True))
        aff_ref[...] = e / jnp.sum(e, axis=-1, keepdims=True)

    @pl.when(is_ctx)
    def _():
        block(xc_ref, attc_ref, plc_ref, gac_ref, gbc_ref)

    @pl.when(jnp.logical_not(is_ctx))
    def _():
        block(xl_ref, attl_ref, pll_ref, gal_ref, gbl_ref)


def _merge(ctx, lat, mods, w_att, w_pool, pool_scale, w_out, norm2_g, w_router, lat_blocks_per_seq):
    n_c, d = ctx[0].shape
    n = n_c + lat[0].shape[0]
    tb = TOKEN_BLOCK
    cb = n_c // tb
    crow = lambda w: pl.BlockSpec((tb, w), lambda i: (jnp.minimum(i, cb - 1), 0))
    lrow = lambda w: pl.BlockSpec((tb, w), lambda i: (jnp.maximum(i - cb, 0), 0))
    orow = lambda w: pl.BlockSpec((tb, w), lambda i: (i, 0))
    full = lambda a: pl.BlockSpec(a.shape, lambda i: (0,) * a.ndim)
    cond = lambda i: jnp.where(i < cb, 0, 1 + (i - cb) // lat_blocks_per_seq)
    pairs, specs = [], []
    for a_c, a_l in zip(ctx, lat):
        pairs += [a_c, a_l]
        specs += [crow(a_c.shape[1]), lrow(a_l.shape[1])]
    return pl.pallas_call(
        functools.partial(_merge_kernel, ctx_blocks=cb),
        grid=(n // tb,),
        in_specs=specs + [pl.BlockSpec((1, N_MOD, d), lambda i: (cond(i), 0, 0)),
                          full(w_att), full(w_pool), full(pool_scale), full(w_out), full(norm2_g), full(w_router)],
        out_specs=[orow(d), orow(d), orow(N_EXPERTS)],
        out_shape=[jax.ShapeDtypeStruct((n, d), F32), jax.ShapeDtypeStruct((n, d), F32),
                   jax.ShapeDtypeStruct((n, N_EXPERTS), F32)],
        compiler_params=_params(("arbitrary",)),
        name="merge",
    )(*pairs, mods, w_att, w_pool, pool_scale, w_out, norm2_g, w_router)


def _prefix_incl(mask_f32, out_ref):
    e, n = mask_f32.shape
    i = lax.broadcasted_iota(I32, (128, 128), 0)
    j = lax.broadcasted_iota(I32, (128, 128), 1)
    tri = jnp.where(i <= j, 1.0, 0.0).astype(BF16)
    carry = jnp.zeros((e, 1), F32)
    for c in range(n // 128):
        inc = _dot(mask_f32[:, c * 128:(c + 1) * 128].astype(BF16), tri) + carry
        out_ref[:, c * 128:(c + 1) * 128] = inc.astype(I32)
        carry = inc[:, 127:128]


def _route_kernel(aff_ref, sel_ref, rinc_ref, tmp_ref, *, cap):
    aff = aff_ref[...]

    def bit_step(i, thr):
        cand = thr | jnp.left_shift(jnp.int32(1), 30 - i)
        cnt = jnp.sum(jnp.where(aff >= pltpu.bitcast(cand, F32), 1, 0), axis=1, keepdims=True)
        return jnp.where(cnt >= cap, cand, thr)

    thr = lax.fori_loop(0, 31, bit_step, jnp.zeros((aff.shape[0], 1), I32))
    gt = aff >= pltpu.bitcast(thr + 1, F32)
    eq = (aff >= pltpu.bitcast(thr, F32)) & jnp.logical_not(gt)
    need = cap - jnp.sum(jnp.where(gt, 1, 0), axis=1, keepdims=True)
    _prefix_incl(jnp.where(eq, 1.0, 0.0), tmp_ref)
    sel = gt | (eq & (tmp_ref[...] <= need))
    sel_ref[...] = jnp.where(sel, 1, 0)
    _prefix_incl(jnp.where(sel, 1.0, 0.0), rinc_ref)


def _route(aff_t, cap):
    e, n = aff_t.shape
    full = pl.BlockSpec((e, n), lambda: (0, 0))
    return pl.pallas_call(
        functools.partial(_route_kernel, cap=cap),
        in_specs=[full], out_specs=[full, full],
        out_shape=[jax.ShapeDtypeStruct((e, n), I32)] * 2,
        scratch_shapes=[pltpu.VMEM((e, n), I32)],
        compiler_params=pltpu.CompilerParams(vmem_limit_bytes=VMEM_LIMIT),
        name="route",
    )(aff_t)


def _moe_kernel(gidx_ref, qpos_ref, h_ref, gate_ref, wg_ref, wu_ref, wd_ref, z_ref,
                xf_ref, xb_ref, acc_ref, y_ref, sem_g, sem_s, *, m_tot, n_f, step_rows, z_zero_rows):
    e = pl.program_id(0)
    j = pl.program_id(1)
    n_e = pl.num_programs(0)
    m_iss = step_rows * n_f
    tf = wg_ref.shape[1]
    slot = e % 2

    def gather_row(lst, s, dst_slot):
        t = gidx_ref[lst * m_iss + s]
        return pltpu.make_async_copy(h_ref.at[pl.ds(t, 1)], xf_ref.at[dst_slot, pl.ds(s, 1)], sem_g.at[dst_slot])

    def scatter_row(lst, s):
        q = qpos_ref[lst * m_iss + s]
        return pltpu.make_async_copy(y_ref.at[pl.ds(s, 1)], z_ref.at[pl.ds(q, 1)], sem_s)

    def wait_gathers(dst_slot):
        pltpu.make_async_copy(h_ref.at[pl.ds(0, m_iss)], xf_ref.at[dst_slot], sem_g.at[dst_slot]).wait()

    def wait_scatters():
        pltpu.make_async_copy(y_ref, z_ref.at[pl.ds(0, m_iss)], sem_s).wait()

    @pl.when((e == 0) & (j == 0))
    def _():
        y_ref[...] = jnp.zeros_like(y_ref)
        cp = pltpu.make_async_copy(y_ref.at[pl.ds(0, z_zero_rows)], z_ref.at[pl.ds(n_e * m_tot, z_zero_rows)], sem_s)
        cp.start()
        cp.wait()

        def first(s, c):
            gather_row(0, s, 0).start()
            return c

        lax.fori_loop(0, m_iss, first, 0)

    @pl.when(j == 0)
    def _():
        wait_gathers(slot)
        xb_ref[...] = xf_ref[slot, 0:m_tot, :].astype(BF16)
        acc_ref[...] = jnp.zeros_like(acc_ref)

    nxt = lax.rem(e + 1, n_e)
    for u in range(step_rows):
        s = j * step_rows + u
        gather_row(nxt, s, 1 - slot).start()
        scatter_row(e, s).start()

    valid = D_EXPERT - j * tf
    cmask = lax.broadcasted_iota(I32, (1, tf), 1) < valid
    rmask = lax.broadcasted_iota(I32, (tf, 1), 0) < valid
    wg = wg_ref[0].astype(BF16)
    wu = wu_ref[0].astype(BF16)
    wd = jnp.where(rmask, wd_ref[0], 0.0).astype(BF16)
    for m in range(m_tot // EXPERT_M_BLOCK):
        rs = slice(m * EXPERT_M_BLOCK, (m + 1) * EXPERT_M_BLOCK)
        xs = xb_ref[rs, :]
        a = _dot_nt(xs, wg)
        b = _dot_nt(xs, wu)
        hm = jnp.where(cmask, a * _sigmoid(a) * b, 0.0).astype(BF16)
        acc_ref[rs, :] += _dot(hm, wd)

    @pl.when(j == n_f - 1)
    def _():
        wait_scatters()
        y_ref[0:m_tot, :] = acc_ref[...] * gate_ref[0]

        @pl.when(e == n_e - 1)
        def _():
            def last(s, c):
                scatter_row(n_e, s).start()
                return c

            lax.fori_loop(0, m_iss, last, 0)
            wait_scatters()
            wait_gathers(1 - slot)


def _moe_step_rows(m_tot, n_f):
    return pl.cdiv(pl.cdiv(m_tot, n_f), ROW_ALIGN) * ROW_ALIGN


def _moe(gidx, qpos, h, gate, w_gate, w_up, w_down, m_tot, z_rows, z_zero_rows):
    d = h.shape[1]
    n_e = w_down.shape[0]
    tf = EXPERT_F_BLOCK
    n_f = pl.cdiv(D_EXPERT, tf)
    step_rows = _moe_step_rows(m_tot, n_f)
    m_iss = step_rows * n_f
    grid_spec = pltpu.PrefetchScalarGridSpec(
        num_scalar_prefetch=2,
        grid=(n_e, n_f),
        in_specs=[pl.BlockSpec(memory_space=pl.ANY),
                  pl.BlockSpec((1, m_tot, 1), lambda e, j, *_: (e, 0, 0)),
                  pl.BlockSpec((1, tf, d), lambda e, j, *_: (e, j, 0)),
                  pl.BlockSpec((1, tf, d), lambda e, j, *_: (e, j, 0)),
                  pl.BlockSpec((1, tf, d), lambda e, j, *_: (e, j, 0))],
        out_specs=pl.BlockSpec(memory_space=pl.ANY),
        scratch_shapes=[pltpu.VMEM((2, m_iss, d), F32), pltpu.VMEM((m_tot, d), BF16), pltpu.VMEM((m_tot, d), F32),
                        pltpu.VMEM((m_iss, d), F32), pltpu.SemaphoreType.DMA((2,)), pltpu.SemaphoreType.DMA],
    )
    return pl.pallas_call(
        functools.partial(_moe_kernel, m_tot=m_tot, n_f=n_f, step_rows=step_rows, z_zero_rows=z_zero_rows),
        grid_spec=grid_spec,
        out_shape=jax.ShapeDtypeStruct((z_rows, d), F32),
        compiler_params=_params(("arbitrary", "arbitrary")),
        name="moe",
    )(gidx, qpos, h, gate, w_gate, w_up, w_down)


def _combine_kernel(w0_ref, nc_ref, first_ref, z_ref, x1_ref, mod_ref, qt_ref, kt_ref, o_ref, zbuf, sem, acc_ref):
    blk = pl.program_id(0)
    n_blk = pl.num_programs(0)
    w0 = w0_ref[blk]
    n_chunks = nc_ref[blk]
    first = first_ref[blk]
    qt = qt_ref[...]
    kt = kt_ref[...]
    acc_ref[...] = jnp.zeros_like(acc_ref)

    def z_copy(start, slot):
        rows = pl.ds(pl.multiple_of(start, ROW_ALIGN), Z_CHUNK)
        return pltpu.make_async_copy(z_ref.at[rows], zbuf.at[slot], sem.at[slot])

    @pl.when(blk == 0)
    def _():
        z_copy(w0, first % 2).start()

    def chunk(c, carry):
        slot = (first + c) % 2
        last = c + 1 == n_chunks
        next_start = jnp.where(last, w0_ref[jnp.minimum(blk + 1, n_blk - 1)], w0 + (c + 1) * Z_CHUNK)

        @pl.when(jnp.logical_not(last) | (blk + 1 < n_blk))
        def _():
            z_copy(next_start, 1 - slot).start()

        z_copy(w0 + c * Z_CHUNK, slot).wait()
        r = w0 + c * Z_CHUNK + lax.broadcasted_iota(I32, (qt.shape[0], Z_CHUNK), 1)
        onehot = jnp.where((r >= qt) & (r < qt + kt), 1.0, 0.0).astype(BF16)
        acc_ref[...] += _dot(onehot, zbuf[slot].astype(BF16))
        return carry

    lax.fori_loop(0, n_chunks, chunk, 0)
    o_ref[...] = x1_ref[...] + mod_ref[0, 5:6, :] * acc_ref[...]


def _combine(w0, nc, z, x1, first_block, mods, qt, kt, cond_of_block):
    n = qt.shape[0]
    d = x1.shape[1]
    tb = COMBINE_BLOCK
    grid_spec = pltpu.PrefetchScalarGridSpec(
        num_scalar_prefetch=3,
        grid=(n // tb,),
        in_specs=[pl.BlockSpec(memory_space=pl.ANY),
                  pl.BlockSpec((tb, d), lambda i, *_: (i + first_block, 0)),
                  pl.BlockSpec((1, N_MOD, d), lambda i, *_: (cond_of_block(i), 0, 0)),
                  pl.BlockSpec((tb, 1), lambda i, *_: (i, 0)),
                  pl.BlockSpec((tb, 1), lambda i, *_: (i, 0))],
        out_specs=pl.BlockSpec((tb, d), lambda i, *_: (i, 0)),
        scratch_shapes=[pltpu.VMEM((2, Z_CHUNK, d), F32), pltpu.SemaphoreType.DMA((2,)), pltpu.VMEM((tb, d), F32)],
    )
    return pl.pallas_call(
        _combine_kernel,
        grid_spec=grid_spec,
        out_shape=jax.ShapeDtypeStruct((n, d), F32),
        compiler_params=_params(("arbitrary",)),
        name="combine",
    )(w0, nc, jnp.cumsum(nc) - nc, z, x1, mods, qt, kt)


def _routing_tables(aff, cap, z_off):
    n = aff.shape[0]
    aff_t = aff.T
    sel, rinc = _route(aff_t, cap)
    slots = jnp.arange(cap, dtype=I32)
    chunk = 128
    rinc3 = rinc.reshape(rinc.shape[0], n // chunk, chunk)
    n_full = jnp.sum((rinc3[:, None, :, chunk - 1] <= slots[None, :, None]).astype(I32), axis=-1)
    pick = (n_full[:, :, None] == jnp.arange(n // chunk, dtype=I32)[None, None, :]).astype(F32)
    inside = jnp.einsum("eca,eab->ecb", pick, rinc3.astype(F32), precision=lax.Precision.HIGHEST)
    idx = n_full * chunk + jnp.sum((inside <= slots[None, :, None].astype(F32)).astype(I32), axis=-1)
    gate = jnp.take_along_axis(aff_t, idx, axis=1)
    k_tok = jnp.sum(sel, axis=0)
    q_tok = jnp.cumsum(k_tok) - k_tok + z_off
    before = jnp.cumsum(sel, axis=0) - sel
    qpos = jnp.take_along_axis(q_tok[None, :] + before, idx, axis=1)
    nblk = n // COMBINE_BLOCK
    q_blk = q_tok[::COMBINE_BLOCK]
    end_blk = q_blk + jnp.sum(k_tok.reshape(nblk, COMBINE_BLOCK), axis=1)
    w0 = (q_blk // ROW_ALIGN) * ROW_ALIGN
    nc = jnp.maximum((end_blk - w0 + Z_CHUNK - 1) // Z_CHUNK, 1)
    return idx, gate, qpos.astype(I32), w0.astype(I32), nc.astype(I32), q_tok.astype(I32)[:, None], k_tok.astype(I32)[:, None]


def kernel(x_prompt, x_sample, cache_k, cache_v, c, c_ctx, norm1_g, norm2_g, w_ada, b_ada, w_in, q_norm_g, k_norm_g,
           rpb, w_att_proj, w_pool, pool_scale, w_out, w_router, w_gate_e, w_up_e, w_down_e):
    assert w_ada.shape[0] == 1, "single trunk layer"
    nb, ls, d = x_prompt.shape
    db, ll, _ = x_sample.shape
    n_c, n_l = nb * ls, db * ll
    assert ls == TOKEN_BLOCK, "context keys / values are written one sequence per projection block"

    cond = jnp.concatenate([c_ctx[None, :], c, jnp.zeros((16 - 1 - db, d), F32)], axis=0)
    mods = _ada(cond, w_ada[0], b_ada).reshape(16, N_MOD, d)

    w_in_b = w_in[0].astype(BF16)
    w_att_b = w_att_proj[0].astype(BF16)
    w_pool_b = w_pool[0].astype(BF16)
    w_out_b = w_out[0].astype(BF16)

    cond_ctx = lambda i: 0
    lat_blocks = ll // TOKEN_BLOCK
    cond_lat = lambda i: 1 + i // lat_blocks

    xc = x_prompt.reshape(n_c, d)
    xl = x_sample.reshape(n_l, d)
    q_gain = jnp.tile(q_norm_g, (1, NA_HEADS))
    k_gain = jnp.tile(k_norm_g, (1, NA_HEADS))
    qc, kc, vc, uc, gac, gbc = _in_proj(xc, mods, norm1_g, w_in_b, q_gain, k_gain, cond_ctx, True)
    ql, kl, vl, ul, gal, gbl = _in_proj(xl, mods, norm1_g, w_in_b, q_gain, k_gain, cond_lat, False)

    seq_c = lambda a: a.reshape(nb, ls, a.shape[-1])
    seq_l = lambda a: a.reshape(db, ll, a.shape[-1])
    att_c = _attn_ctx(seq_c(qc), kc, vc)
    bias = _window_bias(rpb[0], ll // GRID_W)
    att_l = _attn_lat(seq_l(ql), seq_l(kl), seq_l(vl), jnp.transpose(cache_k[:, 0], (0, 2, 3, 1)),
                      jnp.transpose(cache_v[:, 0], (0, 2, 3, 1)), bias)
    pool_c = uc
    pool_l = _pool(seq_l(ul)).reshape(n_l, POOL_WIDTH)

    x1, h2, aff = _merge((xc, att_c.reshape(n_c, NA_WIDTH), pool_c, gac, gbc),
                         (xl, att_l.reshape(n_l, NA_WIDTH), pool_l, gal, gbl),
                         mods, w_att_b, w_pool_b, pool_scale, w_out_b, norm2_g, w_router[0], lat_blocks)

    cap_c = max(1, (CAPACITY_FACTOR * n_c) // N_EXPERTS)
    cap_l = max(1, (CAPACITY_FACTOR * n_l) // N_EXPERTS)
    m_tot = cap_c + cap_l
    m_iss = _moe_step_rows(m_tot, pl.cdiv(D_EXPERT, EXPERT_F_BLOCK)) * pl.cdiv(D_EXPERT, EXPERT_F_BLOCK)
    z_valid = N_EXPERTS * m_tot
    z_zero_rows = Z_CHUNK + ROW_ALIGN
    z_spare = z_valid + z_zero_rows
    z_rows = z_spare + m_iss
    idx_c, gate_c, qpos_c, w0_c, nc_c, qt_c, kt_c = _routing_tables(aff[:n_c], cap_c, 0)
    idx_l, gate_l, qpos_l, w0_l, nc_l, qt_l, kt_l = _routing_tables(aff[n_c:], cap_l, N_EXPERTS * cap_c)
    spare = jnp.broadcast_to(z_spare + jnp.arange(m_iss, dtype=I32), (N_EXPERTS + 1, m_iss))
    gidx = jnp.concatenate([idx_c, idx_l + n_c, jnp.zeros((N_EXPERTS, m_iss - m_tot), I32)], axis=1).reshape(-1)
    qpos = jnp.concatenate([spare[:1], jnp.concatenate([qpos_c, qpos_l, spare[1:, m_tot:]], axis=1)], axis=0).reshape(-1)
    gate = jnp.concatenate([gate_c, gate_l], axis=1)[:, :, None]

    z = _moe(gidx, qpos, h2, gate, jnp.swapaxes(w_gate_e[0], 1, 2), jnp.swapaxes(w_up_e[0], 1, 2), w_down_e[0],
             m_tot, z_rows, z_zero_rows)

    y_c = _combine(w0_c, nc_c, z, x1, 0, mods, qt_c, kt_c, cond_ctx)
    y_l = _combine(w0_l, nc_l, z, x1, n_c // COMBINE_BLOCK, mods, qt_l, kt_l, lambda i: 1 + i // (ll // COMBINE_BLOCK))

    state_k = jnp.transpose(kc, (0, 3, 1, 2))[:, None]
    state_v = jnp.transpose(vc, (0, 3, 1, 2))[:, None]
    return (y_c.reshape(nb, ls, d), y_l.reshape(db, ll, d), state_k, state_v)
```

```python
import functools

import jax
import jax.numpy as jnp
from jax import lax
from jax.experimental import pallas as pl
from jax.experimental.pallas import tpu as pltpu

F32 = jnp.float32
BF16 = jnp.bfloat16
I32 = jnp.int32

D_MODEL = 1024
GRID_W = 64
NA_HEADS = 8
HEAD_DIM = 64
NA_WIDTH = NA_HEADS * HEAD_DIM
WIN_ROWS = 8
WIN_COLS = 16
POOL_WINDOWS = (2, 4, 8, 16)
POOL_GROUP_DIM = 128
POOL_WIDTH = 512
POOL_OUT_DIM = 256
N_EXPERTS = 16
CAPACITY_FACTOR = 2
D_EXPERT = 2752
N_MOD = 6
EPS = 1e-6
NEG_INF = -1e30

TOKEN_BLOCK = 256
COMBINE_BLOCK = 256
LAT_ROW_BLOCK = 4
EXPERT_F_BLOCK = 256
EXPERT_M_BLOCK = 768
Z_CHUNK = 256
ROW_ALIGN = 8
VMEM_LIMIT = 56 * 1024 * 1024


def _dot(a, b):
    return jnp.dot(a, b, preferred_element_type=F32)


def _dot_nt(a, b):
    return lax.dot_general(a, b, (((1,), (1,)), ((), ())), preferred_element_type=F32)


def _dot3(a, b):
    a_hi = a.astype(BF16)
    a_lo = (a - a_hi.astype(F32)).astype(BF16)
    b_hi = b.astype(BF16)
    b_lo = (b - b_hi.astype(F32)).astype(BF16)
    return _dot(a_hi, b_hi) + _dot(a_hi, b_lo) + _dot(a_lo, b_hi)


def _sigmoid(x):
    return 0.5 * jnp.tanh(0.5 * x) + 0.5


def _rms(x, g):
    return x * lax.rsqrt(jnp.mean(x * x, axis=-1, keepdims=True) + EPS) * g


def _pack_halves(x):
    w = x.shape[1] // 2
    return pltpu.pack_elementwise([x[:, :w], x[:, w:]], packed_dtype=BF16)


def _unpack_halves(p):
    return tuple(pltpu.unpack_elementwise(p, index=i, packed_dtype=BF16, unpacked_dtype=F32).astype(BF16)
                 for i in range(2))


def _params(sem):
    return pltpu.CompilerParams(dimension_semantics=sem, vmem_limit_bytes=VMEM_LIMIT)


def _ada_kernel(cond_ref, w_ref, b_ref, o_ref):
    c = cond_ref[...]
    o_ref[...] = _dot3(c * _sigmoid(c), w_ref[...]) + b_ref[...]


def _ada(cond, w_ada, b_ada):
    rows, d = cond.shape
    n_out = w_ada.shape[1]
    bn = 512
    return pl.pallas_call(
        _ada_kernel,
        grid=(n_out // bn,),
        in_specs=[pl.BlockSpec((rows, d), lambda i: (0, 0)),
                  pl.BlockSpec((d, bn), lambda i: (0, i)),
                  pl.BlockSpec((1, bn), lambda i: (0, i))],
        out_specs=pl.BlockSpec((rows, bn), lambda i: (0, i)),
        out_shape=jax.ShapeDtypeStruct((rows, n_out), F32),
        compiler_params=_params(("arbitrary",)),
        name="ada",
    )(cond, w_ada, b_ada)


_IN_CUTS = (0, 512, 1024, 1536, 2048, 3072, 4096)


HEAD_AVG_WIDTH = 256


def _head_avg_matrix():
    head = jnp.arange(HEAD_AVG_WIDTH, dtype=I32) // HEAD_DIM
    return jnp.where(head[:, None] == head[None, :], 1.0 / HEAD_DIM, 0.0).astype(BF16)


def _head_rms(x, g, avg, two_term):
    x2 = x * x
    means = []
    for c in range(0, x.shape[1], HEAD_AVG_WIDTH):
        blk = x2[:, c:c + HEAD_AVG_WIDTH]
        hi = blk.astype(BF16)
        ms = _dot(hi, avg)
        if two_term:
            ms = ms + _dot((blk - hi.astype(F32)).astype(BF16), avg)
        means.append(ms)
    return x * lax.rsqrt(jnp.concatenate(means, axis=-1) + EPS) * g


def _pooled(u):
    l = u.shape[0]
    t = lax.broadcasted_iota(I32, (l, POOL_GROUP_DIM), 0)
    groups = []
    for g, w in enumerate(POOL_WINDOWS):
        x = u[:, g * POOL_GROUP_DIM:(g + 1) * POOL_GROUP_DIM]
        acc = x
        for k in range(-(w // 2), w - w // 2):
            if k == 0:
                continue
            shifted = pltpu.roll(x, (-k) % l, 0)
            acc = acc + jnp.where((t + k >= 0) & (t + k < l), shifted, 0.0)
        cnt = (jnp.minimum(t + (w - w // 2), l) - jnp.maximum(t - w // 2, 0)).astype(F32)
        groups.append(acc / cnt - x)
    return jnp.concatenate(groups, axis=-1)


def _in_kernel(x_ref, mod_ref, g_ref, w_ref, qg_ref, kg_ref, avg_ref, q_ref, k_ref, v_ref, u_ref, ga_ref, gb_ref,
               *, kv_channel_major, pool_here):
    y = _rms(x_ref[...], g_ref[...])
    h = (y * (1.0 + mod_ref[0, 1:2, :]) + mod_ref[0, 0:1, :]).astype(BF16)
    part = lambda i: _dot(h, w_ref[:, _IN_CUTS[i]:_IN_CUTS[i + 1]])
    q_ref[...] = (_head_rms(part(0), qg_ref[...], avg_ref[...], False) * HEAD_DIM ** -0.5).astype(q_ref.dtype)
    kn = _head_rms(part(1), kg_ref[...], avg_ref[...], True)
    v = part(2)
    if kv_channel_major:
        k_ref[0] = kn.T.reshape(k_ref.shape[1:])
        v_ref[0] = v.T.reshape(v_ref.shape[1:])
    else:
        k_ref[...] = kn.astype(k_ref.dtype)
        v_ref[...] = v.astype(v_ref.dtype)
    u = part(3)
    u_ref[...] = _pooled(u).astype(u_ref.dtype) if pool_here else u
    ga_ref[...] = _sigmoid(part(4)).astype(ga_ref.dtype)
    gb_ref[...] = _sigmoid(part(5)).astype(gb_ref.dtype)


def _in_proj(x2d, mods, norm_g, w_in_bf16, q_gain, k_gain, cond_of_block, sequence_blocks):
    n, d = x2d.shape
    tb = TOKEN_BLOCK
    widths = [hi - lo for lo, hi in zip(_IN_CUTS[:-1], _IN_CUTS[1:])]
    dtypes = [BF16, BF16, BF16, BF16 if sequence_blocks else F32, BF16, BF16]
    out_specs = [pl.BlockSpec((tb, w), lambda i: (i, 0)) for w in widths]
    out_shape = [jax.ShapeDtypeStruct((n, w), t) for w, t in zip(widths, dtypes)]
    if sequence_blocks:
        for i in (1, 2):
            out_specs[i] = pl.BlockSpec((1, NA_HEADS, HEAD_DIM, tb), lambda i: (i, 0, 0, 0))
            out_shape[i] = jax.ShapeDtypeStruct((n // tb, NA_HEADS, HEAD_DIM, tb), F32)
    full = lambda a: pl.BlockSpec(a.shape, lambda i: (0,) * a.ndim)
    return pl.pallas_call(
        functools.partial(_in_kernel, kv_channel_major=sequence_blocks, pool_here=sequence_blocks),
        grid=(n // tb,),
        in_specs=[pl.BlockSpec((tb, d), lambda i: (i, 0)),
                  pl.BlockSpec((1, N_MOD, d), lambda i: (cond_of_block(i), 0, 0)),
                  full(norm_g), full(w_in_bf16), full(q_gain), full(k_gain),
                  pl.BlockSpec((HEAD_AVG_WIDTH, HEAD_AVG_WIDTH), lambda i: (0, 0))],
        out_specs=out_specs,
        out_shape=out_shape,
        compiler_params=_params(("arbitrary",)),
        name="in_proj",
    )(x2d, mods, norm_g, w_in_bf16, q_gain, k_gain, _head_avg_matrix())


def _with_ones(v):
    lane = lax.broadcasted_iota(I32, v.shape, 1)
    return jnp.concatenate([v.astype(BF16), jnp.where(lane == 0, 1.0, 0.0).astype(BF16)], axis=-1)


def _attn_ctx_kernel(q_ref, kt_ref, vt_ref, att_ref):
    qn = q_ref[0]
    l = kt_ref.shape[3]
    ones_row = jnp.where(lax.broadcasted_iota(I32, (HEAD_DIM, l), 0) == 0, 1.0, 0.0)
    for h in range(NA_HEADS):
        sl = slice(h * HEAD_DIM, (h + 1) * HEAD_DIM)
        s = _dot(qn[:, sl], kt_ref[0, h].astype(BF16))
        e = jnp.exp(s - jnp.max(s, axis=-1, keepdims=True)).astype(BF16)
        vt = jnp.concatenate([vt_ref[0, h], ones_row], axis=0).astype(BF16)
        o = _dot_nt(e, vt)
        att_ref[0, :, sl] = (o[:, :HEAD_DIM] / o[:, HEAD_DIM:HEAD_DIM + 1]).astype(att_ref.dtype)


def _attn_ctx(q, kt, vt):
    b, l, w = q.shape
    blk = pl.BlockSpec((1, l, w), lambda i: (i, 0, 0))
    tblk = pl.BlockSpec((1,) + kt.shape[1:], lambda i: (i, 0, 0, 0))
    return pl.pallas_call(
        _attn_ctx_kernel,
        grid=(b,),
        in_specs=[blk, tblk, tblk],
        out_specs=blk,
        out_shape=jax.ShapeDtypeStruct((b, l, w), BF16),
        compiler_params=_params(("arbitrary",)),
        name="attn_ctx",
    )(q, kt, vt)


def _lat_windows(rows):
    kr = min(WIN_ROWS, rows)
    rb = min(LAT_ROW_BLOCK, rows)
    wr = min(kr + rb, rows)
    assert rows % rb == 0
    starts = [min(max(i * rb - kr // 2, 0), rows - wr) for i in range(rows // rb)]
    for i, ws in enumerate(starts):
        for r in range(i * rb, (i + 1) * rb):
            rs = min(max(r - kr // 2, 0), rows - kr)
            assert ws <= rs and rs + kr <= ws + wr
    return kr, rb, wr, starts


def _attn_lat_kernel(q_ref, k_ref, v_ref, ckt_ref, cvt_ref, bias_ref, att_ref,
                     kn_s, vb_s, ckb_s, cvb_s, *, rows, kr, rb, wr):
    i = pl.program_id(1)

    @pl.when(i == 0)
    def _():
        lc = ckt_ref.shape[3]
        ones_row = jnp.where(lax.broadcasted_iota(I32, (HEAD_DIM, lc), 0) == 0, 1.0, 0.0)
        for h in range(NA_HEADS):
            sl = slice(h * HEAD_DIM, (h + 1) * HEAD_DIM)
            kn_s[h] = k_ref[0, :, sl]
            vb_s[h] = _with_ones(v_ref[0, :, sl])
            ckb_s[h] = ckt_ref[0, h].astype(BF16)
            cvb_s[h] = jnp.concatenate([cvt_ref[0, h], ones_row], axis=0).astype(BF16)

    win_start = jnp.clip(i * rb - kr // 2, 0, rows - wr)
    key_rows = pl.ds(pl.multiple_of(win_start * GRID_W, GRID_W), wr * GRID_W)
    qn_all = q_ref[0]
    for h in range(NA_HEADS):
        sl = slice(h * HEAD_DIM, (h + 1) * HEAD_DIM)
        qn = qn_all[:, sl]
        bias = bias_ref[h, pl.ds(i * rb, rb)].reshape(rb * GRID_W, wr * GRID_W)
        s_loc = _dot_nt(qn, kn_s[h, key_rows, :]) + bias
        s_ctx = _dot(qn, ckb_s[h])
        m = jnp.maximum(jnp.max(s_loc, axis=-1, keepdims=True), jnp.max(s_ctx, axis=-1, keepdims=True))
        o = (_dot(jnp.exp(s_loc - m).astype(BF16), vb_s[h, key_rows, :])
             + _dot_nt(jnp.exp(s_ctx - m).astype(BF16), cvb_s[h]))
        att_ref[0, :, sl] = (o[:, :HEAD_DIM] / o[:, HEAD_DIM:HEAD_DIM + 1]).astype(att_ref.dtype)


def _attn_lat(q, k, v, ckt, cvt, bias):
    b, l, w = q.shape
    lc = ckt.shape[3]
    rows = l // GRID_W
    kr, rb, wr, _ = _lat_windows(rows)
    seq = pl.BlockSpec((1, l, w), lambda i, r: (i, 0, 0))
    ctx = pl.BlockSpec((1, NA_HEADS, HEAD_DIM, lc), lambda i, r: (i, 0, 0, 0))
    rowblk = pl.BlockSpec((1, rb * GRID_W, w), lambda i, r: (i, r, 0))
    return pl.pallas_call(
        functools.partial(_attn_lat_kernel, rows=rows, kr=kr, rb=rb, wr=wr),
        grid=(b, rows // rb),
        in_specs=[rowblk, seq, seq, ctx, ctx,
                  pl.BlockSpec(bias.shape, lambda i, r: (0, 0, 0, 0), pipeline_mode=pl.Buffered(1))],
        out_specs=rowblk,
        out_shape=jax.ShapeDtypeStruct((b, l, w), BF16),
        scratch_shapes=[pltpu.VMEM((NA_HEADS, l, HEAD_DIM), BF16), pltpu.VMEM((NA_HEADS, l, 2 * HEAD_DIM), BF16),
                        pltpu.VMEM((NA_HEADS, HEAD_DIM, lc), BF16), pltpu.VMEM((NA_HEADS, 2 * HEAD_DIM, lc), BF16)],
        compiler_params=_params(("arbitrary", "arbitrary")),
        name="attn_lat",
    )(q, k, v, ckt, cvt, bias)


def _bias_kernel(rpb_ref, o_ref, *, rows):
    kr, rb, wr, starts = _lat_windows(rows)
    h = pl.program_id(0)
    n_ro, n_co = 2 * WIN_ROWS - 1, 2 * WIN_COLS - 1
    wq = lax.broadcasted_iota(I32, (GRID_W, GRID_W), 0)
    wk = lax.broadcasted_iota(I32, (GRID_W, GRID_W), 1)
    co = jnp.clip(wk - wq, -(WIN_COLS - 1), WIN_COLS - 1) + (WIN_COLS - 1)
    col_start = jnp.clip(wq - WIN_COLS // 2, 0, GRID_W - WIN_COLS)
    in_win = (wk >= col_start) & (wk < col_start + WIN_COLS)
    co = jnp.where(in_win, co, -1)
    masked = jnp.full((GRID_W, GRID_W), NEG_INF, F32)
    tables = {}
    for r in range(rows):
        row_start = min(max(r - kr // 2, 0), rows - kr)
        for j in range(wr):
            key_row = starts[r // rb] + j
            blk = masked
            if row_start <= key_row < row_start + kr:
                ro = key_row - r + (WIN_ROWS - 1)
                if ro not in tables:
                    t = masked
                    for c in range(n_co):
                        t = jnp.where(co == c, rpb_ref[(h * n_ro + ro) * n_co + c], t)
                    tables[ro] = t
                blk = tables[ro]
            o_ref[0, r, :, j * GRID_W:(j + 1) * GRID_W] = blk


def _window_bias(rpb, rows):
    _, _, wr, _ = _lat_windows(rows)
    nh = rpb.shape[0]
    return pl.pallas_call(
        functools.partial(_bias_kernel, rows=rows),
        grid=(nh,),
        in_specs=[pl.BlockSpec(memory_space=pltpu.SMEM)],
        out_specs=pl.BlockSpec((1, rows, GRID_W, wr * GRID_W), lambda h: (h, 0, 0, 0)),
        out_shape=jax.ShapeDtypeStruct((nh, rows, GRID_W, wr * GRID_W), F32),
        compiler_params=_params(("arbitrary",)),
        name="window_bias",
    )(rpb.reshape(-1))


def _pool_kernel(u_ref, o_ref):
    o_ref[0] = _pooled(u_ref[0]).astype(o_ref.dtype)


def _pool(u):
    b, l, w = u.shape
    blk = pl.BlockSpec((1, l, w), lambda i: (i, 0, 0))
    return pl.pallas_call(
        _pool_kernel, grid=(b,), in_specs=[blk], out_specs=blk,
        out_shape=jax.ShapeDtypeStruct((b, l, w), BF16),
        compiler_params=_params(("arbitrary",)),
        name="pool",
    )(u)


def _merge_kernel(xc_ref, xl_ref, attc_ref, attl_ref, plc_ref, pll_ref, gac_ref, gal_ref, gbc_ref, gbl_ref,
                  mod_ref, watt_ref, wpool_ref, ps_ref, wout_ref, g2_ref, wr_ref, x1_ref, h2_ref, aff_ref, *, ctx_blocks):
    is_ctx = pl.program_id(0) < ctx_blocks

    def block(x_ref, att_ref, pl_ref, ga_ref, gb_ref):
        o_a = _dot(att_ref[...], watt_ref[...])
        o_b = jnp.concatenate(
            [_dot(pl_ref[:, g * POOL_GROUP_DIM:(g + 1) * POOL_GROUP_DIM], wpool_ref[g])
             for g in range(len(POOL_WINDOWS))], axis=-1) * ps_ref[...]
        merged = ga_ref[...] * o_a + gb_ref[...] * o_b
        x1 = x_ref[...] + mod_ref[0, 2:3, :] * _dot(merged.astype(BF16), wout_ref[...])
        x1_ref[...] = x1
        h2 = _rms(x1, g2_ref[...]) * (1.0 + mod_ref[0, 4:5, :]) + mod_ref[0, 3:4, :]
        h2_ref[...] = h2
        logits = _dot3(h2, wr_ref[...])
        e = jnp.exp(logits - jnp.max(logits, axis=-1, keepdims=True))
        aff_ref[...] = e / jnp.sum(e, axis=-1, keepdims=True)

    @pl.when(is_ctx)
    def _():
        block(xc_ref, attc_ref, plc_ref, gac_ref, gbc_ref)

    @pl.when(jnp.logical_not(is_ctx))
    def _():
        block(xl_ref, attl_ref, pll_ref, gal_ref, gbl_ref)


def _merge(ctx, lat, mods, w_att, w_pool, pool_scale, w_out, norm2_g, w_router, lat_blocks_per_seq):
    n_c, d = ctx[0].shape
    n = n_c + lat[0].shape[0]
    tb = TOKEN_BLOCK
    cb = n_c // tb
    crow = lambda w: pl.BlockSpec((tb, w), lambda i: (jnp.minimum(i, cb - 1), 0))
    lrow = lambda w: pl.BlockSpec((tb, w), lambda i: (jnp.maximum(i - cb, 0), 0))
    orow = lambda w: pl.BlockSpec((tb, w), lambda i: (i, 0))
    full = lambda a: pl.BlockSpec(a.shape, lambda i: (0,) * a.ndim)
    cond = lambda i: jnp.where(i < cb, 0, 1 + (i - cb) // lat_blocks_per_seq)
    pairs, specs = [], []
    for a_c, a_l in zip(ctx, lat):
        pairs += [a_c, a_l]
        specs += [crow(a_c.shape[1]), lrow(a_l.shape[1])]
    return pl.pallas_call(
        functools.partial(_merge_kernel, ctx_blocks=cb),
        grid=(n // tb,),
        in_specs=specs + [pl.BlockSpec((1, N_MOD, d), lambda i: (cond(i), 0, 0)),
                          full(w_att), full(w_pool), full(pool_scale), full(w_out), full(norm2_g), full(w_router)],
        out_specs=[orow(d), orow(d), orow(N_EXPERTS)],
        out_shape=[jax.ShapeDtypeStruct((n, d), F32), jax.ShapeDtypeStruct((n, d), F32),
                   jax.ShapeDtypeStruct((n, N_EXPERTS), F32)],
        compiler_params=_params(("arbitrary",)),
        name="merge",
    )(*pairs, mods, w_att, w_pool, pool_scale, w_out, norm2_g, w_router)


def _prefix_incl(mask_f32, out_ref):
    e, n = mask_f32.shape
    i = lax.broadcasted_iota(I32, (128, 128), 0)
    j = lax.broadcasted_iota(I32, (128, 128), 1)
    tri = jnp.where(i <= j, 1.0, 0.0).astype(BF16)
    carry = jnp.zeros((e, 1), F32)
    for c in range(n // 128):
        inc = _dot(mask_f32[:, c * 128:(c + 1) * 128].astype(BF16), tri) + carry
        out_ref[:, c * 128:(c + 1) * 128] = inc.astype(I32)
        carry = inc[:, 127:128]


def _route_kernel(aff_ref, sel_ref, rinc_ref, tmp_ref, *, cap):
    aff = aff_ref[...]

    def bit_step(i, thr):
        cand = thr | jnp.left_shift(jnp.int32(1), 30 - i)
        cnt = jnp.sum(jnp.where(aff >= pltpu.bitcast(cand, F32), 1, 0), axis=1, keepdims=True)
        return jnp.where(cnt >= cap, cand, thr)

    thr = lax.fori_loop(0, 31, bit_step, jnp.zeros((aff.shape[0], 1), I32))
    gt = aff >= pltpu.bitcast(thr + 1, F32)
    eq = (aff >= pltpu.bitcast(thr, F32)) & jnp.logical_not(gt)
    need = cap - jnp.sum(jnp.where(gt, 1, 0), axis=1, keepdims=True)
    _prefix_incl(jnp.where(eq, 1.0, 0.0), tmp_ref)
    sel = gt | (eq & (tmp_ref[...] <= need))
    sel_ref[...] = jnp.where(sel, 1, 0)
    _prefix_incl(jnp.where(sel, 1.0, 0.0), rinc_ref)


def _route(aff_t, cap):
    e, n = aff_t.shape
    full = pl.BlockSpec((e, n), lambda: (0, 0))
    return pl.pallas_call(
        functools.partial(_route_kernel, cap=cap),
        in_specs=[full], out_specs=[full, full],
        out_shape=[jax.ShapeDtypeStruct((e, n), I32)] * 2,
        scratch_shapes=[pltpu.VMEM((e, n), I32)],
        compiler_params=pltpu.CompilerParams(vmem_limit_bytes=VMEM_LIMIT),
        name="route",
    )(aff_t)


def _moe_kernel(gidx_ref, qpos_ref, h_ref, gate_ref, wg_ref, wu_ref, wd_ref, z_ref,
                xf_ref, xb_ref, acc_ref, y_ref, sem_g, sem_s, *, m_tot, n_f, step_rows, z_zero_rows):
    e = pl.program_id(0)
    j = pl.program_id(1)
    n_e = pl.num_programs(0)
    m_iss = step_rows * n_f
    tf = wg_ref.shape[1]
    slot = e % 2

    def gather_row(lst, s, dst_slot):
        t = gidx_ref[lst * m_iss + s]
        return pltpu.make_async_copy(h_ref.at[pl.ds(t, 1)], xf_ref.at[dst_slot, pl.ds(s, 1)], sem_g.at[dst_slot])

    def scatter_row(lst, s):
        q = qpos_ref[lst * m_iss + s]
        return pltpu.make_async_copy(y_ref.at[pl.ds(s, 1)], z_ref.at[pl.ds(q, 1)], sem_s)

    def wait_gathers(dst_slot):
        pltpu.make_async_copy(h_ref.at[pl.ds(0, m_iss)], xf_ref.at[dst_slot], sem_g.at[dst_slot]).wait()

    def wait_scatters():
        pltpu.make_async_copy(y_ref, z_ref.at[pl.ds(0, m_iss)], sem_s).wait()

    @pl.when((e == 0) & (j == 0))
    def _():
        y_ref[...] = jnp.zeros_like(y_ref)
        cp = pltpu.make_async_copy(y_ref.at[pl.ds(0, z_zero_rows)], z_ref.at[pl.ds(n_e * m_tot, z_zero_rows)], sem_s)
        cp.start()
        cp.wait()

        def first(s, c):
            gather_row(0, s, 0).start()
            return c

        lax.fori_loop(0, m_iss, first, 0)

    @pl.when(j == 0)
    def _():
        wait_gathers(slot)
        xb_ref[...] = xf_ref[slot, 0:m_tot, :].astype(BF16)
        acc_ref[...] = jnp.zeros_like(acc_ref)

    nxt = lax.rem(e + 1, n_e)
    for u in range(step_rows):
        s = j * step_rows + u
        gather_row(nxt, s, 1 - slot).start()
        scatter_row(e, s).start()

    valid = D_EXPERT - j * tf
    cmask = lax.broadcasted_iota(I32, (1, tf), 1) < valid
    rmask = lax.broadcasted_iota(I32, (tf, 1), 0) < valid
    wg = wg_ref[0].astype(BF16)
    wu = wu_ref[0].astype(BF16)
    wd = jnp.where(rmask, wd_ref[0], 0.0).astype(BF16)
    for m in range(m_tot // EXPERT_M_BLOCK):
        rs = slice(m * EXPERT_M_BLOCK, (m + 1) * EXPERT_M_BLOCK)
        xs = xb_ref[rs, :]
        a = _dot_nt(xs, wg)
        b = _dot_nt(xs, wu)
        hm = jnp.where(cmask, a * _sigmoid(a) * b, 0.0).astype(BF16)
        acc_ref[rs, :] += _dot(hm, wd)

    @pl.when(j == n_f - 1)
    def _():
        wait_scatters()
        y_ref[0:m_tot, :] = _pack_halves(acc_ref[...] * gate_ref[0])

        @pl.when(e == n_e - 1)
        def _():
            def last(s, c):
                scatter_row(n_e, s).start()
                return c

            lax.fori_loop(0, m_iss, last, 0)
            wait_scatters()
            wait_gathers(1 - slot)


def _moe_step_rows(m_tot, n_f):
    return pl.cdiv(pl.cdiv(m_tot, n_f), ROW_ALIGN) * ROW_ALIGN


def _moe(gidx, qpos, h, gate, w_gate, w_up, w_down, m_tot, z_rows, z_zero_rows):
    d = h.shape[1]
    n_e = w_down.shape[0]
    tf = EXPERT_F_BLOCK
    n_f = pl.cdiv(D_EXPERT, tf)
    step_rows = _moe_step_rows(m_tot, n_f)
    m_iss = step_rows * n_f
    grid_spec = pltpu.PrefetchScalarGridSpec(
        num_scalar_prefetch=2,
        grid=(n_e, n_f),
        in_specs=[pl.BlockSpec(memory_space=pl.ANY),
                  pl.BlockSpec((1, m_tot, 1), lambda e, j, *_: (e, 0, 0)),
                  pl.BlockSpec((1, tf, d), lambda e, j, *_: (e, j, 0)),
                  pl.BlockSpec((1, tf, d), lambda e, j, *_: (e, j, 0)),
                  pl.BlockSpec((1, tf, d), lambda e, j, *_: (e, j, 0))],
        out_specs=pl.BlockSpec(memory_space=pl.ANY),
        scratch_shapes=[pltpu.VMEM((2, m_iss, d), F32), pltpu.VMEM((m_tot, d), BF16), pltpu.VMEM((m_tot, d), F32),
                        pltpu.VMEM((m_iss, d // 2), jnp.uint32), pltpu.SemaphoreType.DMA((2,)),
                        pltpu.SemaphoreType.DMA],
    )
    return pl.pallas_call(
        functools.partial(_moe_kernel, m_tot=m_tot, n_f=n_f, step_rows=step_rows, z_zero_rows=z_zero_rows),
        grid_spec=grid_spec,
        out_shape=jax.ShapeDtypeStruct((z_rows, d // 2), jnp.uint32),
        compiler_params=_params(("arbitrary", "arbitrary")),
        name="moe",
    )(gidx, qpos, h, gate, w_gate, w_up, w_down)


def _combine_kernel(w0_ref, nc_ref, first_ref, z_ref, x1_ref, mod_ref, qt_ref, kt_ref, o_ref, zbuf, sem, acc_ref):
    blk = pl.program_id(0)
    n_blk = pl.num_programs(0)
    w0 = w0_ref[blk]
    n_chunks = nc_ref[blk]
    first = first_ref[blk]
    qt = qt_ref[...]
    kt = kt_ref[...]
    acc_ref[...] = jnp.zeros_like(acc_ref)

    def z_copy(start, slot):
        rows = pl.ds(pl.multiple_of(start, ROW_ALIGN), Z_CHUNK)
        return pltpu.make_async_copy(z_ref.at[rows], zbuf.at[slot], sem.at[slot])

    @pl.when(blk == 0)
    def _():
        z_copy(w0, first % 2).start()

    def chunk(c, carry):
        slot = (first + c) % 2
        last = c + 1 == n_chunks
        next_start = jnp.where(last, w0_ref[jnp.minimum(blk + 1, n_blk - 1)], w0 + (c + 1) * Z_CHUNK)

        @pl.when(jnp.logical_not(last) | (blk + 1 < n_blk))
        def _():
            z_copy(next_start, 1 - slot).start()

        z_copy(w0 + c * Z_CHUNK, slot).wait()
        r = w0 + c * Z_CHUNK + lax.broadcasted_iota(I32, (qt.shape[0], Z_CHUNK), 1)
        onehot = jnp.where((r >= qt) & (r < qt + kt), 1.0, 0.0).astype(BF16)
        left, right = _unpack_halves(zbuf[slot])
        half = left.shape[1]
        acc_ref[:, :half] += _dot(onehot, left)
        acc_ref[:, half:] += _dot(onehot, right)
        return carry

    lax.fori_loop(0, n_chunks, chunk, 0)
    o_ref[...] = x1_ref[...] + mod_ref[0, 5:6, :] * acc_ref[...]


def _combine(w0, nc, z, x1, first_block, mods, qt, kt, cond_of_block):
    n = qt.shape[0]
    d = x1.shape[1]
    tb = COMBINE_BLOCK
    grid_spec = pltpu.PrefetchScalarGridSpec(
        num_scalar_prefetch=3,
        grid=(n // tb,),
        in_specs=[pl.BlockSpec(memory_space=pl.ANY),
                  pl.BlockSpec((tb, d), lambda i, *_: (i + first_block, 0)),
                  pl.BlockSpec((1, N_MOD, d), lambda i, *_: (cond_of_block(i), 0, 0)),
                  pl.BlockSpec((tb, 1), lambda i, *_: (i, 0)),
                  pl.BlockSpec((tb, 1), lambda i, *_: (i, 0))],
        out_specs=pl.BlockSpec((tb, d), lambda i, *_: (i, 0)),
        scratch_shapes=[pltpu.VMEM((2, Z_CHUNK, z.shape[1]), z.dtype), pltpu.SemaphoreType.DMA((2,)),
                        pltpu.VMEM((tb, d), F32)],
    )
    return pl.pallas_call(
        _combine_kernel,
        grid_spec=grid_spec,
        out_shape=jax.ShapeDtypeStruct((n, d), F32),
        compiler_params=_params(("arbitrary",)),
        name="combine",
    )(w0, nc, jnp.cumsum(nc) - nc, z, x1, mods, qt, kt)


def _routing_tables(aff, cap, z_off):
    n = aff.shape[0]
    aff_t = aff.T
    sel, rinc = _route(aff_t, cap)
    slots = jnp.arange(cap, dtype=I32)
    chunk = 128
    rinc3 = rinc.reshape(rinc.shape[0], n // chunk, chunk)
    n_full = jnp.sum((rinc3[:, None, :, chunk - 1] <= slots[None, :, None]).astype(I32), axis=-1)
    pick = (n_full[:, :, None] == jnp.arange(n // chunk, dtype=I32)[None, None, :]).astype(F32)
    inside = jnp.einsum("eca,eab->ecb", pick, rinc3.astype(F32), precision=lax.Precision.HIGHEST)
    idx = n_full * chunk + jnp.sum((inside <= slots[None, :, None].astype(F32)).astype(I32), axis=-1)
    gate = jnp.take_along_axis(aff_t, idx, axis=1)
    k_tok = jnp.sum(sel, axis=0)
    q_tok = jnp.cumsum(k_tok) - k_tok + z_off
    before = jnp.cumsum(sel, axis=0) - sel
    qpos = jnp.take_along_axis(q_tok[None, :] + before, idx, axis=1)
    nblk = n // COMBINE_BLOCK
    q_blk = q_tok[::COMBINE_BLOCK]
    end_blk = q_blk + jnp.sum(k_tok.reshape(nblk, COMBINE_BLOCK), axis=1)
    w0 = (q_blk // ROW_ALIGN) * ROW_ALIGN
    nc = jnp.maximum((end_blk - w0 + Z_CHUNK - 1) // Z_CHUNK, 1)
    return idx, gate, qpos.astype(I32), w0.astype(I32), nc.astype(I32), q_tok.astype(I32)[:, None], k_tok.astype(I32)[:, None]


def kernel(x_prompt, x_sample, cache_k, cache_v, c, c_ctx, norm1_g, norm2_g, w_ada, b_ada, w_in, q_norm_g, k_norm_g,
           rpb, w_att_proj, w_pool, pool_scale, w_out, w_router, w_gate_e, w_up_e, w_down_e):
    assert w_ada.shape[0] == 1, "single trunk layer"
    nb, ls, d = x_prompt.shape
    db, ll, _ = x_sample.shape
    n_c, n_l = nb * ls, db * ll
    assert ls == TOKEN_BLOCK, "context keys / values are written one sequence per projection block"

    cond = jnp.concatenate([c_ctx[None, :], c, jnp.zeros((16 - 1 - db, d), F32)], axis=0)
    mods = _ada(cond, w_ada[0], b_ada).reshape(16, N_MOD, d)

    w_in_b = w_in[0].astype(BF16)
    w_att_b = w_att_proj[0].astype(BF16)
    w_pool_b = w_pool[0].astype(BF16)
    w_out_b = w_out[0].astype(BF16)

    cond_ctx = lambda i: 0
    lat_blocks = ll // TOKEN_BLOCK
    cond_lat = lambda i: 1 + i // lat_blocks

    xc = x_prompt.reshape(n_c, d)
    xl = x_sample.reshape(n_l, d)
    q_gain = jnp.tile(q_norm_g, (1, NA_HEADS))
    k_gain = jnp.tile(k_norm_g, (1, NA_HEADS))
    qc, kc, vc, uc, gac, gbc = _in_proj(xc, mods, norm1_g, w_in_b, q_gain, k_gain, cond_ctx, True)
    ql, kl, vl, ul, gal, gbl = _in_proj(xl, mods, norm1_g, w_in_b, q_gain, k_gain, cond_lat, False)

    seq_c = lambda a: a.reshape(nb, ls, a.shape[-1])
    seq_l = lambda a: a.reshape(db, ll, a.shape[-1])
    att_c = _attn_ctx(seq_c(qc), kc, vc)
    bias = _window_bias(rpb[0], ll // GRID_W)
    att_l = _attn_lat(seq_l(ql), seq_l(kl), seq_l(vl), jnp.transpose(cache_k[:, 0], (0, 2, 3, 1)),
                      jnp.transpose(cache_v[:, 0], (0, 2, 3, 1)), bias)
    pool_c = uc
    pool_l = _pool(seq_l(ul)).reshape(n_l, POOL_WIDTH)

    x1, h2, aff = _merge((xc, att_c.reshape(n_c, NA_WIDTH), pool_c, gac, gbc),
                         (xl, att_l.reshape(n_l, NA_WIDTH), pool_l, gal, gbl),
                         mods, w_att_b, w_pool_b, pool_scale, w_out_b, norm2_g, w_router[0], lat_blocks)

    cap_c = max(1, (CAPACITY_FACTOR * n_c) // N_EXPERTS)
    cap_l = max(1, (CAPACITY_FACTOR * n_l) // N_EXPERTS)
    m_tot = cap_c + cap_l
    m_iss = _moe_step_rows(m_tot, pl.cdiv(D_EXPERT, EXPERT_F_BLOCK)) * pl.cdiv(D_EXPERT, EXPERT_F_BLOCK)
    z_valid = N_EXPERTS * m_tot
    z_zero_rows = Z_CHUNK + ROW_ALIGN
    z_spare = z_valid + z_zero_rows
    z_rows = z_spare + m_iss
    idx_c, gate_c, qpos_c, w0_c, nc_c, qt_c, kt_c = _routing_tables(aff[:n_c], cap_c, 0)
    idx_l, gate_l, qpos_l, w0_l, nc_l, qt_l, kt_l = _routing_tables(aff[n_c:], cap_l, N_EXPERTS * cap_c)
    spare = jnp.broadcast_to(z_spare + jnp.arange(m_iss, dtype=I32), (N_EXPERTS + 1, m_iss))
    gidx = jnp.concatenate([idx_c, idx_l + n_c, jnp.zeros((N_EXPERTS, m_iss - m_tot), I32)], axis=1).reshape(-1)
    qpos = jnp.concatenate([spare[:1], jnp.concatenate([qpos_c, qpos_l, spare[1:, m_tot:]], axis=1)], axis=0).reshape(-1)
    gate = jnp.concatenate([gate_c, gate_l], axis=1)[:, :, None]

    z = _moe(gidx, qpos, h2, gate, jnp.swapaxes(w_gate_e[0], 1, 2), jnp.swapaxes(w_up_e[0], 1, 2), w_down_e[0],
             m_tot, z_rows, z_zero_rows)

    y_c = _combine(w0_c, nc_c, z, x1, 0, mods, qt_c, kt_c, cond_ctx)
    y_l = _combine(w0_l, nc_l, z, x1, n_c // COMBINE_BLOCK, mods, qt_l, kt_l, lambda i: 1 + i // (ll // COMBINE_BLOCK))

    state_k = jnp.transpose(kc, (0, 3, 1, 2))[:, None]
    state_v = jnp.transpose(vc, (0, 3, 1, 2))[:, None]
    return (y_c.reshape(nb, ls, d), y_l.reshape(db, ll, d), state_k, state_v)
```

```python
import functools

import jax
import jax.numpy as jnp
from jax import lax
from jax.experimental import pallas as pl
from jax.experimental.pallas import tpu as pltpu

F32 = jnp.float32
BF16 = jnp.bfloat16
I32 = jnp.int32

D_MODEL = 1024
GRID_W = 64
NA_HEADS = 8
HEAD_DIM = 64
NA_WIDTH = NA_HEADS * HEAD_DIM
WIN_ROWS = 8
WIN_COLS = 16
POOL_WINDOWS = (2, 4, 8, 16)
POOL_GROUP_DIM = 128
POOL_WIDTH = 512
POOL_OUT_DIM = 256
N_EXPERTS = 16
CAPACITY_FACTOR = 2
D_EXPERT = 2752
N_MOD = 6
EPS = 1e-6
NEG_INF = -1e30

TOKEN_BLOCK = 256
COMBINE_BLOCK = 256
LAT_ROW_BLOCK = 4
EXPERT_F_BLOCK = 256
EXPERT_M_BLOCK = 768
Z_CHUNK = 256
ROW_ALIGN = 8
VMEM_LIMIT = 56 * 1024 * 1024


def _dot(a, b):
    return jnp.dot(a, b, preferred_element_type=F32)


def _dot_nt(a, b):
    return lax.dot_general(a, b, (((1,), (1,)), ((), ())), preferred_element_type=F32)


def _dot3(a, b):
    a_hi = a.astype(BF16)
    a_lo = (a - a_hi.astype(F32)).astype(BF16)
    b_hi = b.astype(BF16)
    b_lo = (b - b_hi.astype(F32)).astype(BF16)
    return _dot(a_hi, b_hi) + _dot(a_hi, b_lo) + _dot(a_lo, b_hi)


def _sigmoid(x):
    return 0.5 * jnp.tanh(0.5 * x) + 0.5


def _rms(x, g):
    return x * lax.rsqrt(jnp.mean(x * x, axis=-1, keepdims=True) + EPS) * g


def _pack_halves(x):
    w = x.shape[1] // 2
    return pltpu.pack_elementwise([x[:, :w], x[:, w:]], packed_dtype=BF16)


def _unpack_halves(p):
    return tuple(pltpu.unpack_elementwise(p, index=i, packed_dtype=BF16, unpacked_dtype=F32).astype(BF16)
                 for i in range(2))


def _params(sem):
    return pltpu.CompilerParams(dimension_semantics=sem, vmem_limit_bytes=VMEM_LIMIT)


def _ada_kernel(cond_ref, w_ref, b_ref, o_ref):
    c = cond_ref[...]
    o_ref[...] = _dot3(c * _sigmoid(c), w_ref[...]) + b_ref[...]


def _ada(cond, w_ada, b_ada):
    rows, d = cond.shape
    n_out = w_ada.shape[1]
    bn = 512
    return pl.pallas_call(
        _ada_kernel,
        grid=(n_out // bn,),
        in_specs=[pl.BlockSpec((rows, d), lambda i: (0, 0)),
                  pl.BlockSpec((d, bn), lambda i: (0, i)),
                  pl.BlockSpec((1, bn), lambda i: (0, i))],
        out_specs=pl.BlockSpec((rows, bn), lambda i: (0, i)),
        out_shape=jax.ShapeDtypeStruct((rows, n_out), F32),
        compiler_params=_params(("arbitrary",)),
        name="ada",
    )(cond, w_ada, b_ada)


_IN_CUTS = (0, 512, 1024, 1536, 2048, 3072, 4096)


HEAD_AVG_WIDTH = 256


def _head_avg_matrix():
    head = jnp.arange(HEAD_AVG_WIDTH, dtype=I32) // HEAD_DIM
    return jnp.where(head[:, None] == head[None, :], 1.0 / HEAD_DIM, 0.0).astype(BF16)


def _head_rms(x, g, avg, two_term):
    x2 = x * x
    means = []
    for c in range(0, x.shape[1], HEAD_AVG_WIDTH):
        blk = x2[:, c:c + HEAD_AVG_WIDTH]
        hi = blk.astype(BF16)
        ms = _dot(hi, avg)
        if two_term:
            ms = ms + _dot((blk - hi.astype(F32)).astype(BF16), avg)
        means.append(ms)
    return x * lax.rsqrt(jnp.concatenate(means, axis=-1) + EPS) * g


def _pooled(u):
    l = u.shape[0]
    t = lax.broadcasted_iota(I32, (l, POOL_GROUP_DIM), 0)
    groups = []
    for g, w in enumerate(POOL_WINDOWS):
        x = u[:, g * POOL_GROUP_DIM:(g + 1) * POOL_GROUP_DIM]
        acc = x
        for k in range(-(w // 2), w - w // 2):
            if k == 0:
                continue
            shifted = pltpu.roll(x, (-k) % l, 0)
            acc = acc + jnp.where((t + k >= 0) & (t + k < l), shifted, 0.0)
        cnt = (jnp.minimum(t + (w - w // 2), l) - jnp.maximum(t - w // 2, 0)).astype(F32)
        groups.append(acc / cnt - x)
    return jnp.concatenate(groups, axis=-1)


def _in_kernel(x_ref, mod_ref, g_ref, w_ref, qg_ref, kg_ref, avg_ref, q_ref, k_ref, v_ref, u_ref, ga_ref, gb_ref,
               *, kv_channel_major, pool_here):
    y = _rms(x_ref[...], g_ref[...])
    h = (y * (1.0 + mod_ref[0, 1:2, :]) + mod_ref[0, 0:1, :]).astype(BF16)
    part = lambda i: _dot(h, w_ref[:, _IN_CUTS[i]:_IN_CUTS[i + 1]])
    q_ref[...] = (_head_rms(part(0), qg_ref[...], avg_ref[...], False) * HEAD_DIM ** -0.5).astype(q_ref.dtype)
    kn = _head_rms(part(1), kg_ref[...], avg_ref[...], True)
    v = part(2)
    if kv_channel_major:
        k_ref[0] = kn.T.reshape(k_ref.shape[1:])
        v_ref[0] = v.T.reshape(v_ref.shape[1:])
    else:
        k_ref[...] = kn.astype(k_ref.dtype)
        v_ref[...] = v.astype(v_ref.dtype)
    u = part(3)
    u_ref[...] = _pooled(u).astype(u_ref.dtype) if pool_here else u
    ga_ref[...] = _sigmoid(part(4)).astype(ga_ref.dtype)
    gb_ref[...] = _sigmoid(part(5)).astype(gb_ref.dtype)


def _in_proj(x2d, mods, norm_g, w_in_bf16, q_gain, k_gain, cond_of_block, sequence_blocks):
    n, d = x2d.shape
    tb = TOKEN_BLOCK
    widths = [hi - lo for lo, hi in zip(_IN_CUTS[:-1], _IN_CUTS[1:])]
    dtypes = [BF16, BF16, BF16, BF16 if sequence_blocks else F32, BF16, BF16]
    out_specs = [pl.BlockSpec((tb, w), lambda i: (i, 0)) for w in widths]
    out_shape = [jax.ShapeDtypeStruct((n, w), t) for w, t in zip(widths, dtypes)]
    if sequence_blocks:
        for i in (1, 2):
            out_specs[i] = pl.BlockSpec((1, NA_HEADS, HEAD_DIM, tb), lambda i: (i, 0, 0, 0))
            out_shape[i] = jax.ShapeDtypeStruct((n // tb, NA_HEADS, HEAD_DIM, tb), F32)
    full = lambda a: pl.BlockSpec(a.shape, lambda i: (0,) * a.ndim)
    return pl.pallas_call(
        functools.partial(_in_kernel, kv_channel_major=sequence_blocks, pool_here=sequence_blocks),
        grid=(n // tb,),
        in_specs=[pl.BlockSpec((tb, d), lambda i: (i, 0)),
                  pl.BlockSpec((1, N_MOD, d), lambda i: (cond_of_block(i), 0, 0)),
                  full(norm_g), full(w_in_bf16), full(q_gain), full(k_gain),
                  pl.BlockSpec((HEAD_AVG_WIDTH, HEAD_AVG_WIDTH), lambda i: (0, 0))],
        out_specs=out_specs,
        out_shape=out_shape,
        compiler_params=_params(("arbitrary",)),
        name="in_proj",
    )(x2d, mods, norm_g, w_in_bf16, q_gain, k_gain, _head_avg_matrix())


def _with_ones(v):
    lane = lax.broadcasted_iota(I32, v.shape, 1)
    return jnp.concatenate([v.astype(BF16), jnp.where(lane == 0, 1.0, 0.0).astype(BF16)], axis=-1)


def _attn_ctx_kernel(q_ref, kt_ref, vt_ref, att_ref):
    qn = q_ref[0]
    l = kt_ref.shape[3]
    ones_row = jnp.where(lax.broadcasted_iota(I32, (HEAD_DIM, l), 0) == 0, 1.0, 0.0)
    for h in range(NA_HEADS):
        sl = slice(h * HEAD_DIM, (h + 1) * HEAD_DIM)
        s = _dot(qn[:, sl], kt_ref[0, h].astype(BF16))
        e = jnp.exp(s - jnp.max(s, axis=-1, keepdims=True)).astype(BF16)
        vt = jnp.concatenate([vt_ref[0, h], ones_row], axis=0).astype(BF16)
        o = _dot_nt(e, vt)
        att_ref[0, :, sl] = (o[:, :HEAD_DIM] / o[:, HEAD_DIM:HEAD_DIM + 1]).astype(att_ref.dtype)


def _attn_ctx(q, kt, vt):
    b, l, w = q.shape
    blk = pl.BlockSpec((1, l, w), lambda i: (i, 0, 0))
    tblk = pl.BlockSpec((1,) + kt.shape[1:], lambda i: (i, 0, 0, 0))
    return pl.pallas_call(
        _attn_ctx_kernel,
        grid=(b,),
        in_specs=[blk, tblk, tblk],
        out_specs=blk,
        out_shape=jax.ShapeDtypeStruct((b, l, w), BF16),
        compiler_params=_params(("arbitrary",)),
        name="attn_ctx",
    )(q, kt, vt)


def _lat_windows(rows):
    kr = min(WIN_ROWS, rows)
    rb = min(LAT_ROW_BLOCK, rows)
    wr = min(kr + rb, rows)
    assert rows % rb == 0
    starts = [min(max(i * rb - kr // 2, 0), rows - wr) for i in range(rows // rb)]
    for i, ws in enumerate(starts):
        for r in range(i * rb, (i + 1) * rb):
            rs = min(max(r - kr // 2, 0), rows - kr)
            assert ws <= rs and rs + kr <= ws + wr
    return kr, rb, wr, starts


def _attn_lat_kernel(q_ref, k_ref, v_ref, ckt_ref, cvt_ref, bias_ref, att_ref,
                     kn_s, vb_s, ckb_s, cvb_s, *, rows, kr, rb, wr):
    i = pl.program_id(1)

    @pl.when(i == 0)
    def _():
        lc = ckt_ref.shape[3]
        ones_row = jnp.where(lax.broadcasted_iota(I32, (HEAD_DIM, lc), 0) == 0, 1.0, 0.0)
        for h in range(NA_HEADS):
            sl = slice(h * HEAD_DIM, (h + 1) * HEAD_DIM)
            kn_s[h] = k_ref[0, :, sl]
            vb_s[h] = _with_ones(v_ref[0, :, sl])
            ckb_s[h] = ckt_ref[0, h].astype(BF16)
            cvb_s[h] = jnp.concatenate([cvt_ref[0, h], ones_row], axis=0).astype(BF16)

    win_start = jnp.clip(i * rb - kr // 2, 0, rows - wr)
    key_rows = pl.ds(pl.multiple_of(win_start * GRID_W, GRID_W), wr * GRID_W)
    qn_all = q_ref[0]
    for h in range(NA_HEADS):
        sl = slice(h * HEAD_DIM, (h + 1) * HEAD_DIM)
        qn = qn_all[:, sl]
        bias = bias_ref[h, pl.ds(i * rb, rb)].reshape(rb * GRID_W, wr * GRID_W)
        s_loc = _dot_nt(qn, kn_s[h, key_rows, :]) + bias
        s_ctx = _dot(qn, ckb_s[h])
        m = jnp.maximum(jnp.max(s_loc, axis=-1, keepdims=True), jnp.max(s_ctx, axis=-1, keepdims=True))
        o = (_dot(jnp.exp(s_loc - m).astype(BF16), vb_s[h, key_rows, :])
             + _dot_nt(jnp.exp(s_ctx - m).astype(BF16), cvb_s[h]))
        att_ref[0, :, sl] = (o[:, :HEAD_DIM] / o[:, HEAD_DIM:HEAD_DIM + 1]).astype(att_ref.dtype)


def _attn_lat(q, k, v, ckt, cvt, bias):
    b, l, w = q.shape
    lc = ckt.shape[3]
    rows = l // GRID_W
    kr, rb, wr, _ = _lat_windows(rows)
    seq = pl.BlockSpec((1, l, w), lambda i, r: (i, 0, 0))
    ctx = pl.BlockSpec((1, NA_HEADS, HEAD_DIM, lc), lambda i, r: (i, 0, 0, 0))
    rowblk = pl.BlockSpec((1, rb * GRID_W, w), lambda i, r: (i, r, 0))
    return pl.pallas_call(
        functools.partial(_attn_lat_kernel, rows=rows, kr=kr, rb=rb, wr=wr),
        grid=(b, rows // rb),
        in_specs=[rowblk, seq, seq, ctx, ctx,
                  pl.BlockSpec(bias.shape, lambda i, r: (0, 0, 0, 0), pipeline_mode=pl.Buffered(1))],
        out_specs=rowblk,
        out_shape=jax.ShapeDtypeStruct((b, l, w), BF16),
        scratch_shapes=[pltpu.VMEM((NA_HEADS, l, HEAD_DIM), BF16), pltpu.VMEM((NA_HEADS, l, 2 * HEAD_DIM), BF16),
                        pltpu.VMEM((NA_HEADS, HEAD_DIM, lc), BF16), pltpu.VMEM((NA_HEADS, 2 * HEAD_DIM, lc), BF16)],
        compiler_params=_params(("arbitrary", "arbitrary")),
        name="attn_lat",
    )(q, k, v, ckt, cvt, bias)


def _bias_kernel(rpb_ref, o_ref, *, rows):
    kr, rb, wr, starts = _lat_windows(rows)
    h = pl.program_id(0)
    n_ro, n_co = 2 * WIN_ROWS - 1, 2 * WIN_COLS - 1
    wq = lax.broadcasted_iota(I32, (GRID_W, GRID_W), 0)
    wk = lax.broadcasted_iota(I32, (GRID_W, GRID_W), 1)
    co = jnp.clip(wk - wq, -(WIN_COLS - 1), WIN_COLS - 1) + (WIN_COLS - 1)
    col_start = jnp.clip(wq - WIN_COLS // 2, 0, GRID_W - WIN_COLS)
    in_win = (wk >= col_start) & (wk < col_start + WIN_COLS)
    co = jnp.where(in_win, co, -1)
    masked = jnp.full((GRID_W, GRID_W), NEG_INF, F32)
    tables = {}
    for r in range(rows):
        row_start = min(max(r - kr // 2, 0), rows - kr)
        for j in range(wr):
            key_row = starts[r // rb] + j
            blk = masked
            if row_start <= key_row < row_start + kr:
                ro = key_row - r + (WIN_ROWS - 1)
                if ro not in tables:
                    t = masked
                    for c in range(n_co):
                        t = jnp.where(co == c, rpb_ref[(h * n_ro + ro) * n_co + c], t)
                    tables[ro] = t
                blk = tables[ro]
            o_ref[0, r, :, j * GRID_W:(j + 1) * GRID_W] = blk


def _window_bias(rpb, rows):
    _, _, wr, _ = _lat_windows(rows)
    nh = rpb.shape[0]
    return pl.pallas_call(
        functools.partial(_bias_kernel, rows=rows),
        grid=(nh,),
        in_specs=[pl.BlockSpec(memory_space=pltpu.SMEM)],
        out_specs=pl.BlockSpec((1, rows, GRID_W, wr * GRID_W), lambda h: (h, 0, 0, 0)),
        out_shape=jax.ShapeDtypeStruct((nh, rows, GRID_W, wr * GRID_W), F32),
        compiler_params=_params(("arbitrary",)),
        name="window_bias",
    )(rpb.reshape(-1))


def _pool_kernel(u_ref, o_ref):
    o_ref[0] = _pooled(u_ref[0]).astype(o_ref.dtype)


def _pool(u):
    b, l, w = u.shape
    blk = pl.BlockSpec((1, l, w), lambda i: (i, 0, 0))
    return pl.pallas_call(
        _pool_kernel, grid=(b,), in_specs=[blk], out_specs=blk,
        out_shape=jax.ShapeDtypeStruct((b, l, w), BF16),
        compiler_params=_params(("arbitrary",)),
        name="pool",
    )(u)


def _merge_kernel(xc_ref, xl_ref, attc_ref, attl_ref, plc_ref, pll_ref, gac_ref, gal_ref, gbc_ref, gbl_ref,
                  mod_ref, watt_ref, wpool_ref, ps_ref, wout_ref, g2_ref, wr_ref, x1_ref, h2_ref, aff_ref, *, ctx_blocks):
    is_ctx = pl.program_id(0) < ctx_blocks

    def block(x_ref, att_ref, pl_ref, ga_ref, gb_ref):
        o_a = _dot(att_ref[...], watt_ref[...])
        o_b = jnp.concatenate(
            [_dot(pl_ref[:, g * POOL_GROUP_DIM:(g + 1) * POOL_GROUP_DIM], wpool_ref[g])
             for g in range(len(POOL_WINDOWS))], axis=-1) * ps_ref[...]
        merged = ga_ref[...] * o_a + gb_ref[...] * o_b
        x1 = x_ref[...] + mod_ref[0, 2:3, :] * _dot(merged.astype(BF16), wout_ref[...])
        x1_ref[...] = x1
        h2 = _rms(x1, g2_ref[...]) * (1.0 + mod_ref[0, 4:5, :]) + mod_ref[0, 3:4, :]
        h2_ref[...] = h2
        logits = _dot3(h2, wr_ref[...])
        e = jnp.exp(logits - jnp.max(logits, axis=-1, keepdims=True))
        aff_ref[...] = e / jnp.sum(e, axis=-1, keepdims=True)

    @pl.when(is_ctx)
    def _():
        block(xc_ref, attc_ref, plc_ref, gac_ref, gbc_ref)

    @pl.when(jnp.logical_not(is_ctx))
    def _():
        block(xl_ref, attl_ref, pll_ref, gal_ref, gbl_ref)


def _merge(ctx, lat, mods, w_att, w_pool, pool_scale, w_out, norm2_g, w_router, lat_blocks_per_seq):
    n_c, d = ctx[0].shape
    n = n_c + lat[0].shape[0]
    tb = TOKEN_BLOCK
    cb = n_c // tb
    crow = lambda w: pl.BlockSpec((tb, w), lambda i: (jnp.minimum(i, cb - 1), 0))
    lrow = lambda w: pl.BlockSpec((tb, w), lambda i: (jnp.maximum(i - cb, 0), 0))
    orow = lambda w: pl.BlockSpec((tb, w), lambda i: (i, 0))
    full = lambda a: pl.BlockSpec(a.shape, lambda i: (0,) * a.ndim)
    cond = lambda i: jnp.where(i < cb, 0, 1 + (i - cb) // lat_blocks_per_seq)
    pairs, specs = [], []
    for a_c, a_l in zip(ctx, lat):
        pairs += [a_c, a_l]
        specs += [crow(a_c.shape[1]), lrow(a_l.shape[1])]
    return pl.pallas_call(
        functools.partial(_merge_kernel, ctx_blocks=cb),
        grid=(n // tb,),
        in_specs=specs + [pl.BlockSpec((1, N_MOD, d), lambda i: (cond(i), 0, 0)),
                          full(w_att), full(w_pool), full(pool_scale), full(w_out), full(norm2_g), full(w_router)],
        out_specs=[orow(d), orow(d), orow(N_EXPERTS)],
        out_shape=[jax.ShapeDtypeStruct((n, d), F32), jax.ShapeDtypeStruct((n, d), F32),
                   jax.ShapeDtypeStruct((n, N_EXPERTS), F32)],
        compiler_params=_params(("arbitrary",)),
        name="merge",
    )(*pairs, mods, w_att, w_pool, pool_scale, w_out, norm2_g, w_router)


def _prefix_incl(mask_f32, out_ref):
    e, n = mask_f32.shape
    i = lax.broadcasted_iota(I32, (128, 128), 0)
    j = lax.broadcasted_iota(I32, (128, 128), 1)
    tri = jnp.where(i <= j, 1.0, 0.0).astype(BF16)
    carry = jnp.zeros((e, 1), F32)
    for c in range(n // 128):
        inc = _dot(mask_f32[:, c * 128:(c + 1) * 128].astype(BF16), tri) + carry
        out_ref[:, c * 128:(c + 1) * 128] = inc.astype(I32)
        carry = inc[:, 127:128]


def _route_kernel(aff_ref, sel_ref, rinc_ref, tmp_ref, *, cap):
    aff = aff_ref[...]

    def bit_step(i, thr):
        cand = thr | jnp.left_shift(jnp.int32(1), 30 - i)
        cnt = jnp.sum(jnp.where(aff >= pltpu.bitcast(cand, F32), 1, 0), axis=1, keepdims=True)
        return jnp.where(cnt >= cap, cand, thr)

    thr = lax.fori_loop(0, 31, bit_step, jnp.zeros((aff.shape[0], 1), I32))
    gt = aff >= pltpu.bitcast(thr + 1, F32)
    eq = (aff >= pltpu.bitcast(thr, F32)) & jnp.logical_not(gt)
    need = cap - jnp.sum(jnp.where(gt, 1, 0), axis=1, keepdims=True)
    _prefix_incl(jnp.where(eq, 1.0, 0.0), tmp_ref)
    sel = gt | (eq & (tmp_ref[...] <= need))
    sel_ref[...] = jnp.where(sel, 1, 0)
    _prefix_incl(jnp.where(sel, 1.0, 0.0), rinc_ref)


def _route(aff_t, cap):
    e, n = aff_t.shape
    full = pl.BlockSpec((e, n), lambda: (0, 0))
    return pl.pallas_call(
        functools.partial(_route_kernel, cap=cap),
        in_specs=[full], out_specs=[full, full],
        out_shape=[jax.ShapeDtypeStruct((e, n), I32)] * 2,
        scratch_shapes=[pltpu.VMEM((e, n), I32)],
        compiler_params=pltpu.CompilerParams(vmem_limit_bytes=VMEM_LIMIT),
        name="route",
    )(aff_t)


def _moe_kernel(gidx_ref, qpos_ref, h_ref, gate_ref, wg_ref, wu_ref, wd_ref, z_ref,
                xf_ref, xb_ref, acc_ref, y_ref, sem_g, sem_s, *, m_tot, n_f, step_rows, z_zero_rows):
    e = pl.program_id(0)
    j = pl.program_id(1)
    n_e = pl.num_programs(0)
    m_iss = step_rows * n_f
    tf = wg_ref.shape[1]
    slot = e % 2

    def gather_row(lst, s, dst_slot):
        t = gidx_ref[lst * m_iss + s]
        return pltpu.make_async_copy(h_ref.at[pl.ds(t, 1)], xf_ref.at[dst_slot, pl.ds(s, 1)], sem_g.at[dst_slot])

    def scatter_row(lst, s):
        q = qpos_ref[lst * m_iss + s]
        return pltpu.make_async_copy(y_ref.at[pl.ds(s, 1)], z_ref.at[pl.ds(q, 1)], sem_s)

    def wait_gathers(dst_slot):
        pltpu.make_async_copy(h_ref.at[pl.ds(0, m_iss)], xf_ref.at[dst_slot], sem_g.at[dst_slot]).wait()

    def wait_scatters():
        pltpu.make_async_copy(y_ref, z_ref.at[pl.ds(0, m_iss)], sem_s).wait()

    @pl.when((e == 0) & (j == 0))
    def _():
        y_ref[...] = jnp.zeros_like(y_ref)
        cp = pltpu.make_async_copy(y_ref.at[pl.ds(0, z_zero_rows)], z_ref.at[pl.ds(n_e * m_tot, z_zero_rows)], sem_s)
        cp.start()
        cp.wait()

        def first(s, c):
            gather_row(0, s, 0).start()
            return c

        lax.fori_loop(0, m_iss, first, 0)

    @pl.when(j == 0)
    def _():
        wait_gathers(slot)
        xb_ref[...] = xf_ref[slot, 0:m_tot, :].astype(BF16)
        acc_ref[...] = jnp.zeros_like(acc_ref)

    nxt = lax.rem(e + 1, n_e)
    for u in range(step_rows):
        s = j * step_rows + u
        gather_row(nxt, s, 1 - slot).start()
        scatter_row(e, s).start()

    valid = D_EXPERT - j * tf
    cmask = lax.broadcasted_iota(I32, (1, tf), 1) < valid
    rmask = lax.broadcasted_iota(I32, (tf, 1), 0) < valid
    wg = wg_ref[0].astype(BF16)
    wu = wu_ref[0].astype(BF16)
    wd = jnp.where(rmask, wd_ref[0], 0.0).astype(BF16)
    for m in range(m_tot // EXPERT_M_BLOCK):
        rs = slice(m * EXPERT_M_BLOCK, (m + 1) * EXPERT_M_BLOCK)
        xs = xb_ref[rs, :]
        a = _dot_nt(xs, wg)
        b = _dot_nt(xs, wu)
        hm = jnp.where(cmask, a * _sigmoid(a) * b, 0.0).astype(BF16)
        acc_ref[rs, :] += _dot(hm, wd)

    @pl.when(j == n_f - 1)
    def _():
        wait_scatters()
        y_ref[0:m_tot, :] = _pack_halves(acc_ref[...] * gate_ref[0])

        @pl.when(e == n_e - 1)
        def _():
            def last(s, c):
                scatter_row(n_e, s).start()
                return c

            lax.fori_loop(0, m_iss, last, 0)
            wait_scatters()
            wait_gathers(1 - slot)


def _moe_step_rows(m_tot, n_f):
    return pl.cdiv(pl.cdiv(m_tot, n_f), ROW_ALIGN) * ROW_ALIGN


def _moe(gidx, qpos, h, gate, w_gate, w_up, w_down, m_tot, z_rows, z_zero_rows):
    d = h.shape[1]
    n_e = w_down.shape[0]
    tf = EXPERT_F_BLOCK
    n_f = pl.cdiv(D_EXPERT, tf)
    step_rows = _moe_step_rows(m_tot, n_f)
    m_iss = step_rows * n_f
    grid_spec = pltpu.PrefetchScalarGridSpec(
        num_scalar_prefetch=2,
        grid=(n_e, n_f),
        in_specs=[pl.BlockSpec(memory_space=pl.ANY),
                  pl.BlockSpec((1, m_tot, 1), lambda e, j, *_: (e, 0, 0)),
                  pl.BlockSpec((1, tf, d), lambda e, j, *_: (e, j, 0)),
                  pl.BlockSpec((1, tf, d), lambda e, j, *_: (e, j, 0)),
                  pl.BlockSpec((1, tf, d), lambda e, j, *_: (e, j, 0))],
        out_specs=pl.BlockSpec(memory_space=pl.ANY),
        scratch_shapes=[pltpu.VMEM((2, m_iss, d), F32), pltpu.VMEM((m_tot, d), BF16), pltpu.VMEM((m_tot, d), F32),
                        pltpu.VMEM((m_iss, d // 2), jnp.uint32), pltpu.SemaphoreType.DMA((2,)),
                        pltpu.SemaphoreType.DMA],
    )
    return pl.pallas_call(
        functools.partial(_moe_kernel, m_tot=m_tot, n_f=n_f, step_rows=step_rows, z_zero_rows=z_zero_rows),
        grid_spec=grid_spec,
        out_shape=jax.ShapeDtypeStruct((z_rows, d // 2), jnp.uint32),
        compiler_params=_params(("arbitrary", "arbitrary")),
        name="moe",
    )(gidx, qpos, h, gate, w_gate, w_up, w_down)


Z_BUFFERS = 4


def _combine_kernel(nc_ref, first_ref, start_ref, total_ref, z_ref, x1_ref, mod_ref, qt_ref, kt_ref, o_ref,
                    zbuf, sem, acc_ref):
    blk = pl.program_id(0)
    n_chunks = nc_ref[blk]
    first = first_ref[blk]
    total = total_ref[0]
    ahead = Z_BUFFERS - 1
    qt = qt_ref[...]
    kt = kt_ref[...]
    acc_ref[...] = jnp.zeros_like(acc_ref)

    def z_copy(g):
        rows = pl.ds(pl.multiple_of(start_ref[g], ROW_ALIGN), Z_CHUNK)
        slot = g % Z_BUFFERS
        return pltpu.make_async_copy(z_ref.at[rows], zbuf.at[slot], sem.at[slot])

    @pl.when(blk == 0)
    def _():
        for g in range(ahead):
            @pl.when(g < total)
            def _():
                z_copy(g).start()

    def chunk(c, carry):
        g = first + c
        slot = g % Z_BUFFERS

        @pl.when(g + ahead < total)
        def _():
            z_copy(g + ahead).start()

        z_copy(g).wait()
        r = start_ref[g] + lax.broadcasted_iota(I32, (qt.shape[0], Z_CHUNK), 1)
        onehot = jnp.where((r >= qt) & (r < qt + kt), 1.0, 0.0).astype(BF16)
        left, right = _unpack_halves(zbuf[slot])
        half = left.shape[1]
        acc_ref[:, :half] += _dot(onehot, left)
        acc_ref[:, half:] += _dot(onehot, right)
        return carry

    lax.fori_loop(0, n_chunks, chunk, 0)
    o_ref[...] = x1_ref[...] + mod_ref[0, 5:6, :] * acc_ref[...]


def _combine(chunks, z, x1, first_block, mods, qt, kt, cond_of_block):
    n = qt.shape[0]
    d = x1.shape[1]
    tb = COMBINE_BLOCK
    grid_spec = pltpu.PrefetchScalarGridSpec(
        num_scalar_prefetch=4,
        grid=(n // tb,),
        in_specs=[pl.BlockSpec(memory_space=pl.ANY),
                  pl.BlockSpec((tb, d), lambda i, *_: (i + first_block, 0)),
                  pl.BlockSpec((1, N_MOD, d), lambda i, *_: (cond_of_block(i), 0, 0)),
                  pl.BlockSpec((tb, 1), lambda i, *_: (i, 0)),
                  pl.BlockSpec((tb, 1), lambda i, *_: (i, 0))],
        out_specs=pl.BlockSpec((tb, d), lambda i, *_: (i, 0)),
        scratch_shapes=[pltpu.VMEM((Z_BUFFERS, Z_CHUNK, z.shape[1]), z.dtype), pltpu.SemaphoreType.DMA((Z_BUFFERS,)),
                        pltpu.VMEM((tb, d), F32)],
    )
    return pl.pallas_call(
        _combine_kernel,
        grid_spec=grid_spec,
        out_shape=jax.ShapeDtypeStruct((n, d), F32),
        compiler_params=_params(("arbitrary",)),
        name="combine",
    )(*chunks, z, x1, mods, qt, kt)


def _routing_tables(aff, cap, z_off):
    n = aff.shape[0]
    aff_t = aff.T
    sel, rinc = _route(aff_t, cap)
    slots = jnp.arange(cap, dtype=I32)
    chunk = 128
    rinc3 = rinc.reshape(rinc.shape[0], n // chunk, chunk)
    n_full = jnp.sum((rinc3[:, None, :, chunk - 1] <= slots[None, :, None]).astype(I32), axis=-1)
    pick = (n_full[:, :, None] == jnp.arange(n // chunk, dtype=I32)[None, None, :]).astype(F32)
    inside = jnp.einsum("eca,eab->ecb", pick, rinc3.astype(F32), precision=lax.Precision.HIGHEST)
    idx = n_full * chunk + jnp.sum((inside <= slots[None, :, None].astype(F32)).astype(I32), axis=-1)
    gate = jnp.take_along_axis(aff_t, idx, axis=1)
    k_tok = jnp.sum(sel, axis=0)
    q_tok = jnp.cumsum(k_tok) - k_tok + z_off
    before = jnp.cumsum(sel, axis=0) - sel
    qpos = jnp.take_along_axis(q_tok[None, :] + before, idx, axis=1)
    nblk = n // COMBINE_BLOCK
    q_blk = q_tok[::COMBINE_BLOCK]
    end_blk = q_blk + jnp.sum(k_tok.reshape(nblk, COMBINE_BLOCK), axis=1)
    w0 = (q_blk // ROW_ALIGN) * ROW_ALIGN
    nc = jnp.maximum((end_blk - w0 + Z_CHUNK - 1) // Z_CHUNK, 1).astype(I32)
    first = jnp.cumsum(nc) - nc
    g = jnp.arange(nblk + (N_EXPERTS * cap) // Z_CHUNK + nblk, dtype=I32)
    blk_of = jnp.sum((first[None, 1:] <= g[:, None]).astype(I32), axis=1)
    start = jnp.where(g < jnp.sum(nc), w0[blk_of] + (g - first[blk_of]) * Z_CHUNK, 0)
    chunks = (nc, first.astype(I32), start.astype(I32), jnp.sum(nc).astype(I32)[None])
    return idx, gate, qpos.astype(I32), chunks, q_tok.astype(I32)[:, None], k_tok.astype(I32)[:, None]


def kernel(x_prompt, x_sample, cache_k, cache_v, c, c_ctx, norm1_g, norm2_g, w_ada, b_ada, w_in, q_norm_g, k_norm_g,
           rpb, w_att_proj, w_pool, pool_scale, w_out, w_router, w_gate_e, w_up_e, w_down_e):
    assert w_ada.shape[0] == 1, "single trunk layer"
    nb, ls, d = x_prompt.shape
    db, ll, _ = x_sample.shape
    n_c, n_l = nb * ls, db * ll
    assert ls == TOKEN_BLOCK, "context keys / values are written one sequence per projection block"

    cond = jnp.concatenate([c_ctx[None, :], c, jnp.zeros((16 - 1 - db, d), F32)], axis=0)
    mods = _ada(cond, w_ada[0], b_ada).reshape(16, N_MOD, d)

    w_in_b = w_in[0].astype(BF16)
    w_att_b = w_att_proj[0].astype(BF16)
    w_pool_b = w_pool[0].astype(BF16)
    w_out_b = w_out[0].astype(BF16)

    cond_ctx = lambda i: 0
    lat_blocks = ll // TOKEN_BLOCK
    cond_lat = lambda i: 1 + i // lat_blocks

    xc = x_prompt.reshape(n_c, d)
    xl = x_sample.reshape(n_l, d)
    q_gain = jnp.tile(q_norm_g, (1, NA_HEADS))
    k_gain = jnp.tile(k_norm_g, (1, NA_HEADS))
    qc, kc, vc, uc, gac, gbc = _in_proj(xc, mods, norm1_g, w_in_b, q_gain, k_gain, cond_ctx, True)
    ql, kl, vl, ul, gal, gbl = _in_proj(xl, mods, norm1_g, w_in_b, q_gain, k_gain, cond_lat, False)

    seq_c = lambda a: a.reshape(nb, ls, a.shape[-1])
    seq_l = lambda a: a.reshape(db, ll, a.shape[-1])
    att_c = _attn_ctx(seq_c(qc), kc, vc)
    bias = _window_bias(rpb[0], ll // GRID_W)
    att_l = _attn_lat(seq_l(ql), seq_l(kl), seq_l(vl), jnp.transpose(cache_k[:, 0], (0, 2, 3, 1)),
                      jnp.transpose(cache_v[:, 0], (0, 2, 3, 1)), bias)
    pool_c = uc
    pool_l = _pool(seq_l(ul)).reshape(n_l, POOL_WIDTH)

    x1, h2, aff = _merge((xc, att_c.reshape(n_c, NA_WIDTH), pool_c, gac, gbc),
                         (xl, att_l.reshape(n_l, NA_WIDTH), pool_l, gal, gbl),
                         mods, w_att_b, w_pool_b, pool_scale, w_out_b, norm2_g, w_router[0], lat_blocks)

    cap_c = max(1, (CAPACITY_FACTOR * n_c) // N_EXPERTS)
    cap_l = max(1, (CAPACITY_FACTOR * n_l) // N_EXPERTS)
    m_tot = cap_c + cap_l
    m_iss = _moe_step_rows(m_tot, pl.cdiv(D_EXPERT, EXPERT_F_BLOCK)) * pl.cdiv(D_EXPERT, EXPERT_F_BLOCK)
    z_valid = N_EXPERTS * m_tot
    z_zero_rows = Z_CHUNK + ROW_ALIGN
    z_spare = z_valid + z_zero_rows
    z_rows = z_spare + m_iss
    idx_c, gate_c, qpos_c, chunks_c, qt_c, kt_c = _routing_tables(aff[:n_c], cap_c, 0)
    idx_l, gate_l, qpos_l, chunks_l, qt_l, kt_l = _routing_tables(aff[n_c:], cap_l, N_EXPERTS * cap_c)
    spare = jnp.broadcast_to(z_spare + jnp.arange(m_iss, dtype=I32), (N_EXPERTS + 1, m_iss))
    gidx = jnp.concatenate([idx_c, idx_l + n_c, jnp.zeros((N_EXPERTS, m_iss - m_tot), I32)], axis=1).reshape(-1)
    qpos = jnp.concatenate([spare[:1], jnp.concatenate([qpos_c, qpos_l, spare[1:, m_tot:]], axis=1)], axis=0).reshape(-1)
    gate = jnp.concatenate([gate_c, gate_l], axis=1)[:, :, None]

    z = _moe(gidx, qpos, h2, gate, jnp.swapaxes(w_gate_e[0], 1, 2), jnp.swapaxes(w_up_e[0], 1, 2), w_down_e[0],
             m_tot, z_rows, z_zero_rows)

    y_c = _combine(chunks_c, z, x1, 0, mods, qt_c, kt_c, cond_ctx)
    y_l = _combine(chunks_l, z, x1, n_c // COMBINE_BLOCK, mods, qt_l, kt_l, lambda i: 1 + i // (ll // COMBINE_BLOCK))

    state_k = jnp.transpose(kc, (0, 3, 1, 2))[:, None]
    state_v = jnp.transpose(vc, (0, 3, 1, 2))[:, None]
    return (y_c.reshape(nb, ls, d), y_l.reshape(db, ll, d), state_k, state_v)
```

```python
import functools

import jax
import jax.numpy as jnp
from jax import lax
from jax.experimental import pallas as pl
from jax.experimental.pallas import tpu as pltpu

F32 = jnp.float32
BF16 = jnp.bfloat16
I32 = jnp.int32

D_MODEL = 1024
GRID_W = 64
NA_HEADS = 8
HEAD_DIM = 64
NA_WIDTH = NA_HEADS * HEAD_DIM
WIN_ROWS = 8
WIN_COLS = 16
POOL_WINDOWS = (2, 4, 8, 16)
POOL_GROUP_DIM = 128
POOL_WIDTH = 512
POOL_OUT_DIM = 256
N_EXPERTS = 16
CAPACITY_FACTOR = 2
D_EXPERT = 2752
N_MOD = 6
EPS = 1e-6
NEG_INF = -1e30

TOKEN_BLOCK = 256
COMBINE_BLOCK = 256
LAT_ROW_BLOCK = 4
EXPERT_F_BLOCK = 256
EXPERT_M_BLOCK = 768
Z_CHUNK = 256
ROW_ALIGN = 8
VMEM_LIMIT = 56 * 1024 * 1024


def _dot(a, b):
    return jnp.dot(a, b, preferred_element_type=F32)


def _dot_nt(a, b):
    return lax.dot_general(a, b, (((1,), (1,)), ((), ())), preferred_element_type=F32)


def _dot3(a, b):
    a_hi = a.astype(BF16)
    a_lo = (a - a_hi.astype(F32)).astype(BF16)
    b_hi = b.astype(BF16)
    b_lo = (b - b_hi.astype(F32)).astype(BF16)
    return _dot(a_hi, b_hi) + _dot(a_hi, b_lo) + _dot(a_lo, b_hi)


def _sigmoid(x):
    return 0.5 * jnp.tanh(0.5 * x) + 0.5


def _rms(x, g):
    return x * lax.rsqrt(jnp.mean(x * x, axis=-1, keepdims=True) + EPS) * g


def _pack_halves(x):
    w = x.shape[1] // 2
    return pltpu.pack_elementwise([x[:, :w], x[:, w:]], packed_dtype=BF16)


def _unpack_halves(p):
    return tuple(pltpu.unpack_elementwise(p, index=i, packed_dtype=BF16, unpacked_dtype=F32).astype(BF16)
                 for i in range(2))


def _params(sem):
    return pltpu.CompilerParams(dimension_semantics=sem, vmem_limit_bytes=VMEM_LIMIT)


def _ada_kernel(cond_ref, w_ref, b_ref, o_ref):
    c = cond_ref[...]
    o_ref[...] = _dot3(c * _sigmoid(c), w_ref[...]) + b_ref[...]


def _ada(cond, w_ada, b_ada):
    rows, d = cond.shape
    n_out = w_ada.shape[1]
    bn = 512
    return pl.pallas_call(
        _ada_kernel,
        grid=(n_out // bn,),
        in_specs=[pl.BlockSpec((rows, d), lambda i: (0, 0)),
                  pl.BlockSpec((d, bn), lambda i: (0, i)),
                  pl.BlockSpec((1, bn), lambda i: (0, i))],
        out_specs=pl.BlockSpec((rows, bn), lambda i: (0, i)),
        out_shape=jax.ShapeDtypeStruct((rows, n_out), F32),
        compiler_params=_params(("arbitrary",)),
        name="ada",
    )(cond, w_ada, b_ada)


_IN_CUTS = (0, 512, 1024, 1536, 2048, 3072, 4096)


HEAD_AVG_WIDTH = 256


def _head_avg_matrix():
    head = jnp.arange(HEAD_AVG_WIDTH, dtype=I32) // HEAD_DIM
    return jnp.where(head[:, None] == head[None, :], 1.0 / HEAD_DIM, 0.0).astype(BF16)


def _head_rms(x, g, avg, two_term):
    x2 = x * x
    means = []
    for c in range(0, x.shape[1], HEAD_AVG_WIDTH):
        blk = x2[:, c:c + HEAD_AVG_WIDTH]
        hi = blk.astype(BF16)
        ms = _dot(hi, avg)
        if two_term:
            ms = ms + _dot((blk - hi.astype(F32)).astype(BF16), avg)
        means.append(ms)
    return x * lax.rsqrt(jnp.concatenate(means, axis=-1) + EPS) * g


def _pooled(u):
    l = u.shape[0]
    t = lax.broadcasted_iota(I32, (l, POOL_GROUP_DIM), 0)
    groups = []
    for g, w in enumerate(POOL_WINDOWS):
        x = u[:, g * POOL_GROUP_DIM:(g + 1) * POOL_GROUP_DIM]
        acc = x
        for k in range(-(w // 2), w - w // 2):
            if k == 0:
                continue
            shifted = pltpu.roll(x, (-k) % l, 0)
            acc = acc + jnp.where((t + k >= 0) & (t + k < l), shifted, 0.0)
        cnt = (jnp.minimum(t + (w - w // 2), l) - jnp.maximum(t - w // 2, 0)).astype(F32)
        groups.append(acc / cnt - x)
    return jnp.concatenate(groups, axis=-1)


def _in_kernel(x_ref, mod_ref, g_ref, w_ref, qg_ref, kg_ref, avg_ref, q_ref, k_ref, v_ref, u_ref, ga_ref, gb_ref,
               *, kv_channel_major, pool_here):
    y = _rms(x_ref[...], g_ref[...])
    h = (y * (1.0 + mod_ref[0, 1:2, :]) + mod_ref[0, 0:1, :]).astype(BF16)
    part = lambda i: _dot(h, w_ref[:, _IN_CUTS[i]:_IN_CUTS[i + 1]])
    q_ref[...] = (_head_rms(part(0), qg_ref[...], avg_ref[...], False) * HEAD_DIM ** -0.5).astype(q_ref.dtype)
    kn = _head_rms(part(1), kg_ref[...], avg_ref[...], True)
    v = part(2)
    if kv_channel_major:
        k_ref[0] = kn.T.reshape(k_ref.shape[1:])
        v_ref[0] = v.T.reshape(v_ref.shape[1:])
    else:
        k_ref[...] = kn.astype(k_ref.dtype)
        v_ref[...] = v.astype(v_ref.dtype)
    u = part(3)
    u_ref[...] = _pooled(u).astype(u_ref.dtype) if pool_here else u
    ga_ref[...] = _sigmoid(part(4)).astype(ga_ref.dtype)
    gb_ref[...] = _sigmoid(part(5)).astype(gb_ref.dtype)


def _in_proj(x2d, mods, norm_g, w_in_bf16, q_gain, k_gain, cond_of_block, sequence_blocks):
    n, d = x2d.shape
    tb = TOKEN_BLOCK
    widths = [hi - lo for lo, hi in zip(_IN_CUTS[:-1], _IN_CUTS[1:])]
    dtypes = [BF16, BF16, BF16, BF16 if sequence_blocks else F32, BF16, BF16]
    out_specs = [pl.BlockSpec((tb, w), lambda i: (i, 0)) for w in widths]
    out_shape = [jax.ShapeDtypeStruct((n, w), t) for w, t in zip(widths, dtypes)]
    if sequence_blocks:
        for i in (1, 2):
            out_specs[i] = pl.BlockSpec((1, NA_HEADS, HEAD_DIM, tb), lambda i: (i, 0, 0, 0))
            out_shape[i] = jax.ShapeDtypeStruct((n // tb, NA_HEADS, HEAD_DIM, tb), F32)
    full = lambda a: pl.BlockSpec(a.shape, lambda i: (0,) * a.ndim)
    return pl.pallas_call(
        functools.partial(_in_kernel, kv_channel_major=sequence_blocks, pool_here=sequence_blocks),
        grid=(n // tb,),
        in_specs=[pl.BlockSpec((tb, d), lambda i: (i, 0)),
                  pl.BlockSpec((1, N_MOD, d), lambda i: (cond_of_block(i), 0, 0)),
                  full(norm_g), full(w_in_bf16), full(q_gain), full(k_gain),
                  pl.BlockSpec((HEAD_AVG_WIDTH, HEAD_AVG_WIDTH), lambda i: (0, 0))],
        out_specs=out_specs,
        out_shape=out_shape,
        compiler_params=_params(("arbitrary",)),
        name="in_proj",
    )(x2d, mods, norm_g, w_in_bf16, q_gain, k_gain, _head_avg_matrix())


def _with_ones(v):
    lane = lax.broadcasted_iota(I32, v.shape, 1)
    return jnp.concatenate([v.astype(BF16), jnp.where(lane == 0, 1.0, 0.0).astype(BF16)], axis=-1)


def _attn_ctx_kernel(q_ref, kt_ref, vt_ref, att_ref):
    qn = q_ref[0]
    l = kt_ref.shape[3]
    ones_row = jnp.where(lax.broadcasted_iota(I32, (HEAD_DIM, l), 0) == 0, 1.0, 0.0)
    for h in range(NA_HEADS):
        sl = slice(h * HEAD_DIM, (h + 1) * HEAD_DIM)
        s = _dot(qn[:, sl], kt_ref[0, h].astype(BF16))
        e = jnp.exp(s - jnp.max(s, axis=-1, keepdims=True)).astype(BF16)
        vt = jnp.concatenate([vt_ref[0, h], ones_row], axis=0).astype(BF16)
        o = _dot_nt(e, vt)
        att_ref[0, :, sl] = (o[:, :HEAD_DIM] / o[:, HEAD_DIM:HEAD_DIM + 1]).astype(att_ref.dtype)


def _attn_ctx(q, kt, vt):
    b, l, w = q.shape
    blk = pl.BlockSpec((1, l, w), lambda i: (i, 0, 0))
    tblk = pl.BlockSpec((1,) + kt.shape[1:], lambda i: (i, 0, 0, 0))
    return pl.pallas_call(
        _attn_ctx_kernel,
        grid=(b,),
        in_specs=[blk, tblk, tblk],
        out_specs=blk,
        out_shape=jax.ShapeDtypeStruct((b, l, w), BF16),
        compiler_params=_params(("arbitrary",)),
        name="attn_ctx",
    )(q, kt, vt)


def _lat_windows(rows):
    kr = min(WIN_ROWS, rows)
    rb = min(LAT_ROW_BLOCK, rows)
    wr = min(kr + rb, rows)
    assert rows % rb == 0
    starts = [min(max(i * rb - kr // 2, 0), rows - wr) for i in range(rows // rb)]
    for i, ws in enumerate(starts):
        for r in range(i * rb, (i + 1) * rb):
            rs = min(max(r - kr // 2, 0), rows - kr)
            assert ws <= rs and rs + kr <= ws + wr
    return kr, rb, wr, starts


def _attn_lat_kernel(q_ref, k_ref, v_ref, ckt_ref, cvt_ref, bias_ref, att_ref,
                     kn_s, vb_s, ckb_s, cvb_s, *, rows, kr, rb, wr):
    i = pl.program_id(1)

    @pl.when(i == 0)
    def _():
        lc = ckt_ref.shape[3]
        ones_row = jnp.where(lax.broadcasted_iota(I32, (HEAD_DIM, lc), 0) == 0, 1.0, 0.0)
        for h in range(NA_HEADS):
            sl = slice(h * HEAD_DIM, (h + 1) * HEAD_DIM)
            kn_s[h] = k_ref[0, :, sl]
            vb_s[h] = _with_ones(v_ref[0, :, sl])
            ckb_s[h] = ckt_ref[0, h].astype(BF16)
            cvb_s[h] = jnp.concatenate([cvt_ref[0, h], ones_row], axis=0).astype(BF16)

    win_start = jnp.clip(i * rb - kr // 2, 0, rows - wr)
    key_rows = pl.ds(pl.multiple_of(win_start * GRID_W, GRID_W), wr * GRID_W)
    qn_all = q_ref[0]
    for h in range(NA_HEADS):
        sl = slice(h * HEAD_DIM, (h + 1) * HEAD_DIM)
        qn = qn_all[:, sl]
        bias = bias_ref[h, pl.ds(i * rb, rb)].reshape(rb * GRID_W, wr * GRID_W)
        s_loc = _dot_nt(qn, kn_s[h, key_rows, :]) + bias
        s_ctx = _dot(qn, ckb_s[h])
        m = jnp.maximum(jnp.max(s_loc, axis=-1, keepdims=True), jnp.max(s_ctx, axis=-1, keepdims=True))
        o = (_dot(jnp.exp(s_loc - m).astype(BF16), vb_s[h, key_rows, :])
             + _dot_nt(jnp.exp(s_ctx - m).astype(BF16), cvb_s[h]))
        att_ref[0, :, sl] = (o[:, :HEAD_DIM] / o[:, HEAD_DIM:HEAD_DIM + 1]).astype(att_ref.dtype)


def _attn_lat(q, k, v, ckt, cvt, bias):
    b, l, w = q.shape
    lc = ckt.shape[3]
    rows = l // GRID_W
    kr, rb, wr, _ = _lat_windows(rows)
    seq = pl.BlockSpec((1, l, w), lambda i, r: (i, 0, 0))
    ctx = pl.BlockSpec((1, NA_HEADS, HEAD_DIM, lc), lambda i, r: (i, 0, 0, 0))
    rowblk = pl.BlockSpec((1, rb * GRID_W, w), lambda i, r: (i, r, 0))
    return pl.pallas_call(
        functools.partial(_attn_lat_kernel, rows=rows, kr=kr, rb=rb, wr=wr),
        grid=(b, rows // rb),
        in_specs=[rowblk, seq, seq, ctx, ctx,
                  pl.BlockSpec(bias.shape, lambda i, r: (0, 0, 0, 0), pipeline_mode=pl.Buffered(1))],
        out_specs=rowblk,
        out_shape=jax.ShapeDtypeStruct((b, l, w), BF16),
        scratch_shapes=[pltpu.VMEM((NA_HEADS, l, HEAD_DIM), BF16), pltpu.VMEM((NA_HEADS, l, 2 * HEAD_DIM), BF16),
                        pltpu.VMEM((NA_HEADS, HEAD_DIM, lc), BF16), pltpu.VMEM((NA_HEADS, 2 * HEAD_DIM, lc), BF16)],
        compiler_params=_params(("arbitrary", "arbitrary")),
        name="attn_lat",
    )(q, k, v, ckt, cvt, bias)


def _bias_kernel(rpb_ref, o_ref, *, rows):
    kr, rb, wr, starts = _lat_windows(rows)
    h = pl.program_id(0)
    n_ro, n_co = 2 * WIN_ROWS - 1, 2 * WIN_COLS - 1
    wq = lax.broadcasted_iota(I32, (GRID_W, GRID_W), 0)
    wk = lax.broadcasted_iota(I32, (GRID_W, GRID_W), 1)
    co = jnp.clip(wk - wq, -(WIN_COLS - 1), WIN_COLS - 1) + (WIN_COLS - 1)
    col_start = jnp.clip(wq - WIN_COLS // 2, 0, GRID_W - WIN_COLS)
    in_win = (wk >= col_start) & (wk < col_start + WIN_COLS)
    co = jnp.where(in_win, co, -1)
    masked = jnp.full((GRID_W, GRID_W), NEG_INF, F32)
    tables = {}
    for r in range(rows):
        row_start = min(max(r - kr // 2, 0), rows - kr)
        for j in range(wr):
            key_row = starts[r // rb] + j
            blk = masked
            if row_start <= key_row < row_start + kr:
                ro = key_row - r + (WIN_ROWS - 1)
                if ro not in tables:
                    t = masked
                    for c in range(n_co):
                        t = jnp.where(co == c, rpb_ref[(h * n_ro + ro) * n_co + c], t)
                    tables[ro] = t
                blk = tables[ro]
            o_ref[0, r, :, j * GRID_W:(j + 1) * GRID_W] = blk


def _window_bias(rpb, rows):
    _, _, wr, _ = _lat_windows(rows)
    nh = rpb.shape[0]
    return pl.pallas_call(
        functools.partial(_bias_kernel, rows=rows),
        grid=(nh,),
        in_specs=[pl.BlockSpec(memory_space=pltpu.SMEM)],
        out_specs=pl.BlockSpec((1, rows, GRID_W, wr * GRID_W), lambda h: (h, 0, 0, 0)),
        out_shape=jax.ShapeDtypeStruct((nh, rows, GRID_W, wr * GRID_W), F32),
        compiler_params=_params(("arbitrary",)),
        name="window_bias",
    )(rpb.reshape(-1))


def _pool_kernel(u_ref, o_ref):
    o_ref[0] = _pooled(u_ref[0]).astype(o_ref.dtype)


def _pool(u):
    b, l, w = u.shape
    blk = pl.BlockSpec((1, l, w), lambda i: (i, 0, 0))
    return pl.pallas_call(
        _pool_kernel, grid=(b,), in_specs=[blk], out_specs=blk,
        out_shape=jax.ShapeDtypeStruct((b, l, w), BF16),
        compiler_params=_params(("arbitrary",)),
        name="pool",
    )(u)


def _merge_kernel(xc_ref, xl_ref, attc_ref, attl_ref, plc_ref, pll_ref, gac_ref, gal_ref, gbc_ref, gbl_ref,
                  mod_ref, watt_ref, wpool_ref, ps_ref, wout_ref, g2_ref, wr_ref, x1_ref, h2_ref, aff_ref, *, ctx_blocks):
    is_ctx = pl.program_id(0) < ctx_blocks

    def block(x_ref, att_ref, pl_ref, ga_ref, gb_ref):
        o_a = _dot(att_ref[...], watt_ref[...])
        o_b = jnp.concatenate(
            [_dot(pl_ref[:, g * POOL_GROUP_DIM:(g + 1) * POOL_GROUP_DIM], wpool_ref[g])
             for g in range(len(POOL_WINDOWS))], axis=-1) * ps_ref[...]
        merged = ga_ref[...] * o_a + gb_ref[...] * o_b
        x1 = x_ref[...] + mod_ref[0, 2:3, :] * _dot(merged.astype(BF16), wout_ref[...])
        x1_ref[...] = x1
        h2 = _rms(x1, g2_ref[...]) * (1.0 + mod_ref[0, 4:5, :]) + mod_ref[0, 3:4, :]
        h2_ref[...] = h2
        logits = _dot3(h2, wr_ref[...])
        e = jnp.exp(logits - jnp.max(logits, axis=-1, keepdims=True))
        aff_ref[...] = e / jnp.sum(e, axis=-1, keepdims=True)

    @pl.when(is_ctx)
    def _():
        block(xc_ref, attc_ref, plc_ref, gac_ref, gbc_ref)

    @pl.when(jnp.logical_not(is_ctx))
    def _():
        block(xl_ref, attl_ref, pll_ref, gal_ref, gbl_ref)


def _merge(ctx, lat, mods, w_att, w_pool, pool_scale, w_out, norm2_g, w_router, lat_blocks_per_seq):
    n_c, d = ctx[0].shape
    n = n_c + lat[0].shape[0]
    tb = TOKEN_BLOCK
    cb = n_c // tb
    crow = lambda w: pl.BlockSpec((tb, w), lambda i: (jnp.minimum(i, cb - 1), 0))
    lrow = lambda w: pl.BlockSpec((tb, w), lambda i: (jnp.maximum(i - cb, 0), 0))
    orow = lambda w: pl.BlockSpec((tb, w), lambda i: (i, 0))
    full = lambda a: pl.BlockSpec(a.shape, lambda i: (0,) * a.ndim)
    cond = lambda i: jnp.where(i < cb, 0, 1 + (i - cb) // lat_blocks_per_seq)
    pairs, specs = [], []
    for a_c, a_l in zip(ctx, lat):
        pairs += [a_c, a_l]
        specs += [crow(a_c.shape[1]), lrow(a_l.shape[1])]
    return pl.pallas_call(
        functools.partial(_merge_kernel, ctx_blocks=cb),
        grid=(n // tb,),
        in_specs=specs + [pl.BlockSpec((1, N_MOD, d), lambda i: (cond(i), 0, 0)),
                          full(w_att), full(w_pool), full(pool_scale), full(w_out), full(norm2_g), full(w_router)],
        out_specs=[orow(d), orow(d), orow(N_EXPERTS)],
        out_shape=[jax.ShapeDtypeStruct((n, d), F32), jax.ShapeDtypeStruct((n, d), F32),
                   jax.ShapeDtypeStruct((n, N_EXPERTS), F32)],
        compiler_params=_params(("arbitrary",)),
        name="merge",
    )(*pairs, mods, w_att, w_pool, pool_scale, w_out, norm2_g, w_router)


def _prefix_incl(mask_f32, out_ref):
    e, n = mask_f32.shape
    i = lax.broadcasted_iota(I32, (128, 128), 0)
    j = lax.broadcasted_iota(I32, (128, 128), 1)
    tri = jnp.where(i <= j, 1.0, 0.0).astype(BF16)
    carry = jnp.zeros((e, 1), F32)
    for c in range(n // 128):
        inc = _dot(mask_f32[:, c * 128:(c + 1) * 128].astype(BF16), tri) + carry
        out_ref[:, c * 128:(c + 1) * 128] = inc.astype(I32)
        carry = inc[:, 127:128]


def _route_kernel(aff_ref, sel_ref, rinc_ref, tmp_ref, *, cap):
    aff = aff_ref[...]

    def bit_step(i, thr):
        cand = thr | jnp.left_shift(jnp.int32(1), 30 - i)
        cnt = jnp.sum(jnp.where(aff >= pltpu.bitcast(cand, F32), 1, 0), axis=1, keepdims=True)
        return jnp.where(cnt >= cap, cand, thr)

    thr = lax.fori_loop(0, 31, bit_step, jnp.zeros((aff.shape[0], 1), I32))
    gt = aff >= pltpu.bitcast(thr + 1, F32)
    eq = (aff >= pltpu.bitcast(thr, F32)) & jnp.logical_not(gt)
    need = cap - jnp.sum(jnp.where(gt, 1, 0), axis=1, keepdims=True)
    _prefix_incl(jnp.where(eq, 1.0, 0.0), tmp_ref)
    sel = gt | (eq & (tmp_ref[...] <= need))
    sel_ref[...] = jnp.where(sel, 1, 0)
    _prefix_incl(jnp.where(sel, 1.0, 0.0), rinc_ref)


def _route(aff_t, cap):
    e, n = aff_t.shape
    full = pl.BlockSpec((e, n), lambda: (0, 0))
    return pl.pallas_call(
        functools.partial(_route_kernel, cap=cap),
        in_specs=[full], out_specs=[full, full],
        out_shape=[jax.ShapeDtypeStruct((e, n), I32)] * 2,
        scratch_shapes=[pltpu.VMEM((e, n), I32)],
        compiler_params=pltpu.CompilerParams(vmem_limit_bytes=VMEM_LIMIT),
        name="route",
    )(aff_t)


def _moe_kernel(gidx_ref, qpos_ref, h_ref, gate_ref, wg_ref, wu_ref, wd_ref, z_ref,
                xf_ref, xb_ref, acc_ref, y_ref, sem_g, sem_s, *, m_tot, n_f, step_rows, z_zero_rows):
    e = pl.program_id(0)
    j = pl.program_id(1)
    n_e = pl.num_programs(0)
    m_iss = step_rows * n_f
    tf = wg_ref.shape[1]
    slot = e % 2

    def gather_row(lst, s, dst_slot):
        t = gidx_ref[lst * m_iss + s]
        return pltpu.make_async_copy(h_ref.at[pl.ds(t, 1)], xf_ref.at[dst_slot, pl.ds(s, 1)], sem_g.at[dst_slot])

    def scatter_row(lst, s):
        q = qpos_ref[lst * m_iss + s]
        return pltpu.make_async_copy(y_ref.at[pl.ds(s, 1)], z_ref.at[pl.ds(q, 1)], sem_s)

    def wait_gathers(dst_slot):
        pltpu.make_async_copy(h_ref.at[pl.ds(0, m_iss)], xf_ref.at[dst_slot], sem_g.at[dst_slot]).wait()

    def wait_scatters():
        pltpu.make_async_copy(y_ref, z_ref.at[pl.ds(0, m_iss)], sem_s).wait()

    @pl.when((e == 0) & (j == 0))
    def _():
        y_ref[...] = jnp.zeros_like(y_ref)
        cp = pltpu.make_async_copy(y_ref.at[pl.ds(0, z_zero_rows)], z_ref.at[pl.ds(n_e * m_tot, z_zero_rows)], sem_s)
        cp.start()
        cp.wait()

        def first(s, c):
            gather_row(0, s, 0).start()
            return c

        lax.fori_loop(0, m_iss, first, 0)

    @pl.when(j == 0)
    def _():
        wait_gathers(slot)
        xb_ref[...] = xf_ref[slot, 0:m_tot, :].astype(BF16)
        acc_ref[...] = jnp.zeros_like(acc_ref)

    nxt = lax.rem(e + 1, n_e)
    for u in range(step_rows):
        s = j * step_rows + u
        gather_row(nxt, s, 1 - slot).start(priority=1)
        scatter_row(e, s).start(priority=1)

    valid = D_EXPERT - j * tf
    cmask = lax.broadcasted_iota(I32, (1, tf), 1) < valid
    rmask = lax.broadcasted_iota(I32, (tf, 1), 0) < valid
    wg = wg_ref[0].astype(BF16)
    wu = wu_ref[0].astype(BF16)
    wd = jnp.where(rmask, wd_ref[0], 0.0).astype(BF16)
    for m in range(m_tot // EXPERT_M_BLOCK):
        rs = slice(m * EXPERT_M_BLOCK, (m + 1) * EXPERT_M_BLOCK)
        xs = xb_ref[rs, :]
        a = _dot_nt(xs, wg)
        b = _dot_nt(xs, wu)
        hm = jnp.where(cmask, a * _sigmoid(a) * b, 0.0).astype(BF16)
        acc_ref[rs, :] += _dot(hm, wd)

    @pl.when(j == n_f - 1)
    def _():
        wait_scatters()
        y_ref[0:m_tot, :] = _pack_halves(acc_ref[...] * gate_ref[0])

        @pl.when(e == n_e - 1)
        def _():
            def last(s, c):
                scatter_row(n_e, s).start()
                return c

            lax.fori_loop(0, m_iss, last, 0)
            wait_scatters()
            wait_gathers(1 - slot)


def _moe_step_rows(m_tot, n_f):
    return pl.cdiv(pl.cdiv(m_tot, n_f), ROW_ALIGN) * ROW_ALIGN


def _moe(gidx, qpos, h, gate, w_gate, w_up, w_down, m_tot, z_rows, z_zero_rows):
    d = h.shape[1]
    n_e = w_down.shape[0]
    tf = EXPERT_F_BLOCK
    n_f = pl.cdiv(D_EXPERT, tf)
    step_rows = _moe_step_rows(m_tot, n_f)
    m_iss = step_rows * n_f
    weights = pl.BlockSpec((1, tf, d), lambda e, j, *_: (e, j, 0))
    grid_spec = pltpu.PrefetchScalarGridSpec(
        num_scalar_prefetch=2,
        grid=(n_e, n_f),
        in_specs=[pl.BlockSpec(memory_space=pl.ANY),
                  pl.BlockSpec((1, m_tot, 1), lambda e, j, *_: (e, 0, 0)),
                  weights, weights, weights],
        out_specs=pl.BlockSpec(memory_space=pl.ANY),
        scratch_shapes=[pltpu.VMEM((2, m_iss, d), F32), pltpu.VMEM((m_tot, d), BF16), pltpu.VMEM((m_tot, d), F32),
                        pltpu.VMEM((m_iss, d // 2), jnp.uint32), pltpu.SemaphoreType.DMA((2,)),
                        pltpu.SemaphoreType.DMA],
    )
    return pl.pallas_call(
        functools.partial(_moe_kernel, m_tot=m_tot, n_f=n_f, step_rows=step_rows, z_zero_rows=z_zero_rows),
        grid_spec=grid_spec,
        out_shape=jax.ShapeDtypeStruct((z_rows, d // 2), jnp.uint32),
        compiler_params=_params(("arbitrary", "arbitrary")),
        name="moe",
    )(gidx, qpos, h, gate, w_gate, w_up, w_down)


Z_BUFFERS = 4


def _combine_kernel(nc_ref, first_ref, start_ref, total_ref, z_ref, x1_ref, mod_ref, qt_ref, kt_ref, o_ref,
                    zbuf, sem, acc_ref):
    blk = pl.program_id(0)
    n_chunks = nc_ref[blk]
    first = first_ref[blk]
    total = total_ref[0]
    ahead = Z_BUFFERS - 1
    qt = qt_ref[...]
    kt = kt_ref[...]
    acc_ref[...] = jnp.zeros_like(acc_ref)

    def z_copy(g):
        rows = pl.ds(pl.multiple_of(start_ref[g], ROW_ALIGN), Z_CHUNK)
        slot = g % Z_BUFFERS
        return pltpu.make_async_copy(z_ref.at[rows], zbuf.at[slot], sem.at[slot])

    @pl.when(blk == 0)
    def _():
        for g in range(ahead):
            @pl.when(g < total)
            def _():
                z_copy(g).start()

    def chunk(c, carry):
        g = first + c
        slot = g % Z_BUFFERS

        @pl.when(g + ahead < total)
        def _():
            z_copy(g + ahead).start()

        z_copy(g).wait()
        r = start_ref[g] + lax.broadcasted_iota(I32, (qt.shape[0], Z_CHUNK), 1)
        onehot = jnp.where((r >= qt) & (r < qt + kt), 1.0, 0.0).astype(BF16)
        left, right = _unpack_halves(zbuf[slot])
        half = left.shape[1]
        acc_ref[:, :half] += _dot(onehot, left)
        acc_ref[:, half:] += _dot(onehot, right)
        return carry

    lax.fori_loop(0, n_chunks, chunk, 0)
    o_ref[...] = x1_ref[...] + mod_ref[0, 5:6, :] * acc_ref[...]


def _combine(chunks, z, x1, first_block, mods, qt, kt, cond_of_block):
    n = qt.shape[0]
    d = x1.shape[1]
    tb = COMBINE_BLOCK
    grid_spec = pltpu.PrefetchScalarGridSpec(
        num_scalar_prefetch=4,
        grid=(n // tb,),
        in_specs=[pl.BlockSpec(memory_space=pl.ANY),
                  pl.BlockSpec((tb, d), lambda i, *_: (i + first_block, 0)),
                  pl.BlockSpec((1, N_MOD, d), lambda i, *_: (cond_of_block(i), 0, 0)),
                  pl.BlockSpec((tb, 1), lambda i, *_: (i, 0)),
                  pl.BlockSpec((tb, 1), lambda i, *_: (i, 0))],
        out_specs=pl.BlockSpec((tb, d), lambda i, *_: (i, 0)),
        scratch_shapes=[pltpu.VMEM((Z_BUFFERS, Z_CHUNK, z.shape[1]), z.dtype), pltpu.SemaphoreType.DMA((Z_BUFFERS,)),
                        pltpu.VMEM((tb, d), F32)],
    )
    return pl.pallas_call(
        _combine_kernel,
        grid_spec=grid_spec,
        out_shape=jax.ShapeDtypeStruct((n, d), F32),
        compiler_params=_params(("arbitrary",)),
        name="combine",
    )(*chunks, z, x1, mods, qt, kt)


def _routing_tables(aff, cap, z_off):
    n = aff.shape[0]
    aff_t = aff.T
    sel, rinc = _route(aff_t, cap)
    slots = jnp.arange(cap, dtype=I32)
    chunk = 128
    rinc3 = rinc.reshape(rinc.shape[0], n // chunk, chunk)
    n_full = jnp.sum((rinc3[:, None, :, chunk - 1] <= slots[None, :, None]).astype(I32), axis=-1)
    pick = (n_full[:, :, None] == jnp.arange(n // chunk, dtype=I32)[None, None, :]).astype(F32)
    inside = jnp.einsum("eca,eab->ecb", pick, rinc3.astype(F32), precision=lax.Precision.HIGHEST)
    idx = n_full * chunk + jnp.sum((inside <= slots[None, :, None].astype(F32)).astype(I32), axis=-1)
    gate = jnp.take_along_axis(aff_t, idx, axis=1)
    k_tok = jnp.sum(sel, axis=0)
    q_tok = jnp.cumsum(k_tok) - k_tok + z_off
    before = jnp.cumsum(sel, axis=0) - sel
    qpos = jnp.take_along_axis(q_tok[None, :] + before, idx, axis=1)
    nblk = n // COMBINE_BLOCK
    q_blk = q_tok[::COMBINE_BLOCK]
    end_blk = q_blk + jnp.sum(k_tok.reshape(nblk, COMBINE_BLOCK), axis=1)
    w0 = (q_blk // ROW_ALIGN) * ROW_ALIGN
    nc = jnp.maximum((end_blk - w0 + Z_CHUNK - 1) // Z_CHUNK, 1).astype(I32)
    first = jnp.cumsum(nc) - nc
    g = jnp.arange(nblk + (N_EXPERTS * cap) // Z_CHUNK + nblk, dtype=I32)
    blk_of = jnp.sum((first[None, 1:] <= g[:, None]).astype(I32), axis=1)
    start = jnp.where(g < jnp.sum(nc), w0[blk_of] + (g - first[blk_of]) * Z_CHUNK, 0)
    chunks = (nc, first.astype(I32), start.astype(I32), jnp.sum(nc).astype(I32)[None])
    return idx, gate, qpos.astype(I32), chunks, q_tok.astype(I32)[:, None], k_tok.astype(I32)[:, None]


def kernel(x_prompt, x_sample, cache_k, cache_v, c, c_ctx, norm1_g, norm2_g, w_ada, b_ada, w_in, q_norm_g, k_norm_g,
           rpb, w_att_proj, w_pool, pool_scale, w_out, w_router, w_gate_e, w_up_e, w_down_e):
    assert w_ada.shape[0] == 1, "single trunk layer"
    nb, ls, d = x_prompt.shape
    db, ll, _ = x_sample.shape
    n_c, n_l = nb * ls, db * ll
    assert ls == TOKEN_BLOCK, "context keys / values are written one sequence per projection block"

    cond = jnp.concatenate([c_ctx[None, :], c, jnp.zeros((16 - 1 - db, d), F32)], axis=0)
    mods = _ada(cond, w_ada[0], b_ada).reshape(16, N_MOD, d)

    w_in_b = w_in[0].astype(BF16)
    w_att_b = w_att_proj[0].astype(BF16)
    w_pool_b = w_pool[0].astype(BF16)
    w_out_b = w_out[0].astype(BF16)

    cond_ctx = lambda i: 0
    lat_blocks = ll // TOKEN_BLOCK
    cond_lat = lambda i: 1 + i // lat_blocks

    xc = x_prompt.reshape(n_c, d)
    xl = x_sample.reshape(n_l, d)
    q_gain = jnp.tile(q_norm_g, (1, NA_HEADS))
    k_gain = jnp.tile(k_norm_g, (1, NA_HEADS))
    qc, kc, vc, uc, gac, gbc = _in_proj(xc, mods, norm1_g, w_in_b, q_gain, k_gain, cond_ctx, True)
    ql, kl, vl, ul, gal, gbl = _in_proj(xl, mods, norm1_g, w_in_b, q_gain, k_gain, cond_lat, False)

    seq_c = lambda a: a.reshape(nb, ls, a.shape[-1])
    seq_l = lambda a: a.reshape(db, ll, a.shape[-1])
    att_c = _attn_ctx(seq_c(qc), kc, vc)
    bias = _window_bias(rpb[0], ll // GRID_W)
    att_l = _attn_lat(seq_l(ql), seq_l(kl), seq_l(vl), jnp.transpose(cache_k[:, 0], (0, 2, 3, 1)),
                      jnp.transpose(cache_v[:, 0], (0, 2, 3, 1)), bias)
    pool_c = uc
    pool_l = _pool(seq_l(ul)).reshape(n_l, POOL_WIDTH)

    x1, h2, aff = _merge((xc, att_c.reshape(n_c, NA_WIDTH), pool_c, gac, gbc),
                         (xl, att_l.reshape(n_l, NA_WIDTH), pool_l, gal, gbl),
                         mods, w_att_b, w_pool_b, pool_scale, w_out_b, norm2_g, w_router[0], lat_blocks)

    cap_c = max(1, (CAPACITY_FACTOR * n_c) // N_EXPERTS)
    cap_l = max(1, (CAPACITY_FACTOR * n_l) // N_EXPERTS)
    m_tot = cap_c + cap_l
    m_iss = _moe_step_rows(m_tot, pl.cdiv(D_EXPERT, EXPERT_F_BLOCK)) * pl.cdiv(D_EXPERT, EXPERT_F_BLOCK)
    z_valid = N_EXPERTS * m_tot
    z_zero_rows = Z_CHUNK + ROW_ALIGN
    z_spare = z_valid + z_zero_rows
    z_rows = z_spare + m_iss
    idx_c, gate_c, qpos_c, chunks_c, qt_c, kt_c = _routing_tables(aff[:n_c], cap_c, 0)
    idx_l, gate_l, qpos_l, chunks_l, qt_l, kt_l = _routing_tables(aff[n_c:], cap_l, N_EXPERTS * cap_c)
    spare = jnp.broadcast_to(z_spare + jnp.arange(m_iss, dtype=I32), (N_EXPERTS + 1, m_iss))
    gidx = jnp.concatenate([idx_c, idx_l + n_c, jnp.zeros((N_EXPERTS, m_iss - m_tot), I32)], axis=1).reshape(-1)
    qpos = jnp.concatenate([spare[:1], jnp.concatenate([qpos_c, qpos_l, spare[1:, m_tot:]], axis=1)], axis=0).reshape(-1)
    gate = jnp.concatenate([gate_c, gate_l], axis=1)[:, :, None]

    z = _moe(gidx, qpos, h2, gate, jnp.swapaxes(w_gate_e[0], 1, 2), jnp.swapaxes(w_up_e[0], 1, 2), w_down_e[0],
             m_tot, z_rows, z_zero_rows)

    y_c = _combine(chunks_c, z, x1, 0, mods, qt_c, kt_c, cond_ctx)
    y_l = _combine(chunks_l, z, x1, n_c // COMBINE_BLOCK, mods, qt_l, kt_l, lambda i: 1 + i // (ll // COMBINE_BLOCK))

    state_k = jnp.transpose(kc, (0, 3, 1, 2))[:, None]
    state_v = jnp.transpose(vc, (0, 3, 1, 2))[:, None]
    return (y_c.reshape(nb, ls, d), y_l.reshape(db, ll, d), state_k, state_v)
```

```python
import functools

import jax
import jax.numpy as jnp
from jax import lax
from jax.experimental import pallas as pl
from jax.experimental.pallas import tpu as pltpu

F32 = jnp.float32
BF16 = jnp.bfloat16
I32 = jnp.int32

D_MODEL = 1024
GRID_W = 64
NA_HEADS = 8
HEAD_DIM = 64
NA_WIDTH = NA_HEADS * HEAD_DIM
WIN_ROWS = 8
WIN_COLS = 16
POOL_WINDOWS = (2, 4, 8, 16)
POOL_GROUP_DIM = 128
POOL_WIDTH = 512
POOL_OUT_DIM = 256
N_EXPERTS = 16
CAPACITY_FACTOR = 2
D_EXPERT = 2752
N_MOD = 6
EPS = 1e-6
NEG_INF = -1e30

TOKEN_BLOCK = 256
COMBINE_BLOCK = 256
MERGE_BLOCK = 512
MERGE_ROWS = 256
LAT_ROW_BLOCK = 4
EXPERT_F_BLOCK = 256
EXPERT_M_BLOCK = 768
Z_CHUNK = 256
ROW_ALIGN = 8
VMEM_LIMIT = 56 * 1024 * 1024


def _dot(a, b):
    return jnp.dot(a, b, preferred_element_type=F32)


def _dot_nt(a, b):
    return lax.dot_general(a, b, (((1,), (1,)), ((), ())), preferred_element_type=F32)


def _dot3(a, b):
    a_hi = a.astype(BF16)
    a_lo = (a - a_hi.astype(F32)).astype(BF16)
    b_hi = b.astype(BF16)
    b_lo = (b - b_hi.astype(F32)).astype(BF16)
    return _dot(a_hi, b_hi) + _dot(a_hi, b_lo) + _dot(a_lo, b_hi)


def _sigmoid(x):
    return 0.5 * jnp.tanh(0.5 * x) + 0.5


def _rms(x, g):
    return x * lax.rsqrt(jnp.mean(x * x, axis=-1, keepdims=True) + EPS) * g


def _pack_halves(x):
    w = x.shape[1] // 2
    return pltpu.pack_elementwise([x[:, :w], x[:, w:]], packed_dtype=BF16)


def _unpack_halves(p):
    return tuple(pltpu.unpack_elementwise(p, index=i, packed_dtype=BF16, unpacked_dtype=F32).astype(BF16)
                 for i in range(2))


def _params(sem):
    return pltpu.CompilerParams(dimension_semantics=sem, vmem_limit_bytes=VMEM_LIMIT)


def _ada_kernel(cond_ref, w_ref, b_ref, o_ref):
    c = cond_ref[...]
    o_ref[...] = _dot3(c * _sigmoid(c), w_ref[...]) + b_ref[...]


def _ada(cond, w_ada, b_ada):
    rows, d = cond.shape
    n_out = w_ada.shape[1]
    bn = 512
    return pl.pallas_call(
        _ada_kernel,
        grid=(n_out // bn,),
        in_specs=[pl.BlockSpec((rows, d), lambda i: (0, 0)),
                  pl.BlockSpec((d, bn), lambda i: (0, i)),
                  pl.BlockSpec((1, bn), lambda i: (0, i))],
        out_specs=pl.BlockSpec((rows, bn), lambda i: (0, i)),
        out_shape=jax.ShapeDtypeStruct((rows, n_out), F32),
        compiler_params=_params(("arbitrary",)),
        name="ada",
    )(cond, w_ada, b_ada)


_IN_CUTS = (0, 512, 1024, 1536, 2048, 3072, 4096)


HEAD_AVG_WIDTH = 256


def _head_avg_matrix():
    head = jnp.arange(HEAD_AVG_WIDTH, dtype=I32) // HEAD_DIM
    return jnp.where(head[:, None] == head[None, :], 1.0 / HEAD_DIM, 0.0).astype(BF16)


def _head_rms(x, g, avg, two_term):
    x2 = x * x
    means = []
    for c in range(0, x.shape[1], HEAD_AVG_WIDTH):
        blk = x2[:, c:c + HEAD_AVG_WIDTH]
        hi = blk.astype(BF16)
        ms = _dot(hi, avg)
        if two_term:
            ms = ms + _dot((blk - hi.astype(F32)).astype(BF16), avg)
        means.append(ms)
    return x * lax.rsqrt(jnp.concatenate(means, axis=-1) + EPS) * g


def _pooled(u):
    l = u.shape[0]
    t = lax.broadcasted_iota(I32, (l, POOL_GROUP_DIM), 0)
    groups = []
    for g, w in enumerate(POOL_WINDOWS):
        x = u[:, g * POOL_GROUP_DIM:(g + 1) * POOL_GROUP_DIM]
        acc = x
        for k in range(-(w // 2), w - w // 2):
            if k == 0:
                continue
            shifted = pltpu.roll(x, (-k) % l, 0)
            acc = acc + jnp.where((t + k >= 0) & (t + k < l), shifted, 0.0)
        cnt = (jnp.minimum(t + (w - w // 2), l) - jnp.maximum(t - w // 2, 0)).astype(F32)
        groups.append(acc / cnt - x)
    return jnp.concatenate(groups, axis=-1)


def _in_kernel(x_ref, mod_ref, g_ref, w_ref, qg_ref, kg_ref, avg_ref, q_ref, k_ref, v_ref, u_ref, ga_ref, gb_ref,
               *, kv_channel_major, pool_here):
    y = _rms(x_ref[...], g_ref[...])
    h = (y * (1.0 + mod_ref[0, 1:2, :]) + mod_ref[0, 0:1, :]).astype(BF16)
    part = lambda i: _dot(h, w_ref[:, _IN_CUTS[i]:_IN_CUTS[i + 1]])
    q_ref[...] = (_head_rms(part(0), qg_ref[...], avg_ref[...], False) * HEAD_DIM ** -0.5).astype(q_ref.dtype)
    kn = _head_rms(part(1), kg_ref[...], avg_ref[...], True)
    v = part(2)
    if kv_channel_major:
        k_ref[0] = kn.T.reshape(k_ref.shape[1:])
        v_ref[0] = v.T.reshape(v_ref.shape[1:])
    else:
        k_ref[...] = kn.astype(k_ref.dtype)
        v_ref[...] = v.astype(v_ref.dtype)
    u = part(3)
    u_ref[...] = _pooled(u).astype(u_ref.dtype) if pool_here else u
    ga_ref[...] = _sigmoid(part(4)).astype(ga_ref.dtype)
    gb_ref[...] = _sigmoid(part(5)).astype(gb_ref.dtype)


def _in_proj(x2d, mods, norm_g, w_in_bf16, q_gain, k_gain, cond_of_block, sequence_blocks):
    n, d = x2d.shape
    tb = TOKEN_BLOCK
    widths = [hi - lo for lo, hi in zip(_IN_CUTS[:-1], _IN_CUTS[1:])]
    dtypes = [BF16, BF16, BF16, BF16 if sequence_blocks else F32, BF16, BF16]
    out_specs = [pl.BlockSpec((tb, w), lambda i: (i, 0)) for w in widths]
    out_shape = [jax.ShapeDtypeStruct((n, w), t) for w, t in zip(widths, dtypes)]
    if sequence_blocks:
        for i in (1, 2):
            out_specs[i] = pl.BlockSpec((1, NA_HEADS, HEAD_DIM, tb), lambda i: (i, 0, 0, 0))
            out_shape[i] = jax.ShapeDtypeStruct((n // tb, NA_HEADS, HEAD_DIM, tb), F32)
    full = lambda a: pl.BlockSpec(a.shape, lambda i: (0,) * a.ndim)
    return pl.pallas_call(
        functools.partial(_in_kernel, kv_channel_major=sequence_blocks, pool_here=sequence_blocks),
        grid=(n // tb,),
        in_specs=[pl.BlockSpec((tb, d), lambda i: (i, 0)),
                  pl.BlockSpec((1, N_MOD, d), lambda i: (cond_of_block(i), 0, 0)),
                  full(norm_g), full(w_in_bf16), full(q_gain), full(k_gain),
                  pl.BlockSpec((HEAD_AVG_WIDTH, HEAD_AVG_WIDTH), lambda i: (0, 0))],
        out_specs=out_specs,
        out_shape=out_shape,
        compiler_params=_params(("arbitrary",)),
        name="in_proj",
    )(x2d, mods, norm_g, w_in_bf16, q_gain, k_gain, _head_avg_matrix())


def _with_ones(v):
    lane = lax.broadcasted_iota(I32, v.shape, 1)
    return jnp.concatenate([v.astype(BF16), jnp.where(lane == 0, 1.0, 0.0).astype(BF16)], axis=-1)


def _attn_ctx_kernel(q_ref, kt_ref, vt_ref, att_ref):
    qn = q_ref[0]
    l = kt_ref.shape[3]
    ones_row = jnp.where(lax.broadcasted_iota(I32, (HEAD_DIM, l), 0) == 0, 1.0, 0.0)
    for h in range(NA_HEADS):
        sl = slice(h * HEAD_DIM, (h + 1) * HEAD_DIM)
        s = _dot(qn[:, sl], kt_ref[0, h].astype(BF16))
        e = jnp.exp(s - jnp.max(s, axis=-1, keepdims=True)).astype(BF16)
        vt = jnp.concatenate([vt_ref[0, h], ones_row], axis=0).astype(BF16)
        o = _dot_nt(e, vt)
        att_ref[0, :, sl] = (o[:, :HEAD_DIM] / o[:, HEAD_DIM:HEAD_DIM + 1]).astype(att_ref.dtype)


def _attn_ctx(q, kt, vt):
    b, l, w = q.shape
    blk = pl.BlockSpec((1, l, w), lambda i: (i, 0, 0))
    tblk = pl.BlockSpec((1,) + kt.shape[1:], lambda i: (i, 0, 0, 0))
    return pl.pallas_call(
        _attn_ctx_kernel,
        grid=(b,),
        in_specs=[blk, tblk, tblk],
        out_specs=blk,
        out_shape=jax.ShapeDtypeStruct((b, l, w), BF16),
        compiler_params=_params(("arbitrary",)),
        name="attn_ctx",
    )(q, kt, vt)


def _lat_windows(rows):
    kr = min(WIN_ROWS, rows)
    rb = min(LAT_ROW_BLOCK, rows)
    wr = min(kr + rb, rows)
    assert rows % rb == 0
    starts = [min(max(i * rb - kr // 2, 0), rows - wr) for i in range(rows // rb)]
    for i, ws in enumerate(starts):
        for r in range(i * rb, (i + 1) * rb):
            rs = min(max(r - kr // 2, 0), rows - kr)
            assert ws <= rs and rs + kr <= ws + wr
    return kr, rb, wr, starts


def _attn_lat_kernel(q_ref, k_ref, v_ref, ckt_ref, cvt_ref, bias_ref, att_ref,
                     kn_s, vb_s, ckb_s, cvb_s, *, rows, kr, rb, wr):
    i = pl.program_id(1)

    @pl.when(i == 0)
    def _():
        lc = ckt_ref.shape[3]
        ones_row = jnp.where(lax.broadcasted_iota(I32, (HEAD_DIM, lc), 0) == 0, 1.0, 0.0)
        for h in range(NA_HEADS):
            sl = slice(h * HEAD_DIM, (h + 1) * HEAD_DIM)
            kn_s[h] = k_ref[0, :, sl]
            vb_s[h] = _with_ones(v_ref[0, :, sl])
            ckb_s[h] = ckt_ref[0, h].astype(BF16)
            cvb_s[h] = jnp.concatenate([cvt_ref[0, h], ones_row], axis=0).astype(BF16)

    win_start = jnp.clip(i * rb - kr // 2, 0, rows - wr)
    key_rows = pl.ds(pl.multiple_of(win_start * GRID_W, GRID_W), wr * GRID_W)
    qn_all = q_ref[0]
    for h in range(NA_HEADS):
        sl = slice(h * HEAD_DIM, (h + 1) * HEAD_DIM)
        qn = qn_all[:, sl]
        bias = bias_ref[h, pl.ds(i * rb, rb)].reshape(rb * GRID_W, wr * GRID_W)
        s_loc = _dot_nt(qn, kn_s[h, key_rows, :]) + bias
        s_ctx = _dot(qn, ckb_s[h])
        m = jnp.maximum(jnp.max(s_loc, axis=-1, keepdims=True), jnp.max(s_ctx, axis=-1, keepdims=True))
        o = (_dot(jnp.exp(s_loc - m).astype(BF16), vb_s[h, key_rows, :])
             + _dot_nt(jnp.exp(s_ctx - m).astype(BF16), cvb_s[h]))
        att_ref[0, :, sl] = (o[:, :HEAD_DIM] / o[:, HEAD_DIM:HEAD_DIM + 1]).astype(att_ref.dtype)


def _attn_lat(q, k, v, ckt, cvt, bias):
    b, l, w = q.shape
    lc = ckt.shape[3]
    rows = l // GRID_W
    kr, rb, wr, _ = _lat_windows(rows)
    seq = pl.BlockSpec((1, l, w), lambda i, r: (i, 0, 0))
    ctx = pl.BlockSpec((1, NA_HEADS, HEAD_DIM, lc), lambda i, r: (i, 0, 0, 0))
    rowblk = pl.BlockSpec((1, rb * GRID_W, w), lambda i, r: (i, r, 0))
    return pl.pallas_call(
        functools.partial(_attn_lat_kernel, rows=rows, kr=kr, rb=rb, wr=wr),
        grid=(b, rows // rb),
        in_specs=[rowblk, seq, seq, ctx, ctx,
                  pl.BlockSpec(bias.shape, lambda i, r: (0, 0, 0, 0), pipeline_mode=pl.Buffered(1))],
        out_specs=rowblk,
        out_shape=jax.ShapeDtypeStruct((b, l, w), BF16),
        scratch_shapes=[pltpu.VMEM((NA_HEADS, l, HEAD_DIM), BF16), pltpu.VMEM((NA_HEADS, l, 2 * HEAD_DIM), BF16),
                        pltpu.VMEM((NA_HEADS, HEAD_DIM, lc), BF16), pltpu.VMEM((NA_HEADS, 2 * HEAD_DIM, lc), BF16)],
        compiler_params=_params(("arbitrary", "arbitrary")),
        name="attn_lat",
    )(q, k, v, ckt, cvt, bias)


def _bias_kernel(rpb_ref, o_ref, *, rows):
    kr, rb, wr, starts = _lat_windows(rows)
    h = pl.program_id(0)
    n_ro, n_co = 2 * WIN_ROWS - 1, 2 * WIN_COLS - 1
    wq = lax.broadcasted_iota(I32, (GRID_W, GRID_W), 0)
    wk = lax.broadcasted_iota(I32, (GRID_W, GRID_W), 1)
    co = jnp.clip(wk - wq, -(WIN_COLS - 1), WIN_COLS - 1) + (WIN_COLS - 1)
    col_start = jnp.clip(wq - WIN_COLS // 2, 0, GRID_W - WIN_COLS)
    in_win = (wk >= col_start) & (wk < col_start + WIN_COLS)
    co = jnp.where(in_win, co, -1)
    masked = jnp.full((GRID_W, GRID_W), NEG_INF, F32)
    tables = {}
    for r in range(rows):
        row_start = min(max(r - kr // 2, 0), rows - kr)
        for j in range(wr):
            key_row = starts[r // rb] + j
            blk = masked
            if row_start <= key_row < row_start + kr:
                ro = key_row - r + (WIN_ROWS - 1)
                if ro not in tables:
                    t = masked
                    for c in range(n_co):
                        t = jnp.where(co == c, rpb_ref[(h * n_ro + ro) * n_co + c], t)
                    tables[ro] = t
                blk = tables[ro]
            o_ref[0, r, :, j * GRID_W:(j + 1) * GRID_W] = blk


def _window_bias(rpb, rows):
    _, _, wr, _ = _lat_windows(rows)
    nh = rpb.shape[0]
    return pl.pallas_call(
        functools.partial(_bias_kernel, rows=rows),
        grid=(nh,),
        in_specs=[pl.BlockSpec(memory_space=pltpu.SMEM)],
        out_specs=pl.BlockSpec((1, rows, GRID_W, wr * GRID_W), lambda h: (h, 0, 0, 0)),
        out_shape=jax.ShapeDtypeStruct((nh, rows, GRID_W, wr * GRID_W), F32),
        compiler_params=_params(("arbitrary",)),
        name="window_bias",
    )(rpb.reshape(-1))


def _pool_kernel(u_ref, o_ref):
    o_ref[0] = _pooled(u_ref[0]).astype(o_ref.dtype)


def _pool(u):
    b, l, w = u.shape
    blk = pl.BlockSpec((1, l, w), lambda i: (i, 0, 0))
    return pl.pallas_call(
        _pool_kernel, grid=(b,), in_specs=[blk], out_specs=blk,
        out_shape=jax.ShapeDtypeStruct((b, l, w), BF16),
        compiler_params=_params(("arbitrary",)),
        name="pool",
    )(u)


def _merge_kernel(xc_ref, xl_ref, attc_ref, attl_ref, plc_ref, pll_ref, gac_ref, gal_ref, gbc_ref, gbl_ref,
                  mod_ref, watt_ref, wpool_ref, ps_ref, wout_ref, g2_ref, wr_ref, x1_ref, h2_ref, aff_ref, *, ctx_blocks):
    is_ctx = pl.program_id(0) < ctx_blocks

    def block(x_ref, att_ref, pl_ref, ga_ref, gb_ref):
        for r0 in range(0, x_ref.shape[0], MERGE_ROWS):
            rows = slice(r0, r0 + MERGE_ROWS)
            o_a = _dot(att_ref[rows, :], watt_ref[...])
            o_b = jnp.concatenate(
                [_dot(pl_ref[rows, g * POOL_GROUP_DIM:(g + 1) * POOL_GROUP_DIM], wpool_ref[g])
                 for g in range(len(POOL_WINDOWS))], axis=-1) * ps_ref[...]
            merged = ga_ref[rows, :] * o_a + gb_ref[rows, :] * o_b
            x1 = x_ref[rows, :] + mod_ref[0, 2:3, :] * _dot(merged.astype(BF16), wout_ref[...])
            x1_ref[rows, :] = x1
            h2 = _rms(x1, g2_ref[...]) * (1.0 + mod_ref[0, 4:5, :]) + mod_ref[0, 3:4, :]
            h2_ref[rows, :] = h2
            logits = _dot3(h2, wr_ref[...])
            e = jnp.exp(logits - jnp.max(logits, axis=-1, keepdims=True))
            aff_ref[rows, :] = e / jnp.sum(e, axis=-1, keepdims=True)

    @pl.when(is_ctx)
    def _():
        block(xc_ref, attc_ref, plc_ref, gac_ref, gbc_ref)

    @pl.when(jnp.logical_not(is_ctx))
    def _():
        block(xl_ref, attl_ref, pll_ref, gal_ref, gbl_ref)


def _merge(ctx, lat, mods, w_att, w_pool, pool_scale, w_out, norm2_g, w_router, lat_blocks_per_seq):
    n_c, d = ctx[0].shape
    n = n_c + lat[0].shape[0]
    tb = MERGE_BLOCK
    cb = n_c // tb
    crow = lambda w: pl.BlockSpec((tb, w), lambda i: (jnp.minimum(i, cb - 1), 0))
    lrow = lambda w: pl.BlockSpec((tb, w), lambda i: (jnp.maximum(i - cb, 0), 0))
    orow = lambda w: pl.BlockSpec((tb, w), lambda i: (i, 0))
    full = lambda a: pl.BlockSpec(a.shape, lambda i: (0,) * a.ndim)
    cond = lambda i: jnp.where(i < cb, 0, 1 + (i - cb) // lat_blocks_per_seq)
    pairs, specs = [], []
    for a_c, a_l in zip(ctx, lat):
        pairs += [a_c, a_l]
        specs += [crow(a_c.shape[1]), lrow(a_l.shape[1])]
    return pl.pallas_call(
        functools.partial(_merge_kernel, ctx_blocks=cb),
        grid=(n // tb,),
        in_specs=specs + [pl.BlockSpec((1, N_MOD, d), lambda i: (cond(i), 0, 0)),
                          full(w_att), full(w_pool), full(pool_scale), full(w_out), full(norm2_g), full(w_router)],
        out_specs=[orow(d), orow(d), orow(N_EXPERTS)],
        out_shape=[jax.ShapeDtypeStruct((n, d), F32), jax.ShapeDtypeStruct((n, d), F32),
                   jax.ShapeDtypeStruct((n, N_EXPERTS), F32)],
        compiler_params=_params(("arbitrary",)),
        name="merge",
    )(*pairs, mods, w_att, w_pool, pool_scale, w_out, norm2_g, w_router)


def _prefix_incl(mask_f32, out_ref):
    e, n = mask_f32.shape
    i = lax.broadcasted_iota(I32, (128, 128), 0)
    j = lax.broadcasted_iota(I32, (128, 128), 1)
    tri = jnp.where(i <= j, 1.0, 0.0).astype(BF16)
    carry = jnp.zeros((e, 1), F32)
    for c in range(n // 128):
        inc = _dot(mask_f32[:, c * 128:(c + 1) * 128].astype(BF16), tri) + carry
        out_ref[:, c * 128:(c + 1) * 128] = inc.astype(I32)
        carry = inc[:, 127:128]


def _route_kernel(aff_ref, sel_ref, rinc_ref, tmp_ref, *, cap):
    aff = aff_ref[...]

    def bit_step(i, thr):
        cand = thr | jnp.left_shift(jnp.int32(1), 30 - i)
        cnt = jnp.sum(jnp.where(aff >= pltpu.bitcast(cand, F32), 1, 0), axis=1, keepdims=True)
        return jnp.where(cnt >= cap, cand, thr)

    thr = lax.fori_loop(0, 31, bit_step, jnp.zeros((aff.shape[0], 1), I32))
    gt = aff >= pltpu.bitcast(thr + 1, F32)
    eq = (aff >= pltpu.bitcast(thr, F32)) & jnp.logical_not(gt)
    need = cap - jnp.sum(jnp.where(gt, 1, 0), axis=1, keepdims=True)
    _prefix_incl(jnp.where(eq, 1.0, 0.0), tmp_ref)
    sel = gt | (eq & (tmp_ref[...] <= need))
    sel_ref[...] = jnp.where(sel, 1, 0)
    _prefix_incl(jnp.where(sel, 1.0, 0.0), rinc_ref)


def _route(aff_t, cap):
    e, n = aff_t.shape
    full = pl.BlockSpec((e, n), lambda: (0, 0))
    return pl.pallas_call(
        functools.partial(_route_kernel, cap=cap),
        in_specs=[full], out_specs=[full, full],
        out_shape=[jax.ShapeDtypeStruct((e, n), I32)] * 2,
        scratch_shapes=[pltpu.VMEM((e, n), I32)],
        compiler_params=pltpu.CompilerParams(vmem_limit_bytes=VMEM_LIMIT),
        name="route",
    )(aff_t)


def _moe_kernel(gidx_ref, qpos_ref, h_ref, gate_ref, wg_ref, wu_ref, wd_ref, z_ref,
                xf_ref, xb_ref, acc_ref, y_ref, sem_g, sem_s, *, m_tot, n_f, step_rows, z_zero_rows):
    e = pl.program_id(0)
    j = pl.program_id(1)
    n_e = pl.num_programs(0)
    m_iss = step_rows * n_f
    tf = wg_ref.shape[1]
    slot = e % 2

    def gather_row(lst, s, dst_slot):
        t = gidx_ref[lst * m_iss + s]
        return pltpu.make_async_copy(h_ref.at[pl.ds(t, 1)], xf_ref.at[dst_slot, pl.ds(s, 1)], sem_g.at[dst_slot])

    def scatter_row(lst, s):
        q = qpos_ref[lst * m_iss + s]
        return pltpu.make_async_copy(y_ref.at[pl.ds(s, 1)], z_ref.at[pl.ds(q, 1)], sem_s)

    def wait_gathers(dst_slot):
        pltpu.make_async_copy(h_ref.at[pl.ds(0, m_iss)], xf_ref.at[dst_slot], sem_g.at[dst_slot]).wait()

    def wait_scatters():
        pltpu.make_async_copy(y_ref, z_ref.at[pl.ds(0, m_iss)], sem_s).wait()

    @pl.when((e == 0) & (j == 0))
    def _():
        y_ref[...] = jnp.zeros_like(y_ref)
        cp = pltpu.make_async_copy(y_ref.at[pl.ds(0, z_zero_rows)], z_ref.at[pl.ds(n_e * m_tot, z_zero_rows)], sem_s)
        cp.start()
        cp.wait()

        def first(s, c):
            gather_row(0, s, 0).start()
            return c

        lax.fori_loop(0, m_iss, first, 0)

    @pl.when(j == 0)
    def _():
        wait_gathers(slot)
        xb_ref[...] = xf_ref[slot, 0:m_tot, :].astype(BF16)
        acc_ref[...] = jnp.zeros_like(acc_ref)

    nxt = lax.rem(e + 1, n_e)
    for u in range(step_rows):
        s = j * step_rows + u
        gather_row(nxt, s, 1 - slot).start()
        scatter_row(e, s).start()

    valid = D_EXPERT - j * tf
    cmask = lax.broadcasted_iota(I32, (1, tf), 1) < valid
    rmask = lax.broadcasted_iota(I32, (tf, 1), 0) < valid
    wg = wg_ref[0].astype(BF16)
    wu = wu_ref[0].astype(BF16)
    wd = jnp.where(rmask, wd_ref[0], 0.0).astype(BF16)
    for m in range(m_tot // EXPERT_M_BLOCK):
        rs = slice(m * EXPERT_M_BLOCK, (m + 1) * EXPERT_M_BLOCK)
        xs = xb_ref[rs, :]
        a = _dot_nt(xs, wg)
        b = _dot_nt(xs, wu)
        hm = jnp.where(cmask, a * _sigmoid(a) * b, 0.0).astype(BF16)
        acc_ref[rs, :] += _dot(hm, wd)

    @pl.when(j == n_f - 1)
    def _():
        wait_scatters()
        y_ref[0:m_tot, :] = _pack_halves(acc_ref[...] * gate_ref[0])

        @pl.when(e == n_e - 1)
        def _():
            def last(s, c):
                scatter_row(n_e, s).start()
                return c

            lax.fori_loop(0, m_iss, last, 0)
            wait_scatters()
            wait_gathers(1 - slot)


def _moe_step_rows(m_tot, n_f):
    return pl.cdiv(pl.cdiv(m_tot, n_f), ROW_ALIGN) * ROW_ALIGN


def _moe(gidx, qpos, h, gate, w_gate, w_up, w_down, m_tot, z_rows, z_zero_rows):
    d = h.shape[1]
    n_e = w_down.shape[0]
    tf = EXPERT_F_BLOCK
    n_f = pl.cdiv(D_EXPERT, tf)
    step_rows = _moe_step_rows(m_tot, n_f)
    m_iss = step_rows * n_f
    weights = pl.BlockSpec((1, tf, d), lambda e, j, *_: (e, j, 0))
    grid_spec = pltpu.PrefetchScalarGridSpec(
        num_scalar_prefetch=2,
        grid=(n_e, n_f),
        in_specs=[pl.BlockSpec(memory_space=pl.ANY),
                  pl.BlockSpec((1, m_tot, 1), lambda e, j, *_: (e, 0, 0)),
                  weights, weights, weights],
        out_specs=pl.BlockSpec(memory_space=pl.ANY),
        scratch_shapes=[pltpu.VMEM((2, m_iss, d), F32), pltpu.VMEM((m_tot, d), BF16), pltpu.VMEM((m_tot, d), F32),
                        pltpu.VMEM((m_iss, d // 2), jnp.uint32), pltpu.SemaphoreType.DMA((2,)),
                        pltpu.SemaphoreType.DMA],
    )
    return pl.pallas_call(
        functools.partial(_moe_kernel, m_tot=m_tot, n_f=n_f, step_rows=step_rows, z_zero_rows=z_zero_rows),
        grid_spec=grid_spec,
        out_shape=jax.ShapeDtypeStruct((z_rows, d // 2), jnp.uint32),
        compiler_params=_params(("arbitrary", "arbitrary")),
        name="moe",
    )(gidx, qpos, h, gate, w_gate, w_up, w_down)


Z_BUFFERS = 4


def _combine_kernel(nc_ref, first_ref, start_ref, total_ref, z_ref, x1_ref, mod_ref, qt_ref, kt_ref, o_ref,
                    zbuf, sem, acc_ref):
    blk = pl.program_id(0)
    n_chunks = nc_ref[blk]
    first = first_ref[blk]
    total = total_ref[0]
    ahead = Z_BUFFERS - 1
    qt = qt_ref[...]
    kt = kt_ref[...]
    acc_ref[...] = jnp.zeros_like(acc_ref)

    def z_copy(g):
        rows = pl.ds(pl.multiple_of(start_ref[g], ROW_ALIGN), Z_CHUNK)
        slot = g % Z_BUFFERS
        return pltpu.make_async_copy(z_ref.at[rows], zbuf.at[slot], sem.at[slot])

    @pl.when(blk == 0)
    def _():
        for g in range(ahead):
            @pl.when(g < total)
            def _():
                z_copy(g).start()

    def chunk(c, carry):
        g = first + c
        slot = g % Z_BUFFERS

        @pl.when(g + ahead < total)
        def _():
            z_copy(g + ahead).start()

        z_copy(g).wait()
        r = start_ref[g] + lax.broadcasted_iota(I32, (qt.shape[0], Z_CHUNK), 1)
        onehot = jnp.where((r >= qt) & (r < qt + kt), 1.0, 0.0).astype(BF16)
        left, right = _unpack_halves(zbuf[slot])
        half = left.shape[1]
        acc_ref[:, :half] += _dot(onehot, left)
        acc_ref[:, half:] += _dot(onehot, right)
        return carry

    lax.fori_loop(0, n_chunks, chunk, 0)
    o_ref[...] = x1_ref[...] + mod_ref[0, 5:6, :] * acc_ref[...]


def _combine(chunks, z, x1, first_block, mods, qt, kt, cond_of_block):
    n = qt.shape[0]
    d = x1.shape[1]
    tb = COMBINE_BLOCK
    grid_spec = pltpu.PrefetchScalarGridSpec(
        num_scalar_prefetch=4,
        grid=(n // tb,),
        in_specs=[pl.BlockSpec(memory_space=pl.ANY),
                  pl.BlockSpec((tb, d), lambda i, *_: (i + first_block, 0)),
                  pl.BlockSpec((1, N_MOD, d), lambda i, *_: (cond_of_block(i), 0, 0)),
                  pl.BlockSpec((tb, 1), lambda i, *_: (i, 0)),
                  pl.BlockSpec((tb, 1), lambda i, *_: (i, 0))],
        out_specs=pl.BlockSpec((tb, d), lambda i, *_: (i, 0)),
        scratch_shapes=[pltpu.VMEM((Z_BUFFERS, Z_CHUNK, z.shape[1]), z.dtype), pltpu.SemaphoreType.DMA((Z_BUFFERS,)),
                        pltpu.VMEM((tb, d), F32)],
    )
    return pl.pallas_call(
        _combine_kernel,
        grid_spec=grid_spec,
        out_shape=jax.ShapeDtypeStruct((n, d), F32),
        compiler_params=_params(("arbitrary",)),
        name="combine",
    )(*chunks, z, x1, mods, qt, kt)


def _routing_tables(aff, cap, z_off):
    n = aff.shape[0]
    aff_t = aff.T
    sel, rinc = _route(aff_t, cap)
    slots = jnp.arange(cap, dtype=I32)
    chunk = 128
    rinc3 = rinc.reshape(rinc.shape[0], n // chunk, chunk)
    n_full = jnp.sum((rinc3[:, None, :, chunk - 1] <= slots[None, :, None]).astype(I32), axis=-1)
    pick = (n_full[:, :, None] == jnp.arange(n // chunk, dtype=I32)[None, None, :]).astype(F32)
    inside = jnp.einsum("eca,eab->ecb", pick, rinc3.astype(F32), precision=lax.Precision.HIGHEST)
    idx = n_full * chunk + jnp.sum((inside <= slots[None, :, None].astype(F32)).astype(I32), axis=-1)
    gate = jnp.take_along_axis(aff_t, idx, axis=1)
    k_tok = jnp.sum(sel, axis=0)
    q_tok = jnp.cumsum(k_tok) - k_tok + z_off
    before = jnp.cumsum(sel, axis=0) - sel
    qpos = jnp.take_along_axis(q_tok[None, :] + before, idx, axis=1)
    nblk = n // COMBINE_BLOCK
    q_blk = q_tok[::COMBINE_BLOCK]
    end_blk = q_blk + jnp.sum(k_tok.reshape(nblk, COMBINE_BLOCK), axis=1)
    w0 = (q_blk // ROW_ALIGN) * ROW_ALIGN
    nc = jnp.maximum((end_blk - w0 + Z_CHUNK - 1) // Z_CHUNK, 1).astype(I32)
    first = jnp.cumsum(nc) - nc
    g = jnp.arange(nblk + (N_EXPERTS * cap) // Z_CHUNK + nblk, dtype=I32)
    blk_of = jnp.sum((first[None, 1:] <= g[:, None]).astype(I32), axis=1)
    start = jnp.where(g < jnp.sum(nc), w0[blk_of] + (g - first[blk_of]) * Z_CHUNK, 0)
    chunks = (nc, first.astype(I32), start.astype(I32), jnp.sum(nc).astype(I32)[None])
    return idx, gate, qpos.astype(I32), chunks, q_tok.astype(I32)[:, None], k_tok.astype(I32)[:, None]


def kernel(x_prompt, x_sample, cache_k, cache_v, c, c_ctx, norm1_g, norm2_g, w_ada, b_ada, w_in, q_norm_g, k_norm_g,
           rpb, w_att_proj, w_pool, pool_scale, w_out, w_router, w_gate_e, w_up_e, w_down_e):
    assert w_ada.shape[0] == 1, "single trunk layer"
    nb, ls, d = x_prompt.shape
    db, ll, _ = x_sample.shape
    n_c, n_l = nb * ls, db * ll
    assert ls == TOKEN_BLOCK, "context keys / values are written one sequence per projection block"

    cond = jnp.concatenate([c_ctx[None, :], c, jnp.zeros((16 - 1 - db, d), F32)], axis=0)
    mods = _ada(cond, w_ada[0], b_ada).reshape(16, N_MOD, d)

    w_in_b = w_in[0].astype(BF16)
    w_att_b = w_att_proj[0].astype(BF16)
    w_pool_b = w_pool[0].astype(BF16)
    w_out_b = w_out[0].astype(BF16)

    cond_ctx = lambda i: 0
    lat_blocks = ll // TOKEN_BLOCK
    cond_lat = lambda i: 1 + i // lat_blocks

    xc = x_prompt.reshape(n_c, d)
    xl = x_sample.reshape(n_l, d)
    q_gain = jnp.tile(q_norm_g, (1, NA_HEADS))
    k_gain = jnp.tile(k_norm_g, (1, NA_HEADS))
    qc, kc, vc, uc, gac, gbc = _in_proj(xc, mods, norm1_g, w_in_b, q_gain, k_gain, cond_ctx, True)
    ql, kl, vl, ul, gal, gbl = _in_proj(xl, mods, norm1_g, w_in_b, q_gain, k_gain, cond_lat, False)

    seq_c = lambda a: a.reshape(nb, ls, a.shape[-1])
    seq_l = lambda a: a.reshape(db, ll, a.shape[-1])
    att_c = _attn_ctx(seq_c(qc), kc, vc)
    bias = _window_bias(rpb[0], ll // GRID_W)
    att_l = _attn_lat(seq_l(ql), seq_l(kl), seq_l(vl), jnp.transpose(cache_k[:, 0], (0, 2, 3, 1)),
                      jnp.transpose(cache_v[:, 0], (0, 2, 3, 1)), bias)
    pool_c = uc
    pool_l = _pool(seq_l(ul)).reshape(n_l, POOL_WIDTH)

    x1, h2, aff = _merge((xc, att_c.reshape(n_c, NA_WIDTH), pool_c, gac, gbc),
                         (xl, att_l.reshape(n_l, NA_WIDTH), pool_l, gal, gbl),
                         mods, w_att_b, w_pool_b, pool_scale, w_out_b, norm2_g, w_router[0], ll // MERGE_BLOCK)

    cap_c = max(1, (CAPACITY_FACTOR * n_c) // N_EXPERTS)
    cap_l = max(1, (CAPACITY_FACTOR * n_l) // N_EXPERTS)
    m_tot = cap_c + cap_l
    m_iss = _moe_step_rows(m_tot, pl.cdiv(D_EXPERT, EXPERT_F_BLOCK)) * pl.cdiv(D_EXPERT, EXPERT_F_BLOCK)
    z_valid = N_EXPERTS * m_tot
    z_zero_rows = Z_CHUNK + ROW_ALIGN
    z_spare = z_valid + z_zero_rows
    z_rows = z_spare + m_iss
    idx_c, gate_c, qpos_c, chunks_c, qt_c, kt_c = _routing_tables(aff[:n_c], cap_c, 0)
    idx_l, gate_l, qpos_l, chunks_l, qt_l, kt_l = _routing_tables(aff[n_c:], cap_l, N_EXPERTS * cap_c)
    spare = jnp.broadcast_to(z_spare + jnp.arange(m_iss, dtype=I32), (N_EXPERTS + 1, m_iss))
    gidx = jnp.concatenate([idx_c, idx_l + n_c, jnp.zeros((N_EXPERTS, m_iss - m_tot), I32)], axis=1).reshape(-1)
    qpos = jnp.concatenate([spare[:1], jnp.concatenate([qpos_c, qpos_l, spare[1:, m_tot:]], axis=1)], axis=0).reshape(-1)
    gate = jnp.concatenate([gate_c, gate_l], axis=1)[:, :, None]

    z = _moe(gidx, qpos, h2, gate, jnp.swapaxes(w_gate_e[0], 1, 2), jnp.swapaxes(w_up_e[0], 1, 2), w_down_e[0],
             m_tot, z_rows, z_zero_rows)

    y_c = _combine(chunks_c, z, x1, 0, mods, qt_c, kt_c, cond_ctx)
    y_l = _combine(chunks_l, z, x1, n_c // COMBINE_BLOCK, mods, qt_l, kt_l, lambda i: 1 + i // (ll // COMBINE_BLOCK))

    state_k = jnp.transpose(kc, (0, 3, 1, 2))[:, None]
    state_v = jnp.transpose(vc, (0, 3, 1, 2))[:, None]
    return (y_c.reshape(nb, ls, d), y_l.reshape(db, ll, d), state_k, state_v)
```

```python
import functools

import jax
import jax.numpy as jnp
from jax import lax
from jax.experimental import pallas as pl
from jax.experimental.pallas import tpu as pltpu

F32 = jnp.float32
BF16 = jnp.bfloat16
I32 = jnp.int32

D_MODEL = 1024
GRID_W = 64
NA_HEADS = 8
HEAD_DIM = 64
NA_WIDTH = NA_HEADS * HEAD_DIM
WIN_ROWS = 8
WIN_COLS = 16
POOL_WINDOWS = (2, 4, 8, 16)
POOL_GROUP_DIM = 128
POOL_WIDTH = 512
POOL_OUT_DIM = 256
N_EXPERTS = 16
CAPACITY_FACTOR = 2
D_EXPERT = 2752
N_MOD = 6
EPS = 1e-6
NEG_INF = -1e30

TOKEN_BLOCK = 256
COMBINE_BLOCK = 256
MERGE_BLOCK = 512
MERGE_ROWS = 256
LAT_ROW_BLOCK = 4
EXPERT_F_BLOCK = 256
EXPERT_M_BLOCK = 768
Z_CHUNK = 256
ROW_ALIGN = 8
VMEM_LIMIT = 56 * 1024 * 1024


def _dot(a, b):
    return jnp.dot(a, b, preferred_element_type=F32)


def _dot_nt(a, b):
    return lax.dot_general(a, b, (((1,), (1,)), ((), ())), preferred_element_type=F32)


def _dot3(a, b):
    a_hi = a.astype(BF16)
    a_lo = (a - a_hi.astype(F32)).astype(BF16)
    b_hi = b.astype(BF16)
    b_lo = (b - b_hi.astype(F32)).astype(BF16)
    return _dot(a_hi, b_hi) + _dot(a_hi, b_lo) + _dot(a_lo, b_hi)


def _sigmoid(x):
    return 0.5 * jnp.tanh(0.5 * x) + 0.5


def _rms(x, g):
    return x * lax.rsqrt(jnp.mean(x * x, axis=-1, keepdims=True) + EPS) * g


def _pack_halves(x):
    w = x.shape[1] // 2
    return pltpu.pack_elementwise([x[:, :w], x[:, w:]], packed_dtype=BF16)


def _unpack_halves(p):
    return tuple(pltpu.unpack_elementwise(p, index=i, packed_dtype=BF16, unpacked_dtype=F32).astype(BF16)
                 for i in range(2))


def _params(sem):
    return pltpu.CompilerParams(dimension_semantics=sem, vmem_limit_bytes=VMEM_LIMIT)


def _ada_kernel(cond_ref, w_ref, b_ref, o_ref):
    c = cond_ref[...]
    o_ref[...] = _dot3(c * _sigmoid(c), w_ref[...]) + b_ref[...]


def _ada(cond, w_ada, b_ada):
    rows, d = cond.shape
    n_out = w_ada.shape[1]
    bn = 512
    return pl.pallas_call(
        _ada_kernel,
        grid=(n_out // bn,),
        in_specs=[pl.BlockSpec((rows, d), lambda i: (0, 0)),
                  pl.BlockSpec((d, bn), lambda i: (0, i)),
                  pl.BlockSpec((1, bn), lambda i: (0, i))],
        out_specs=pl.BlockSpec((rows, bn), lambda i: (0, i)),
        out_shape=jax.ShapeDtypeStruct((rows, n_out), F32),
        compiler_params=_params(("arbitrary",)),
        name="ada",
    )(cond, w_ada, b_ada)


_IN_CUTS = (0, 512, 1024, 1536, 2048, 3072, 4096)


HEAD_AVG_WIDTH = 256


def _head_avg_matrix():
    head = jnp.arange(HEAD_AVG_WIDTH, dtype=I32) // HEAD_DIM
    return jnp.where(head[:, None] == head[None, :], 1.0 / HEAD_DIM, 0.0).astype(BF16)


def _head_rms(x, g, avg, two_term):
    x2 = x * x
    means = []
    for c in range(0, x.shape[1], HEAD_AVG_WIDTH):
        blk = x2[:, c:c + HEAD_AVG_WIDTH]
        hi = blk.astype(BF16)
        ms = _dot(hi, avg)
        if two_term:
            ms = ms + _dot((blk - hi.astype(F32)).astype(BF16), avg)
        means.append(ms)
    return x * lax.rsqrt(jnp.concatenate(means, axis=-1) + EPS) * g


def _pooled(u):
    l = u.shape[0]
    t = lax.broadcasted_iota(I32, (l, POOL_GROUP_DIM), 0)
    groups = []
    for g, w in enumerate(POOL_WINDOWS):
        x = u[:, g * POOL_GROUP_DIM:(g + 1) * POOL_GROUP_DIM]
        acc = x
        for k in range(-(w // 2), w - w // 2):
            if k == 0:
                continue
            shifted = pltpu.roll(x, (-k) % l, 0)
            acc = acc + jnp.where((t + k >= 0) & (t + k < l), shifted, 0.0)
        cnt = (jnp.minimum(t + (w - w // 2), l) - jnp.maximum(t - w // 2, 0)).astype(F32)
        groups.append(acc / cnt - x)
    return jnp.concatenate(groups, axis=-1)


def _in_kernel(x_ref, mod_ref, g_ref, w_ref, qg_ref, kg_ref, avg_ref, q_ref, k_ref, v_ref, u_ref, ga_ref, gb_ref,
               *, kv_channel_major):
    us = []
    for r0 in range(0, x_ref.shape[0], TOKEN_BLOCK):
        rows = slice(r0, r0 + TOKEN_BLOCK)
        y = _rms(x_ref[rows, :], g_ref[...])
        h = (y * (1.0 + mod_ref[0, 1:2, :]) + mod_ref[0, 0:1, :]).astype(BF16)
        part = lambda i: _dot(h, w_ref[:, _IN_CUTS[i]:_IN_CUTS[i + 1]])
        q_ref[rows, :] = (_head_rms(part(0), qg_ref[...], avg_ref[...], False) * HEAD_DIM ** -0.5).astype(q_ref.dtype)
        kn = _head_rms(part(1), kg_ref[...], avg_ref[...], True)
        v = part(2)
        if kv_channel_major:
            k_ref[0] = kn.T.reshape(k_ref.shape[1:])
            v_ref[0] = v.T.reshape(v_ref.shape[1:])
        else:
            k_ref[rows, :] = kn.astype(k_ref.dtype)
            v_ref[rows, :] = v.astype(v_ref.dtype)
        us.append(part(3))
        ga_ref[rows, :] = _sigmoid(part(4)).astype(ga_ref.dtype)
        gb_ref[rows, :] = _sigmoid(part(5)).astype(gb_ref.dtype)
    u_ref[...] = _pooled(jnp.concatenate(us, axis=0)).astype(u_ref.dtype)


def _in_proj(x2d, seq_len, mods, norm_g, w_in_bf16, q_gain, k_gain, cond_of_seq, kv_results):
    n, d = x2d.shape
    tb = seq_len
    assert not kv_results or tb == TOKEN_BLOCK
    widths = [hi - lo for lo, hi in zip(_IN_CUTS[:-1], _IN_CUTS[1:])]
    out_specs = [pl.BlockSpec((tb, w), lambda i: (i, 0)) for w in widths]
    out_shape = [jax.ShapeDtypeStruct((n, w), BF16) for w in widths]
    if kv_results:
        for i in (1, 2):
            out_specs[i] = pl.BlockSpec((1, NA_HEADS, HEAD_DIM, tb), lambda i: (i, 0, 0, 0))
            out_shape[i] = jax.ShapeDtypeStruct((n // tb, NA_HEADS, HEAD_DIM, tb), F32)
    full = lambda a: pl.BlockSpec(a.shape, lambda i: (0,) * a.ndim)
    return pl.pallas_call(
        functools.partial(_in_kernel, kv_channel_major=kv_results),
        grid=(n // tb,),
        in_specs=[pl.BlockSpec((tb, d), lambda i: (i, 0)),
                  pl.BlockSpec((1, N_MOD, d), lambda i: (cond_of_seq(i), 0, 0)),
                  full(norm_g), pl.BlockSpec(w_in_bf16.shape, lambda i: (0, 0), pipeline_mode=pl.Buffered(1)),
                  full(q_gain), full(k_gain),
                  pl.BlockSpec((HEAD_AVG_WIDTH, HEAD_AVG_WIDTH), lambda i: (0, 0))],
        out_specs=out_specs,
        out_shape=out_shape,
        compiler_params=_params(("arbitrary",)),
        name="in_proj",
    )(x2d, mods, norm_g, w_in_bf16, q_gain, k_gain, _head_avg_matrix())


def _with_ones(v):
    lane = lax.broadcasted_iota(I32, v.shape, 1)
    return jnp.concatenate([v.astype(BF16), jnp.where(lane == 0, 1.0, 0.0).astype(BF16)], axis=-1)


def _attn_ctx_kernel(q_ref, kt_ref, vt_ref, att_ref):
    qn = q_ref[0]
    l = kt_ref.shape[3]
    ones_row = jnp.where(lax.broadcasted_iota(I32, (HEAD_DIM, l), 0) == 0, 1.0, 0.0)
    for h in range(NA_HEADS):
        sl = slice(h * HEAD_DIM, (h + 1) * HEAD_DIM)
        s = _dot(qn[:, sl], kt_ref[0, h].astype(BF16))
        e = jnp.exp(s - jnp.max(s, axis=-1, keepdims=True)).astype(BF16)
        vt = jnp.concatenate([vt_ref[0, h], ones_row], axis=0).astype(BF16)
        o = _dot_nt(e, vt)
        att_ref[0, :, sl] = (o[:, :HEAD_DIM] / o[:, HEAD_DIM:HEAD_DIM + 1]).astype(att_ref.dtype)


def _attn_ctx(q, kt, vt):
    b, l, w = q.shape
    blk = pl.BlockSpec((1, l, w), lambda i: (i, 0, 0))
    tblk = pl.BlockSpec((1,) + kt.shape[1:], lambda i: (i, 0, 0, 0))
    return pl.pallas_call(
        _attn_ctx_kernel,
        grid=(b,),
        in_specs=[blk, tblk, tblk],
        out_specs=blk,
        out_shape=jax.ShapeDtypeStruct((b, l, w), BF16),
        compiler_params=_params(("arbitrary",)),
        name="attn_ctx",
    )(q, kt, vt)


def _lat_windows(rows):
    kr = min(WIN_ROWS, rows)
    rb = min(LAT_ROW_BLOCK, rows)
    wr = min(kr + rb, rows)
    assert rows % rb == 0
    starts = [min(max(i * rb - kr // 2, 0), rows - wr) for i in range(rows // rb)]
    for i, ws in enumerate(starts):
        for r in range(i * rb, (i + 1) * rb):
            rs = min(max(r - kr // 2, 0), rows - kr)
            assert ws <= rs and rs + kr <= ws + wr
    return kr, rb, wr, starts


def _attn_lat_kernel(q_ref, k_ref, v_ref, ckt_ref, cvt_ref, bias_ref, att_ref,
                     kn_s, vb_s, ckb_s, cvb_s, *, rows, kr, rb, wr):
    i = pl.program_id(1)

    @pl.when(i == 0)
    def _():
        lc = ckt_ref.shape[3]
        ones_row = jnp.where(lax.broadcasted_iota(I32, (HEAD_DIM, lc), 0) == 0, 1.0, 0.0)
        for h in range(NA_HEADS):
            sl = slice(h * HEAD_DIM, (h + 1) * HEAD_DIM)
            kn_s[h] = k_ref[0, :, sl]
            vb_s[h] = _with_ones(v_ref[0, :, sl])
            ckb_s[h] = ckt_ref[0, h].astype(BF16)
            cvb_s[h] = jnp.concatenate([cvt_ref[0, h], ones_row], axis=0).astype(BF16)

    win_start = jnp.clip(i * rb - kr // 2, 0, rows - wr)
    key_rows = pl.ds(pl.multiple_of(win_start * GRID_W, GRID_W), wr * GRID_W)
    qn_all = q_ref[0]
    for h in range(NA_HEADS):
        sl = slice(h * HEAD_DIM, (h + 1) * HEAD_DIM)
        qn = qn_all[:, sl]
        bias = bias_ref[h, pl.ds(i * rb, rb)].reshape(rb * GRID_W, wr * GRID_W)
        s_loc = _dot_nt(qn, kn_s[h, key_rows, :]) + bias
        s_ctx = _dot(qn, ckb_s[h])
        m = jnp.maximum(jnp.max(s_loc, axis=-1, keepdims=True), jnp.max(s_ctx, axis=-1, keepdims=True))
        o = (_dot(jnp.exp(s_loc - m).astype(BF16), vb_s[h, key_rows, :])
             + _dot_nt(jnp.exp(s_ctx - m).astype(BF16), cvb_s[h]))
        att_ref[0, :, sl] = (o[:, :HEAD_DIM] / o[:, HEAD_DIM:HEAD_DIM + 1]).astype(att_ref.dtype)


def _attn_lat(q, k, v, ckt, cvt, bias):
    b, l, w = q.shape
    lc = ckt.shape[3]
    rows = l // GRID_W
    kr, rb, wr, _ = _lat_windows(rows)
    seq = pl.BlockSpec((1, l, w), lambda i, r: (i, 0, 0))
    ctx = pl.BlockSpec((1, NA_HEADS, HEAD_DIM, lc), lambda i, r: (i, 0, 0, 0))
    rowblk = pl.BlockSpec((1, rb * GRID_W, w), lambda i, r: (i, r, 0))
    return pl.pallas_call(
        functools.partial(_attn_lat_kernel, rows=rows, kr=kr, rb=rb, wr=wr),
        grid=(b, rows // rb),
        in_specs=[rowblk, seq, seq, ctx, ctx,
                  pl.BlockSpec(bias.shape, lambda i, r: (0, 0, 0, 0), pipeline_mode=pl.Buffered(1))],
        out_specs=rowblk,
        out_shape=jax.ShapeDtypeStruct((b, l, w), BF16),
        scratch_shapes=[pltpu.VMEM((NA_HEADS, l, HEAD_DIM), BF16), pltpu.VMEM((NA_HEADS, l, 2 * HEAD_DIM), BF16),
                        pltpu.VMEM((NA_HEADS, HEAD_DIM, lc), BF16), pltpu.VMEM((NA_HEADS, 2 * HEAD_DIM, lc), BF16)],
        compiler_params=_params(("arbitrary", "arbitrary")),
        name="attn_lat",
    )(q, k, v, ckt, cvt, bias)


def _bias_kernel(rpb_ref, o_ref, *, rows):
    kr, rb, wr, starts = _lat_windows(rows)
    h = pl.program_id(0)
    n_ro, n_co = 2 * WIN_ROWS - 1, 2 * WIN_COLS - 1
    wq = lax.broadcasted_iota(I32, (GRID_W, GRID_W), 0)
    wk = lax.broadcasted_iota(I32, (GRID_W, GRID_W), 1)
    co = jnp.clip(wk - wq, -(WIN_COLS - 1), WIN_COLS - 1) + (WIN_COLS - 1)
    col_start = jnp.clip(wq - WIN_COLS // 2, 0, GRID_W - WIN_COLS)
    in_win = (wk >= col_start) & (wk < col_start + WIN_COLS)
    co = jnp.where(in_win, co, -1)
    masked = jnp.full((GRID_W, GRID_W), NEG_INF, F32)
    tables = {}
    for r in range(rows):
        row_start = min(max(r - kr // 2, 0), rows - kr)
        for j in range(wr):
            key_row = starts[r // rb] + j
            blk = masked
            if row_start <= key_row < row_start + kr:
                ro = key_row - r + (WIN_ROWS - 1)
                if ro not in tables:
                    t = masked
                    for c in range(n_co):
                        t = jnp.where(co == c, rpb_ref[(h * n_ro + ro) * n_co + c], t)
                    tables[ro] = t
                blk = tables[ro]
            o_ref[0, r, :, j * GRID_W:(j + 1) * GRID_W] = blk


def _window_bias(rpb, rows):
    _, _, wr, _ = _lat_windows(rows)
    nh = rpb.shape[0]
    return pl.pallas_call(
        functools.partial(_bias_kernel, rows=rows),
        grid=(nh,),
        in_specs=[pl.BlockSpec(memory_space=pltpu.SMEM)],
        out_specs=pl.BlockSpec((1, rows, GRID_W, wr * GRID_W), lambda h: (h, 0, 0, 0)),
        out_shape=jax.ShapeDtypeStruct((nh, rows, GRID_W, wr * GRID_W), F32),
        compiler_params=_params(("arbitrary",)),
        name="window_bias",
    )(rpb.reshape(-1))


def _merge_kernel(xc_ref, xl_ref, attc_ref, attl_ref, plc_ref, pll_ref, gac_ref, gal_ref, gbc_ref, gbl_ref,
                  mod_ref, watt_ref, wpool_ref, ps_ref, wout_ref, g2_ref, wr_ref, x1_ref, h2_ref, aff_ref, *, ctx_blocks):
    is_ctx = pl.program_id(0) < ctx_blocks

    def block(x_ref, att_ref, pl_ref, ga_ref, gb_ref):
        for r0 in range(0, x_ref.shape[0], MERGE_ROWS):
            rows = slice(r0, r0 + MERGE_ROWS)
            o_a = _dot(att_ref[rows, :], watt_ref[...])
            o_b = jnp.concatenate(
                [_dot(pl_ref[rows, g * POOL_GROUP_DIM:(g + 1) * POOL_GROUP_DIM], wpool_ref[g])
                 for g in range(len(POOL_WINDOWS))], axis=-1) * ps_ref[...]
            merged = ga_ref[rows, :] * o_a + gb_ref[rows, :] * o_b
            x1 = x_ref[rows, :] + mod_ref[0, 2:3, :] * _dot(merged.astype(BF16), wout_ref[...])
            x1_ref[rows, :] = x1
            h2 = _rms(x1, g2_ref[...]) * (1.0 + mod_ref[0, 4:5, :]) + mod_ref[0, 3:4, :]
            h2_ref[rows, :] = h2
            logits = _dot3(h2, wr_ref[...])
            e = jnp.exp(logits - jnp.max(logits, axis=-1, keepdims=True))
            aff_ref[rows, :] = e / jnp.sum(e, axis=-1, keepdims=True)

    @pl.when(is_ctx)
    def _():
        block(xc_ref, attc_ref, plc_ref, gac_ref, gbc_ref)

    @pl.when(jnp.logical_not(is_ctx))
    def _():
        block(xl_ref, attl_ref, pll_ref, gal_ref, gbl_ref)


def _merge(ctx, lat, mods, w_att, w_pool, pool_scale, w_out, norm2_g, w_router, lat_blocks_per_seq):
    n_c, d = ctx[0].shape
    n = n_c + lat[0].shape[0]
    tb = MERGE_BLOCK
    cb = n_c // tb
    crow = lambda w: pl.BlockSpec((tb, w), lambda i: (jnp.minimum(i, cb - 1), 0))
    lrow = lambda w: pl.BlockSpec((tb, w), lambda i: (jnp.maximum(i - cb, 0), 0))
    orow = lambda w: pl.BlockSpec((tb, w), lambda i: (i, 0))
    full = lambda a: pl.BlockSpec(a.shape, lambda i: (0,) * a.ndim)
    cond = lambda i: jnp.where(i < cb, 0, 1 + (i - cb) // lat_blocks_per_seq)
    pairs, specs = [], []
    for a_c, a_l in zip(ctx, lat):
        pairs += [a_c, a_l]
        specs += [crow(a_c.shape[1]), lrow(a_l.shape[1])]
    return pl.pallas_call(
        functools.partial(_merge_kernel, ctx_blocks=cb),
        grid=(n // tb,),
        in_specs=specs + [pl.BlockSpec((1, N_MOD, d), lambda i: (cond(i), 0, 0)),
                          full(w_att), full(w_pool), full(pool_scale), full(w_out), full(norm2_g), full(w_router)],
        out_specs=[orow(d), orow(d), orow(N_EXPERTS)],
        out_shape=[jax.ShapeDtypeStruct((n, d), F32), jax.ShapeDtypeStruct((n, d), F32),
                   jax.ShapeDtypeStruct((n, N_EXPERTS), F32)],
        compiler_params=_params(("arbitrary",)),
        name="merge",
    )(*pairs, mods, w_att, w_pool, pool_scale, w_out, norm2_g, w_router)


def _prefix_incl(mask_f32, out_ref):
    e, n = mask_f32.shape
    i = lax.broadcasted_iota(I32, (128, 128), 0)
    j = lax.broadcasted_iota(I32, (128, 128), 1)
    tri = jnp.where(i <= j, 1.0, 0.0).astype(BF16)
    carry = jnp.zeros((e, 1), F32)
    for c in range(n // 128):
        inc = _dot(mask_f32[:, c * 128:(c + 1) * 128].astype(BF16), tri) + carry
        out_ref[:, c * 128:(c + 1) * 128] = inc.astype(I32)
        carry = inc[:, 127:128]


def _route_kernel(aff_ref, sel_ref, rinc_ref, tmp_ref, *, cap):
    aff = aff_ref[...]

    def bit_step(i, thr):
        cand = thr | jnp.left_shift(jnp.int32(1), 30 - i)
        cnt = jnp.sum(jnp.where(aff >= pltpu.bitcast(cand, F32), 1, 0), axis=1, keepdims=True)
        return jnp.where(cnt >= cap, cand, thr)

    thr = lax.fori_loop(0, 31, bit_step, jnp.zeros((aff.shape[0], 1), I32))
    gt = aff >= pltpu.bitcast(thr + 1, F32)
    eq = (aff >= pltpu.bitcast(thr, F32)) & jnp.logical_not(gt)
    need = cap - jnp.sum(jnp.where(gt, 1, 0), axis=1, keepdims=True)
    _prefix_incl(jnp.where(eq, 1.0, 0.0), tmp_ref)
    sel = gt | (eq & (tmp_ref[...] <= need))
    sel_ref[...] = jnp.where(sel, 1, 0)
    _prefix_incl(jnp.where(sel, 1.0, 0.0), rinc_ref)


def _route(aff_t, cap):
    e, n = aff_t.shape
    full = pl.BlockSpec((e, n), lambda: (0, 0))
    return pl.pallas_call(
        functools.partial(_route_kernel, cap=cap),
        in_specs=[full], out_specs=[full, full],
        out_shape=[jax.ShapeDtypeStruct((e, n), I32)] * 2,
        scratch_shapes=[pltpu.VMEM((e, n), I32)],
        compiler_params=pltpu.CompilerParams(vmem_limit_bytes=VMEM_LIMIT),
        name="route",
    )(aff_t)


def _moe_kernel(gidx_ref, qpos_ref, h_ref, gate_ref, wg_ref, wu_ref, wd_ref, z_ref,
                xf_ref, xb_ref, acc_ref, y_ref, sem_g, sem_s, *, m_tot, n_f, step_rows, z_zero_rows):
    e = pl.program_id(0)
    j = pl.program_id(1)
    n_e = pl.num_programs(0)
    m_iss = step_rows * n_f
    tf = wg_ref.shape[1]
    slot = e % 2

    def gather_row(lst, s, dst_slot):
        t = gidx_ref[lst * m_iss + s]
        return pltpu.make_async_copy(h_ref.at[pl.ds(t, 1)], xf_ref.at[dst_slot, pl.ds(s, 1)], sem_g.at[dst_slot])

    def scatter_row(lst, s):
        q = qpos_ref[lst * m_iss + s]
        return pltpu.make_async_copy(y_ref.at[pl.ds(s, 1)], z_ref.at[pl.ds(q, 1)], sem_s)

    def wait_gathers(dst_slot):
        pltpu.make_async_copy(h_ref.at[pl.ds(0, m_iss)], xf_ref.at[dst_slot], sem_g.at[dst_slot]).wait()

    def wait_scatters():
        pltpu.make_async_copy(y_ref, z_ref.at[pl.ds(0, m_iss)], sem_s).wait()

    @pl.when((e == 0) & (j == 0))
    def _():
        y_ref[...] = jnp.zeros_like(y_ref)
        cp = pltpu.make_async_copy(y_ref.at[pl.ds(0, z_zero_rows)], z_ref.at[pl.ds(n_e * m_tot, z_zero_rows)], sem_s)
        cp.start()
        cp.wait()

        def first(s, c):
            gather_row(0, s, 0).start()
            return c

        lax.fori_loop(0, m_iss, first, 0)

    @pl.when(j == 0)
    def _():
        wait_gathers(slot)
        xb_ref[...] = xf_ref[slot, 0:m_tot, :].astype(BF16)
        acc_ref[...] = jnp.zeros_like(acc_ref)

    nxt = lax.rem(e + 1, n_e)
    for u in range(step_rows):
        s = j * step_rows + u
        gather_row(nxt, s, 1 - slot).start()
        scatter_row(e, s).start()

    valid = D_EXPERT - j * tf
    cmask = lax.broadcasted_iota(I32, (1, tf), 1) < valid
    rmask = lax.broadcasted_iota(I32, (tf, 1), 0) < valid
    wg = wg_ref[0].astype(BF16)
    wu = wu_ref[0].astype(BF16)
    wd = jnp.where(rmask, wd_ref[0], 0.0).astype(BF16)
    for m in range(m_tot // EXPERT_M_BLOCK):
        rs = slice(m * EXPERT_M_BLOCK, (m + 1) * EXPERT_M_BLOCK)
        xs = xb_ref[rs, :]
        a = _dot_nt(xs, wg)
        b = _dot_nt(xs, wu)
        hm = jnp.where(cmask, a * _sigmoid(a) * b, 0.0).astype(BF16)
        acc_ref[rs, :] += _dot(hm, wd)

    @pl.when(j == n_f - 1)
    def _():
        wait_scatters()
        y_ref[0:m_tot, :] = _pack_halves(acc_ref[...] * gate_ref[0])

        @pl.when(e == n_e - 1)
        def _():
            def last(s, c):
                scatter_row(n_e, s).start()
                return c

            lax.fori_loop(0, m_iss, last, 0)
            wait_scatters()
            wait_gathers(1 - slot)


def _moe_step_rows(m_tot, n_f):
    return pl.cdiv(pl.cdiv(m_tot, n_f), ROW_ALIGN) * ROW_ALIGN


def _moe(gidx, qpos, h, gate, w_gate, w_up, w_down, m_tot, z_rows, z_zero_rows):
    d = h.shape[1]
    n_e = w_down.shape[0]
    tf = EXPERT_F_BLOCK
    n_f = pl.cdiv(D_EXPERT, tf)
    step_rows = _moe_step_rows(m_tot, n_f)
    m_iss = step_rows * n_f
    weights = pl.BlockSpec((1, tf, d), lambda e, j, *_: (e, j, 0))
    grid_spec = pltpu.PrefetchScalarGridSpec(
        num_scalar_prefetch=2,
        grid=(n_e, n_f),
        in_specs=[pl.BlockSpec(memory_space=pl.ANY),
                  pl.BlockSpec((1, m_tot, 1), lambda e, j, *_: (e, 0, 0)),
                  weights, weights, weights],
        out_specs=pl.BlockSpec(memory_space=pl.ANY),
        scratch_shapes=[pltpu.VMEM((2, m_iss, d), F32), pltpu.VMEM((m_tot, d), BF16), pltpu.VMEM((m_tot, d), F32),
                        pltpu.VMEM((m_iss, d // 2), jnp.uint32), pltpu.SemaphoreType.DMA((2,)),
                        pltpu.SemaphoreType.DMA],
    )
    return pl.pallas_call(
        functools.partial(_moe_kernel, m_tot=m_tot, n_f=n_f, step_rows=step_rows, z_zero_rows=z_zero_rows),
        grid_spec=grid_spec,
        out_shape=jax.ShapeDtypeStruct((z_rows, d // 2), jnp.uint32),
        compiler_params=_params(("arbitrary", "arbitrary")),
        name="moe",
    )(gidx, qpos, h, gate, w_gate, w_up, w_down)


Z_BUFFERS = 4


def _combine_kernel(nc_ref, first_ref, start_ref, total_ref, z_ref, x1_ref, mod_ref, qt_ref, kt_ref, o_ref,
                    zbuf, sem, acc_ref):
    blk = pl.program_id(0)
    n_chunks = nc_ref[blk]
    first = first_ref[blk]
    total = total_ref[0]
    ahead = Z_BUFFERS - 1
    qt = qt_ref[...]
    kt = kt_ref[...]
    acc_ref[...] = jnp.zeros_like(acc_ref)

    def z_copy(g):
        rows = pl.ds(pl.multiple_of(start_ref[g], ROW_ALIGN), Z_CHUNK)
        slot = g % Z_BUFFERS
        return pltpu.make_async_copy(z_ref.at[rows], zbuf.at[slot], sem.at[slot])

    @pl.when(blk == 0)
    def _():
        for g in range(ahead):
            @pl.when(g < total)
            def _():
                z_copy(g).start()

    def chunk(c, carry):
        g = first + c
        slot = g % Z_BUFFERS

        @pl.when(g + ahead < total)
        def _():
            z_copy(g + ahead).start()

        z_copy(g).wait()
        r = start_ref[g] + lax.broadcasted_iota(I32, (qt.shape[0], Z_CHUNK), 1)
        onehot = jnp.where((r >= qt) & (r < qt + kt), 1.0, 0.0).astype(BF16)
        left, right = _unpack_halves(zbuf[slot])
        half = left.shape[1]
        acc_ref[:, :half] += _dot(onehot, left)
        acc_ref[:, half:] += _dot(onehot, right)
        return carry

    lax.fori_loop(0, n_chunks, chunk, 0)
    o_ref[...] = x1_ref[...] + mod_ref[0, 5:6, :] * acc_ref[...]


def _combine(chunks, z, x1, first_block, mods, qt, kt, cond_of_block):
    n = qt.shape[0]
    d = x1.shape[1]
    tb = COMBINE_BLOCK
    grid_spec = pltpu.PrefetchScalarGridSpec(
        num_scalar_prefetch=4,
        grid=(n // tb,),
        in_specs=[pl.BlockSpec(memory_space=pl.ANY),
                  pl.BlockSpec((tb, d), lambda i, *_: (i + first_block, 0)),
                  pl.BlockSpec((1, N_MOD, d), lambda i, *_: (cond_of_block(i), 0, 0)),
                  pl.BlockSpec((tb, 1), lambda i, *_: (i, 0)),
                  pl.BlockSpec((tb, 1), lambda i, *_: (i, 0))],
        out_specs=pl.BlockSpec((tb, d), lambda i, *_: (i, 0)),
        scratch_shapes=[pltpu.VMEM((Z_BUFFERS, Z_CHUNK, z.shape[1]), z.dtype), pltpu.SemaphoreType.DMA((Z_BUFFERS,)),
                        pltpu.VMEM((tb, d), F32)],
    )
    return pl.pallas_call(
        _combine_kernel,
        grid_spec=grid_spec,
        out_shape=jax.ShapeDtypeStruct((n, d), F32),
        compiler_params=_params(("arbitrary",)),
        name="combine",
    )(*chunks, z, x1, mods, qt, kt)


def _routing_tables(aff, cap, z_off):
    n = aff.shape[0]
    aff_t = aff.T
    sel, rinc = _route(aff_t, cap)
    slots = jnp.arange(cap, dtype=I32)
    chunk = 128
    rinc3 = rinc.reshape(rinc.shape[0], n // chunk, chunk)
    n_full = jnp.sum((rinc3[:, None, :, chunk - 1] <= slots[None, :, None]).astype(I32), axis=-1)
    pick = (n_full[:, :, None] == jnp.arange(n // chunk, dtype=I32)[None, None, :]).astype(F32)
    inside = jnp.einsum("eca,eab->ecb", pick, rinc3.astype(F32), precision=lax.Precision.HIGHEST)
    idx = n_full * chunk + jnp.sum((inside <= slots[None, :, None].astype(F32)).astype(I32), axis=-1)
    gate = jnp.take_along_axis(aff_t, idx, axis=1)
    k_tok = jnp.sum(sel, axis=0)
    q_tok = jnp.cumsum(k_tok) - k_tok + z_off
    before = jnp.cumsum(sel, axis=0) - sel
    qpos = jnp.take_along_axis(q_tok[None, :] + before, idx, axis=1)
    nblk = n // COMBINE_BLOCK
    q_blk = q_tok[::COMBINE_BLOCK]
    end_blk = q_blk + jnp.sum(k_tok.reshape(nblk, COMBINE_BLOCK), axis=1)
    w0 = (q_blk // ROW_ALIGN) * ROW_ALIGN
    nc = jnp.maximum((end_blk - w0 + Z_CHUNK - 1) // Z_CHUNK, 1).astype(I32)
    first = jnp.cumsum(nc) - nc
    g = jnp.arange(nblk + (N_EXPERTS * cap) // Z_CHUNK + nblk, dtype=I32)
    blk_of = jnp.sum((first[None, 1:] <= g[:, None]).astype(I32), axis=1)
    start = jnp.where(g < jnp.sum(nc), w0[blk_of] + (g - first[blk_of]) * Z_CHUNK, 0)
    chunks = (nc, first.astype(I32), start.astype(I32), jnp.sum(nc).astype(I32)[None])
    return idx, gate, qpos.astype(I32), chunks, q_tok.astype(I32)[:, None], k_tok.astype(I32)[:, None]


def kernel(x_prompt, x_sample, cache_k, cache_v, c, c_ctx, norm1_g, norm2_g, w_ada, b_ada, w_in, q_norm_g, k_norm_g,
           rpb, w_att_proj, w_pool, pool_scale, w_out, w_router, w_gate_e, w_up_e, w_down_e):
    assert w_ada.shape[0] == 1, "single trunk layer"
    nb, ls, d = x_prompt.shape
    db, ll, _ = x_sample.shape
    n_c, n_l = nb * ls, db * ll
    assert ls == TOKEN_BLOCK, "context keys / values are written one sequence per projection block"

    cond = jnp.concatenate([c_ctx[None, :], c, jnp.zeros((16 - 1 - db, d), F32)], axis=0)
    mods = _ada(cond, w_ada[0], b_ada).reshape(16, N_MOD, d)

    w_in_b = w_in[0].astype(BF16)
    w_att_b = w_att_proj[0].astype(BF16)
    w_pool_b = w_pool[0].astype(BF16)
    w_out_b = w_out[0].astype(BF16)


    xc = x_prompt.reshape(n_c, d)
    xl = x_sample.reshape(n_l, d)
    q_gain = jnp.tile(q_norm_g, (1, NA_HEADS))
    k_gain = jnp.tile(k_norm_g, (1, NA_HEADS))
    qc, kc, vc, pool_c, gac, gbc = _in_proj(xc, ls, mods, norm1_g, w_in_b, q_gain, k_gain, lambda i: 0, True)
    ql, kl, vl, pool_l, gal, gbl = _in_proj(xl, ll, mods, norm1_g, w_in_b, q_gain, k_gain, lambda i: 1 + i, False)

    seq_c = lambda a: a.reshape(nb, ls, a.shape[-1])
    seq_l = lambda a: a.reshape(db, ll, a.shape[-1])
    att_c = _attn_ctx(seq_c(qc), kc, vc)
    bias = _window_bias(rpb[0], ll // GRID_W)
    att_l = _attn_lat(seq_l(ql), seq_l(kl), seq_l(vl), jnp.transpose(cache_k[:, 0], (0, 2, 3, 1)),
                      jnp.transpose(cache_v[:, 0], (0, 2, 3, 1)), bias)

    x1, h2, aff = _merge((xc, att_c.reshape(n_c, NA_WIDTH), pool_c, gac, gbc),
                         (xl, att_l.reshape(n_l, NA_WIDTH), pool_l, gal, gbl),
                         mods, w_att_b, w_pool_b, pool_scale, w_out_b, norm2_g, w_router[0], ll // MERGE_BLOCK)

    cap_c = max(1, (CAPACITY_FACTOR * n_c) // N_EXPERTS)
    cap_l = max(1, (CAPACITY_FACTOR * n_l) // N_EXPERTS)
    m_tot = cap_c + cap_l
    m_iss = _moe_step_rows(m_tot, pl.cdiv(D_EXPERT, EXPERT_F_BLOCK)) * pl.cdiv(D_EXPERT, EXPERT_F_BLOCK)
    z_valid = N_EXPERTS * m_tot
    z_zero_rows = Z_CHUNK + ROW_ALIGN
    z_spare = z_valid + z_zero_rows
    z_rows = z_spare + m_iss
    idx_c, gate_c, qpos_c, chunks_c, qt_c, kt_c = _routing_tables(aff[:n_c], cap_c, 0)
    idx_l, gate_l, qpos_l, chunks_l, qt_l, kt_l = _routing_tables(aff[n_c:], cap_l, N_EXPERTS * cap_c)
    spare = jnp.broadcast_to(z_spare + jnp.arange(m_iss, dtype=I32), (N_EXPERTS + 1, m_iss))
    gidx = jnp.concatenate([idx_c, idx_l + n_c, jnp.zeros((N_EXPERTS, m_iss - m_tot), I32)], axis=1).reshape(-1)
    qpos = jnp.concatenate([spare[:1], jnp.concatenate([qpos_c, qpos_l, spare[1:, m_tot:]], axis=1)], axis=0).reshape(-1)
    gate = jnp.concatenate([gate_c, gate_l], axis=1)[:, :, None]

    z = _moe(gidx, qpos, h2, gate, jnp.swapaxes(w_gate_e[0], 1, 2), jnp.swapaxes(w_up_e[0], 1, 2), w_down_e[0],
             m_tot, z_rows, z_zero_rows)

    y_c = _combine(chunks_c, z, x1, 0, mods, qt_c, kt_c, lambda i: 0)
    y_l = _combine(chunks_l, z, x1, n_c // COMBINE_BLOCK, mods, qt_l, kt_l, lambda i: 1 + i // (ll // COMBINE_BLOCK))

    state_k = jnp.transpose(kc, (0, 3, 1, 2))[:, None]
    state_v = jnp.transpose(vc, (0, 3, 1, 2))[:, None]
    return (y_c.reshape(nb, ls, d), y_l.reshape(db, ll, d), state_k, state_v)
```

```python
import functools

import jax
import jax.numpy as jnp
from jax import lax
from jax.experimental import pallas as pl
from jax.experimental.pallas import tpu as pltpu

F32 = jnp.float32
BF16 = jnp.bfloat16
I32 = jnp.int32

GRID_W = 64
NA_HEADS = 8
HEAD_DIM = 64
NA_WIDTH = NA_HEADS * HEAD_DIM
WIN_ROWS = 8
WIN_COLS = 16
POOL_WINDOWS = (2, 4, 8, 16)
POOL_GROUP_DIM = 128
POOL_WIDTH = len(POOL_WINDOWS) * POOL_GROUP_DIM
N_EXPERTS = 16
CAPACITY_FACTOR = 2
D_EXPERT = 2752
N_MOD = 6
EPS = 1e-6
NEG_INF = -1e30

TOKEN_BLOCK = 256
COMBINE_BLOCK = 256
MERGE_BLOCK = 512
MERGE_ROWS = 256
LAT_ROW_BLOCK = 4
EXPERT_F_BLOCK = 256
EXPERT_M_BLOCK = 768
Z_CHUNK = 256
Z_BUFFERS = 4
HEAD_AVG_WIDTH = 256
ROW_ALIGN = 8
VMEM_LIMIT = 56 * 1024 * 1024


def _dot(a, b):
    return jnp.dot(a, b, preferred_element_type=F32)


def _dot_nt(a, b):
    return lax.dot_general(a, b, (((1,), (1,)), ((), ())), preferred_element_type=F32)


def _dot3(a, b):
    a_hi = a.astype(BF16)
    a_lo = (a - a_hi.astype(F32)).astype(BF16)
    b_hi = b.astype(BF16)
    b_lo = (b - b_hi.astype(F32)).astype(BF16)
    return _dot(a_hi, b_hi) + _dot(a_hi, b_lo) + _dot(a_lo, b_hi)


def _sigmoid(x):
    return 0.5 * jnp.tanh(0.5 * x) + 0.5


def _rms(x, g):
    return x * lax.rsqrt(jnp.mean(x * x, axis=-1, keepdims=True) + EPS) * g


def _pack_halves(x):
    w = x.shape[1] // 2
    return pltpu.pack_elementwise([x[:, :w], x[:, w:]], packed_dtype=BF16)


def _unpack_halves(p):
    return tuple(pltpu.unpack_elementwise(p, index=i, packed_dtype=BF16, unpacked_dtype=F32).astype(BF16)
                 for i in range(2))


def _params(sem):
    return pltpu.CompilerParams(dimension_semantics=sem, vmem_limit_bytes=VMEM_LIMIT)


def _ada_kernel(cond_ref, w_ref, b_ref, o_ref):
    c = cond_ref[...]
    o_ref[...] = _dot3(c * _sigmoid(c), w_ref[...]) + b_ref[...]


def _ada(cond, w_ada, b_ada):
    rows, d = cond.shape
    n_out = w_ada.shape[1]
    bn = 1024
    return pl.pallas_call(
        _ada_kernel,
        grid=(n_out // bn,),
        in_specs=[pl.BlockSpec((rows, d), lambda i: (0, 0)),
                  pl.BlockSpec((d, bn), lambda i: (0, i)),
                  pl.BlockSpec((1, bn), lambda i: (0, i))],
        out_specs=pl.BlockSpec((rows, bn), lambda i: (0, i)),
        out_shape=jax.ShapeDtypeStruct((rows, n_out), F32),
        compiler_params=_params(("arbitrary",)),
        name="ada",
    )(cond, w_ada, b_ada)


_IN_CUTS = (0, 512, 1024, 1536, 2048, 3072, 4096)


def _head_avg_matrix():
    head = jnp.arange(HEAD_AVG_WIDTH, dtype=I32) // HEAD_DIM
    return jnp.where(head[:, None] == head[None, :], 1.0 / HEAD_DIM, 0.0).astype(BF16)


def _head_rms(x, g, avg, two_term):
    x2 = x * x
    means = []
    for c in range(0, x.shape[1], HEAD_AVG_WIDTH):
        blk = x2[:, c:c + HEAD_AVG_WIDTH]
        hi = blk.astype(BF16)
        ms = _dot(hi, avg)
        if two_term:
            ms = ms + _dot((blk - hi.astype(F32)).astype(BF16), avg)
        means.append(ms)
    return x * lax.rsqrt(jnp.concatenate(means, axis=-1) + EPS) * g


def _pooled(u):
    l = u.shape[0]
    t = lax.broadcasted_iota(I32, (l, POOL_GROUP_DIM), 0)
    groups = []
    for g, w in enumerate(POOL_WINDOWS):
        x = u[:, g * POOL_GROUP_DIM:(g + 1) * POOL_GROUP_DIM]
        acc = x
        for k in range(-(w // 2), w - w // 2):
            if k == 0:
                continue
            shifted = pltpu.roll(x, (-k) % l, 0)
            acc = acc + jnp.where((t + k >= 0) & (t + k < l), shifted, 0.0)
        cnt = (jnp.minimum(t + (w - w // 2), l) - jnp.maximum(t - w // 2, 0)).astype(F32)
        groups.append(acc / cnt - x)
    return jnp.concatenate(groups, axis=-1)


def _in_kernel(x_ref, mod_ref, g_ref, w_ref, qg_ref, kg_ref, avg_ref, q_ref, k_ref, v_ref, u_ref, ga_ref, gb_ref,
               *, kv_channel_major):
    us = []
    for r0 in range(0, x_ref.shape[0], TOKEN_BLOCK):
        rows = slice(r0, r0 + TOKEN_BLOCK)
        y = _rms(x_ref[rows, :], g_ref[...])
        h = (y * (1.0 + mod_ref[0, 1:2, :]) + mod_ref[0, 0:1, :]).astype(BF16)
        part = lambda i: _dot(h, w_ref[:, _IN_CUTS[i]:_IN_CUTS[i + 1]])
        q_ref[rows, :] = (_head_rms(part(0), qg_ref[...], avg_ref[...], False) * HEAD_DIM ** -0.5).astype(q_ref.dtype)
        kn = _head_rms(part(1), kg_ref[...], avg_ref[...], True)
        v = part(2)
        if kv_channel_major:
            k_ref[0] = kn.T.reshape(k_ref.shape[1:])
            v_ref[0] = v.T.reshape(v_ref.shape[1:])
        else:
            k_ref[rows, :] = kn.astype(k_ref.dtype)
            v_ref[rows, :] = v.astype(v_ref.dtype)
        us.append(part(3))
        ga_ref[rows, :] = _sigmoid(part(4)).astype(ga_ref.dtype)
        gb_ref[rows, :] = _sigmoid(part(5)).astype(gb_ref.dtype)
    u_ref[...] = _pooled(jnp.concatenate(us, axis=0)).astype(u_ref.dtype)


def _in_proj(x2d, seq_len, mods, norm_g, w_in_bf16, q_gain, k_gain, cond_of_seq, kv_results):
    n, d = x2d.shape
    tb = seq_len
    assert not kv_results or tb == TOKEN_BLOCK
    widths = [hi - lo for lo, hi in zip(_IN_CUTS[:-1], _IN_CUTS[1:])]
    out_specs = [pl.BlockSpec((tb, w), lambda i: (i, 0)) for w in widths]
    out_shape = [jax.ShapeDtypeStruct((n, w), BF16) for w in widths]
    if kv_results:
        for i in (1, 2):
            out_specs[i] = pl.BlockSpec((1, NA_HEADS, HEAD_DIM, tb), lambda i: (i, 0, 0, 0))
            out_shape[i] = jax.ShapeDtypeStruct((n // tb, NA_HEADS, HEAD_DIM, tb), F32)
    full = lambda a: pl.BlockSpec(a.shape, lambda i: (0,) * a.ndim)
    return pl.pallas_call(
        functools.partial(_in_kernel, kv_channel_major=kv_results),
        grid=(n // tb,),
        in_specs=[pl.BlockSpec((tb, d), lambda i: (i, 0)),
                  pl.BlockSpec((1, N_MOD, d), lambda i: (cond_of_seq(i), 0, 0)),
                  full(norm_g), pl.BlockSpec(w_in_bf16.shape, lambda i: (0, 0), pipeline_mode=pl.Buffered(1)),
                  full(q_gain), full(k_gain),
                  pl.BlockSpec((HEAD_AVG_WIDTH, HEAD_AVG_WIDTH), lambda i: (0, 0))],
        out_specs=out_specs,
        out_shape=out_shape,
        compiler_params=_params(("arbitrary",)),
        name="in_proj",
    )(x2d, mods, norm_g, w_in_bf16, q_gain, k_gain, _head_avg_matrix())


def _with_ones(v):
    lane = lax.broadcasted_iota(I32, v.shape, 1)
    return jnp.concatenate([v.astype(BF16), jnp.where(lane == 0, 1.0, 0.0).astype(BF16)], axis=-1)


def _attn_ctx_kernel(q_ref, kt_ref, vt_ref, att_ref):
    qn = q_ref[0]
    l = kt_ref.shape[3]
    ones_row = jnp.where(lax.broadcasted_iota(I32, (HEAD_DIM, l), 0) == 0, 1.0, 0.0)
    for h in range(NA_HEADS):
        sl = slice(h * HEAD_DIM, (h + 1) * HEAD_DIM)
        s = _dot(qn[:, sl], kt_ref[0, h].astype(BF16))
        e = jnp.exp(s - jnp.max(s, axis=-1, keepdims=True)).astype(BF16)
        vt = jnp.concatenate([vt_ref[0, h], ones_row], axis=0).astype(BF16)
        o = _dot_nt(e, vt)
        att_ref[0, :, sl] = (o[:, :HEAD_DIM] / o[:, HEAD_DIM:HEAD_DIM + 1]).astype(att_ref.dtype)


def _attn_ctx(q, kt, vt):
    b, l, w = q.shape
    blk = pl.BlockSpec((1, l, w), lambda i: (i, 0, 0))
    tblk = pl.BlockSpec((1,) + kt.shape[1:], lambda i: (i, 0, 0, 0))
    return pl.pallas_call(
        _attn_ctx_kernel,
        grid=(b,),
        in_specs=[blk, tblk, tblk],
        out_specs=blk,
        out_shape=jax.ShapeDtypeStruct((b, l, w), BF16),
        compiler_params=_params(("arbitrary",)),
        name="attn_ctx",
    )(q, kt, vt)


def _lat_windows(rows):
    kr = min(WIN_ROWS, rows)
    rb = min(LAT_ROW_BLOCK, rows)
    wr = min(kr + rb, rows)
    assert rows % rb == 0
    starts = [min(max(i * rb - kr // 2, 0), rows - wr) for i in range(rows // rb)]
    for i, ws in enumerate(starts):
        for r in range(i * rb, (i + 1) * rb):
            rs = min(max(r - kr // 2, 0), rows - kr)
            assert ws <= rs and rs + kr <= ws + wr
    return kr, rb, wr, starts


def _attn_lat_kernel(q_ref, k_ref, v_ref, ckt_ref, cvt_ref, bias_ref, att_ref,
                     kn_s, vb_s, ckb_s, cvb_s, *, rows, kr, rb, wr):
    i = pl.program_id(1)

    @pl.when(i == 0)
    def _():
        lc = ckt_ref.shape[3]
        ones_row = jnp.where(lax.broadcasted_iota(I32, (HEAD_DIM, lc), 0) == 0, 1.0, 0.0)
        for h in range(NA_HEADS):
            sl = slice(h * HEAD_DIM, (h + 1) * HEAD_DIM)
            kn_s[h] = k_ref[0, :, sl]
            vb_s[h] = _with_ones(v_ref[0, :, sl])
            ckb_s[h] = ckt_ref[0, h].astype(BF16)
            cvb_s[h] = jnp.concatenate([cvt_ref[0, h], ones_row], axis=0).astype(BF16)

    win_start = jnp.clip(i * rb - kr // 2, 0, rows - wr)
    key_rows = pl.ds(pl.multiple_of(win_start * GRID_W, GRID_W), wr * GRID_W)
    qn_all = q_ref[0]
    for h in range(NA_HEADS):
        sl = slice(h * HEAD_DIM, (h + 1) * HEAD_DIM)
        qn = qn_all[:, sl]
        bias = bias_ref[h, pl.ds(i * rb, rb)].reshape(rb * GRID_W, wr * GRID_W)
        s_loc = _dot_nt(qn, kn_s[h, key_rows, :]) + bias
        s_ctx = _dot(qn, ckb_s[h])
        m = jnp.maximum(jnp.max(s_loc, axis=-1, keepdims=True), jnp.max(s_ctx, axis=-1, keepdims=True))
        o = (_dot(jnp.exp(s_loc - m).astype(BF16), vb_s[h, key_rows, :])
             + _dot_nt(jnp.exp(s_ctx - m).astype(BF16), cvb_s[h]))
        att_ref[0, :, sl] = (o[:, :HEAD_DIM] / o[:, HEAD_DIM:HEAD_DIM + 1]).astype(att_ref.dtype)


def _attn_lat(q, k, v, ckt, cvt, bias):
    b, l, w = q.shape
    lc = ckt.shape[3]
    rows = l // GRID_W
    kr, rb, wr, _ = _lat_windows(rows)
    seq = pl.BlockSpec((1, l, w), lambda i, r: (i, 0, 0))
    ctx = pl.BlockSpec((1, NA_HEADS, HEAD_DIM, lc), lambda i, r: (i, 0, 0, 0))
    rowblk = pl.BlockSpec((1, rb * GRID_W, w), lambda i, r: (i, r, 0))
    return pl.pallas_call(
        functools.partial(_attn_lat_kernel, rows=rows, kr=kr, rb=rb, wr=wr),
        grid=(b, rows // rb),
        in_specs=[rowblk, seq, seq, ctx, ctx,
                  pl.BlockSpec(bias.shape, lambda i, r: (0, 0, 0, 0), pipeline_mode=pl.Buffered(1))],
        out_specs=rowblk,
        out_shape=jax.ShapeDtypeStruct((b, l, w), BF16),
        scratch_shapes=[pltpu.VMEM((NA_HEADS, l, HEAD_DIM), BF16), pltpu.VMEM((NA_HEADS, l, 2 * HEAD_DIM), BF16),
                        pltpu.VMEM((NA_HEADS, HEAD_DIM, lc), BF16), pltpu.VMEM((NA_HEADS, 2 * HEAD_DIM, lc), BF16)],
        compiler_params=_params(("arbitrary", "arbitrary")),
        name="attn_lat",
    )(q, k, v, ckt, cvt, bias)


def _bias_kernel(rpb_ref, o_ref, *, rows):
    kr, rb, wr, starts = _lat_windows(rows)
    h = pl.program_id(0)
    n_ro, n_co = 2 * WIN_ROWS - 1, 2 * WIN_COLS - 1
    wq = lax.broadcasted_iota(I32, (GRID_W, GRID_W), 0)
    wk = lax.broadcasted_iota(I32, (GRID_W, GRID_W), 1)
    co = jnp.clip(wk - wq, -(WIN_COLS - 1), WIN_COLS - 1) + (WIN_COLS - 1)
    col_start = jnp.clip(wq - WIN_COLS // 2, 0, GRID_W - WIN_COLS)
    in_win = (wk >= col_start) & (wk < col_start + WIN_COLS)
    co = jnp.where(in_win, co, -1)
    masked = jnp.full((GRID_W, GRID_W), NEG_INF, F32)
    tables = {}
    for r in range(rows):
        row_start = min(max(r - kr // 2, 0), rows - kr)
        for j in range(wr):
            key_row = starts[r // rb] + j
            blk = masked
            if row_start <= key_row < row_start + kr:
                ro = key_row - r + (WIN_ROWS - 1)
                if ro not in tables:
                    t = masked
                    for c in range(n_co):
                        t = jnp.where(co == c, rpb_ref[(h * n_ro + ro) * n_co + c], t)
                    tables[ro] = t
                blk = tables[ro]
            o_ref[0, r, :, j * GRID_W:(j + 1) * GRID_W] = blk


def _window_bias(rpb, rows):
    _, _, wr, _ = _lat_windows(rows)
    nh = rpb.shape[0]
    return pl.pallas_call(
        functools.partial(_bias_kernel, rows=rows),
        grid=(nh,),
        in_specs=[pl.BlockSpec(memory_space=pltpu.SMEM)],
        out_specs=pl.BlockSpec((1, rows, GRID_W, wr * GRID_W), lambda h: (h, 0, 0, 0)),
        out_shape=jax.ShapeDtypeStruct((nh, rows, GRID_W, wr * GRID_W), F32),
        compiler_params=_params(("arbitrary",)),
        name="window_bias",
    )(rpb.reshape(-1))


def _merge_kernel(xc_ref, xl_ref, attc_ref, attl_ref, plc_ref, pll_ref, gac_ref, gal_ref, gbc_ref, gbl_ref,
                  mod_ref, watt_ref, wpool_ref, ps_ref, wout_ref, g2_ref, wr_ref, x1_ref, h2_ref, aff_ref, *, ctx_blocks):
    is_ctx = pl.program_id(0) < ctx_blocks

    def block(x_ref, att_ref, pl_ref, ga_ref, gb_ref):
        for r0 in range(0, x_ref.shape[0], MERGE_ROWS):
            rows = slice(r0, r0 + MERGE_ROWS)
            o_a = _dot(att_ref[rows, :], watt_ref[...])
            o_b = jnp.concatenate(
                [_dot(pl_ref[rows, g * POOL_GROUP_DIM:(g + 1) * POOL_GROUP_DIM], wpool_ref[g])
                 for g in range(len(POOL_WINDOWS))], axis=-1) * ps_ref[...]
            merged = ga_ref[rows, :] * o_a + gb_ref[rows, :] * o_b
            x1 = x_ref[rows, :] + mod_ref[0, 2:3, :] * _dot(merged.astype(BF16), wout_ref[...])
            x1_ref[rows, :] = x1
            h2 = _rms(x1, g2_ref[...]) * (1.0 + mod_ref[0, 4:5, :]) + mod_ref[0, 3:4, :]
            h2_ref[rows, :] = h2
            logits = _dot3(h2, wr_ref[...])
            e = jnp.exp(logits - jnp.max(logits, axis=-1, keepdims=True))
            aff_ref[rows, :] = e / jnp.sum(e, axis=-1, keepdims=True)

    @pl.when(is_ctx)
    def _():
        block(xc_ref, attc_ref, plc_ref, gac_ref, gbc_ref)

    @pl.when(jnp.logical_not(is_ctx))
    def _():
        block(xl_ref, attl_ref, pll_ref, gal_ref, gbl_ref)


def _merge(ctx, lat, mods, w_att, w_pool, pool_scale, w_out, norm2_g, w_router, lat_blocks_per_seq):
    n_c, d = ctx[0].shape
    n = n_c + lat[0].shape[0]
    tb = MERGE_BLOCK
    cb = n_c // tb
    crow = lambda w: pl.BlockSpec((tb, w), lambda i: (jnp.minimum(i, cb - 1), 0))
    lrow = lambda w: pl.BlockSpec((tb, w), lambda i: (jnp.maximum(i - cb, 0), 0))
    orow = lambda w: pl.BlockSpec((tb, w), lambda i: (i, 0))
    full = lambda a: pl.BlockSpec(a.shape, lambda i: (0,) * a.ndim)
    cond = lambda i: jnp.where(i < cb, 0, 1 + (i - cb) // lat_blocks_per_seq)
    pairs, specs = [], []
    for a_c, a_l in zip(ctx, lat):
        pairs += [a_c, a_l]
        specs += [crow(a_c.shape[1]), lrow(a_l.shape[1])]
    return pl.pallas_call(
        functools.partial(_merge_kernel, ctx_blocks=cb),
        grid=(n // tb,),
        in_specs=specs + [pl.BlockSpec((1, N_MOD, d), lambda i: (cond(i), 0, 0)),
                          full(w_att), full(w_pool), full(pool_scale), full(w_out), full(norm2_g), full(w_router)],
        out_specs=[orow(d), orow(d), orow(N_EXPERTS)],
        out_shape=[jax.ShapeDtypeStruct((n, d), F32), jax.ShapeDtypeStruct((n, d), F32),
                   jax.ShapeDtypeStruct((n, N_EXPERTS), F32)],
        compiler_params=_params(("arbitrary",)),
        name="merge",
    )(*pairs, mods, w_att, w_pool, pool_scale, w_out, norm2_g, w_router)


def _prefix_incl(mask_f32, out_ref):
    e, n = mask_f32.shape
    i = lax.broadcasted_iota(I32, (128, 128), 0)
    j = lax.broadcasted_iota(I32, (128, 128), 1)
    tri = jnp.where(i <= j, 1.0, 0.0).astype(BF16)
    carry = jnp.zeros((e, 1), F32)
    for c in range(n // 128):
        inc = _dot(mask_f32[:, c * 128:(c + 1) * 128].astype(BF16), tri) + carry
        out_ref[:, c * 128:(c + 1) * 128] = inc.astype(I32)
        carry = inc[:, 127:128]


def _route_kernel(aff_ref, sel_ref, rinc_ref, tmp_ref, *, cap):
    aff = aff_ref[...]

    def bit_step(i, thr):
        cand = thr | jnp.left_shift(jnp.int32(1), 30 - i)
        cnt = jnp.sum(jnp.where(aff >= pltpu.bitcast(cand, F32), 1, 0), axis=1, keepdims=True)
        return jnp.where(cnt >= cap, cand, thr)

    thr = lax.fori_loop(0, 31, bit_step, jnp.zeros((aff.shape[0], 1), I32))
    gt = aff >= pltpu.bitcast(thr + 1, F32)
    eq = (aff >= pltpu.bitcast(thr, F32)) & jnp.logical_not(gt)
    need = cap - jnp.sum(jnp.where(gt, 1, 0), axis=1, keepdims=True)
    _prefix_incl(jnp.where(eq, 1.0, 0.0), tmp_ref)
    sel = gt | (eq & (tmp_ref[...] <= need))
    sel_ref[...] = jnp.where(sel, 1, 0)
    _prefix_incl(jnp.where(sel, 1.0, 0.0), rinc_ref)


def _route(aff_t, cap):
    e, n = aff_t.shape
    full = pl.BlockSpec((e, n), lambda: (0, 0))
    return pl.pallas_call(
        functools.partial(_route_kernel, cap=cap),
        in_specs=[full], out_specs=[full, full],
        out_shape=[jax.ShapeDtypeStruct((e, n), I32)] * 2,
        scratch_shapes=[pltpu.VMEM((e, n), I32)],
        compiler_params=pltpu.CompilerParams(vmem_limit_bytes=VMEM_LIMIT),
        name="route",
    )(aff_t)


def _moe_kernel(gidx_ref, qpos_ref, h_ref, gate_ref, wg_ref, wu_ref, wd_ref, z_ref,
                xf_ref, xb_ref, acc_ref, y_ref, sem_g, sem_s, *, m_tot, n_f, step_rows, z_zero_rows):
    e = pl.program_id(0)
    j = pl.program_id(1)
    n_e = pl.num_programs(0)
    m_iss = step_rows * n_f
    tf = wg_ref.shape[1]
    slot = e % 2

    def gather_row(lst, s, dst_slot):
        t = gidx_ref[lst * m_iss + s]
        return pltpu.make_async_copy(h_ref.at[pl.ds(t, 1)], xf_ref.at[dst_slot, pl.ds(s, 1)], sem_g.at[dst_slot])

    def scatter_row(lst, s):
        q = qpos_ref[lst * m_iss + s]
        return pltpu.make_async_copy(y_ref.at[pl.ds(s, 1)], z_ref.at[pl.ds(q, 1)], sem_s)

    def wait_gathers(dst_slot):
        pltpu.make_async_copy(h_ref.at[pl.ds(0, m_iss)], xf_ref.at[dst_slot], sem_g.at[dst_slot]).wait()

    def wait_scatters():
        pltpu.make_async_copy(y_ref, z_ref.at[pl.ds(0, m_iss)], sem_s).wait()

    @pl.when((e == 0) & (j == 0))
    def _():
        y_ref[...] = jnp.zeros_like(y_ref)
        cp = pltpu.make_async_copy(y_ref.at[pl.ds(0, z_zero_rows)], z_ref.at[pl.ds(n_e * m_tot, z_zero_rows)], sem_s)
        cp.start()
        cp.wait()

        def first(s, c):
            gather_row(0, s, 0).start()
            return c

        lax.fori_loop(0, m_iss, first, 0)

    @pl.when(j == 0)
    def _():
        wait_gathers(slot)
        xb_ref[...] = xf_ref[slot, 0:m_tot, :].astype(BF16)
        acc_ref[...] = jnp.zeros_like(acc_ref)

    nxt = lax.rem(e + 1, n_e)
    for u in range(step_rows):
        s = j * step_rows + u
        gather_row(nxt, s, 1 - slot).start()
        scatter_row(e, s).start()

    valid = D_EXPERT - j * tf
    cmask = lax.broadcasted_iota(I32, (1, tf), 1) < valid
    rmask = lax.broadcasted_iota(I32, (tf, 1), 0) < valid
    wg = wg_ref[0].astype(BF16)
    wu = wu_ref[0].astype(BF16)
    wd = jnp.where(rmask, wd_ref[0], 0.0).astype(BF16)
    for m in range(m_tot // EXPERT_M_BLOCK):
        rs = slice(m * EXPERT_M_BLOCK, (m + 1) * EXPERT_M_BLOCK)
        xs = xb_ref[rs, :]
        a = _dot_nt(xs, wg)
        b = _dot_nt(xs, wu)
        hm = jnp.where(cmask, a * _sigmoid(a) * b, 0.0).astype(BF16)
        acc_ref[rs, :] += _dot(hm, wd)

    @pl.when(j == n_f - 1)
    def _():
        wait_scatters()
        y_ref[0:m_tot, :] = _pack_halves(acc_ref[...] * gate_ref[0])

        @pl.when(e == n_e - 1)
        def _():
            def last(s, c):
                scatter_row(n_e, s).start()
                return c

            lax.fori_loop(0, m_iss, last, 0)
            wait_scatters()
            wait_gathers(1 - slot)


def _moe_step_rows(m_tot, n_f):
    return pl.cdiv(pl.cdiv(m_tot, n_f), ROW_ALIGN) * ROW_ALIGN


def _moe(gidx, qpos, h, gate, w_gate, w_up, w_down, m_tot, z_rows, z_zero_rows):
    d = h.shape[1]
    n_e = w_down.shape[0]
    tf = EXPERT_F_BLOCK
    n_f = pl.cdiv(D_EXPERT, tf)
    step_rows = _moe_step_rows(m_tot, n_f)
    m_iss = step_rows * n_f
    weights = pl.BlockSpec((1, tf, d), lambda e, j, *_: (e, j, 0))
    grid_spec = pltpu.PrefetchScalarGridSpec(
        num_scalar_prefetch=2,
        grid=(n_e, n_f),
        in_specs=[pl.BlockSpec(memory_space=pl.ANY),
                  pl.BlockSpec((1, m_tot, 1), lambda e, j, *_: (e, 0, 0)),
                  weights, weights, weights],
        out_specs=pl.BlockSpec(memory_space=pl.ANY),
        scratch_shapes=[pltpu.VMEM((2, m_iss, d), F32), pltpu.VMEM((m_tot, d), BF16), pltpu.VMEM((m_tot, d), F32),
                        pltpu.VMEM((m_iss, d // 2), jnp.uint32), pltpu.SemaphoreType.DMA((2,)),
                        pltpu.SemaphoreType.DMA],
    )
    return pl.pallas_call(
        functools.partial(_moe_kernel, m_tot=m_tot, n_f=n_f, step_rows=step_rows, z_zero_rows=z_zero_rows),
        grid_spec=grid_spec,
        out_shape=jax.ShapeDtypeStruct((z_rows, d // 2), jnp.uint32),
        compiler_params=_params(("arbitrary", "arbitrary")),
        name="moe",
    )(gidx, qpos, h, gate, w_gate, w_up, w_down)


def _combine_kernel(nc_ref, first_ref, start_ref, total_ref, z_ref, x1_ref, mod_ref, qt_ref, kt_ref, o_ref,
                    zbuf, sem, acc_ref):
    blk = pl.program_id(0)
    n_chunks = nc_ref[blk]
    first = first_ref[blk]
    total = total_ref[0]
    ahead = Z_BUFFERS - 1
    qt = qt_ref[...]
    kt = kt_ref[...]
    acc_ref[...] = jnp.zeros_like(acc_ref)

    def z_copy(g):
        rows = pl.ds(pl.multiple_of(start_ref[g], ROW_ALIGN), Z_CHUNK)
        slot = g % Z_BUFFERS
        return pltpu.make_async_copy(z_ref.at[rows], zbuf.at[slot], sem.at[slot])

    @pl.when(blk == 0)
    def _():
        for g in range(ahead):
            @pl.when(g < total)
            def _():
                z_copy(g).start()

    def chunk(c, carry):
        g = first + c
        slot = g % Z_BUFFERS

        @pl.when(g + ahead < total)
        def _():
            z_copy(g + ahead).start()

        z_copy(g).wait()
        r = start_ref[g] + lax.broadcasted_iota(I32, (qt.shape[0], Z_CHUNK), 1)
        onehot = jnp.where((r >= qt) & (r < qt + kt), 1.0, 0.0).astype(BF16)
        left, right = _unpack_halves(zbuf[slot])
        half = left.shape[1]
        acc_ref[:, :half] += _dot(onehot, left)
        acc_ref[:, half:] += _dot(onehot, right)
        return carry

    lax.fori_loop(0, n_chunks, chunk, 0)
    o_ref[...] = x1_ref[...] + mod_ref[0, 5:6, :] * acc_ref[...]


def _combine(chunks, z, x1, first_block, mods, qt, kt, cond_of_block):
    n = qt.shape[0]
    d = x1.shape[1]
    tb = COMBINE_BLOCK
    grid_spec = pltpu.PrefetchScalarGridSpec(
        num_scalar_prefetch=4,
        grid=(n // tb,),
        in_specs=[pl.BlockSpec(memory_space=pl.ANY),
                  pl.BlockSpec((tb, d), lambda i, *_: (i + first_block, 0)),
                  pl.BlockSpec((1, N_MOD, d), lambda i, *_: (cond_of_block(i), 0, 0)),
                  pl.BlockSpec((tb, 1), lambda i, *_: (i, 0)),
                  pl.BlockSpec((tb, 1), lambda i, *_: (i, 0))],
        out_specs=pl.BlockSpec((tb, d), lambda i, *_: (i, 0)),
        scratch_shapes=[pltpu.VMEM((Z_BUFFERS, Z_CHUNK, z.shape[1]), z.dtype), pltpu.SemaphoreType.DMA((Z_BUFFERS,)),
                        pltpu.VMEM((tb, d), F32)],
    )
    return pl.pallas_call(
        _combine_kernel,
        grid_spec=grid_spec,
        out_shape=jax.ShapeDtypeStruct((n, d), F32),
        compiler_params=_params(("arbitrary",)),
        name="combine",
    )(*chunks, z, x1, mods, qt, kt)


def _routing_tables(aff, cap, z_off):
    n = aff.shape[0]
    aff_t = aff.T
    sel, rinc = _route(aff_t, cap)
    slots = jnp.arange(cap, dtype=I32)
    chunk = 128
    rinc3 = rinc.reshape(rinc.shape[0], n // chunk, chunk)
    n_full = jnp.sum((rinc3[:, None, :, chunk - 1] <= slots[None, :, None]).astype(I32), axis=-1)
    pick = (n_full[:, :, None] == jnp.arange(n // chunk, dtype=I32)[None, None, :]).astype(F32)
    inside = jnp.einsum("eca,eab->ecb", pick, rinc3.astype(F32), precision=lax.Precision.HIGHEST)
    idx = n_full * chunk + jnp.sum((inside <= slots[None, :, None].astype(F32)).astype(I32), axis=-1)
    gate = jnp.take_along_axis(aff_t, idx, axis=1)
    k_tok = jnp.sum(sel, axis=0)
    q_tok = jnp.cumsum(k_tok) - k_tok + z_off
    before = jnp.cumsum(sel, axis=0) - sel
    qpos = jnp.take_along_axis(q_tok[None, :] + before, idx, axis=1)
    nblk = n // COMBINE_BLOCK
    q_blk = q_tok[::COMBINE_BLOCK]
    end_blk = q_blk + jnp.sum(k_tok.reshape(nblk, COMBINE_BLOCK), axis=1)
    w0 = (q_blk // ROW_ALIGN) * ROW_ALIGN
    nc = jnp.maximum((end_blk - w0 + Z_CHUNK - 1) // Z_CHUNK, 1).astype(I32)
    first = jnp.cumsum(nc) - nc
    g = jnp.arange(nblk + (N_EXPERTS * cap) // Z_CHUNK + nblk, dtype=I32)
    blk_of = jnp.sum((first[None, 1:] <= g[:, None]).astype(I32), axis=1)
    start = jnp.where(g < jnp.sum(nc), w0[blk_of] + (g - first[blk_of]) * Z_CHUNK, 0)
    chunks = (nc, first.astype(I32), start.astype(I32), jnp.sum(nc).astype(I32)[None])
    return idx, gate, qpos.astype(I32), chunks, q_tok.astype(I32)[:, None], k_tok.astype(I32)[:, None]


def kernel(x_prompt, x_sample, cache_k, cache_v, c, c_ctx, norm1_g, norm2_g, w_ada, b_ada, w_in, q_norm_g, k_norm_g,
           rpb, w_att_proj, w_pool, pool_scale, w_out, w_router, w_gate_e, w_up_e, w_down_e):
    assert w_ada.shape[0] == 1, "single trunk layer"
    nb, ls, d = x_prompt.shape
    db, ll, _ = x_sample.shape
    n_c, n_l = nb * ls, db * ll
    assert ls == TOKEN_BLOCK, "context keys / values are written one sequence per projection block"

    cond = jnp.concatenate([c_ctx[None, :], c, jnp.zeros((16 - 1 - db, d), F32)], axis=0)
    mods = _ada(cond, w_ada[0], b_ada).reshape(16, N_MOD, d)

    w_in_b = w_in[0].astype(BF16)
    w_att_b = w_att_proj[0].astype(BF16)
    w_pool_b = w_pool[0].astype(BF16)
    w_out_b = w_out[0].astype(BF16)


    xc = x_prompt.reshape(n_c, d)
    xl = x_sample.reshape(n_l, d)
    q_gain = jnp.tile(q_norm_g, (1, NA_HEADS))
    k_gain = jnp.tile(k_norm_g, (1, NA_HEADS))
    qc, kc, vc, pool_c, gac, gbc = _in_proj(xc, ls, mods, norm1_g, w_in_b, q_gain, k_gain, lambda i: 0, True)
    ql, kl, vl, pool_l, gal, gbl = _in_proj(xl, ll, mods, norm1_g, w_in_b, q_gain, k_gain, lambda i: 1 + i, False)

    seq_c = lambda a: a.reshape(nb, ls, a.shape[-1])
    seq_l = lambda a: a.reshape(db, ll, a.shape[-1])
    att_c = _attn_ctx(seq_c(qc), kc, vc)
    bias = _window_bias(rpb[0], ll // GRID_W)
    att_l = _attn_lat(seq_l(ql), seq_l(kl), seq_l(vl), jnp.transpose(cache_k[:, 0], (0, 2, 3, 1)),
                      jnp.transpose(cache_v[:, 0], (0, 2, 3, 1)), bias)

    x1, h2, aff = _merge((xc, att_c.reshape(n_c, NA_WIDTH), pool_c, gac, gbc),
                         (xl, att_l.reshape(n_l, NA_WIDTH), pool_l, gal, gbl),
                         mods, w_att_b, w_pool_b, pool_scale, w_out_b, norm2_g, w_router[0], ll // MERGE_BLOCK)

    cap_c = max(1, (CAPACITY_FACTOR * n_c) // N_EXPERTS)
    cap_l = max(1, (CAPACITY_FACTOR * n_l) // N_EXPERTS)
    m_tot = cap_c + cap_l
    m_iss = _moe_step_rows(m_tot, pl.cdiv(D_EXPERT, EXPERT_F_BLOCK)) * pl.cdiv(D_EXPERT, EXPERT_F_BLOCK)
    z_valid = N_EXPERTS * m_tot
    z_zero_rows = Z_CHUNK + ROW_ALIGN
    z_spare = z_valid + z_zero_rows
    z_rows = z_spare + m_iss
    idx_c, gate_c, qpos_c, chunks_c, qt_c, kt_c = _routing_tables(aff[:n_c], cap_c, 0)
    idx_l, gate_l, qpos_l, chunks_l, qt_l, kt_l = _routing_tables(aff[n_c:], cap_l, N_EXPERTS * cap_c)
    spare = jnp.broadcast_to(z_spare + jnp.arange(m_iss, dtype=I32), (N_EXPERTS + 1, m_iss))
    gidx = jnp.concatenate([idx_c, idx_l + n_c, jnp.zeros((N_EXPERTS, m_iss - m_tot), I32)], axis=1).reshape(-1)
    qpos = jnp.concatenate([spare[:1], jnp.concatenate([qpos_c, qpos_l, spare[1:, m_tot:]], axis=1)], axis=0).reshape(-1)
    gate = jnp.concatenate([gate_c, gate_l], axis=1)[:, :, None]

    z = _moe(gidx, qpos, h2, gate, jnp.swapaxes(w_gate_e[0], 1, 2), jnp.swapaxes(w_up_e[0], 1, 2), w_down_e[0],
             m_tot, z_rows, z_zero_rows)

    y_c = _combine(chunks_c, z, x1, 0, mods, qt_c, kt_c, lambda i: 0)
    y_l = _combine(chunks_l, z, x1, n_c // COMBINE_BLOCK, mods, qt_l, kt_l, lambda i: 1 + i // (ll // COMBINE_BLOCK))

    state_k = jnp.transpose(kc, (0, 3, 1, 2))[:, None]
    state_v = jnp.transpose(vc, (0, 3, 1, 2))[:, None]
    return (y_c.reshape(nb, ls, d), y_l.reshape(db, ll, d), state_k, state_v)
```

```python
import functools

import jax
import jax.numpy as jnp
from jax import lax
from jax.experimental import pallas as pl
from jax.experimental.pallas import tpu as pltpu

F32 = jnp.float32
BF16 = jnp.bfloat16
I32 = jnp.int32

GRID_W = 64
NA_HEADS = 8
HEAD_DIM = 64
NA_WIDTH = NA_HEADS * HEAD_DIM
WIN_ROWS = 8
WIN_COLS = 16
POOL_WINDOWS = (2, 4, 8, 16)
POOL_GROUP_DIM = 128
POOL_WIDTH = len(POOL_WINDOWS) * POOL_GROUP_DIM
N_EXPERTS = 16
CAPACITY_FACTOR = 2
D_EXPERT = 2752
N_MOD = 6
EPS = 1e-6
NEG_INF = -1e30

TOKEN_BLOCK = 256
COMBINE_BLOCK = 256
MERGE_BLOCK = 512
MERGE_ROWS = 256
LAT_ROW_BLOCK = 4
EXPERT_F_BLOCK = 256
EXPERT_M_BLOCK = 768
Z_CHUNK = 256
Z_BUFFERS = 4
HEAD_AVG_WIDTH = 256
ROW_ALIGN = 8
VMEM_LIMIT = 56 * 1024 * 1024


def _dot(a, b):
    return jnp.dot(a, b, preferred_element_type=F32)


def _dot_nt(a, b):
    return lax.dot_general(a, b, (((1,), (1,)), ((), ())), preferred_element_type=F32)


def _dot3(a, b):
    a_hi = a.astype(BF16)
    a_lo = (a - a_hi.astype(F32)).astype(BF16)
    b_hi = b.astype(BF16)
    b_lo = (b - b_hi.astype(F32)).astype(BF16)
    return _dot(a_hi, b_hi) + _dot(a_hi, b_lo) + _dot(a_lo, b_hi)


def _sigmoid(x):
    return 0.5 * jnp.tanh(0.5 * x) + 0.5


def _rms(x, g):
    return x * lax.rsqrt(jnp.mean(x * x, axis=-1, keepdims=True) + EPS) * g


def _pack_halves(x):
    w = x.shape[1] // 2
    return pltpu.pack_elementwise([x[:, :w], x[:, w:]], packed_dtype=BF16)


def _unpack_halves(p):
    return tuple(pltpu.unpack_elementwise(p, index=i, packed_dtype=BF16, unpacked_dtype=F32).astype(BF16)
                 for i in range(2))


def _params(sem):
    return pltpu.CompilerParams(dimension_semantics=sem, vmem_limit_bytes=VMEM_LIMIT)


def _ada_kernel(cond_ref, w_ref, b_ref, o_ref):
    c = cond_ref[...]
    o_ref[...] = _dot3(c * _sigmoid(c), w_ref[...]) + b_ref[...]


def _ada(cond, w_ada, b_ada):
    rows, d = cond.shape
    n_out = w_ada.shape[1]
    bn = 1024
    return pl.pallas_call(
        _ada_kernel,
        grid=(n_out // bn,),
        in_specs=[pl.BlockSpec((rows, d), lambda i: (0, 0)),
                  pl.BlockSpec((d, bn), lambda i: (0, i)),
                  pl.BlockSpec((1, bn), lambda i: (0, i))],
        out_specs=pl.BlockSpec((rows, bn), lambda i: (0, i)),
        out_shape=jax.ShapeDtypeStruct((rows, n_out), F32),
        compiler_params=_params(("arbitrary",)),
        name="ada",
    )(cond, w_ada, b_ada)


_IN_CUTS = (0, 512, 1024, 1536, 2048, 3072, 4096)


def _head_avg_matrix():
    head = jnp.arange(HEAD_AVG_WIDTH, dtype=I32) // HEAD_DIM
    return jnp.where(head[:, None] == head[None, :], 1.0 / HEAD_DIM, 0.0).astype(BF16)


def _head_rms(x, g, avg, two_term):
    x2 = x * x
    means = []
    for c in range(0, x.shape[1], HEAD_AVG_WIDTH):
        blk = x2[:, c:c + HEAD_AVG_WIDTH]
        hi = blk.astype(BF16)
        ms = _dot(hi, avg)
        if two_term:
            ms = ms + _dot((blk - hi.astype(F32)).astype(BF16), avg)
        means.append(ms)
    return x * lax.rsqrt(jnp.concatenate(means, axis=-1) + EPS) * g


def _pooled(u):
    l = u.shape[0]
    t = lax.broadcasted_iota(I32, (l, POOL_GROUP_DIM), 0)
    groups = []
    for g, w in enumerate(POOL_WINDOWS):
        x = u[:, g * POOL_GROUP_DIM:(g + 1) * POOL_GROUP_DIM]
        acc = x
        for k in range(-(w // 2), w - w // 2):
            if k == 0:
                continue
            shifted = pltpu.roll(x, (-k) % l, 0)
            acc = acc + jnp.where((t + k >= 0) & (t + k < l), shifted, 0.0)
        cnt = (jnp.minimum(t + (w - w // 2), l) - jnp.maximum(t - w // 2, 0)).astype(F32)
        groups.append(acc / cnt - x)
    return jnp.concatenate(groups, axis=-1)


def _with_ones(v):
    lane = lax.broadcasted_iota(I32, v.shape, 1)
    return jnp.concatenate([v.astype(BF16), jnp.where(lane == 0, 1.0, 0.0).astype(BF16)], axis=-1)


def _in_kernel(x_ref, mod_ref, g_ref, w_ref, qg_ref, kg_ref, avg_ref, q_ref, k_ref, v_ref, u_ref, ga_ref, gb_ref,
               *, kv_channel_major):
    us = []
    for r0 in range(0, x_ref.shape[0], TOKEN_BLOCK):
        rows = slice(r0, r0 + TOKEN_BLOCK)
        y = _rms(x_ref[rows, :], g_ref[...])
        h = (y * (1.0 + mod_ref[0, 1:2, :]) + mod_ref[0, 0:1, :]).astype(BF16)
        part = lambda i: _dot(h, w_ref[:, _IN_CUTS[i]:_IN_CUTS[i + 1]])
        qn = (_head_rms(part(0), qg_ref[...], avg_ref[...], False) * HEAD_DIM ** -0.5).astype(BF16)
        kn = _head_rms(part(1), kg_ref[...], avg_ref[...], True)
        v = part(2)
        if kv_channel_major:
            k_ref[0] = kn.T.reshape(k_ref.shape[1:])
            v_ref[0] = v.T.reshape(v_ref.shape[1:])
            knb = kn.astype(BF16)
            for hd in range(NA_HEADS):
                sl = slice(hd * HEAD_DIM, (hd + 1) * HEAD_DIM)
                s = _dot_nt(qn[:, sl], knb[:, sl])
                e = jnp.exp(s - jnp.max(s, axis=-1, keepdims=True)).astype(BF16)
                o = _dot(e, _with_ones(v[:, sl]))
                q_ref[rows, sl] = (o[:, :HEAD_DIM] / o[:, HEAD_DIM:HEAD_DIM + 1]).astype(q_ref.dtype)
        else:
            q_ref[rows, :] = qn
            k_ref[rows, :] = kn.astype(k_ref.dtype)
            v_ref[rows, :] = v.astype(v_ref.dtype)
        us.append(part(3))
        ga_ref[rows, :] = _sigmoid(part(4)).astype(ga_ref.dtype)
        gb_ref[rows, :] = _sigmoid(part(5)).astype(gb_ref.dtype)
    u_ref[...] = _pooled(jnp.concatenate(us, axis=0)).astype(u_ref.dtype)


def _in_proj(x2d, seq_len, mods, norm_g, w_in_bf16, q_gain, k_gain, cond_of_seq, kv_results):
    n, d = x2d.shape
    tb = seq_len
    assert not kv_results or tb == TOKEN_BLOCK
    widths = [hi - lo for lo, hi in zip(_IN_CUTS[:-1], _IN_CUTS[1:])]
    out_specs = [pl.BlockSpec((tb, w), lambda i: (i, 0)) for w in widths]
    out_shape = [jax.ShapeDtypeStruct((n, w), BF16) for w in widths]
    if kv_results:
        for i in (1, 2):
            out_specs[i] = pl.BlockSpec((1, NA_HEADS, HEAD_DIM, tb), lambda i: (i, 0, 0, 0))
            out_shape[i] = jax.ShapeDtypeStruct((n // tb, NA_HEADS, HEAD_DIM, tb), F32)
    full = lambda a: pl.BlockSpec(a.shape, lambda i: (0,) * a.ndim)
    return pl.pallas_call(
        functools.partial(_in_kernel, kv_channel_major=kv_results),
        grid=(n // tb,),
        in_specs=[pl.BlockSpec((tb, d), lambda i: (i, 0)),
                  pl.BlockSpec((1, N_MOD, d), lambda i: (cond_of_seq(i), 0, 0)),
                  full(norm_g), pl.BlockSpec(w_in_bf16.shape, lambda i: (0, 0), pipeline_mode=pl.Buffered(1)),
                  full(q_gain), full(k_gain),
                  pl.BlockSpec((HEAD_AVG_WIDTH, HEAD_AVG_WIDTH), lambda i: (0, 0))],
        out_specs=out_specs,
        out_shape=out_shape,
        compiler_params=_params(("arbitrary",)),
        name="in_proj",
    )(x2d, mods, norm_g, w_in_bf16, q_gain, k_gain, _head_avg_matrix())


def _lat_windows(rows):
    kr = min(WIN_ROWS, rows)
    rb = min(LAT_ROW_BLOCK, rows)
    wr = min(kr + rb, rows)
    assert rows % rb == 0
    starts = [min(max(i * rb - kr // 2, 0), rows - wr) for i in range(rows // rb)]
    for i, ws in enumerate(starts):
        for r in range(i * rb, (i + 1) * rb):
            rs = min(max(r - kr // 2, 0), rows - kr)
            assert ws <= rs and rs + kr <= ws + wr
    return kr, rb, wr, starts


def _attn_lat_kernel(q_ref, k_ref, v_ref, ckt_ref, cvt_ref, bias_ref, att_ref,
                     kn_s, vb_s, ckb_s, cvb_s, *, rows, kr, rb, wr):
    i = pl.program_id(1)

    @pl.when(i == 0)
    def _():
        lc = ckt_ref.shape[3]
        ones_row = jnp.where(lax.broadcasted_iota(I32, (HEAD_DIM, lc), 0) == 0, 1.0, 0.0)
        for h in range(NA_HEADS):
            sl = slice(h * HEAD_DIM, (h + 1) * HEAD_DIM)
            kn_s[h] = k_ref[0, :, sl]
            vb_s[h] = _with_ones(v_ref[0, :, sl])
            ckb_s[h] = ckt_ref[0, h].astype(BF16)
            cvb_s[h] = jnp.concatenate([cvt_ref[0, h], ones_row], axis=0).astype(BF16)

    win_start = jnp.clip(i * rb - kr // 2, 0, rows - wr)
    key_rows = pl.ds(pl.multiple_of(win_start * GRID_W, GRID_W), wr * GRID_W)
    qn_all = q_ref[0]
    for h in range(NA_HEADS):
        sl = slice(h * HEAD_DIM, (h + 1) * HEAD_DIM)
        qn = qn_all[:, sl]
        bias = bias_ref[h, pl.ds(i * rb, rb)].reshape(rb * GRID_W, wr * GRID_W)
        s_loc = _dot_nt(qn, kn_s[h, key_rows, :]) + bias
        s_ctx = _dot(qn, ckb_s[h])
        m = jnp.maximum(jnp.max(s_loc, axis=-1, keepdims=True), jnp.max(s_ctx, axis=-1, keepdims=True))
        o = (_dot(jnp.exp(s_loc - m).astype(BF16), vb_s[h, key_rows, :])
             + _dot_nt(jnp.exp(s_ctx - m).astype(BF16), cvb_s[h]))
        att_ref[0, :, sl] = (o[:, :HEAD_DIM] / o[:, HEAD_DIM:HEAD_DIM + 1]).astype(att_ref.dtype)


def _attn_lat(q, k, v, ckt, cvt, bias):
    b, l, w = q.shape
    lc = ckt.shape[3]
    rows = l // GRID_W
    kr, rb, wr, _ = _lat_windows(rows)
    seq = pl.BlockSpec((1, l, w), lambda i, r: (i, 0, 0))
    ctx = pl.BlockSpec((1, NA_HEADS, HEAD_DIM, lc), lambda i, r: (i, 0, 0, 0))
    rowblk = pl.BlockSpec((1, rb * GRID_W, w), lambda i, r: (i, r, 0))
    return pl.pallas_call(
        functools.partial(_attn_lat_kernel, rows=rows, kr=kr, rb=rb, wr=wr),
        grid=(b, rows // rb),
        in_specs=[rowblk, seq, seq, ctx, ctx,
                  pl.BlockSpec(bias.shape, lambda i, r: (0, 0, 0, 0), pipeline_mode=pl.Buffered(1))],
        out_specs=rowblk,
        out_shape=jax.ShapeDtypeStruct((b, l, w), BF16),
        scratch_shapes=[pltpu.VMEM((NA_HEADS, l, HEAD_DIM), BF16), pltpu.VMEM((NA_HEADS, l, 2 * HEAD_DIM), BF16),
                        pltpu.VMEM((NA_HEADS, HEAD_DIM, lc), BF16), pltpu.VMEM((NA_HEADS, 2 * HEAD_DIM, lc), BF16)],
        compiler_params=_params(("arbitrary", "arbitrary")),
        name="attn_lat",
    )(q, k, v, ckt, cvt, bias)


def _bias_kernel(rpb_ref, o_ref, *, rows):
    kr, rb, wr, starts = _lat_windows(rows)
    h = pl.program_id(0)
    n_ro, n_co = 2 * WIN_ROWS - 1, 2 * WIN_COLS - 1
    wq = lax.broadcasted_iota(I32, (GRID_W, GRID_W), 0)
    wk = lax.broadcasted_iota(I32, (GRID_W, GRID_W), 1)
    co = jnp.clip(wk - wq, -(WIN_COLS - 1), WIN_COLS - 1) + (WIN_COLS - 1)
    col_start = jnp.clip(wq - WIN_COLS // 2, 0, GRID_W - WIN_COLS)
    in_win = (wk >= col_start) & (wk < col_start + WIN_COLS)
    co = jnp.where(in_win, co, -1)
    masked = jnp.full((GRID_W, GRID_W), NEG_INF, F32)
    tables = {}
    for r in range(rows):
        row_start = min(max(r - kr // 2, 0), rows - kr)
        for j in range(wr):
            key_row = starts[r // rb] + j
            blk = masked
            if row_start <= key_row < row_start + kr:
                ro = key_row - r + (WIN_ROWS - 1)
                if ro not in tables:
                    t = masked
                    for c in range(n_co):
                        t = jnp.where(co == c, rpb_ref[(h * n_ro + ro) * n_co + c], t)
                    tables[ro] = t
                blk = tables[ro]
            o_ref[0, r, :, j * GRID_W:(j + 1) * GRID_W] = blk


def _window_bias(rpb, rows):
    _, _, wr, _ = _lat_windows(rows)
    nh = rpb.shape[0]
    return pl.pallas_call(
        functools.partial(_bias_kernel, rows=rows),
        grid=(nh,),
        in_specs=[pl.BlockSpec(memory_space=pltpu.SMEM)],
        out_specs=pl.BlockSpec((1, rows, GRID_W, wr * GRID_W), lambda h: (h, 0, 0, 0)),
        out_shape=jax.ShapeDtypeStruct((nh, rows, GRID_W, wr * GRID_W), F32),
        compiler_params=_params(("arbitrary",)),
        name="window_bias",
    )(rpb.reshape(-1))


def _merge_kernel(xc_ref, xl_ref, attc_ref, attl_ref, plc_ref, pll_ref, gac_ref, gal_ref, gbc_ref, gbl_ref,
                  mod_ref, watt_ref, wpool_ref, ps_ref, wout_ref, g2_ref, wr_ref, x1_ref, h2_ref, aff_ref, *, ctx_blocks):
    is_ctx = pl.program_id(0) < ctx_blocks

    def block(x_ref, att_ref, pl_ref, ga_ref, gb_ref):
        for r0 in range(0, x_ref.shape[0], MERGE_ROWS):
            rows = slice(r0, r0 + MERGE_ROWS)
            o_a = _dot(att_ref[rows, :], watt_ref[...])
            o_b = jnp.concatenate(
                [_dot(pl_ref[rows, g * POOL_GROUP_DIM:(g + 1) * POOL_GROUP_DIM], wpool_ref[g])
                 for g in range(len(POOL_WINDOWS))], axis=-1) * ps_ref[...]
            merged = ga_ref[rows, :] * o_a + gb_ref[rows, :] * o_b
            x1 = x_ref[rows, :] + mod_ref[0, 2:3, :] * _dot(merged.astype(BF16), wout_ref[...])
            x1_ref[rows, :] = x1
            h2 = _rms(x1, g2_ref[...]) * (1.0 + mod_ref[0, 4:5, :]) + mod_ref[0, 3:4, :]
            h2_ref[rows, :] = h2
            logits = _dot3(h2, wr_ref[...])
            e = jnp.exp(logits - jnp.max(logits, axis=-1, keepdims=True))
            aff_ref[rows, :] = e / jnp.sum(e, axis=-1, keepdims=True)

    @pl.when(is_ctx)
    def _():
        block(xc_ref, attc_ref, plc_ref, gac_ref, gbc_ref)

    @pl.when(jnp.logical_not(is_ctx))
    def _():
        block(xl_ref, attl_ref, pll_ref, gal_ref, gbl_ref)


def _merge(ctx, lat, mods, w_att, w_pool, pool_scale, w_out, norm2_g, w_router, lat_blocks_per_seq):
    n_c, d = ctx[0].shape
    n = n_c + lat[0].shape[0]
    tb = MERGE_BLOCK
    cb = n_c // tb
    crow = lambda w: pl.BlockSpec((tb, w), lambda i: (jnp.minimum(i, cb - 1), 0))
    lrow = lambda w: pl.BlockSpec((tb, w), lambda i: (jnp.maximum(i - cb, 0), 0))
    orow = lambda w: pl.BlockSpec((tb, w), lambda i: (i, 0))
    full = lambda a: pl.BlockSpec(a.shape, lambda i: (0,) * a.ndim)
    cond = lambda i: jnp.where(i < cb, 0, 1 + (i - cb) // lat_blocks_per_seq)
    pairs, specs = [], []
    for a_c, a_l in zip(ctx, lat):
        pairs += [a_c, a_l]
        specs += [crow(a_c.shape[1]), lrow(a_l.shape[1])]
    return pl.pallas_call(
        functools.partial(_merge_kernel, ctx_blocks=cb),
        grid=(n // tb,),
        in_specs=specs + [pl.BlockSpec((1, N_MOD, d), lambda i: (cond(i), 0, 0)),
                          full(w_att), full(w_pool), full(pool_scale), full(w_out), full(norm2_g), full(w_router)],
        out_specs=[orow(d), orow(d), orow(N_EXPERTS)],
        out_shape=[jax.ShapeDtypeStruct((n, d), F32), jax.ShapeDtypeStruct((n, d), F32),
                   jax.ShapeDtypeStruct((n, N_EXPERTS), F32)],
        compiler_params=_params(("arbitrary",)),
        name="merge",
    )(*pairs, mods, w_att, w_pool, pool_scale, w_out, norm2_g, w_router)


def _prefix_incl(mask_f32, out_ref):
    e, n = mask_f32.shape
    i = lax.broadcasted_iota(I32, (128, 128), 0)
    j = lax.broadcasted_iota(I32, (128, 128), 1)
    tri = jnp.where(i <= j, 1.0, 0.0).astype(BF16)
    carry = jnp.zeros((e, 1), F32)
    for c in range(n // 128):
        inc = _dot(mask_f32[:, c * 128:(c + 1) * 128].astype(BF16), tri) + carry
        out_ref[:, c * 128:(c + 1) * 128] = inc.astype(I32)
        carry = inc[:, 127:128]


def _route_kernel(aff_ref, sel_ref, rinc_ref, tmp_ref, *, cap):
    aff = aff_ref[...]

    def bit_step(i, thr):
        cand = thr | jnp.left_shift(jnp.int32(1), 30 - i)
        cnt = jnp.sum(jnp.where(aff >= pltpu.bitcast(cand, F32), 1, 0), axis=1, keepdims=True)
        return jnp.where(cnt >= cap, cand, thr)

    thr = lax.fori_loop(0, 31, bit_step, jnp.zeros((aff.shape[0], 1), I32))
    gt = aff >= pltpu.bitcast(thr + 1, F32)
    eq = (aff >= pltpu.bitcast(thr, F32)) & jnp.logical_not(gt)
    need = cap - jnp.sum(jnp.where(gt, 1, 0), axis=1, keepdims=True)
    _prefix_incl(jnp.where(eq, 1.0, 0.0), tmp_ref)
    sel = gt | (eq & (tmp_ref[...] <= need))
    sel_ref[...] = jnp.where(sel, 1, 0)
    _prefix_incl(jnp.where(sel, 1.0, 0.0), rinc_ref)


def _route(aff_t, cap):
    e, n = aff_t.shape
    full = pl.BlockSpec((e, n), lambda: (0, 0))
    return pl.pallas_call(
        functools.partial(_route_kernel, cap=cap),
        in_specs=[full], out_specs=[full, full],
        out_shape=[jax.ShapeDtypeStruct((e, n), I32)] * 2,
        scratch_shapes=[pltpu.VMEM((e, n), I32)],
        compiler_params=pltpu.CompilerParams(vmem_limit_bytes=VMEM_LIMIT),
        name="route",
    )(aff_t)


def _moe_kernel(gidx_ref, qpos_ref, h_ref, gate_ref, wg_ref, wu_ref, wd_ref, z_ref,
                xf_ref, xb_ref, acc_ref, y_ref, sem_g, sem_s, *, m_tot, n_f, step_rows, z_zero_rows):
    e = pl.program_id(0)
    j = pl.program_id(1)
    n_e = pl.num_programs(0)
    m_iss = step_rows * n_f
    tf = wg_ref.shape[1]
    slot = e % 2

    def gather_row(lst, s, dst_slot):
        t = gidx_ref[lst * m_iss + s]
        return pltpu.make_async_copy(h_ref.at[pl.ds(t, 1)], xf_ref.at[dst_slot, pl.ds(s, 1)], sem_g.at[dst_slot])

    def scatter_row(lst, s):
        q = qpos_ref[lst * m_iss + s]
        return pltpu.make_async_copy(y_ref.at[pl.ds(s, 1)], z_ref.at[pl.ds(q, 1)], sem_s)

    def wait_gathers(dst_slot):
        pltpu.make_async_copy(h_ref.at[pl.ds(0, m_iss)], xf_ref.at[dst_slot], sem_g.at[dst_slot]).wait()

    def wait_scatters():
        pltpu.make_async_copy(y_ref, z_ref.at[pl.ds(0, m_iss)], sem_s).wait()

    @pl.when((e == 0) & (j == 0))
    def _():
        y_ref[...] = jnp.zeros_like(y_ref)
        cp = pltpu.make_async_copy(y_ref.at[pl.ds(0, z_zero_rows)], z_ref.at[pl.ds(n_e * m_tot, z_zero_rows)], sem_s)
        cp.start()
        cp.wait()

        def first(s, c):
            gather_row(0, s, 0).start()
            return c

        lax.fori_loop(0, m_iss, first, 0)

    @pl.when(j == 0)
    def _():
        wait_gathers(slot)
        xb_ref[...] = xf_ref[slot, 0:m_tot, :].astype(BF16)
        acc_ref[...] = jnp.zeros_like(acc_ref)

    nxt = lax.rem(e + 1, n_e)
    for u in range(step_rows):
        s = j * step_rows + u
        gather_row(nxt, s, 1 - slot).start()
        scatter_row(e, s).start()

    valid = D_EXPERT - j * tf
    cmask = lax.broadcasted_iota(I32, (1, tf), 1) < valid
    rmask = lax.broadcasted_iota(I32, (tf, 1), 0) < valid
    wg = wg_ref[0].astype(BF16)
    wu = wu_ref[0].astype(BF16)
    wd = jnp.where(rmask, wd_ref[0], 0.0).astype(BF16)
    for m in range(m_tot // EXPERT_M_BLOCK):
        rs = slice(m * EXPERT_M_BLOCK, (m + 1) * EXPERT_M_BLOCK)
        xs = xb_ref[rs, :]
        a = _dot_nt(xs, wg)
        b = _dot_nt(xs, wu)
        hm = jnp.where(cmask, a * _sigmoid(a) * b, 0.0).astype(BF16)
        acc_ref[rs, :] += _dot(hm, wd)

    @pl.when(j == n_f - 1)
    def _():
        wait_scatters()
        y_ref[0:m_tot, :] = _pack_halves(acc_ref[...] * gate_ref[0])

        @pl.when(e == n_e - 1)
        def _():
            def last(s, c):
                scatter_row(n_e, s).start()
                return c

            lax.fori_loop(0, m_iss, last, 0)
            wait_scatters()
            wait_gathers(1 - slot)


def _moe_step_rows(m_tot, n_f):
    return pl.cdiv(pl.cdiv(m_tot, n_f), ROW_ALIGN) * ROW_ALIGN


def _moe(gidx, qpos, h, gate, w_gate, w_up, w_down, m_tot, z_rows, z_zero_rows):
    d = h.shape[1]
    n_e = w_down.shape[0]
    tf = EXPERT_F_BLOCK
    n_f = pl.cdiv(D_EXPERT, tf)
    step_rows = _moe_step_rows(m_tot, n_f)
    m_iss = step_rows * n_f
    weights = pl.BlockSpec((1, tf, d), lambda e, j, *_: (e, j, 0))
    grid_spec = pltpu.PrefetchScalarGridSpec(
        num_scalar_prefetch=2,
        grid=(n_e, n_f),
        in_specs=[pl.BlockSpec(memory_space=pl.ANY),
                  pl.BlockSpec((1, m_tot, 1), lambda e, j, *_: (e, 0, 0)),
                  weights, weights, weights],
        out_specs=pl.BlockSpec(memory_space=pl.ANY),
        scratch_shapes=[pltpu.VMEM((2, m_iss, d), F32), pltpu.VMEM((m_tot, d), BF16), pltpu.VMEM((m_tot, d), F32),
                        pltpu.VMEM((m_iss, d // 2), jnp.uint32), pltpu.SemaphoreType.DMA((2,)),
                        pltpu.SemaphoreType.DMA],
    )
    return pl.pallas_call(
        functools.partial(_moe_kernel, m_tot=m_tot, n_f=n_f, step_rows=step_rows, z_zero_rows=z_zero_rows),
        grid_spec=grid_spec,
        out_shape=jax.ShapeDtypeStruct((z_rows, d // 2), jnp.uint32),
        compiler_params=_params(("arbitrary", "arbitrary")),
        name="moe",
    )(gidx, qpos, h, gate, w_gate, w_up, w_down)


def _combine_kernel(nc_ref, first_ref, start_ref, total_ref, z_ref, x1_ref, mod_ref, qt_ref, kt_ref, o_ref,
                    zbuf, sem, acc_ref):
    blk = pl.program_id(0)
    n_chunks = nc_ref[blk]
    first = first_ref[blk]
    total = total_ref[0]
    ahead = Z_BUFFERS - 1
    qt = qt_ref[...]
    kt = kt_ref[...]
    acc_ref[...] = jnp.zeros_like(acc_ref)

    def z_copy(g):
        rows = pl.ds(pl.multiple_of(start_ref[g], ROW_ALIGN), Z_CHUNK)
        slot = g % Z_BUFFERS
        return pltpu.make_async_copy(z_ref.at[rows], zbuf.at[slot], sem.at[slot])

    @pl.when(blk == 0)
    def _():
        for g in range(ahead):
            @pl.when(g < total)
            def _():
                z_copy(g).start()

    def chunk(c, carry):
        g = first + c
        slot = g % Z_BUFFERS

        @pl.when(g + ahead < total)
        def _():
            z_copy(g + ahead).start()

        z_copy(g).wait()
        r = start_ref[g] + lax.broadcasted_iota(I32, (qt.shape[0], Z_CHUNK), 1)
        onehot = jnp.where((r >= qt) & (r < qt + kt), 1.0, 0.0).astype(BF16)
        left, right = _unpack_halves(zbuf[slot])
        half = left.shape[1]
        acc_ref[:, :half] += _dot(onehot, left)
        acc_ref[:, half:] += _dot(onehot, right)
        return carry

    lax.fori_loop(0, n_chunks, chunk, 0)
    o_ref[...] = x1_ref[...] + mod_ref[0, 5:6, :] * acc_ref[...]


def _combine(chunks, z, x1, first_block, mods, qt, kt, cond_of_block):
    n = qt.shape[0]
    d = x1.shape[1]
    tb = COMBINE_BLOCK
    grid_spec = pltpu.PrefetchScalarGridSpec(
        num_scalar_prefetch=4,
        grid=(n // tb,),
        in_specs=[pl.BlockSpec(memory_space=pl.ANY),
                  pl.BlockSpec((tb, d), lambda i, *_: (i + first_block, 0)),
                  pl.BlockSpec((1, N_MOD, d), lambda i, *_: (cond_of_block(i), 0, 0)),
                  pl.BlockSpec((tb, 1), lambda i, *_: (i, 0)),
                  pl.BlockSpec((tb, 1), lambda i, *_: (i, 0))],
        out_specs=pl.BlockSpec((tb, d), lambda i, *_: (i, 0)),
        scratch_shapes=[pltpu.VMEM((Z_BUFFERS, Z_CHUNK, z.shape[1]), z.dtype), pltpu.SemaphoreType.DMA((Z_BUFFERS,)),
                        pltpu.VMEM((tb, d), F32)],
    )
    return pl.pallas_call(
        _combine_kernel,
        grid_spec=grid_spec,
        out_shape=jax.ShapeDtypeStruct((n, d), F32),
        compiler_params=_params(("arbitrary",)),
        name="combine",
    )(*chunks, z, x1, mods, qt, kt)


def _routing_tables(aff, cap, z_off):
    n = aff.shape[0]
    aff_t = aff.T
    sel, rinc = _route(aff_t, cap)
    slots = jnp.arange(cap, dtype=I32)
    chunk = 128
    rinc3 = rinc.reshape(rinc.shape[0], n // chunk, chunk)
    n_full = jnp.sum((rinc3[:, None, :, chunk - 1] <= slots[None, :, None]).astype(I32), axis=-1)
    pick = (n_full[:, :, None] == jnp.arange(n // chunk, dtype=I32)[None, None, :]).astype(F32)
    inside = jnp.einsum("eca,eab->ecb", pick, rinc3.astype(F32), precision=lax.Precision.HIGHEST)
    idx = n_full * chunk + jnp.sum((inside <= slots[None, :, None].astype(F32)).astype(I32), axis=-1)
    gate = jnp.take_along_axis(aff_t, idx, axis=1)
    k_tok = jnp.sum(sel, axis=0)
    q_tok = jnp.cumsum(k_tok) - k_tok + z_off
    before = jnp.cumsum(sel, axis=0) - sel
    qpos = jnp.take_along_axis(q_tok[None, :] + before, idx, axis=1)
    nblk = n // COMBINE_BLOCK
    q_blk = q_tok[::COMBINE_BLOCK]
    end_blk = q_blk + jnp.sum(k_tok.reshape(nblk, COMBINE_BLOCK), axis=1)
    w0 = (q_blk // ROW_ALIGN) * ROW_ALIGN
    nc = jnp.maximum((end_blk - w0 + Z_CHUNK - 1) // Z_CHUNK, 1).astype(I32)
    first = jnp.cumsum(nc) - nc
    g = jnp.arange(nblk + (N_EXPERTS * cap) // Z_CHUNK + nblk, dtype=I32)
    blk_of = jnp.sum((first[None, 1:] <= g[:, None]).astype(I32), axis=1)
    start = jnp.where(g < jnp.sum(nc), w0[blk_of] + (g - first[blk_of]) * Z_CHUNK, 0)
    chunks = (nc, first.astype(I32), start.astype(I32), jnp.sum(nc).astype(I32)[None])
    return idx, gate, qpos.astype(I32), chunks, q_tok.astype(I32)[:, None], k_tok.astype(I32)[:, None]


def kernel(x_prompt, x_sample, cache_k, cache_v, c, c_ctx, norm1_g, norm2_g, w_ada, b_ada, w_in, q_norm_g, k_norm_g,
           rpb, w_att_proj, w_pool, pool_scale, w_out, w_router, w_gate_e, w_up_e, w_down_e):
    assert w_ada.shape[0] == 1, "single trunk layer"
    nb, ls, d = x_prompt.shape
    db, ll, _ = x_sample.shape
    n_c, n_l = nb * ls, db * ll
    assert ls == TOKEN_BLOCK, "context keys / values are written one sequence per projection block"

    cond = jnp.concatenate([c_ctx[None, :], c, jnp.zeros((16 - 1 - db, d), F32)], axis=0)
    mods = _ada(cond, w_ada[0], b_ada).reshape(16, N_MOD, d)

    w_in_b = w_in[0].astype(BF16)
    w_att_b = w_att_proj[0].astype(BF16)
    w_pool_b = w_pool[0].astype(BF16)
    w_out_b = w_out[0].astype(BF16)


    xc = x_prompt.reshape(n_c, d)
    xl = x_sample.reshape(n_l, d)
    q_gain = jnp.tile(q_norm_g, (1, NA_HEADS))
    k_gain = jnp.tile(k_norm_g, (1, NA_HEADS))
    att_c, kc, vc, pool_c, gac, gbc = _in_proj(xc, ls, mods, norm1_g, w_in_b, q_gain, k_gain, lambda i: 0, True)
    ql, kl, vl, pool_l, gal, gbl = _in_proj(xl, ll, mods, norm1_g, w_in_b, q_gain, k_gain, lambda i: 1 + i, False)

    seq_l = lambda a: a.reshape(db, ll, a.shape[-1])
    bias = _window_bias(rpb[0], ll // GRID_W)
    att_l = _attn_lat(seq_l(ql), seq_l(kl), seq_l(vl), jnp.transpose(cache_k[:, 0], (0, 2, 3, 1)),
                      jnp.transpose(cache_v[:, 0], (0, 2, 3, 1)), bias)

    x1, h2, aff = _merge((xc, att_c, pool_c, gac, gbc),
                         (xl, att_l.reshape(n_l, NA_WIDTH), pool_l, gal, gbl),
                         mods, w_att_b, w_pool_b, pool_scale, w_out_b, norm2_g, w_router[0], ll // MERGE_BLOCK)

    cap_c = max(1, (CAPACITY_FACTOR * n_c) // N_EXPERTS)
    cap_l = max(1, (CAPACITY_FACTOR * n_l) // N_EXPERTS)
    m_tot = cap_c + cap_l
    m_iss = _moe_step_rows(m_tot, pl.cdiv(D_EXPERT, EXPERT_F_BLOCK)) * pl.cdiv(D_EXPERT, EXPERT_F_BLOCK)
    z_valid = N_EXPERTS * m_tot
    z_zero_rows = Z_CHUNK + ROW_ALIGN
    z_spare = z_valid + z_zero_rows
    z_rows = z_spare + m_iss
    idx_c, gate_c, qpos_c, chunks_c, qt_c, kt_c = _routing_tables(aff[:n_c], cap_c, 0)
    idx_l, gate_l, qpos_l, chunks_l, qt_l, kt_l = _routing_tables(aff[n_c:], cap_l, N_EXPERTS * cap_c)
    spare = jnp.broadcast_to(z_spare + jnp.arange(m_iss, dtype=I32), (N_EXPERTS + 1, m_iss))
    gidx = jnp.concatenate([idx_c, idx_l + n_c, jnp.zeros((N_EXPERTS, m_iss - m_tot), I32)], axis=1).reshape(-1)
    qpos = jnp.concatenate([spare[:1], jnp.concatenate([qpos_c, qpos_l, spare[1:, m_tot:]], axis=1)], axis=0).reshape(-1)
    gate = jnp.concatenate([gate_c, gate_l], axis=1)[:, :, None]

    z = _moe(gidx, qpos, h2, gate, jnp.swapaxes(w_gate_e[0], 1, 2), jnp.swapaxes(w_up_e[0], 1, 2), w_down_e[0],
             m_tot, z_rows, z_zero_rows)

    y_c = _combine(chunks_c, z, x1, 0, mods, qt_c, kt_c, lambda i: 0)
    y_l = _combine(chunks_l, z, x1, n_c // COMBINE_BLOCK, mods, qt_l, kt_l, lambda i: 1 + i // (ll // COMBINE_BLOCK))

    state_k = jnp.transpose(kc, (0, 3, 1, 2))[:, None]
    state_v = jnp.transpose(vc, (0, 3, 1, 2))[:, None]
    return (y_c.reshape(nb, ls, d), y_l.reshape(db, ll, d), state_k, state_v)
```

```python
import functools

import jax
import jax.numpy as jnp
from jax import lax
from jax.experimental import pallas as pl
from jax.experimental.pallas import tpu as pltpu

F32 = jnp.float32
BF16 = jnp.bfloat16
I32 = jnp.int32

GRID_W = 64
NA_HEADS = 8
HEAD_DIM = 64
NA_WIDTH = NA_HEADS * HEAD_DIM
WIN_ROWS = 8
WIN_COLS = 16
POOL_WINDOWS = (2, 4, 8, 16)
POOL_GROUP_DIM = 128
POOL_WIDTH = len(POOL_WINDOWS) * POOL_GROUP_DIM
N_EXPERTS = 16
CAPACITY_FACTOR = 2
D_EXPERT = 2752
N_MOD = 6
EPS = 1e-6
NEG_INF = -1e30

TOKEN_BLOCK = 256
COMBINE_BLOCK = 256
MERGE_BLOCK = 512
MERGE_ROWS = 256
LAT_ROW_BLOCK = 4
EXPERT_F_BLOCK = 256
EXPERT_M_BLOCK = 768
Z_CHUNK = 256
Z_BUFFERS = 4
HEAD_AVG_WIDTH = 256
ROW_ALIGN = 8
VMEM_LIMIT = 56 * 1024 * 1024


def _dot(a, b):
    return jnp.dot(a, b, preferred_element_type=F32)


def _dot_nt(a, b):
    return lax.dot_general(a, b, (((1,), (1,)), ((), ())), preferred_element_type=F32)


def _dot3(a, b):
    a_hi = a.astype(BF16)
    a_lo = (a - a_hi.astype(F32)).astype(BF16)
    b_hi = b.astype(BF16)
    b_lo = (b - b_hi.astype(F32)).astype(BF16)
    return _dot(a_hi, b_hi) + _dot(a_hi, b_lo) + _dot(a_lo, b_hi)


def _sigmoid(x):
    return 0.5 * jnp.tanh(0.5 * x) + 0.5


def _rms(x, g):
    return x * lax.rsqrt(jnp.mean(x * x, axis=-1, keepdims=True) + EPS) * g


def _pack_halves(x):
    w = x.shape[1] // 2
    return pltpu.pack_elementwise([x[:, :w], x[:, w:]], packed_dtype=BF16)


def _unpack_halves(p):
    return tuple(pltpu.unpack_elementwise(p, index=i, packed_dtype=BF16, unpacked_dtype=F32).astype(BF16)
                 for i in range(2))


def _params(sem):
    return pltpu.CompilerParams(dimension_semantics=sem, vmem_limit_bytes=VMEM_LIMIT)


def _ada_kernel(cond_ref, w_ref, b_ref, o_ref):
    c = cond_ref[...]
    o_ref[...] = _dot3(c * _sigmoid(c), w_ref[...]) + b_ref[...]


def _ada(cond, w_ada, b_ada):
    rows, d = cond.shape
    n_out = w_ada.shape[1]
    bn = 1024
    return pl.pallas_call(
        _ada_kernel,
        grid=(n_out // bn,),
        in_specs=[pl.BlockSpec((rows, d), lambda i: (0, 0)),
                  pl.BlockSpec((d, bn), lambda i: (0, i)),
                  pl.BlockSpec((1, bn), lambda i: (0, i))],
        out_specs=pl.BlockSpec((rows, bn), lambda i: (0, i)),
        out_shape=jax.ShapeDtypeStruct((rows, n_out), F32),
        compiler_params=_params(("arbitrary",)),
        name="ada",
    )(cond, w_ada, b_ada)


_IN_CUTS = (0, 512, 1024, 1536, 2048, 3072, 4096)


def _head_avg_matrix():
    head = jnp.arange(HEAD_AVG_WIDTH, dtype=I32) // HEAD_DIM
    return jnp.where(head[:, None] == head[None, :], 1.0 / HEAD_DIM, 0.0).astype(BF16)


def _head_rms(x, g, avg, two_term):
    x2 = x * x
    means = []
    for c in range(0, x.shape[1], HEAD_AVG_WIDTH):
        blk = x2[:, c:c + HEAD_AVG_WIDTH]
        hi = blk.astype(BF16)
        ms = _dot(hi, avg)
        if two_term:
            ms = ms + _dot((blk - hi.astype(F32)).astype(BF16), avg)
        means.append(ms)
    return x * lax.rsqrt(jnp.concatenate(means, axis=-1) + EPS) * g


def _pooled(u):
    l = u.shape[0]
    t = lax.broadcasted_iota(I32, (l, POOL_GROUP_DIM), 0)
    groups = []
    for g, w in enumerate(POOL_WINDOWS):
        x = u[:, g * POOL_GROUP_DIM:(g + 1) * POOL_GROUP_DIM]
        acc = x
        for k in range(-(w // 2), w - w // 2):
            if k == 0:
                continue
            shifted = pltpu.roll(x, (-k) % l, 0)
            acc = acc + jnp.where((t + k >= 0) & (t + k < l), shifted, 0.0)
        cnt = (jnp.minimum(t + (w - w // 2), l) - jnp.maximum(t - w // 2, 0)).astype(F32)
        groups.append(acc / cnt - x)
    return jnp.concatenate(groups, axis=-1)


def _with_ones(v):
    lane = lax.broadcasted_iota(I32, v.shape, 1)
    return jnp.concatenate([v.astype(BF16), jnp.where(lane == 0, 1.0, 0.0).astype(BF16)], axis=-1)


def _in_kernel(x_ref, mod_ref, g_ref, w_ref, qg_ref, kg_ref, avg_ref, q_ref, k_ref, v_ref, u_ref, ga_ref, gb_ref,
               *, kv_channel_major):
    us = []
    for r0 in range(0, x_ref.shape[0], TOKEN_BLOCK):
        rows = slice(r0, r0 + TOKEN_BLOCK)
        y = _rms(x_ref[rows, :], g_ref[...])
        h = (y * (1.0 + mod_ref[0, 1:2, :]) + mod_ref[0, 0:1, :]).astype(BF16)
        part = lambda i: _dot(h, w_ref[:, _IN_CUTS[i]:_IN_CUTS[i + 1]])
        qn = (_head_rms(part(0), qg_ref[...], avg_ref[...], False) * HEAD_DIM ** -0.5).astype(BF16)
        kn = _head_rms(part(1), kg_ref[...], avg_ref[...], True)
        v = part(2)
        if kv_channel_major:
            k_ref[0] = kn.T.reshape(k_ref.shape[1:])
            v_ref[0] = v.T.reshape(v_ref.shape[1:])
            knb = kn.astype(BF16)
            for hd in range(NA_HEADS):
                sl = slice(hd * HEAD_DIM, (hd + 1) * HEAD_DIM)
                s = _dot_nt(qn[:, sl], knb[:, sl])
                e = jnp.exp(s - jnp.max(s, axis=-1, keepdims=True)).astype(BF16)
                o = _dot(e, _with_ones(v[:, sl]))
                q_ref[rows, sl] = (o[:, :HEAD_DIM] / o[:, HEAD_DIM:HEAD_DIM + 1]).astype(q_ref.dtype)
        else:
            q_ref[rows, :] = qn
            k_ref[rows, :] = kn.astype(k_ref.dtype)
            v_ref[rows, :] = v.astype(v_ref.dtype)
        us.append(part(3))
        ga_ref[rows, :] = _sigmoid(part(4)).astype(ga_ref.dtype)
        gb_ref[rows, :] = _sigmoid(part(5)).astype(gb_ref.dtype)
    u_ref[...] = _pooled(jnp.concatenate(us, axis=0)).astype(u_ref.dtype)


def _in_proj(x2d, seq_len, mods, norm_g, w_in_bf16, q_gain, k_gain, cond_of_seq, kv_results):
    n, d = x2d.shape
    tb = seq_len
    assert not kv_results or tb == TOKEN_BLOCK
    widths = [hi - lo for lo, hi in zip(_IN_CUTS[:-1], _IN_CUTS[1:])]
    out_specs = [pl.BlockSpec((tb, w), lambda i: (i, 0)) for w in widths]
    out_shape = [jax.ShapeDtypeStruct((n, w), BF16) for w in widths]
    if kv_results:
        for i in (1, 2):
            out_specs[i] = pl.BlockSpec((1, NA_HEADS, HEAD_DIM, tb), lambda i: (i, 0, 0, 0))
            out_shape[i] = jax.ShapeDtypeStruct((n // tb, NA_HEADS, HEAD_DIM, tb), F32)
    full = lambda a: pl.BlockSpec(a.shape, lambda i: (0,) * a.ndim)
    return pl.pallas_call(
        functools.partial(_in_kernel, kv_channel_major=kv_results),
        grid=(n // tb,),
        in_specs=[pl.BlockSpec((tb, d), lambda i: (i, 0)),
                  pl.BlockSpec((1, N_MOD, d), lambda i: (cond_of_seq(i), 0, 0)),
                  full(norm_g), pl.BlockSpec(w_in_bf16.shape, lambda i: (0, 0), pipeline_mode=pl.Buffered(1)),
                  full(q_gain), full(k_gain),
                  pl.BlockSpec((HEAD_AVG_WIDTH, HEAD_AVG_WIDTH), lambda i: (0, 0))],
        out_specs=out_specs,
        out_shape=out_shape,
        compiler_params=_params(("arbitrary",)),
        name="in_proj",
    )(x2d, mods, norm_g, w_in_bf16, q_gain, k_gain, _head_avg_matrix())


def _lat_windows(rows):
    kr = min(WIN_ROWS, rows)
    rb = min(LAT_ROW_BLOCK, rows)
    wr = min(kr + rb, rows)
    assert rows % rb == 0
    starts = [min(max(i * rb - kr // 2, 0), rows - wr) for i in range(rows // rb)]
    for i, ws in enumerate(starts):
        for r in range(i * rb, (i + 1) * rb):
            rs = min(max(r - kr // 2, 0), rows - kr)
            assert ws <= rs and rs + kr <= ws + wr
    return kr, rb, wr, starts


def _attn_lat_kernel(q_ref, k_ref, v_ref, ckt_ref, cvt_ref, bias_ref, att_ref,
                     kn_s, vb_s, ckb_s, cvb_s, *, rows, kr, rb, wr):
    i = pl.program_id(1)

    @pl.when(i == 0)
    def _():
        lc = ckt_ref.shape[3]
        ones_row = jnp.where(lax.broadcasted_iota(I32, (HEAD_DIM, lc), 0) == 0, 1.0, 0.0)
        for h in range(NA_HEADS):
            sl = slice(h * HEAD_DIM, (h + 1) * HEAD_DIM)
            kn_s[h] = k_ref[0, :, sl]
            vb_s[h] = _with_ones(v_ref[0, :, sl])
            ckb_s[h] = ckt_ref[0, h].astype(BF16)
            cvb_s[h] = jnp.concatenate([cvt_ref[0, h], ones_row], axis=0).astype(BF16)

    win_start = jnp.clip(i * rb - kr // 2, 0, rows - wr)
    key_rows = pl.ds(pl.multiple_of(win_start * GRID_W, GRID_W), wr * GRID_W)
    qn_all = q_ref[0]
    for h in range(NA_HEADS):
        sl = slice(h * HEAD_DIM, (h + 1) * HEAD_DIM)
        qn = qn_all[:, sl]
        bias = bias_ref[h, pl.ds(i * rb, rb)].reshape(rb * GRID_W, wr * GRID_W)
        s_loc = _dot_nt(qn, kn_s[h, key_rows, :]) + bias
        s_ctx = _dot(qn, ckb_s[h])
        m = jnp.maximum(jnp.max(s_loc, axis=-1, keepdims=True), jnp.max(s_ctx, axis=-1, keepdims=True))
        o = (_dot(jnp.exp(s_loc - m).astype(BF16), vb_s[h, key_rows, :])
             + _dot_nt(jnp.exp(s_ctx - m).astype(BF16), cvb_s[h]))
        att_ref[0, :, sl] = (o[:, :HEAD_DIM] / o[:, HEAD_DIM:HEAD_DIM + 1]).astype(att_ref.dtype)


def _attn_lat(q, k, v, ckt, cvt, bias):
    b, l, w = q.shape
    lc = ckt.shape[3]
    rows = l // GRID_W
    kr, rb, wr, _ = _lat_windows(rows)
    seq = pl.BlockSpec((1, l, w), lambda i, r: (i, 0, 0))
    ctx = pl.BlockSpec((1, NA_HEADS, HEAD_DIM, lc), lambda i, r: (i, 0, 0, 0))
    rowblk = pl.BlockSpec((1, rb * GRID_W, w), lambda i, r: (i, r, 0))
    return pl.pallas_call(
        functools.partial(_attn_lat_kernel, rows=rows, kr=kr, rb=rb, wr=wr),
        grid=(b, rows // rb),
        in_specs=[rowblk, seq, seq, ctx, ctx,
                  pl.BlockSpec(bias.shape, lambda i, r: (0, 0, 0, 0), pipeline_mode=pl.Buffered(1))],
        out_specs=rowblk,
        out_shape=jax.ShapeDtypeStruct((b, l, w), BF16),
        scratch_shapes=[pltpu.VMEM((NA_HEADS, l, HEAD_DIM), BF16), pltpu.VMEM((NA_HEADS, l, 2 * HEAD_DIM), BF16),
                        pltpu.VMEM((NA_HEADS, HEAD_DIM, lc), BF16), pltpu.VMEM((NA_HEADS, 2 * HEAD_DIM, lc), BF16)],
        compiler_params=_params(("arbitrary", "arbitrary")),
        name="attn_lat",
    )(q, k, v, ckt, cvt, bias)


def _bias_kernel(rpb_ref, o_ref, *, rows):
    kr, rb, wr, starts = _lat_windows(rows)
    h = pl.program_id(0)
    n_ro, n_co = 2 * WIN_ROWS - 1, 2 * WIN_COLS - 1
    wq = lax.broadcasted_iota(I32, (GRID_W, GRID_W), 0)
    wk = lax.broadcasted_iota(I32, (GRID_W, GRID_W), 1)
    co = jnp.clip(wk - wq, -(WIN_COLS - 1), WIN_COLS - 1) + (WIN_COLS - 1)
    col_start = jnp.clip(wq - WIN_COLS // 2, 0, GRID_W - WIN_COLS)
    in_win = (wk >= col_start) & (wk < col_start + WIN_COLS)
    co = jnp.where(in_win, co, -1)
    masked = jnp.full((GRID_W, GRID_W), NEG_INF, F32)
    tables = {}
    for r in range(rows):
        row_start = min(max(r - kr // 2, 0), rows - kr)
        for j in range(wr):
            key_row = starts[r // rb] + j
            blk = masked
            if row_start <= key_row < row_start + kr:
                ro = key_row - r + (WIN_ROWS - 1)
                if ro not in tables:
                    t = masked
                    for c in range(n_co):
                        t = jnp.where(co == c, rpb_ref[(h * n_ro + ro) * n_co + c], t)
                    tables[ro] = t
                blk = tables[ro]
            o_ref[0, r, :, j * GRID_W:(j + 1) * GRID_W] = blk


def _window_bias(rpb, rows):
    _, _, wr, _ = _lat_windows(rows)
    nh = rpb.shape[0]
    return pl.pallas_call(
        functools.partial(_bias_kernel, rows=rows),
        grid=(nh,),
        in_specs=[pl.BlockSpec(memory_space=pltpu.SMEM)],
        out_specs=pl.BlockSpec((1, rows, GRID_W, wr * GRID_W), lambda h: (h, 0, 0, 0)),
        out_shape=jax.ShapeDtypeStruct((nh, rows, GRID_W, wr * GRID_W), F32),
        compiler_params=_params(("arbitrary",)),
        name="window_bias",
    )(rpb.reshape(-1))


def _merge_kernel(xc_ref, xl_ref, attc_ref, attl_ref, plc_ref, pll_ref, gac_ref, gal_ref, gbc_ref, gbl_ref,
                  mod_ref, watt_ref, wpool_ref, ps_ref, wout_ref, g2_ref, wr_ref, x1_ref, h2_ref, aff_ref, *, ctx_blocks):
    is_ctx = pl.program_id(0) < ctx_blocks

    def block(x_ref, att_ref, pl_ref, ga_ref, gb_ref):
        for r0 in range(0, x_ref.shape[0], MERGE_ROWS):
            rows = slice(r0, r0 + MERGE_ROWS)
            o_a = _dot(att_ref[rows, :], watt_ref[...])
            o_b = jnp.concatenate(
                [_dot(pl_ref[rows, g * POOL_GROUP_DIM:(g + 1) * POOL_GROUP_DIM], wpool_ref[g])
                 for g in range(len(POOL_WINDOWS))], axis=-1) * ps_ref[...]
            merged = ga_ref[rows, :] * o_a + gb_ref[rows, :] * o_b
            x1 = x_ref[rows, :] + mod_ref[0, 2:3, :] * _dot(merged.astype(BF16), wout_ref[...])
            x1_ref[rows, :] = x1
            h2 = _rms(x1, g2_ref[...]) * (1.0 + mod_ref[0, 4:5, :]) + mod_ref[0, 3:4, :]
            h2_ref[rows, :] = _pack_halves(h2)
            logits = _dot3(h2, wr_ref[...])
            e = jnp.exp(logits - jnp.max(logits, axis=-1, keepdims=True))
            aff_ref[rows, :] = e / jnp.sum(e, axis=-1, keepdims=True)

    @pl.when(is_ctx)
    def _():
        block(xc_ref, attc_ref, plc_ref, gac_ref, gbc_ref)

    @pl.when(jnp.logical_not(is_ctx))
    def _():
        block(xl_ref, attl_ref, pll_ref, gal_ref, gbl_ref)


def _merge(ctx, lat, mods, w_att, w_pool, pool_scale, w_out, norm2_g, w_router, lat_blocks_per_seq):
    n_c, d = ctx[0].shape
    n = n_c + lat[0].shape[0]
    tb = MERGE_BLOCK
    cb = n_c // tb
    crow = lambda w: pl.BlockSpec((tb, w), lambda i: (jnp.minimum(i, cb - 1), 0))
    lrow = lambda w: pl.BlockSpec((tb, w), lambda i: (jnp.maximum(i - cb, 0), 0))
    orow = lambda w: pl.BlockSpec((tb, w), lambda i: (i, 0))
    full = lambda a: pl.BlockSpec(a.shape, lambda i: (0,) * a.ndim)
    cond = lambda i: jnp.where(i < cb, 0, 1 + (i - cb) // lat_blocks_per_seq)
    pairs, specs = [], []
    for a_c, a_l in zip(ctx, lat):
        pairs += [a_c, a_l]
        specs += [crow(a_c.shape[1]), lrow(a_l.shape[1])]
    return pl.pallas_call(
        functools.partial(_merge_kernel, ctx_blocks=cb),
        grid=(n // tb,),
        in_specs=specs + [pl.BlockSpec((1, N_MOD, d), lambda i: (cond(i), 0, 0)),
                          full(w_att), full(w_pool), full(pool_scale), full(w_out), full(norm2_g), full(w_router)],
        out_specs=[orow(d), orow(d // 2), orow(N_EXPERTS)],
        out_shape=[jax.ShapeDtypeStruct((n, d), F32), jax.ShapeDtypeStruct((n, d // 2), jnp.uint32),
                   jax.ShapeDtypeStruct((n, N_EXPERTS), F32)],
        compiler_params=_params(("arbitrary",)),
        name="merge",
    )(*pairs, mods, w_att, w_pool, pool_scale, w_out, norm2_g, w_router)


def _prefix_incl(mask_f32, out_ref):
    e, n = mask_f32.shape
    i = lax.broadcasted_iota(I32, (128, 128), 0)
    j = lax.broadcasted_iota(I32, (128, 128), 1)
    tri = jnp.where(i <= j, 1.0, 0.0).astype(BF16)
    carry = jnp.zeros((e, 1), F32)
    for c in range(n // 128):
        inc = _dot(mask_f32[:, c * 128:(c + 1) * 128].astype(BF16), tri) + carry
        out_ref[:, c * 128:(c + 1) * 128] = inc.astype(I32)
        carry = inc[:, 127:128]


def _route_kernel(aff_ref, sel_ref, rinc_ref, tmp_ref, *, cap):
    aff = aff_ref[...]

    def bit_step(i, thr):
        cand = thr | jnp.left_shift(jnp.int32(1), 30 - i)
        cnt = jnp.sum(jnp.where(aff >= pltpu.bitcast(cand, F32), 1, 0), axis=1, keepdims=True)
        return jnp.where(cnt >= cap, cand, thr)

    thr = lax.fori_loop(0, 31, bit_step, jnp.zeros((aff.shape[0], 1), I32))
    gt = aff >= pltpu.bitcast(thr + 1, F32)
    eq = (aff >= pltpu.bitcast(thr, F32)) & jnp.logical_not(gt)
    need = cap - jnp.sum(jnp.where(gt, 1, 0), axis=1, keepdims=True)
    _prefix_incl(jnp.where(eq, 1.0, 0.0), tmp_ref)
    sel = gt | (eq & (tmp_ref[...] <= need))
    sel_ref[...] = jnp.where(sel, 1, 0)
    _prefix_incl(jnp.where(sel, 1.0, 0.0), rinc_ref)


def _route(aff_t, cap):
    e, n = aff_t.shape
    full = pl.BlockSpec((e, n), lambda: (0, 0))
    return pl.pallas_call(
        functools.partial(_route_kernel, cap=cap),
        in_specs=[full], out_specs=[full, full],
        out_shape=[jax.ShapeDtypeStruct((e, n), I32)] * 2,
        scratch_shapes=[pltpu.VMEM((e, n), I32)],
        compiler_params=pltpu.CompilerParams(vmem_limit_bytes=VMEM_LIMIT),
        name="route",
    )(aff_t)


def _moe_kernel(gidx_ref, qpos_ref, h_ref, gate_ref, wg_ref, wu_ref, wd_ref, z_ref,
                xf_ref, xb_ref, acc_ref, y_ref, sem_g, sem_s, *, m_tot, n_f, step_rows, z_zero_rows):
    e = pl.program_id(0)
    j = pl.program_id(1)
    n_e = pl.num_programs(0)
    m_iss = step_rows * n_f
    tf = wg_ref.shape[1]
    slot = e % 2

    def gather_row(lst, s, dst_slot):
        t = gidx_ref[lst * m_iss + s]
        return pltpu.make_async_copy(h_ref.at[pl.ds(t, 1)], xf_ref.at[dst_slot, pl.ds(s, 1)], sem_g.at[dst_slot])

    def scatter_row(lst, s):
        q = qpos_ref[lst * m_iss + s]
        return pltpu.make_async_copy(y_ref.at[pl.ds(s, 1)], z_ref.at[pl.ds(q, 1)], sem_s)

    def wait_gathers(dst_slot):
        pltpu.make_async_copy(h_ref.at[pl.ds(0, m_iss)], xf_ref.at[dst_slot], sem_g.at[dst_slot]).wait()

    def wait_scatters():
        pltpu.make_async_copy(y_ref, z_ref.at[pl.ds(0, m_iss)], sem_s).wait()

    @pl.when((e == 0) & (j == 0))
    def _():
        y_ref[...] = jnp.zeros_like(y_ref)
        cp = pltpu.make_async_copy(y_ref.at[pl.ds(0, z_zero_rows)], z_ref.at[pl.ds(n_e * m_tot, z_zero_rows)], sem_s)
        cp.start()
        cp.wait()

        def first(s, c):
            gather_row(0, s, 0).start()
            return c

        lax.fori_loop(0, m_iss, first, 0)

    @pl.when(j == 0)
    def _():
        wait_gathers(slot)
        half = xf_ref.shape[2]
        xb_ref[:, :half], xb_ref[:, half:] = _unpack_halves(xf_ref[slot, 0:m_tot, :])
        acc_ref[...] = jnp.zeros_like(acc_ref)

    nxt = lax.rem(e + 1, n_e)
    for u in range(step_rows):
        s = j * step_rows + u
        gather_row(nxt, s, 1 - slot).start()
        scatter_row(e, s).start()

    valid = D_EXPERT - j * tf
    cmask = lax.broadcasted_iota(I32, (1, tf), 1) < valid
    rmask = lax.broadcasted_iota(I32, (tf, 1), 0) < valid
    wg = wg_ref[0].astype(BF16)
    wu = wu_ref[0].astype(BF16)
    wd = jnp.where(rmask, wd_ref[0], 0.0).astype(BF16)
    for m in range(m_tot // EXPERT_M_BLOCK):
        rs = slice(m * EXPERT_M_BLOCK, (m + 1) * EXPERT_M_BLOCK)
        xs = xb_ref[rs, :]
        a = _dot_nt(xs, wg)
        b = _dot_nt(xs, wu)
        hm = jnp.where(cmask, a * _sigmoid(a) * b, 0.0).astype(BF16)
        acc_ref[rs, :] += _dot(hm, wd)

    @pl.when(j == n_f - 1)
    def _():
        wait_scatters()
        y_ref[0:m_tot, :] = _pack_halves(acc_ref[...] * gate_ref[0])

        @pl.when(e == n_e - 1)
        def _():
            def last(s, c):
                scatter_row(n_e, s).start()
                return c

            lax.fori_loop(0, m_iss, last, 0)
            wait_scatters()
            wait_gathers(1 - slot)


def _moe_step_rows(m_tot, n_f):
    return pl.cdiv(pl.cdiv(m_tot, n_f), ROW_ALIGN) * ROW_ALIGN


def _moe(gidx, qpos, h, gate, w_gate, w_up, w_down, m_tot, z_rows, z_zero_rows):
    d = w_down.shape[2]
    n_e = w_down.shape[0]
    tf = EXPERT_F_BLOCK
    n_f = pl.cdiv(D_EXPERT, tf)
    step_rows = _moe_step_rows(m_tot, n_f)
    m_iss = step_rows * n_f
    weights = pl.BlockSpec((1, tf, d), lambda e, j, *_: (e, j, 0))
    grid_spec = pltpu.PrefetchScalarGridSpec(
        num_scalar_prefetch=2,
        grid=(n_e, n_f),
        in_specs=[pl.BlockSpec(memory_space=pl.ANY),
                  pl.BlockSpec((1, m_tot, 1), lambda e, j, *_: (e, 0, 0)),
                  weights, weights, weights],
        out_specs=pl.BlockSpec(memory_space=pl.ANY),
        scratch_shapes=[pltpu.VMEM((2, m_iss, d // 2), jnp.uint32), pltpu.VMEM((m_tot, d), BF16),
                        pltpu.VMEM((m_tot, d), F32),
                        pltpu.VMEM((m_iss, d // 2), jnp.uint32), pltpu.SemaphoreType.DMA((2,)),
                        pltpu.SemaphoreType.DMA],
    )
    return pl.pallas_call(
        functools.partial(_moe_kernel, m_tot=m_tot, n_f=n_f, step_rows=step_rows, z_zero_rows=z_zero_rows),
        grid_spec=grid_spec,
        out_shape=jax.ShapeDtypeStruct((z_rows, d // 2), jnp.uint32),
        compiler_params=_params(("arbitrary", "arbitrary")),
        name="moe",
    )(gidx, qpos, h, gate, w_gate, w_up, w_down)


def _combine_kernel(nc_ref, first_ref, start_ref, total_ref, z_ref, x1_ref, mod_ref, qt_ref, kt_ref, o_ref,
                    zbuf, sem, acc_ref):
    blk = pl.program_id(0)
    n_chunks = nc_ref[blk]
    first = first_ref[blk]
    total = total_ref[0]
    ahead = Z_BUFFERS - 1
    qt = qt_ref[...]
    kt = kt_ref[...]
    acc_ref[...] = jnp.zeros_like(acc_ref)

    def z_copy(g):
        rows = pl.ds(pl.multiple_of(start_ref[g], ROW_ALIGN), Z_CHUNK)
        slot = g % Z_BUFFERS
        return pltpu.make_async_copy(z_ref.at[rows], zbuf.at[slot], sem.at[slot])

    @pl.when(blk == 0)
    def _():
        for g in range(ahead):
            @pl.when(g < total)
            def _():
                z_copy(g).start()

    def chunk(c, carry):
        g = first + c
        slot = g % Z_BUFFERS

        @pl.when(g + ahead < total)
        def _():
            z_copy(g + ahead).start()

        z_copy(g).wait()
        r = start_ref[g] + lax.broadcasted_iota(I32, (qt.shape[0], Z_CHUNK), 1)
        onehot = jnp.where((r >= qt) & (r < qt + kt), 1.0, 0.0).astype(BF16)
        left, right = _unpack_halves(zbuf[slot])
        half = left.shape[1]
        acc_ref[:, :half] += _dot(onehot, left)
        acc_ref[:, half:] += _dot(onehot, right)
        return carry

    lax.fori_loop(0, n_chunks, chunk, 0)
    o_ref[...] = x1_ref[...] + mod_ref[0, 5:6, :] * acc_ref[...]


def _combine(chunks, z, x1, first_block, mods, qt, kt, cond_of_block):
    n = qt.shape[0]
    d = x1.shape[1]
    tb = COMBINE_BLOCK
    grid_spec = pltpu.PrefetchScalarGridSpec(
        num_scalar_prefetch=4,
        grid=(n // tb,),
        in_specs=[pl.BlockSpec(memory_space=pl.ANY),
                  pl.BlockSpec((tb, d), lambda i, *_: (i + first_block, 0)),
                  pl.BlockSpec((1, N_MOD, d), lambda i, *_: (cond_of_block(i), 0, 0)),
                  pl.BlockSpec((tb, 1), lambda i, *_: (i, 0)),
                  pl.BlockSpec((tb, 1), lambda i, *_: (i, 0))],
        out_specs=pl.BlockSpec((tb, d), lambda i, *_: (i, 0)),
        scratch_shapes=[pltpu.VMEM((Z_BUFFERS, Z_CHUNK, z.shape[1]), z.dtype), pltpu.SemaphoreType.DMA((Z_BUFFERS,)),
                        pltpu.VMEM((tb, d), F32)],
    )
    return pl.pallas_call(
        _combine_kernel,
        grid_spec=grid_spec,
        out_shape=jax.ShapeDtypeStruct((n, d), F32),
        compiler_params=_params(("arbitrary",)),
        name="combine",
    )(*chunks, z, x1, mods, qt, kt)


def _routing_tables(aff, cap, z_off):
    n = aff.shape[0]
    aff_t = aff.T
    sel, rinc = _route(aff_t, cap)
    slots = jnp.arange(cap, dtype=I32)
    chunk = 128
    rinc3 = rinc.reshape(rinc.shape[0], n // chunk, chunk)
    n_full = jnp.sum((rinc3[:, None, :, chunk - 1] <= slots[None, :, None]).astype(I32), axis=-1)
    pick = (n_full[:, :, None] == jnp.arange(n // chunk, dtype=I32)[None, None, :]).astype(F32)
    inside = jnp.einsum("eca,eab->ecb", pick, rinc3.astype(F32), precision=lax.Precision.HIGHEST)
    idx = n_full * chunk + jnp.sum((inside <= slots[None, :, None].astype(F32)).astype(I32), axis=-1)
    gate = jnp.take_along_axis(aff_t, idx, axis=1)
    k_tok = jnp.sum(sel, axis=0)
    q_tok = jnp.cumsum(k_tok) - k_tok + z_off
    before = jnp.cumsum(sel, axis=0) - sel
    qpos = jnp.take_along_axis(q_tok[None, :] + before, idx, axis=1)
    nblk = n // COMBINE_BLOCK
    q_blk = q_tok[::COMBINE_BLOCK]
    end_blk = q_blk + jnp.sum(k_tok.reshape(nblk, COMBINE_BLOCK), axis=1)
    w0 = (q_blk // ROW_ALIGN) * ROW_ALIGN
    nc = jnp.maximum((end_blk - w0 + Z_CHUNK - 1) // Z_CHUNK, 1).astype(I32)
    first = jnp.cumsum(nc) - nc
    g = jnp.arange(nblk + (N_EXPERTS * cap) // Z_CHUNK + nblk, dtype=I32)
    blk_of = jnp.sum((first[None, 1:] <= g[:, None]).astype(I32), axis=1)
    start = jnp.where(g < jnp.sum(nc), w0[blk_of] + (g - first[blk_of]) * Z_CHUNK, 0)
    chunks = (nc, first.astype(I32), start.astype(I32), jnp.sum(nc).astype(I32)[None])
    return idx, gate, qpos.astype(I32), chunks, q_tok.astype(I32)[:, None], k_tok.astype(I32)[:, None]


def kernel(x_prompt, x_sample, cache_k, cache_v, c, c_ctx, norm1_g, norm2_g, w_ada, b_ada, w_in, q_norm_g, k_norm_g,
           rpb, w_att_proj, w_pool, pool_scale, w_out, w_router, w_gate_e, w_up_e, w_down_e):
    assert w_ada.shape[0] == 1, "single trunk layer"
    nb, ls, d = x_prompt.shape
    db, ll, _ = x_sample.shape
    n_c, n_l = nb * ls, db * ll
    assert ls == TOKEN_BLOCK, "context keys / values are written one sequence per projection block"

    cond = jnp.concatenate([c_ctx[None, :], c, jnp.zeros((16 - 1 - db, d), F32)], axis=0)
    mods = _ada(cond, w_ada[0], b_ada).reshape(16, N_MOD, d)

    w_in_b = w_in[0].astype(BF16)
    w_att_b = w_att_proj[0].astype(BF16)
    w_pool_b = w_pool[0].astype(BF16)
    w_out_b = w_out[0].astype(BF16)


    xc = x_prompt.reshape(n_c, d)
    xl = x_sample.reshape(n_l, d)
    q_gain = jnp.tile(q_norm_g, (1, NA_HEADS))
    k_gain = jnp.tile(k_norm_g, (1, NA_HEADS))
    att_c, kc, vc, pool_c, gac, gbc = _in_proj(xc, ls, mods, norm1_g, w_in_b, q_gain, k_gain, lambda i: 0, True)
    ql, kl, vl, pool_l, gal, gbl = _in_proj(xl, ll, mods, norm1_g, w_in_b, q_gain, k_gain, lambda i: 1 + i, False)

    seq_l = lambda a: a.reshape(db, ll, a.shape[-1])
    bias = _window_bias(rpb[0], ll // GRID_W)
    att_l = _attn_lat(seq_l(ql), seq_l(kl), seq_l(vl), jnp.transpose(cache_k[:, 0], (0, 2, 3, 1)),
                      jnp.transpose(cache_v[:, 0], (0, 2, 3, 1)), bias)

    x1, h2, aff = _merge((xc, att_c, pool_c, gac, gbc),
                         (xl, att_l.reshape(n_l, NA_WIDTH), pool_l, gal, gbl),
                         mods, w_att_b, w_pool_b, pool_scale, w_out_b, norm2_g, w_router[0], ll // MERGE_BLOCK)

    cap_c = max(1, (CAPACITY_FACTOR * n_c) // N_EXPERTS)
    cap_l = max(1, (CAPACITY_FACTOR * n_l) // N_EXPERTS)
    m_tot = cap_c + cap_l
    m_iss = _moe_step_rows(m_tot, pl.cdiv(D_EXPERT, EXPERT_F_BLOCK)) * pl.cdiv(D_EXPERT, EXPERT_F_BLOCK)
    z_valid = N_EXPERTS * m_tot
    z_zero_rows = Z_CHUNK + ROW_ALIGN
    z_spare = z_valid + z_zero_rows
    z_rows = z_spare + m_iss
    idx_c, gate_c, qpos_c, chunks_c, qt_c, kt_c = _routing_tables(aff[:n_c], cap_c, 0)
    idx_l, gate_l, qpos_l, chunks_l, qt_l, kt_l = _routing_tables(aff[n_c:], cap_l, N_EXPERTS * cap_c)
    spare = jnp.broadcast_to(z_spare + jnp.arange(m_iss, dtype=I32), (N_EXPERTS + 1, m_iss))
    gidx = jnp.concatenate([idx_c, idx_l + n_c, jnp.zeros((N_EXPERTS, m_iss - m_tot), I32)], axis=1).reshape(-1)
    qpos = jnp.concatenate([spare[:1], jnp.concatenate([qpos_c, qpos_l, spare[1:, m_tot:]], axis=1)], axis=0).reshape(-1)
    gate = jnp.concatenate([gate_c, gate_l], axis=1)[:, :, None]

    z = _moe(gidx, qpos, h2, gate, jnp.swapaxes(w_gate_e[0], 1, 2), jnp.swapaxes(w_up_e[0], 1, 2), w_down_e[0],
             m_tot, z_rows, z_zero_rows)

    y_c = _combine(chunks_c, z, x1, 0, mods, qt_c, kt_c, lambda i: 0)
    y_l = _combine(chunks_l, z, x1, n_c // COMBINE_BLOCK, mods, qt_l, kt_l, lambda i: 1 + i // (ll // COMBINE_BLOCK))

    state_k = jnp.transpose(kc, (0, 3, 1, 2))[:, None]
    state_v = jnp.transpose(vc, (0, 3, 1, 2))[:, None]
    return (y_c.reshape(nb, ls, d), y_l.reshape(db, ll, d), state_k, state_v)
```

```python
import functools

import jax
import jax.numpy as jnp
from jax import lax
from jax.experimental import pallas as pl
from jax.experimental.pallas import tpu as pltpu

F32 = jnp.float32
BF16 = jnp.bfloat16
I32 = jnp.int32

GRID_W = 64
NA_HEADS = 8
HEAD_DIM = 64
NA_WIDTH = NA_HEADS * HEAD_DIM
WIN_ROWS = 8
WIN_COLS = 16
POOL_WINDOWS = (2, 4, 8, 16)
POOL_GROUP_DIM = 128
POOL_WIDTH = len(POOL_WINDOWS) * POOL_GROUP_DIM
N_EXPERTS = 16
CAPACITY_FACTOR = 2
D_EXPERT = 2752
N_MOD = 6
EPS = 1e-6
NEG_INF = -1e30

TOKEN_BLOCK = 256
COMBINE_BLOCK = 256
MERGE_BLOCK = 512
MERGE_ROWS = 256
LAT_ROW_BLOCK = 4
EXPERT_F_BLOCK = 256
EXPERT_M_BLOCK = 768
Z_CHUNK = 256
Z_BUFFERS = 4
HEAD_AVG_WIDTH = 256
ROW_ALIGN = 8
LANES = 128
VMEM_LIMIT = 56 * 1024 * 1024


def _dot(a, b):
    return jnp.dot(a, b, preferred_element_type=F32)


def _dot_nt(a, b):
    return lax.dot_general(a, b, (((1,), (1,)), ((), ())), preferred_element_type=F32)


def _dot3(a, b):
    a_hi = a.astype(BF16)
    a_lo = (a - a_hi.astype(F32)).astype(BF16)
    b_hi = b.astype(BF16)
    b_lo = (b - b_hi.astype(F32)).astype(BF16)
    return _dot(a_hi, b_hi) + _dot(a_hi, b_lo) + _dot(a_lo, b_hi)


def _sigmoid(x):
    return 0.5 * jnp.tanh(0.5 * x) + 0.5


def _rms(x, g):
    return x * lax.rsqrt(jnp.mean(x * x, axis=-1, keepdims=True) + EPS) * g


def _pack_halves(x):
    w = x.shape[1] // 2
    return pltpu.pack_elementwise([x[:, :w], x[:, w:]], packed_dtype=BF16)


def _unpack_halves(p):
    return tuple(pltpu.unpack_elementwise(p, index=i, packed_dtype=BF16, unpacked_dtype=F32).astype(BF16)
                 for i in range(2))


def _params(sem):
    return pltpu.CompilerParams(dimension_semantics=sem, vmem_limit_bytes=VMEM_LIMIT)


def _ada_kernel(cond_ref, w_ref, b_ref, o_ref):
    c = cond_ref[...]
    o_ref[...] = _dot3(c * _sigmoid(c), w_ref[...]) + b_ref[...]


def _ada(cond, w_ada, b_ada):
    rows, d = cond.shape
    n_out = w_ada.shape[1]
    bn = 1024
    return pl.pallas_call(
        _ada_kernel,
        grid=(n_out // bn,),
        in_specs=[pl.BlockSpec((rows, d), lambda i: (0, 0)),
                  pl.BlockSpec((d, bn), lambda i: (0, i)),
                  pl.BlockSpec((1, bn), lambda i: (0, i))],
        out_specs=pl.BlockSpec((rows, bn), lambda i: (0, i)),
        out_shape=jax.ShapeDtypeStruct((rows, n_out), F32),
        compiler_params=_params(("arbitrary",)),
        name="ada",
    )(cond, w_ada, b_ada)


_IN_CUTS = (0, 512, 1024, 1536, 2048, 3072, 4096)


def _head_avg_matrix():
    head = jnp.arange(HEAD_AVG_WIDTH, dtype=I32) // HEAD_DIM
    return jnp.where(head[:, None] == head[None, :], 1.0 / HEAD_DIM, 0.0).astype(BF16)


def _head_rms(x, g, avg, two_term):
    x2 = x * x
    means = []
    for c in range(0, x.shape[1], HEAD_AVG_WIDTH):
        blk = x2[:, c:c + HEAD_AVG_WIDTH]
        hi = blk.astype(BF16)
        ms = _dot(hi, avg)
        if two_term:
            ms = ms + _dot((blk - hi.astype(F32)).astype(BF16), avg)
        means.append(ms)
    return x * lax.rsqrt(jnp.concatenate(means, axis=-1) + EPS) * g


def _pooled(u):
    l = u.shape[0]
    t = lax.broadcasted_iota(I32, (l, POOL_GROUP_DIM), 0)
    groups = []
    for g, w in enumerate(POOL_WINDOWS):
        x = u[:, g * POOL_GROUP_DIM:(g + 1) * POOL_GROUP_DIM]
        acc = x
        for k in range(-(w // 2), w - w // 2):
            if k == 0:
                continue
            shifted = pltpu.roll(x, (-k) % l, 0)
            acc = acc + jnp.where((t + k >= 0) & (t + k < l), shifted, 0.0)
        cnt = (jnp.minimum(t + (w - w // 2), l) - jnp.maximum(t - w // 2, 0)).astype(F32)
        groups.append(acc / cnt - x)
    return jnp.concatenate(groups, axis=-1)


def _with_ones(v):
    lane = lax.broadcasted_iota(I32, v.shape, 1)
    return jnp.concatenate([v.astype(BF16), jnp.where(lane == 0, 1.0, 0.0).astype(BF16)], axis=-1)


def _in_kernel(x_ref, mod_ref, g_ref, w_ref, qg_ref, kg_ref, avg_ref, q_ref, k_ref, v_ref, u_ref, ga_ref, gb_ref,
               *, kv_channel_major):
    us = []
    for r0 in range(0, x_ref.shape[0], TOKEN_BLOCK):
        rows = slice(r0, r0 + TOKEN_BLOCK)
        y = _rms(x_ref[rows, :], g_ref[...])
        h = (y * (1.0 + mod_ref[0, 1:2, :]) + mod_ref[0, 0:1, :]).astype(BF16)
        part = lambda i: _dot(h, w_ref[:, _IN_CUTS[i]:_IN_CUTS[i + 1]])
        qn = (_head_rms(part(0), qg_ref[...], avg_ref[...], False) * HEAD_DIM ** -0.5).astype(BF16)
        kn = _head_rms(part(1), kg_ref[...], avg_ref[...], True)
        v = part(2)
        if kv_channel_major:
            k_ref[0] = kn.T.reshape(k_ref.shape[1:])
            v_ref[0] = v.T.reshape(v_ref.shape[1:])
            knb = kn.astype(BF16)
            for hd in range(NA_HEADS):
                sl = slice(hd * HEAD_DIM, (hd + 1) * HEAD_DIM)
                s = _dot_nt(qn[:, sl], knb[:, sl])
                e = jnp.exp(s - jnp.max(s, axis=-1, keepdims=True)).astype(BF16)
                o = _dot(e, _with_ones(v[:, sl]))
                q_ref[rows, sl] = (o[:, :HEAD_DIM] / o[:, HEAD_DIM:HEAD_DIM + 1]).astype(q_ref.dtype)
        else:
            q_ref[rows, :] = qn
            k_ref[rows, :] = kn.astype(k_ref.dtype)
            v_ref[rows, :] = v.astype(v_ref.dtype)
        us.append(part(3))
        ga_ref[rows, :] = _sigmoid(part(4)).astype(ga_ref.dtype)
        gb_ref[rows, :] = _sigmoid(part(5)).astype(gb_ref.dtype)
    u_ref[...] = _pooled(jnp.concatenate(us, axis=0)).astype(u_ref.dtype)


def _in_proj(x2d, seq_len, mods, norm_g, w_in_bf16, q_gain, k_gain, cond_of_seq, kv_results):
    n, d = x2d.shape
    tb = seq_len
    assert not kv_results or tb == TOKEN_BLOCK
    widths = [hi - lo for lo, hi in zip(_IN_CUTS[:-1], _IN_CUTS[1:])]
    out_specs = [pl.BlockSpec((tb, w), lambda i: (i, 0)) for w in widths]
    out_shape = [jax.ShapeDtypeStruct((n, w), BF16) for w in widths]
    if kv_results:
        for i in (1, 2):
            out_specs[i] = pl.BlockSpec((1, NA_HEADS, HEAD_DIM, tb), lambda i: (i, 0, 0, 0))
            out_shape[i] = jax.ShapeDtypeStruct((n // tb, NA_HEADS, HEAD_DIM, tb), F32)
    full = lambda a: pl.BlockSpec(a.shape, lambda i: (0,) * a.ndim)
    return pl.pallas_call(
        functools.partial(_in_kernel, kv_channel_major=kv_results),
        grid=(n // tb,),
        in_specs=[pl.BlockSpec((tb, d), lambda i: (i, 0)),
                  pl.BlockSpec((1, N_MOD, d), lambda i: (cond_of_seq(i), 0, 0)),
                  full(norm_g), pl.BlockSpec(w_in_bf16.shape, lambda i: (0, 0), pipeline_mode=pl.Buffered(1)),
                  full(q_gain), full(k_gain),
                  pl.BlockSpec((HEAD_AVG_WIDTH, HEAD_AVG_WIDTH), lambda i: (0, 0))],
        out_specs=out_specs,
        out_shape=out_shape,
        compiler_params=_params(("arbitrary",)),
        name="in_proj",
    )(x2d, mods, norm_g, w_in_bf16, q_gain, k_gain, _head_avg_matrix())


def _lat_windows(rows):
    kr = min(WIN_ROWS, rows)
    rb = min(LAT_ROW_BLOCK, rows)
    wr = min(kr + rb, rows)
    assert rows % rb == 0
    starts = [min(max(i * rb - kr // 2, 0), rows - wr) for i in range(rows // rb)]
    for i, ws in enumerate(starts):
        for r in range(i * rb, (i + 1) * rb):
            rs = min(max(r - kr // 2, 0), rows - kr)
            assert ws <= rs and rs + kr <= ws + wr
    return kr, rb, wr, starts


def _attn_lat_kernel(q_ref, k_ref, v_ref, ckt_ref, cvt_ref, bias_ref, att_ref,
                     kn_s, vb_s, ckb_s, cvb_s, *, rows, kr, rb, wr):
    i = pl.program_id(1)

    @pl.when(i == 0)
    def _():
        lc = ckt_ref.shape[3]
        ones_row = jnp.where(lax.broadcasted_iota(I32, (HEAD_DIM, lc), 0) == 0, 1.0, 0.0)
        for h in range(NA_HEADS):
            sl = slice(h * HEAD_DIM, (h + 1) * HEAD_DIM)
            kn_s[h] = k_ref[0, :, sl]
            vb_s[h] = _with_ones(v_ref[0, :, sl])
            ckb_s[h] = ckt_ref[0, h].astype(BF16)
            cvb_s[h] = jnp.concatenate([cvt_ref[0, h], ones_row], axis=0).astype(BF16)

    win_start = jnp.clip(i * rb - kr // 2, 0, rows - wr)
    key_rows = pl.ds(pl.multiple_of(win_start * GRID_W, GRID_W), wr * GRID_W)
    qn_all = q_ref[0]
    for h in range(NA_HEADS):
        sl = slice(h * HEAD_DIM, (h + 1) * HEAD_DIM)
        qn = qn_all[:, sl]
        bias = bias_ref[h, pl.ds(i * rb, rb)].reshape(rb * GRID_W, wr * GRID_W)
        s_loc = _dot_nt(qn, kn_s[h, key_rows, :]) + bias
        s_ctx = _dot(qn, ckb_s[h])
        m = jnp.maximum(jnp.max(s_loc, axis=-1, keepdims=True), jnp.max(s_ctx, axis=-1, keepdims=True))
        o = (_dot(jnp.exp(s_loc - m).astype(BF16), vb_s[h, key_rows, :])
             + _dot_nt(jnp.exp(s_ctx - m).astype(BF16), cvb_s[h]))
        att_ref[0, :, sl] = (o[:, :HEAD_DIM] / o[:, HEAD_DIM:HEAD_DIM + 1]).astype(att_ref.dtype)


def _attn_lat(q, k, v, ckt, cvt, bias):
    b, l, w = q.shape
    lc = ckt.shape[3]
    rows = l // GRID_W
    kr, rb, wr, _ = _lat_windows(rows)
    seq = pl.BlockSpec((1, l, w), lambda i, r: (i, 0, 0))
    ctx = pl.BlockSpec((1, NA_HEADS, HEAD_DIM, lc), lambda i, r: (i, 0, 0, 0))
    rowblk = pl.BlockSpec((1, rb * GRID_W, w), lambda i, r: (i, r, 0))
    return pl.pallas_call(
        functools.partial(_attn_lat_kernel, rows=rows, kr=kr, rb=rb, wr=wr),
        grid=(b, rows // rb),
        in_specs=[rowblk, seq, seq, ctx, ctx,
                  pl.BlockSpec(bias.shape, lambda i, r: (0, 0, 0, 0), pipeline_mode=pl.Buffered(1))],
        out_specs=rowblk,
        out_shape=jax.ShapeDtypeStruct((b, l, w), BF16),
        scratch_shapes=[pltpu.VMEM((NA_HEADS, l, HEAD_DIM), BF16), pltpu.VMEM((NA_HEADS, l, 2 * HEAD_DIM), BF16),
                        pltpu.VMEM((NA_HEADS, HEAD_DIM, lc), BF16), pltpu.VMEM((NA_HEADS, 2 * HEAD_DIM, lc), BF16)],
        compiler_params=_params(("arbitrary", "arbitrary")),
        name="attn_lat",
    )(q, k, v, ckt, cvt, bias)


def _bias_kernel(rpb_ref, o_ref, *, rows):
    kr, rb, wr, starts = _lat_windows(rows)
    h = pl.program_id(0)
    n_ro, n_co = 2 * WIN_ROWS - 1, 2 * WIN_COLS - 1
    wq = lax.broadcasted_iota(I32, (GRID_W, GRID_W), 0)
    wk = lax.broadcasted_iota(I32, (GRID_W, GRID_W), 1)
    co = jnp.clip(wk - wq, -(WIN_COLS - 1), WIN_COLS - 1) + (WIN_COLS - 1)
    col_start = jnp.clip(wq - WIN_COLS // 2, 0, GRID_W - WIN_COLS)
    in_win = (wk >= col_start) & (wk < col_start + WIN_COLS)
    co = jnp.where(in_win, co, -1)
    masked = jnp.full((GRID_W, GRID_W), NEG_INF, F32)
    tables = {}
    for r in range(rows):
        row_start = min(max(r - kr // 2, 0), rows - kr)
        for j in range(wr):
            key_row = starts[r // rb] + j
            blk = masked
            if row_start <= key_row < row_start + kr:
                ro = key_row - r + (WIN_ROWS - 1)
                if ro not in tables:
                    t = masked
                    for c in range(n_co):
                        t = jnp.where(co == c, rpb_ref[(h * n_ro + ro) * n_co + c], t)
                    tables[ro] = t
                blk = tables[ro]
            o_ref[0, r, :, j * GRID_W:(j + 1) * GRID_W] = blk


def _window_bias(rpb, rows):
    _, _, wr, _ = _lat_windows(rows)
    nh = rpb.shape[0]
    return pl.pallas_call(
        functools.partial(_bias_kernel, rows=rows),
        grid=(nh,),
        in_specs=[pl.BlockSpec(memory_space=pltpu.SMEM)],
        out_specs=pl.BlockSpec((1, rows, GRID_W, wr * GRID_W), lambda h: (h, 0, 0, 0)),
        out_shape=jax.ShapeDtypeStruct((nh, rows, GRID_W, wr * GRID_W), F32),
        compiler_params=_params(("arbitrary",)),
        name="window_bias",
    )(rpb.reshape(-1))


def _merge_kernel(xc_ref, xl_ref, attc_ref, attl_ref, plc_ref, pll_ref, gac_ref, gal_ref, gbc_ref, gbl_ref,
                  mod_ref, watt_ref, wpool_ref, ps_ref, wout_ref, g2_ref, wr_ref, x1_ref, h2_ref, aff_ref, *, ctx_blocks):
    is_ctx = pl.program_id(0) < ctx_blocks

    def block(x_ref, att_ref, pl_ref, ga_ref, gb_ref):
        for r0 in range(0, x_ref.shape[0], MERGE_ROWS):
            rows = slice(r0, r0 + MERGE_ROWS)
            o_a = _dot(att_ref[rows, :], watt_ref[...])
            o_b = jnp.concatenate(
                [_dot(pl_ref[rows, g * POOL_GROUP_DIM:(g + 1) * POOL_GROUP_DIM], wpool_ref[g])
                 for g in range(len(POOL_WINDOWS))], axis=-1) * ps_ref[...]
            merged = ga_ref[rows, :] * o_a + gb_ref[rows, :] * o_b
            x1 = x_ref[rows, :] + mod_ref[0, 2:3, :] * _dot(merged.astype(BF16), wout_ref[...])
            x1_ref[rows, :] = x1
            h2 = _rms(x1, g2_ref[...]) * (1.0 + mod_ref[0, 4:5, :]) + mod_ref[0, 3:4, :]
            h2_ref[rows, :] = h2
            logits = _dot3(h2, wr_ref[...])
            e = jnp.exp(logits - jnp.max(logits, axis=-1, keepdims=True))
            aff_ref[rows, :] = e / jnp.sum(e, axis=-1, keepdims=True)

    @pl.when(is_ctx)
    def _():
        block(xc_ref, attc_ref, plc_ref, gac_ref, gbc_ref)

    @pl.when(jnp.logical_not(is_ctx))
    def _():
        block(xl_ref, attl_ref, pll_ref, gal_ref, gbl_ref)


def _merge(ctx, lat, mods, w_att, w_pool, pool_scale, w_out, norm2_g, w_router, lat_blocks_per_seq):
    n_c, d = ctx[0].shape
    n = n_c + lat[0].shape[0]
    tb = MERGE_BLOCK
    cb = n_c // tb
    crow = lambda w: pl.BlockSpec((tb, w), lambda i: (jnp.minimum(i, cb - 1), 0))
    lrow = lambda w: pl.BlockSpec((tb, w), lambda i: (jnp.maximum(i - cb, 0), 0))
    orow = lambda w: pl.BlockSpec((tb, w), lambda i: (i, 0))
    full = lambda a: pl.BlockSpec(a.shape, lambda i: (0,) * a.ndim)
    cond = lambda i: jnp.where(i < cb, 0, 1 + (i - cb) // lat_blocks_per_seq)
    pairs, specs = [], []
    for a_c, a_l in zip(ctx, lat):
        pairs += [a_c, a_l]
        specs += [crow(a_c.shape[1]), lrow(a_l.shape[1])]
    return pl.pallas_call(
        functools.partial(_merge_kernel, ctx_blocks=cb),
        grid=(n // tb,),
        in_specs=specs + [pl.BlockSpec((1, N_MOD, d), lambda i: (cond(i), 0, 0)),
                          full(w_att), full(w_pool), full(pool_scale), full(w_out), full(norm2_g), full(w_router)],
        out_specs=[orow(d), orow(d), orow(N_EXPERTS)],
        out_shape=[jax.ShapeDtypeStruct((n, d), F32), jax.ShapeDtypeStruct((n, d), F32),
                   jax.ShapeDtypeStruct((n, N_EXPERTS), F32)],
        compiler_params=_params(("arbitrary",)),
        name="merge",
    )(*pairs, mods, w_att, w_pool, pool_scale, w_out, norm2_g, w_router)


def _prefix_incl(mask_f32, out_ref):
    e, n = mask_f32.shape
    i = lax.broadcasted_iota(I32, (LANES, LANES), 0)
    j = lax.broadcasted_iota(I32, (LANES, LANES), 1)
    tri = jnp.where(i <= j, 1.0, 0.0).astype(BF16)
    carry = jnp.zeros((e, 1), F32)
    for c in range(n // LANES):
        lanes = slice(c * LANES, (c + 1) * LANES)
        inc = _dot(mask_f32[:, lanes].astype(BF16), tri) + carry
        out_ref[:, lanes] = inc.astype(I32)
        carry = inc[:, LANES - 1:LANES]


def _route_kernel(aff_ref, sel_ref, rinc_ref, tmp_ref, *, cap):
    aff = aff_ref[...]

    def bit_step(i, thr):
        cand = thr | jnp.left_shift(jnp.int32(1), 30 - i)
        cnt = jnp.sum(jnp.where(aff >= pltpu.bitcast(cand, F32), 1, 0), axis=1, keepdims=True)
        return jnp.where(cnt >= cap, cand, thr)

    thr = lax.fori_loop(0, 31, bit_step, jnp.zeros((aff.shape[0], 1), I32))
    gt = aff >= pltpu.bitcast(thr + 1, F32)
    eq = (aff >= pltpu.bitcast(thr, F32)) & jnp.logical_not(gt)
    need = cap - jnp.sum(jnp.where(gt, 1, 0), axis=1, keepdims=True)
    _prefix_incl(jnp.where(eq, 1.0, 0.0), tmp_ref)
    sel = gt | (eq & (tmp_ref[...] <= need))
    sel_ref[...] = jnp.where(sel, 1, 0)
    _prefix_incl(jnp.where(sel, 1.0, 0.0), rinc_ref)


def _route(aff_t, cap):
    e, n = aff_t.shape
    full = pl.BlockSpec((e, n), lambda: (0, 0))
    return pl.pallas_call(
        functools.partial(_route_kernel, cap=cap),
        in_specs=[full], out_specs=[full, full],
        out_shape=[jax.ShapeDtypeStruct((e, n), I32)] * 2,
        scratch_shapes=[pltpu.VMEM((e, n), I32)],
        compiler_params=pltpu.CompilerParams(vmem_limit_bytes=VMEM_LIMIT),
        name="route",
    )(aff_t)


def _moe_kernel(gidx_ref, qpos_ref, h_ref, gate_ref, wg_ref, wu_ref, wd_ref, z_ref,
                xf_ref, xb_ref, acc_ref, y_ref, sem_g, sem_s, *, m_tot, n_f, step_rows, z_zero_rows):
    e = pl.program_id(0)
    j = pl.program_id(1)
    n_e = pl.num_programs(0)
    m_iss = step_rows * n_f
    tf = wg_ref.shape[1]
    slot = e % 2

    def gather_row(lst, s, dst_slot):
        t = gidx_ref[lst * m_iss + s]
        return pltpu.make_async_copy(h_ref.at[pl.ds(t, 1)], xf_ref.at[dst_slot, pl.ds(s, 1)], sem_g.at[dst_slot])

    def scatter_row(lst, s):
        q = qpos_ref[lst * m_iss + s]
        return pltpu.make_async_copy(y_ref.at[pl.ds(s, 1)], z_ref.at[pl.ds(q, 1)], sem_s)

    def wait_gathers(dst_slot):
        pltpu.make_async_copy(h_ref.at[pl.ds(0, m_iss)], xf_ref.at[dst_slot], sem_g.at[dst_slot]).wait()

    def wait_scatters():
        pltpu.make_async_copy(y_ref, z_ref.at[pl.ds(0, m_iss)], sem_s).wait()

    @pl.when((e == 0) & (j == 0))
    def _():
        y_ref[...] = jnp.zeros_like(y_ref)
        cp = pltpu.make_async_copy(y_ref.at[pl.ds(0, z_zero_rows)], z_ref.at[pl.ds(n_e * m_tot, z_zero_rows)], sem_s)
        cp.start()
        cp.wait()

        def first(s, c):
            gather_row(0, s, 0).start()
            return c

        lax.fori_loop(0, m_iss, first, 0)

    @pl.when(j == 0)
    def _():
        wait_gathers(slot)
        xb_ref[...] = xf_ref[slot, 0:m_tot, :].astype(BF16)
        acc_ref[...] = jnp.zeros_like(acc_ref)

    nxt = lax.rem(e + 1, n_e)
    for u in range(step_rows):
        s = j * step_rows + u
        gather_row(nxt, s, 1 - slot).start()
        scatter_row(e, s).start()

    valid = D_EXPERT - j * tf
    cmask = lax.broadcasted_iota(I32, (1, tf), 1) < valid
    rmask = lax.broadcasted_iota(I32, (tf, 1), 0) < valid
    wg = wg_ref[0].astype(BF16)
    wu = wu_ref[0].astype(BF16)
    wd = jnp.where(rmask, wd_ref[0], 0.0).astype(BF16)
    for m in range(m_tot // EXPERT_M_BLOCK):
        rs = slice(m * EXPERT_M_BLOCK, (m + 1) * EXPERT_M_BLOCK)
        xs = xb_ref[rs, :]
        a = _dot_nt(xs, wg)
        b = _dot_nt(xs, wu)
        hm = jnp.where(cmask, a * _sigmoid(a) * b, 0.0).astype(BF16)
        acc_ref[rs, :] += _dot(hm, wd)

    @pl.when(j == n_f - 1)
    def _():
        wait_scatters()
        y_ref[0:m_tot, :] = _pack_halves(acc_ref[...] * gate_ref[0])

        @pl.when(e == n_e - 1)
        def _():
            def last(s, c):
                scatter_row(n_e, s).start()
                return c

            lax.fori_loop(0, m_iss, last, 0)
            wait_scatters()
            wait_gathers(1 - slot)


def _moe_step_rows(m_tot, n_f):
    return pl.cdiv(pl.cdiv(m_tot, n_f), ROW_ALIGN) * ROW_ALIGN


def _moe(gidx, qpos, h, gate, w_gate, w_up, w_down, m_tot, z_rows, z_zero_rows):
    d = h.shape[1]
    n_e = w_down.shape[0]
    tf = EXPERT_F_BLOCK
    n_f = pl.cdiv(D_EXPERT, tf)
    step_rows = _moe_step_rows(m_tot, n_f)
    m_iss = step_rows * n_f
    weights = pl.BlockSpec((1, tf, d), lambda e, j, *_: (e, j, 0))
    grid_spec = pltpu.PrefetchScalarGridSpec(
        num_scalar_prefetch=2,
        grid=(n_e, n_f),
        in_specs=[pl.BlockSpec(memory_space=pl.ANY),
                  pl.BlockSpec((1, m_tot, 1), lambda e, j, *_: (e, 0, 0)),
                  weights, weights, weights],
        out_specs=pl.BlockSpec(memory_space=pl.ANY),
        scratch_shapes=[pltpu.VMEM((2, m_iss, d), F32), pltpu.VMEM((m_tot, d), BF16), pltpu.VMEM((m_tot, d), F32),
                        pltpu.VMEM((m_iss, d // 2), jnp.uint32), pltpu.SemaphoreType.DMA((2,)),
                        pltpu.SemaphoreType.DMA],
    )
    return pl.pallas_call(
        functools.partial(_moe_kernel, m_tot=m_tot, n_f=n_f, step_rows=step_rows, z_zero_rows=z_zero_rows),
        grid_spec=grid_spec,
        out_shape=jax.ShapeDtypeStruct((z_rows, d // 2), jnp.uint32),
        compiler_params=_params(("arbitrary", "arbitrary")),
        name="moe",
    )(gidx, qpos, h, gate, w_gate, w_up, w_down)


def _combine_kernel(nc_ref, first_ref, start_ref, total_ref, z_ref, x1_ref, mod_ref, qt_ref, kt_ref, o_ref,
                    zbuf, sem, acc_ref):
    blk = pl.program_id(0)
    n_chunks = nc_ref[blk]
    first = first_ref[blk]
    total = total_ref[0]
    ahead = Z_BUFFERS - 1
    qt = qt_ref[...]
    kt = kt_ref[...]
    acc_ref[...] = jnp.zeros_like(acc_ref)

    def z_copy(g):
        rows = pl.ds(pl.multiple_of(start_ref[g], ROW_ALIGN), Z_CHUNK)
        slot = g % Z_BUFFERS
        return pltpu.make_async_copy(z_ref.at[rows], zbuf.at[slot], sem.at[slot])

    @pl.when(blk == 0)
    def _():
        for g in range(ahead):
            @pl.when(g < total)
            def _():
                z_copy(g).start()

    def chunk(c, carry):
        g = first + c
        slot = g % Z_BUFFERS

        @pl.when(g + ahead < total)
        def _():
            z_copy(g + ahead).start()

        z_copy(g).wait()
        r = start_ref[g] + lax.broadcasted_iota(I32, (qt.shape[0], Z_CHUNK), 1)
        onehot = jnp.where((r >= qt) & (r < qt + kt), 1.0, 0.0).astype(BF16)
        left, right = _unpack_halves(zbuf[slot])
        half = left.shape[1]
        acc_ref[:, :half] += _dot(onehot, left)
        acc_ref[:, half:] += _dot(onehot, right)
        return carry

    lax.fori_loop(0, n_chunks, chunk, 0)
    o_ref[...] = x1_ref[...] + mod_ref[0, 5:6, :] * acc_ref[...]


def _combine(chunks, z, x1, first_block, mods, qt, kt, cond_of_block):
    n = qt.shape[0]
    d = x1.shape[1]
    tb = COMBINE_BLOCK
    grid_spec = pltpu.PrefetchScalarGridSpec(
        num_scalar_prefetch=4,
        grid=(n // tb,),
        in_specs=[pl.BlockSpec(memory_space=pl.ANY),
                  pl.BlockSpec((tb, d), lambda i, *_: (i + first_block, 0)),
                  pl.BlockSpec((1, N_MOD, d), lambda i, *_: (cond_of_block(i), 0, 0)),
                  pl.BlockSpec((tb, 1), lambda i, *_: (i, 0)),
                  pl.BlockSpec((tb, 1), lambda i, *_: (i, 0))],
        out_specs=pl.BlockSpec((tb, d), lambda i, *_: (i, 0)),
        scratch_shapes=[pltpu.VMEM((Z_BUFFERS, Z_CHUNK, z.shape[1]), z.dtype), pltpu.SemaphoreType.DMA((Z_BUFFERS,)),
                        pltpu.VMEM((tb, d), F32)],
    )
    return pl.pallas_call(
        _combine_kernel,
        grid_spec=grid_spec,
        out_shape=jax.ShapeDtypeStruct((n, d), F32),
        compiler_params=_params(("arbitrary",)),
        name="combine",
    )(*chunks, z, x1, mods, qt, kt)


def _routing_tables(aff, cap, z_off):
    n = aff.shape[0]
    aff_t = aff.T
    sel, rinc = _route(aff_t, cap)
    slots = jnp.arange(cap, dtype=I32)
    chunk = LANES
    rinc3 = rinc.reshape(rinc.shape[0], n // chunk, chunk)
    n_full = jnp.sum((rinc3[:, None, :, chunk - 1] <= slots[None, :, None]).astype(I32), axis=-1)
    pick = (n_full[:, :, None] == jnp.arange(n // chunk, dtype=I32)[None, None, :]).astype(F32)
    inside = jnp.einsum("eca,eab->ecb", pick, rinc3.astype(F32), precision=lax.Precision.HIGHEST)
    idx = n_full * chunk + jnp.sum((inside <= slots[None, :, None].astype(F32)).astype(I32), axis=-1)
    gate = jnp.take_along_axis(aff_t, idx, axis=1)
    k_tok = jnp.sum(sel, axis=0)
    q_tok = jnp.cumsum(k_tok) - k_tok + z_off
    before = jnp.cumsum(sel, axis=0) - sel
    qpos = jnp.take_along_axis(q_tok[None, :] + before, idx, axis=1)
    nblk = n // COMBINE_BLOCK
    q_blk = q_tok[::COMBINE_BLOCK]
    end_blk = q_blk + jnp.sum(k_tok.reshape(nblk, COMBINE_BLOCK), axis=1)
    w0 = (q_blk // ROW_ALIGN) * ROW_ALIGN
    nc = jnp.maximum((end_blk - w0 + Z_CHUNK - 1) // Z_CHUNK, 1).astype(I32)
    first = jnp.cumsum(nc) - nc
    g = jnp.arange(nblk + (N_EXPERTS * cap) // Z_CHUNK + nblk, dtype=I32)
    blk_of = jnp.sum((first[None, 1:] <= g[:, None]).astype(I32), axis=1)
    start = jnp.where(g < jnp.sum(nc), w0[blk_of] + (g - first[blk_of]) * Z_CHUNK, 0)
    chunks = (nc, first.astype(I32), start.astype(I32), jnp.sum(nc).astype(I32)[None])
    return idx, gate, qpos.astype(I32), chunks, q_tok.astype(I32)[:, None], k_tok.astype(I32)[:, None]


def kernel(x_prompt, x_sample, cache_k, cache_v, c, c_ctx, norm1_g, norm2_g, w_ada, b_ada, w_in, q_norm_g, k_norm_g,
           rpb, w_att_proj, w_pool, pool_scale, w_out, w_router, w_gate_e, w_up_e, w_down_e):
    assert w_ada.shape[0] == 1, "single trunk layer"
    nb, ls, d = x_prompt.shape
    db, ll, _ = x_sample.shape
    n_c, n_l = nb * ls, db * ll
    assert ls == TOKEN_BLOCK, "context keys / values are written one sequence per projection block"

    cond_rows = pl.cdiv(1 + db, ROW_ALIGN) * ROW_ALIGN
    cond = jnp.concatenate([c_ctx[None, :], c, jnp.zeros((cond_rows - 1 - db, d), F32)], axis=0)
    mods = _ada(cond, w_ada[0], b_ada).reshape(cond_rows, N_MOD, d)

    w_in_b = w_in[0].astype(BF16)
    w_att_b = w_att_proj[0].astype(BF16)
    w_pool_b = w_pool[0].astype(BF16)
    w_out_b = w_out[0].astype(BF16)


    xc = x_prompt.reshape(n_c, d)
    xl = x_sample.reshape(n_l, d)
    q_gain = jnp.tile(q_norm_g, (1, NA_HEADS))
    k_gain = jnp.tile(k_norm_g, (1, NA_HEADS))
    att_c, kc, vc, pool_c, gac, gbc = _in_proj(xc, ls, mods, norm1_g, w_in_b, q_gain, k_gain, lambda i: 0, True)
    ql, kl, vl, pool_l, gal, gbl = _in_proj(xl, ll, mods, norm1_g, w_in_b, q_gain, k_gain, lambda i: 1 + i, False)

    seq_l = lambda a: a.reshape(db, ll, a.shape[-1])
    bias = _window_bias(rpb[0], ll // GRID_W)
    att_l = _attn_lat(seq_l(ql), seq_l(kl), seq_l(vl), jnp.transpose(cache_k[:, 0], (0, 2, 3, 1)),
                      jnp.transpose(cache_v[:, 0], (0, 2, 3, 1)), bias)

    x1, h2, aff = _merge((xc, att_c, pool_c, gac, gbc),
                         (xl, att_l.reshape(n_l, NA_WIDTH), pool_l, gal, gbl),
                         mods, w_att_b, w_pool_b, pool_scale, w_out_b, norm2_g, w_router[0], ll // MERGE_BLOCK)

    cap_c = max(1, (CAPACITY_FACTOR * n_c) // N_EXPERTS)
    cap_l = max(1, (CAPACITY_FACTOR * n_l) // N_EXPERTS)
    m_tot = cap_c + cap_l
    m_iss = _moe_step_rows(m_tot, pl.cdiv(D_EXPERT, EXPERT_F_BLOCK)) * pl.cdiv(D_EXPERT, EXPERT_F_BLOCK)
    z_valid = N_EXPERTS * m_tot
    z_zero_rows = Z_CHUNK + ROW_ALIGN
    z_spare = z_valid + z_zero_rows
    z_rows = z_spare + m_iss
    idx_c, gate_c, qpos_c, chunks_c, qt_c, kt_c = _routing_tables(aff[:n_c], cap_c, 0)
    idx_l, gate_l, qpos_l, chunks_l, qt_l, kt_l = _routing_tables(aff[n_c:], cap_l, N_EXPERTS * cap_c)
    spare = jnp.broadcast_to(z_spare + jnp.arange(m_iss, dtype=I32), (N_EXPERTS + 1, m_iss))
    gidx = jnp.concatenate([idx_c, idx_l + n_c, jnp.zeros((N_EXPERTS, m_iss - m_tot), I32)], axis=1).reshape(-1)
    qpos = jnp.concatenate([spare[:1], jnp.concatenate([qpos_c, qpos_l, spare[1:, m_tot:]], axis=1)], axis=0).reshape(-1)
    gate = jnp.concatenate([gate_c, gate_l], axis=1)[:, :, None]

    z = _moe(gidx, qpos, h2, gate, jnp.swapaxes(w_gate_e[0], 1, 2), jnp.swapaxes(w_up_e[0], 1, 2), w_down_e[0],
             m_tot, z_rows, z_zero_rows)

    y_c = _combine(chunks_c, z, x1, 0, mods, qt_c, kt_c, lambda i: 0)
    y_l = _combine(chunks_l, z, x1, n_c // COMBINE_BLOCK, mods, qt_l, kt_l, lambda i: 1 + i // (ll // COMBINE_BLOCK))

    state_k = jnp.transpose(kc, (0, 3, 1, 2))[:, None]
    state_v = jnp.transpose(vc, (0, 3, 1, 2))[:, None]
    return (y_c.reshape(nb, ls, d), y_l.reshape(db, ll, d), state_k, state_v)
```

```python
import functools

import jax
import jax.numpy as jnp
from jax import lax
from jax.experimental import pallas as pl
from jax.experimental.pallas import tpu as pltpu

F32 = jnp.float32
BF16 = jnp.bfloat16
I32 = jnp.int32

GRID_W = 64
NA_HEADS = 8
HEAD_DIM = 64
NA_WIDTH = NA_HEADS * HEAD_DIM
WIN_ROWS = 8
WIN_COLS = 16
POOL_WINDOWS = (2, 4, 8, 16)
POOL_GROUP_DIM = 128
POOL_WIDTH = len(POOL_WINDOWS) * POOL_GROUP_DIM
N_EXPERTS = 16
CAPACITY_FACTOR = 2
D_EXPERT = 2752
N_MOD = 6
EPS = 1e-6
NEG_INF = -1e30

TOKEN_BLOCK = 256
COMBINE_BLOCK = 256
MERGE_BLOCK = 512
MERGE_ROWS = 256
LAT_ROW_BLOCK = 4
EXPERT_F_BLOCK = 256
EXPERT_M_BLOCK = 768
Z_CHUNK = 256
Z_BUFFERS = 4
HEAD_AVG_WIDTH = 256
ROW_ALIGN = 8
LANES = 128
VMEM_LIMIT = 56 * 1024 * 1024


def _dot(a, b):
    return jnp.dot(a, b, preferred_element_type=F32)


def _dot_nt(a, b):
    return lax.dot_general(a, b, (((1,), (1,)), ((), ())), preferred_element_type=F32)


def _dot3(a, b):
    a_hi = a.astype(BF16)
    a_lo = (a - a_hi.astype(F32)).astype(BF16)
    b_hi = b.astype(BF16)
    b_lo = (b - b_hi.astype(F32)).astype(BF16)
    return _dot(a_hi, b_hi) + _dot(a_hi, b_lo) + _dot(a_lo, b_hi)


def _sigmoid(x):
    return 0.5 * jnp.tanh(0.5 * x) + 0.5


def _rms(x, g):
    return x * lax.rsqrt(jnp.mean(x * x, axis=-1, keepdims=True) + EPS) * g


def _pack_halves(x):
    w = x.shape[1] // 2
    return pltpu.pack_elementwise([x[:, :w], x[:, w:]], packed_dtype=BF16)


def _unpack_halves(p):
    return tuple(pltpu.unpack_elementwise(p, index=i, packed_dtype=BF16, unpacked_dtype=F32).astype(BF16)
                 for i in range(2))


def _params(sem):
    return pltpu.CompilerParams(dimension_semantics=sem, vmem_limit_bytes=VMEM_LIMIT)


def _ada_kernel(cond_ref, w_ref, b_ref, o_ref):
    c = cond_ref[...]
    o_ref[...] = _dot3(c * _sigmoid(c), w_ref[...]) + b_ref[...]


def _ada(cond, w_ada, b_ada):
    rows, d = cond.shape
    n_out = w_ada.shape[1]
    bn = 1024
    return pl.pallas_call(
        _ada_kernel,
        grid=(n_out // bn,),
        in_specs=[pl.BlockSpec((rows, d), lambda i: (0, 0)),
                  pl.BlockSpec((d, bn), lambda i: (0, i)),
                  pl.BlockSpec((1, bn), lambda i: (0, i))],
        out_specs=pl.BlockSpec((rows, bn), lambda i: (0, i)),
        out_shape=jax.ShapeDtypeStruct((rows, n_out), F32),
        compiler_params=_params(("arbitrary",)),
        name="ada",
    )(cond, w_ada, b_ada)


_IN_CUTS = (0, 512, 1024, 1536, 2048, 3072, 4096)


def _head_avg_matrix():
    head = jnp.arange(HEAD_AVG_WIDTH, dtype=I32) // HEAD_DIM
    return jnp.where(head[:, None] == head[None, :], 1.0 / HEAD_DIM, 0.0).astype(BF16)


def _head_rms(x, g, avg, two_term):
    x2 = x * x
    means = []
    for c in range(0, x.shape[1], HEAD_AVG_WIDTH):
        blk = x2[:, c:c + HEAD_AVG_WIDTH]
        hi = blk.astype(BF16)
        ms = _dot(hi, avg)
        if two_term:
            ms = ms + _dot((blk - hi.astype(F32)).astype(BF16), avg)
        means.append(ms)
    return x * lax.rsqrt(jnp.concatenate(means, axis=-1) + EPS) * g


def _pooled(u):
    l = u.shape[0]
    t = lax.broadcasted_iota(I32, (l, POOL_GROUP_DIM), 0)
    groups = []
    for g, w in enumerate(POOL_WINDOWS):
        x = u[:, g * POOL_GROUP_DIM:(g + 1) * POOL_GROUP_DIM]
        acc = x
        for k in range(-(w // 2), w - w // 2):
            if k == 0:
                continue
            shifted = pltpu.roll(x, (-k) % l, 0)
            acc = acc + jnp.where((t + k >= 0) & (t + k < l), shifted, 0.0)
        cnt = (jnp.minimum(t + (w - w // 2), l) - jnp.maximum(t - w // 2, 0)).astype(F32)
        groups.append(acc / cnt - x)
    return jnp.concatenate(groups, axis=-1)


def _with_ones(v):
    lane = lax.broadcasted_iota(I32, v.shape, 1)
    return jnp.concatenate([v.astype(BF16), jnp.where(lane == 0, 1.0, 0.0).astype(BF16)], axis=-1)


def _in_kernel(x_ref, mod_ref, g_ref, w_ref, qg_ref, kg_ref, avg_ref, q_ref, k_ref, v_ref, u_ref, ga_ref, gb_ref,
               *, kv_channel_major):
    us = []
    for r0 in range(0, x_ref.shape[0], TOKEN_BLOCK):
        rows = slice(r0, r0 + TOKEN_BLOCK)
        y = _rms(x_ref[rows, :], g_ref[...])
        h = (y * (1.0 + mod_ref[0, 1:2, :]) + mod_ref[0, 0:1, :]).astype(BF16)
        part = lambda i: _dot(h, w_ref[:, _IN_CUTS[i]:_IN_CUTS[i + 1]])
        qn = (_head_rms(part(0), qg_ref[...], avg_ref[...], False) * HEAD_DIM ** -0.5).astype(BF16)
        kn = _head_rms(part(1), kg_ref[...], avg_ref[...], True)
        v = part(2)
        if kv_channel_major:
            k_ref[0] = kn.T.reshape(k_ref.shape[1:])
            v_ref[0] = v.T.reshape(v_ref.shape[1:])
            knb = kn.astype(BF16)
            for hd in range(NA_HEADS):
                sl = slice(hd * HEAD_DIM, (hd + 1) * HEAD_DIM)
                s = _dot_nt(qn[:, sl], knb[:, sl])
                e = jnp.exp(s - jnp.max(s, axis=-1, keepdims=True)).astype(BF16)
                o = _dot(e, _with_ones(v[:, sl]))
                q_ref[rows, sl] = (o[:, :HEAD_DIM] / o[:, HEAD_DIM:HEAD_DIM + 1]).astype(q_ref.dtype)
        else:
            q_ref[rows, :] = qn
            k_ref[rows, :] = kn.astype(k_ref.dtype)
            v_ref[rows, :] = v.astype(v_ref.dtype)
        us.append(part(3))
        ga_ref[rows, :] = _sigmoid(part(4)).astype(ga_ref.dtype)
        gb_ref[rows, :] = _sigmoid(part(5)).astype(gb_ref.dtype)
    u_ref[...] = _pooled(jnp.concatenate(us, axis=0)).astype(u_ref.dtype)


def _in_proj(x2d, seq_len, mods, norm_g, w_in_bf16, q_gain, k_gain, cond_of_seq, kv_results):
    n, d = x2d.shape
    tb = seq_len
    assert not kv_results or tb == TOKEN_BLOCK
    widths = [hi - lo for lo, hi in zip(_IN_CUTS[:-1], _IN_CUTS[1:])]
    out_specs = [pl.BlockSpec((tb, w), lambda i: (i, 0)) for w in widths]
    out_shape = [jax.ShapeDtypeStruct((n, w), BF16) for w in widths]
    if kv_results:
        for i in (1, 2):
            out_specs[i] = pl.BlockSpec((1, NA_HEADS, HEAD_DIM, tb), lambda i: (i, 0, 0, 0))
            out_shape[i] = jax.ShapeDtypeStruct((n // tb, NA_HEADS, HEAD_DIM, tb), F32)
    full = lambda a: pl.BlockSpec(a.shape, lambda i: (0,) * a.ndim)
    return pl.pallas_call(
        functools.partial(_in_kernel, kv_channel_major=kv_results),
        grid=(n // tb,),
        in_specs=[pl.BlockSpec((tb, d), lambda i: (i, 0)),
                  pl.BlockSpec((1, N_MOD, d), lambda i: (cond_of_seq(i), 0, 0)),
                  full(norm_g), pl.BlockSpec(w_in_bf16.shape, lambda i: (0, 0), pipeline_mode=pl.Buffered(1)),
                  full(q_gain), full(k_gain),
                  pl.BlockSpec((HEAD_AVG_WIDTH, HEAD_AVG_WIDTH), lambda i: (0, 0))],
        out_specs=out_specs,
        out_shape=out_shape,
        compiler_params=_params(("arbitrary",)),
        name="in_proj",
    )(x2d, mods, norm_g, w_in_bf16, q_gain, k_gain, _head_avg_matrix())


def _lat_windows(rows):
    kr = min(WIN_ROWS, rows)
    rb = min(LAT_ROW_BLOCK, rows)
    wr = min(kr + rb, rows)
    assert rows % rb == 0
    starts = [min(max(i * rb - kr // 2, 0), rows - wr) for i in range(rows // rb)]
    for i, ws in enumerate(starts):
        for r in range(i * rb, (i + 1) * rb):
            rs = min(max(r - kr // 2, 0), rows - kr)
            assert ws <= rs and rs + kr <= ws + wr
    return kr, rb, wr, starts


def _attn_lat_kernel(q_ref, k_ref, v_ref, ckt_ref, cvt_ref, bias_ref, att_ref,
                     kn_s, vb_s, ckb_s, cvb_s, *, rows, kr, rb, wr):
    i = pl.program_id(1)

    @pl.when(i == 0)
    def _():
        lc = ckt_ref.shape[3]
        ones_row = jnp.where(lax.broadcasted_iota(I32, (HEAD_DIM, lc), 0) == 0, 1.0, 0.0)
        for h in range(NA_HEADS):
            sl = slice(h * HEAD_DIM, (h + 1) * HEAD_DIM)
            kn_s[h] = k_ref[0, :, sl]
            vb_s[h] = _with_ones(v_ref[0, :, sl])
            ckb_s[h] = ckt_ref[0, h].astype(BF16)
            cvb_s[h] = jnp.concatenate([cvt_ref[0, h], ones_row], axis=0).astype(BF16)

    win_start = jnp.clip(i * rb - kr // 2, 0, rows - wr)
    key_rows = pl.ds(pl.multiple_of(win_start * GRID_W, GRID_W), wr * GRID_W)
    qn_all = q_ref[0]
    for h in range(NA_HEADS):
        sl = slice(h * HEAD_DIM, (h + 1) * HEAD_DIM)
        qn = qn_all[:, sl]
        bias = bias_ref[h, pl.ds(i * rb, rb)].reshape(rb * GRID_W, wr * GRID_W)
        s_loc = _dot_nt(qn, kn_s[h, key_rows, :]) + bias
        s_ctx = _dot(qn, ckb_s[h])
        m = jnp.maximum(jnp.max(s_loc, axis=-1, keepdims=True), jnp.max(s_ctx, axis=-1, keepdims=True))
        o = (_dot(jnp.exp(s_loc - m).astype(BF16), vb_s[h, key_rows, :])
             + _dot_nt(jnp.exp(s_ctx - m).astype(BF16), cvb_s[h]))
        att_ref[0, :, sl] = (o[:, :HEAD_DIM] / o[:, HEAD_DIM:HEAD_DIM + 1]).astype(att_ref.dtype)


def _attn_lat(q, k, v, ckt, cvt, bias):
    b, l, w = q.shape
    lc = ckt.shape[3]
    rows = l // GRID_W
    kr, rb, wr, _ = _lat_windows(rows)
    seq = pl.BlockSpec((1, l, w), lambda i, r: (i, 0, 0))
    ctx = pl.BlockSpec((1, NA_HEADS, HEAD_DIM, lc), lambda i, r: (i, 0, 0, 0))
    rowblk = pl.BlockSpec((1, rb * GRID_W, w), lambda i, r: (i, r, 0))
    return pl.pallas_call(
        functools.partial(_attn_lat_kernel, rows=rows, kr=kr, rb=rb, wr=wr),
        grid=(b, rows // rb),
        in_specs=[rowblk, seq, seq, ctx, ctx,
                  pl.BlockSpec(bias.shape, lambda i, r: (0, 0, 0, 0), pipeline_mode=pl.Buffered(1))],
        out_specs=rowblk,
        out_shape=jax.ShapeDtypeStruct((b, l, w), BF16),
        scratch_shapes=[pltpu.VMEM((NA_HEADS, l, HEAD_DIM), BF16), pltpu.VMEM((NA_HEADS, l, 2 * HEAD_DIM), BF16),
                        pltpu.VMEM((NA_HEADS, HEAD_DIM, lc), BF16), pltpu.VMEM((NA_HEADS, 2 * HEAD_DIM, lc), BF16)],
        compiler_params=_params(("arbitrary", "arbitrary")),
        name="attn_lat",
    )(q, k, v, ckt, cvt, bias)


def _bias_kernel(rpb_ref, o_ref, *, rows):
    kr, rb, wr, starts = _lat_windows(rows)
    h = pl.program_id(0)
    n_ro, n_co = 2 * WIN_ROWS - 1, 2 * WIN_COLS - 1
    wq = lax.broadcasted_iota(I32, (GRID_W, GRID_W), 0)
    wk = lax.broadcasted_iota(I32, (GRID_W, GRID_W), 1)
    co = jnp.clip(wk - wq, -(WIN_COLS - 1), WIN_COLS - 1) + (WIN_COLS - 1)
    col_start = jnp.clip(wq - WIN_COLS // 2, 0, GRID_W - WIN_COLS)
    in_win = (wk >= col_start) & (wk < col_start + WIN_COLS)
    co = jnp.where(in_win, co, -1)
    masked = jnp.full((GRID_W, GRID_W), NEG_INF, F32)
    tables = {}
    for r in range(rows):
        row_start = min(max(r - kr // 2, 0), rows - kr)
        for j in range(wr):
            key_row = starts[r // rb] + j
            blk = masked
            if row_start <= key_row < row_start + kr:
                ro = key_row - r + (WIN_ROWS - 1)
                if ro not in tables:
                    t = masked
                    for c in range(n_co):
                        t = jnp.where(co == c, rpb_ref[(h * n_ro + ro) * n_co + c], t)
                    tables[ro] = t
                blk = tables[ro]
            o_ref[0, r, :, j * GRID_W:(j + 1) * GRID_W] = blk


def _window_bias(rpb, rows):
    _, _, wr, _ = _lat_windows(rows)
    nh = rpb.shape[0]
    return pl.pallas_call(
        functools.partial(_bias_kernel, rows=rows),
        grid=(nh,),
        in_specs=[pl.BlockSpec(memory_space=pltpu.SMEM)],
        out_specs=pl.BlockSpec((1, rows, GRID_W, wr * GRID_W), lambda h: (h, 0, 0, 0)),
        out_shape=jax.ShapeDtypeStruct((nh, rows, GRID_W, wr * GRID_W), F32),
        compiler_params=_params(("arbitrary",)),
        name="window_bias",
    )(rpb.reshape(-1))


def _merge_kernel(xc_ref, xl_ref, attc_ref, attl_ref, plc_ref, pll_ref, gac_ref, gal_ref, gbc_ref, gbl_ref,
                  mod_ref, watt_ref, wpool_ref, ps_ref, wout_ref, g2_ref, wr_ref, x1_ref, h2_ref, aff_ref, *, ctx_blocks):
    is_ctx = pl.program_id(0) < ctx_blocks

    def block(x_ref, att_ref, pl_ref, ga_ref, gb_ref):
        for r0 in range(0, x_ref.shape[0], MERGE_ROWS):
            rows = slice(r0, r0 + MERGE_ROWS)
            o_a = _dot(att_ref[rows, :], watt_ref[...])
            o_b = jnp.concatenate(
                [_dot(pl_ref[rows, g * POOL_GROUP_DIM:(g + 1) * POOL_GROUP_DIM], wpool_ref[g])
                 for g in range(len(POOL_WINDOWS))], axis=-1) * ps_ref[...]
            merged = ga_ref[rows, :] * o_a + gb_ref[rows, :] * o_b
            x1 = x_ref[rows, :] + mod_ref[0, 2:3, :] * _dot(merged.astype(BF16), wout_ref[...])
            x1_ref[rows, :] = x1
            h2 = _rms(x1, g2_ref[...]) * (1.0 + mod_ref[0, 4:5, :]) + mod_ref[0, 3:4, :]
            h2_ref[rows, :] = h2
            logits = _dot3(h2, wr_ref[...])
            e = jnp.exp(logits - jnp.max(logits, axis=-1, keepdims=True))
            aff_ref[rows, :] = e / jnp.sum(e, axis=-1, keepdims=True)

    @pl.when(is_ctx)
    def _():
        block(xc_ref, attc_ref, plc_ref, gac_ref, gbc_ref)

    @pl.when(jnp.logical_not(is_ctx))
    def _():
        block(xl_ref, attl_ref, pll_ref, gal_ref, gbl_ref)


def _merge(ctx, lat, mods, w_att, w_pool, pool_scale, w_out, norm2_g, w_router, lat_blocks_per_seq):
    n_c, d = ctx[0].shape
    n = n_c + lat[0].shape[0]
    tb = MERGE_BLOCK
    cb = n_c // tb
    crow = lambda w: pl.BlockSpec((tb, w), lambda i: (jnp.minimum(i, cb - 1), 0))
    lrow = lambda w: pl.BlockSpec((tb, w), lambda i: (jnp.maximum(i - cb, 0), 0))
    orow = lambda w: pl.BlockSpec((tb, w), lambda i: (i, 0))
    full = lambda a: pl.BlockSpec(a.shape, lambda i: (0,) * a.ndim)
    cond = lambda i: jnp.where(i < cb, 0, 1 + (i - cb) // lat_blocks_per_seq)
    pairs, specs = [], []
    for a_c, a_l in zip(ctx, lat):
        pairs += [a_c, a_l]
        specs += [crow(a_c.shape[1]), lrow(a_l.shape[1])]
    return pl.pallas_call(
        functools.partial(_merge_kernel, ctx_blocks=cb),
        grid=(n // tb,),
        in_specs=specs + [pl.BlockSpec((1, N_MOD, d), lambda i: (cond(i), 0, 0)),
                          full(w_att), full(w_pool), full(pool_scale), full(w_out), full(norm2_g), full(w_router)],
        out_specs=[orow(d), orow(d), orow(N_EXPERTS)],
        out_shape=[jax.ShapeDtypeStruct((n, d), F32), jax.ShapeDtypeStruct((n, d), F32),
                   jax.ShapeDtypeStruct((n, N_EXPERTS), F32)],
        compiler_params=_params(("arbitrary",)),
        name="merge",
    )(*pairs, mods, w_att, w_pool, pool_scale, w_out, norm2_g, w_router)


def _prefix_incl(mask_f32, out_ref):
    e, n = mask_f32.shape
    i = lax.broadcasted_iota(I32, (LANES, LANES), 0)
    j = lax.broadcasted_iota(I32, (LANES, LANES), 1)
    tri = jnp.where(i <= j, 1.0, 0.0).astype(BF16)
    carry = jnp.zeros((e, 1), F32)
    for c in range(n // LANES):
        lanes = slice(c * LANES, (c + 1) * LANES)
        inc = _dot(mask_f32[:, lanes].astype(BF16), tri) + carry
        out_ref[:, lanes] = inc.astype(I32)
        carry = inc[:, LANES - 1:LANES]


def _route_kernel(aff_ref, sel_ref, rinc_ref, tmp_ref, *, cap):
    aff = aff_ref[...]

    def bit_step(i, thr):
        cand = thr | jnp.left_shift(jnp.int32(1), 30 - i)
        cnt = jnp.sum(jnp.where(aff >= pltpu.bitcast(cand, F32), 1, 0), axis=1, keepdims=True)
        return jnp.where(cnt >= cap, cand, thr)

    thr = lax.fori_loop(0, 31, bit_step, jnp.zeros((aff.shape[0], 1), I32))
    gt = aff >= pltpu.bitcast(thr + 1, F32)
    eq = (aff >= pltpu.bitcast(thr, F32)) & jnp.logical_not(gt)
    need = cap - jnp.sum(jnp.where(gt, 1, 0), axis=1, keepdims=True)
    _prefix_incl(jnp.where(eq, 1.0, 0.0), tmp_ref)
    sel = gt | (eq & (tmp_ref[...] <= need))
    sel_ref[...] = jnp.where(sel, 1, 0)
    _prefix_incl(jnp.where(sel, 1.0, 0.0), rinc_ref)


def _route(aff_t, cap):
    e, n = aff_t.shape
    full = pl.BlockSpec((e, n), lambda: (0, 0))
    return pl.pallas_call(
        functools.partial(_route_kernel, cap=cap),
        in_specs=[full], out_specs=[full, full],
        out_shape=[jax.ShapeDtypeStruct((e, n), I32)] * 2,
        scratch_shapes=[pltpu.VMEM((e, n), I32)],
        compiler_params=pltpu.CompilerParams(vmem_limit_bytes=VMEM_LIMIT),
        name="route",
    )(aff_t)


def _moe_kernel(gidx_ref, qpos_ref, h_ref, gate_ref, wg_ref, wu_ref, wd_ref, z_ref,
                xf_ref, xb_ref, acc_ref, y_ref, sem_g, sem_s, *, m_tot, n_f, step_rows, z_zero_rows):
    e = pl.program_id(0)
    j = pl.program_id(1)
    n_e = pl.num_programs(0)
    m_iss = step_rows * n_f
    tf = wg_ref.shape[1]
    slot = e % 2

    def gather_row(lst, s, dst_slot):
        t = gidx_ref[lst * m_iss + s]
        return pltpu.make_async_copy(h_ref.at[pl.ds(t, 1)], xf_ref.at[dst_slot, pl.ds(s, 1)], sem_g.at[dst_slot])

    def scatter_row(lst, s):
        q = qpos_ref[lst * m_iss + s]
        return pltpu.make_async_copy(y_ref.at[pl.ds(s, 1)], z_ref.at[pl.ds(q, 1)], sem_s)

    def wait_gathers(dst_slot):
        pltpu.make_async_copy(h_ref.at[pl.ds(0, m_iss)], xf_ref.at[dst_slot], sem_g.at[dst_slot]).wait()

    def wait_scatters():
        pltpu.make_async_copy(y_ref, z_ref.at[pl.ds(0, m_iss)], sem_s).wait()

    @pl.when((e == 0) & (j == 0))
    def _():
        y_ref[...] = jnp.zeros_like(y_ref)
        cp = pltpu.make_async_copy(y_ref.at[pl.ds(0, z_zero_rows)], z_ref.at[pl.ds(n_e * m_tot, z_zero_rows)], sem_s)
        cp.start()
        cp.wait()

        def first(g, c):
            for u in range(ROW_ALIGN):
                gather_row(0, g * ROW_ALIGN + u, 0).start()
            return c

        lax.fori_loop(0, m_iss // ROW_ALIGN, first, 0)

    @pl.when(j == 0)
    def _():
        wait_gathers(slot)
        xb_ref[...] = xf_ref[slot, 0:m_tot, :].astype(BF16)
        acc_ref[...] = jnp.zeros_like(acc_ref)

    nxt = lax.rem(e + 1, n_e)
    for u in range(step_rows):
        s = j * step_rows + u
        gather_row(nxt, s, 1 - slot).start()
        scatter_row(e, s).start()

    valid = D_EXPERT - j * tf
    cmask = lax.broadcasted_iota(I32, (1, tf), 1) < valid
    rmask = lax.broadcasted_iota(I32, (tf, 1), 0) < valid
    wg = wg_ref[0].astype(BF16)
    wu = wu_ref[0].astype(BF16)
    wd = jnp.where(rmask, wd_ref[0], 0.0).astype(BF16)
    for m in range(m_tot // EXPERT_M_BLOCK):
        rs = slice(m * EXPERT_M_BLOCK, (m + 1) * EXPERT_M_BLOCK)
        xs = xb_ref[rs, :]
        a = _dot_nt(xs, wg)
        b = _dot_nt(xs, wu)
        hm = jnp.where(cmask, a * _sigmoid(a) * b, 0.0).astype(BF16)
        acc_ref[rs, :] += _dot(hm, wd)

    @pl.when(j == n_f - 1)
    def _():
        wait_scatters()
        y_ref[0:m_tot, :] = _pack_halves(acc_ref[...] * gate_ref[0])

        @pl.when(e == n_e - 1)
        def _():
            def last(g, c):
                for u in range(ROW_ALIGN):
                    scatter_row(n_e, g * ROW_ALIGN + u).start()
                return c

            lax.fori_loop(0, m_iss // ROW_ALIGN, last, 0)
            wait_scatters()
            wait_gathers(1 - slot)


def _moe_step_rows(m_tot, n_f):
    return pl.cdiv(pl.cdiv(m_tot, n_f), ROW_ALIGN) * ROW_ALIGN


def _moe(gidx, qpos, h, gate, w_gate, w_up, w_down, m_tot, z_rows, z_zero_rows):
    d = h.shape[1]
    n_e = w_down.shape[0]
    tf = EXPERT_F_BLOCK
    n_f = pl.cdiv(D_EXPERT, tf)
    step_rows = _moe_step_rows(m_tot, n_f)
    m_iss = step_rows * n_f
    weights = pl.BlockSpec((1, tf, d), lambda e, j, *_: (e, j, 0))
    grid_spec = pltpu.PrefetchScalarGridSpec(
        num_scalar_prefetch=2,
        grid=(n_e, n_f),
        in_specs=[pl.BlockSpec(memory_space=pl.ANY),
                  pl.BlockSpec((1, m_tot, 1), lambda e, j, *_: (e, 0, 0)),
                  weights, weights, weights],
        out_specs=pl.BlockSpec(memory_space=pl.ANY),
        scratch_shapes=[pltpu.VMEM((2, m_iss, d), F32), pltpu.VMEM((m_tot, d), BF16), pltpu.VMEM((m_tot, d), F32),
                        pltpu.VMEM((m_iss, d // 2), jnp.uint32), pltpu.SemaphoreType.DMA((2,)),
                        pltpu.SemaphoreType.DMA],
    )
    return pl.pallas_call(
        functools.partial(_moe_kernel, m_tot=m_tot, n_f=n_f, step_rows=step_rows, z_zero_rows=z_zero_rows),
        grid_spec=grid_spec,
        out_shape=jax.ShapeDtypeStruct((z_rows, d // 2), jnp.uint32),
        compiler_params=_params(("arbitrary", "arbitrary")),
        name="moe",
    )(gidx, qpos, h, gate, w_gate, w_up, w_down)


def _combine_kernel(nc_ref, first_ref, start_ref, total_ref, z_ref, x1_ref, mod_ref, qt_ref, kt_ref, o_ref,
                    zbuf, sem, acc_ref):
    blk = pl.program_id(0)
    n_chunks = nc_ref[blk]
    first = first_ref[blk]
    total = total_ref[0]
    ahead = Z_BUFFERS - 1
    qt = qt_ref[...]
    kt = kt_ref[...]
    acc_ref[...] = jnp.zeros_like(acc_ref)

    def z_copy(g):
        rows = pl.ds(pl.multiple_of(start_ref[g], ROW_ALIGN), Z_CHUNK)
        slot = g % Z_BUFFERS
        return pltpu.make_async_copy(z_ref.at[rows], zbuf.at[slot], sem.at[slot])

    @pl.when(blk == 0)
    def _():
        for g in range(ahead):
            @pl.when(g < total)
            def _():
                z_copy(g).start()

    def chunk(c, carry):
        g = first + c
        slot = g % Z_BUFFERS

        @pl.when(g + ahead < total)
        def _():
            z_copy(g + ahead).start()

        z_copy(g).wait()
        r = start_ref[g] + lax.broadcasted_iota(I32, (qt.shape[0], Z_CHUNK), 1)
        onehot = jnp.where((r >= qt) & (r < qt + kt), 1.0, 0.0).astype(BF16)
        left, right = _unpack_halves(zbuf[slot])
        half = left.shape[1]
        acc_ref[:, :half] += _dot(onehot, left)
        acc_ref[:, half:] += _dot(onehot, right)
        return carry

    lax.fori_loop(0, n_chunks, chunk, 0)
    o_ref[...] = x1_ref[...] + mod_ref[0, 5:6, :] * acc_ref[...]


def _combine(chunks, z, x1, first_block, mods, qt, kt, cond_of_block):
    n = qt.shape[0]
    d = x1.shape[1]
    tb = COMBINE_BLOCK
    grid_spec = pltpu.PrefetchScalarGridSpec(
        num_scalar_prefetch=4,
        grid=(n // tb,),
        in_specs=[pl.BlockSpec(memory_space=pl.ANY),
                  pl.BlockSpec((tb, d), lambda i, *_: (i + first_block, 0)),
                  pl.BlockSpec((1, N_MOD, d), lambda i, *_: (cond_of_block(i), 0, 0)),
                  pl.BlockSpec((tb, 1), lambda i, *_: (i, 0)),
                  pl.BlockSpec((tb, 1), lambda i, *_: (i, 0))],
        out_specs=pl.BlockSpec((tb, d), lambda i, *_: (i, 0)),
        scratch_shapes=[pltpu.VMEM((Z_BUFFERS, Z_CHUNK, z.shape[1]), z.dtype), pltpu.SemaphoreType.DMA((Z_BUFFERS,)),
                        pltpu.VMEM((tb, d), F32)],
    )
    return pl.pallas_call(
        _combine_kernel,
        grid_spec=grid_spec,
        out_shape=jax.ShapeDtypeStruct((n, d), F32),
        compiler_params=_params(("arbitrary",)),
        name="combine",
    )(*chunks, z, x1, mods, qt, kt)


def _routing_tables(aff, cap, z_off):
    n = aff.shape[0]
    aff_t = aff.T
    sel, rinc = _route(aff_t, cap)
    slots = jnp.arange(cap, dtype=I32)
    chunk = LANES
    rinc3 = rinc.reshape(rinc.shape[0], n // chunk, chunk)
    n_full = jnp.sum((rinc3[:, None, :, chunk - 1] <= slots[None, :, None]).astype(I32), axis=-1)
    pick = (n_full[:, :, None] == jnp.arange(n // chunk, dtype=I32)[None, None, :]).astype(F32)
    inside = jnp.einsum("eca,eab->ecb", pick, rinc3.astype(F32), precision=lax.Precision.HIGHEST)
    idx = n_full * chunk + jnp.sum((inside <= slots[None, :, None].astype(F32)).astype(I32), axis=-1)
    gate = jnp.take_along_axis(aff_t, idx, axis=1)
    k_tok = jnp.sum(sel, axis=0)
    q_tok = jnp.cumsum(k_tok) - k_tok + z_off
    before = jnp.cumsum(sel, axis=0) - sel
    qpos = jnp.take_along_axis(q_tok[None, :] + before, idx, axis=1)
    nblk = n // COMBINE_BLOCK
    q_blk = q_tok[::COMBINE_BLOCK]
    end_blk = q_blk + jnp.sum(k_tok.reshape(nblk, COMBINE_BLOCK), axis=1)
    w0 = (q_blk // ROW_ALIGN) * ROW_ALIGN
    nc = jnp.maximum((end_blk - w0 + Z_CHUNK - 1) // Z_CHUNK, 1).astype(I32)
    first = jnp.cumsum(nc) - nc
    g = jnp.arange(nblk + (N_EXPERTS * cap) // Z_CHUNK + nblk, dtype=I32)
    blk_of = jnp.sum((first[None, 1:] <= g[:, None]).astype(I32), axis=1)
    start = jnp.where(g < jnp.sum(nc), w0[blk_of] + (g - first[blk_of]) * Z_CHUNK, 0)
    chunks = (nc, first.astype(I32), start.astype(I32), jnp.sum(nc).astype(I32)[None])
    return idx, gate, qpos.astype(I32), chunks, q_tok.astype(I32)[:, None], k_tok.astype(I32)[:, None]


def kernel(x_prompt, x_sample, cache_k, cache_v, c, c_ctx, norm1_g, norm2_g, w_ada, b_ada, w_in, q_norm_g, k_norm_g,
           rpb, w_att_proj, w_pool, pool_scale, w_out, w_router, w_gate_e, w_up_e, w_down_e):
    assert w_ada.shape[0] == 1, "single trunk layer"
    nb, ls, d = x_prompt.shape
    db, ll, _ = x_sample.shape
    n_c, n_l = nb * ls, db * ll
    assert ls == TOKEN_BLOCK, "context keys / values are written one sequence per projection block"

    cond_rows = pl.cdiv(1 + db, ROW_ALIGN) * ROW_ALIGN
    cond = jnp.concatenate([c_ctx[None, :], c, jnp.zeros((cond_rows - 1 - db, d), F32)], axis=0)
    mods = _ada(cond, w_ada[0], b_ada).reshape(cond_rows, N_MOD, d)

    w_in_b = w_in[0].astype(BF16)
    w_att_b = w_att_proj[0].astype(BF16)
    w_pool_b = w_pool[0].astype(BF16)
    w_out_b = w_out[0].astype(BF16)


    xc = x_prompt.reshape(n_c, d)
    xl = x_sample.reshape(n_l, d)
    q_gain = jnp.tile(q_norm_g, (1, NA_HEADS))
    k_gain = jnp.tile(k_norm_g, (1, NA_HEADS))
    att_c, kc, vc, pool_c, gac, gbc = _in_proj(xc, ls, mods, norm1_g, w_in_b, q_gain, k_gain, lambda i: 0, True)
    ql, kl, vl, pool_l, gal, gbl = _in_proj(xl, ll, mods, norm1_g, w_in_b, q_gain, k_gain, lambda i: 1 + i, False)

    seq_l = lambda a: a.reshape(db, ll, a.shape[-1])
    bias = _window_bias(rpb[0], ll // GRID_W)
    att_l = _attn_lat(seq_l(ql), seq_l(kl), seq_l(vl), jnp.transpose(cache_k[:, 0], (0, 2, 3, 1)),
                      jnp.transpose(cache_v[:, 0], (0, 2, 3, 1)), bias)

    x1, h2, aff = _merge((xc, att_c, pool_c, gac, gbc),
                         (xl, att_l.reshape(n_l, NA_WIDTH), pool_l, gal, gbl),
                         mods, w_att_b, w_pool_b, pool_scale, w_out_b, norm2_g, w_router[0], ll // MERGE_BLOCK)

    cap_c = max(1, (CAPACITY_FACTOR * n_c) // N_EXPERTS)
    cap_l = max(1, (CAPACITY_FACTOR * n_l) // N_EXPERTS)
    m_tot = cap_c + cap_l
    m_iss = _moe_step_rows(m_tot, pl.cdiv(D_EXPERT, EXPERT_F_BLOCK)) * pl.cdiv(D_EXPERT, EXPERT_F_BLOCK)
    z_valid = N_EXPERTS * m_tot
    z_zero_rows = Z_CHUNK + ROW_ALIGN
    z_spare = z_valid + z_zero_rows
    z_rows = z_spare + m_iss
    idx_c, gate_c, qpos_c, chunks_c, qt_c, kt_c = _routing_tables(aff[:n_c], cap_c, 0)
    idx_l, gate_l, qpos_l, chunks_l, qt_l, kt_l = _routing_tables(aff[n_c:], cap_l, N_EXPERTS * cap_c)
    spare = jnp.broadcast_to(z_spare + jnp.arange(m_iss, dtype=I32), (N_EXPERTS + 1, m_iss))
    gidx = jnp.concatenate([idx_c, idx_l + n_c, jnp.zeros((N_EXPERTS, m_iss - m_tot), I32)], axis=1).reshape(-1)
    qpos = jnp.concatenate([spare[:1], jnp.concatenate([qpos_c, qpos_l, spare[1:, m_tot:]], axis=1)], axis=0).reshape(-1)
    gate = jnp.concatenate([gate_c, gate_l], axis=1)[:, :, None]

    z = _moe(gidx, qpos, h2, gate, jnp.swapaxes(w_gate_e[0], 1, 2), jnp.swapaxes(w_up_e[0], 1, 2), w_down_e[0],
             m_tot, z_rows, z_zero_rows)

    y_c = _combine(chunks_c, z, x1, 0, mods, qt_c, kt_c, lambda i: 0)
    y_l = _combine(chunks_l, z, x1, n_c // COMBINE_BLOCK, mods, qt_l, kt_l, lambda i: 1 + i // (ll // COMBINE_BLOCK))

    state_k = jnp.transpose(kc, (0, 3, 1, 2))[:, None]
    state_v = jnp.transpose(vc, (0, 3, 1, 2))[:, None]
    return (y_c.reshape(nb, ls, d), y_l.reshape(db, ll, d), state_k, state_v)
```

```python
import functools

import jax
import jax.numpy as jnp
from jax import lax
from jax.experimental import pallas as pl
from jax.experimental.pallas import tpu as pltpu

F32 = jnp.float32
BF16 = jnp.bfloat16
I32 = jnp.int32

GRID_W = 64
NA_HEADS = 8
HEAD_DIM = 64
NA_WIDTH = NA_HEADS * HEAD_DIM
WIN_ROWS = 8
WIN_COLS = 16
POOL_WINDOWS = (2, 4, 8, 16)
POOL_GROUP_DIM = 128
POOL_WIDTH = len(POOL_WINDOWS) * POOL_GROUP_DIM
N_EXPERTS = 16
CAPACITY_FACTOR = 2
D_EXPERT = 2752
N_MOD = 6
EPS = 1e-6
NEG_INF = -1e30

TOKEN_BLOCK = 256
COMBINE_BLOCK = 256
MERGE_BLOCK = 512
MERGE_ROWS = 256
LAT_ROW_BLOCK = 4
EXPERT_F_BLOCK = 256
EXPERT_M_BLOCK = 768
Z_CHUNK = 256
Z_BUFFERS = 6
HEAD_AVG_WIDTH = 256
ROW_ALIGN = 8
LANES = 128
VMEM_LIMIT = 56 * 1024 * 1024


def _dot(a, b):
    return jnp.dot(a, b, preferred_element_type=F32)


def _dot_nt(a, b):
    return lax.dot_general(a, b, (((1,), (1,)), ((), ())), preferred_element_type=F32)


def _dot3(a, b):
    a_hi = a.astype(BF16)
    a_lo = (a - a_hi.astype(F32)).astype(BF16)
    b_hi = b.astype(BF16)
    b_lo = (b - b_hi.astype(F32)).astype(BF16)
    return _dot(a_hi, b_hi) + _dot(a_hi, b_lo) + _dot(a_lo, b_hi)


def _sigmoid(x):
    return 0.5 * jnp.tanh(0.5 * x) + 0.5


def _rms(x, g):
    return x * lax.rsqrt(jnp.mean(x * x, axis=-1, keepdims=True) + EPS) * g


def _pack_halves(x):
    w = x.shape[1] // 2
    return pltpu.pack_elementwise([x[:, :w], x[:, w:]], packed_dtype=BF16)


def _unpack_halves(p):
    return tuple(pltpu.unpack_elementwise(p, index=i, packed_dtype=BF16, unpacked_dtype=F32).astype(BF16)
                 for i in range(2))


def _params(sem):
    return pltpu.CompilerParams(dimension_semantics=sem, vmem_limit_bytes=VMEM_LIMIT)


def _ada_kernel(cond_ref, w_ref, b_ref, o_ref):
    c = cond_ref[...]
    o_ref[...] = _dot3(c * _sigmoid(c), w_ref[...]) + b_ref[...]


def _ada(cond, w_ada, b_ada):
    rows, d = cond.shape
    n_out = w_ada.shape[1]
    bn = 1024
    return pl.pallas_call(
        _ada_kernel,
        grid=(n_out // bn,),
        in_specs=[pl.BlockSpec((rows, d), lambda i: (0, 0)),
                  pl.BlockSpec((d, bn), lambda i: (0, i)),
                  pl.BlockSpec((1, bn), lambda i: (0, i))],
        out_specs=pl.BlockSpec((rows, bn), lambda i: (0, i)),
        out_shape=jax.ShapeDtypeStruct((rows, n_out), F32),
        compiler_params=_params(("arbitrary",)),
        name="ada",
    )(cond, w_ada, b_ada)


_IN_CUTS = (0, 512, 1024, 1536, 2048, 3072, 4096)


def _head_avg_matrix():
    head = jnp.arange(HEAD_AVG_WIDTH, dtype=I32) // HEAD_DIM
    return jnp.where(head[:, None] == head[None, :], 1.0 / HEAD_DIM, 0.0).astype(BF16)


def _head_rms(x, g, avg, two_term):
    x2 = x * x
    means = []
    for c in range(0, x.shape[1], HEAD_AVG_WIDTH):
        blk = x2[:, c:c + HEAD_AVG_WIDTH]
        hi = blk.astype(BF16)
        ms = _dot(hi, avg)
        if two_term:
            ms = ms + _dot((blk - hi.astype(F32)).astype(BF16), avg)
        means.append(ms)
    return x * lax.rsqrt(jnp.concatenate(means, axis=-1) + EPS) * g


def _pooled(u):
    l = u.shape[0]
    t = lax.broadcasted_iota(I32, (l, POOL_GROUP_DIM), 0)
    groups = []
    for g, w in enumerate(POOL_WINDOWS):
        x = u[:, g * POOL_GROUP_DIM:(g + 1) * POOL_GROUP_DIM]
        acc = x
        for k in range(-(w // 2), w - w // 2):
            if k == 0:
                continue
            shifted = pltpu.roll(x, (-k) % l, 0)
            acc = acc + jnp.where((t + k >= 0) & (t + k < l), shifted, 0.0)
        cnt = (jnp.minimum(t + (w - w // 2), l) - jnp.maximum(t - w // 2, 0)).astype(F32)
        groups.append(acc / cnt - x)
    return jnp.concatenate(groups, axis=-1)


def _with_ones(v):
    lane = lax.broadcasted_iota(I32, v.shape, 1)
    return jnp.concatenate([v.astype(BF16), jnp.where(lane == 0, 1.0, 0.0).astype(BF16)], axis=-1)


def _in_kernel(x_ref, mod_ref, g_ref, w_ref, qg_ref, kg_ref, avg_ref, q_ref, k_ref, v_ref, u_ref, ga_ref, gb_ref,
               *, kv_channel_major):
    us = []
    for r0 in range(0, x_ref.shape[0], TOKEN_BLOCK):
        rows = slice(r0, r0 + TOKEN_BLOCK)
        y = _rms(x_ref[rows, :], g_ref[...])
        h = (y * (1.0 + mod_ref[0, 1:2, :]) + mod_ref[0, 0:1, :]).astype(BF16)
        part = lambda i: _dot(h, w_ref[:, _IN_CUTS[i]:_IN_CUTS[i + 1]])
        qn = (_head_rms(part(0), qg_ref[...], avg_ref[...], False) * HEAD_DIM ** -0.5).astype(BF16)
        kn = _head_rms(part(1), kg_ref[...], avg_ref[...], True)
        v = part(2)
        if kv_channel_major:
            k_ref[0] = kn.T.reshape(k_ref.shape[1:])
            v_ref[0] = v.T.reshape(v_ref.shape[1:])
            knb = kn.astype(BF16)
            for hd in range(NA_HEADS):
                sl = slice(hd * HEAD_DIM, (hd + 1) * HEAD_DIM)
                s = _dot_nt(qn[:, sl], knb[:, sl])
                e = jnp.exp(s - jnp.max(s, axis=-1, keepdims=True)).astype(BF16)
                o = _dot(e, _with_ones(v[:, sl]))
                q_ref[rows, sl] = (o[:, :HEAD_DIM] / o[:, HEAD_DIM:HEAD_DIM + 1]).astype(q_ref.dtype)
        else:
            q_ref[rows, :] = qn
            k_ref[rows, :] = kn.astype(k_ref.dtype)
            v_ref[rows, :] = v.astype(v_ref.dtype)
        us.append(part(3))
        ga_ref[rows, :] = _sigmoid(part(4)).astype(ga_ref.dtype)
        gb_ref[rows, :] = _sigmoid(part(5)).astype(gb_ref.dtype)
    u_ref[...] = _pooled(jnp.concatenate(us, axis=0)).astype(u_ref.dtype)


def _in_proj(x2d, seq_len, mods, norm_g, w_in_bf16, q_gain, k_gain, cond_of_seq, kv_results):
    n, d = x2d.shape
    tb = seq_len
    assert not kv_results or tb == TOKEN_BLOCK
    widths = [hi - lo for lo, hi in zip(_IN_CUTS[:-1], _IN_CUTS[1:])]
    out_specs = [pl.BlockSpec((tb, w), lambda i: (i, 0)) for w in widths]
    out_shape = [jax.ShapeDtypeStruct((n, w), BF16) for w in widths]
    if kv_results:
        for i in (1, 2):
            out_specs[i] = pl.BlockSpec((1, NA_HEADS, HEAD_DIM, tb), lambda i: (i, 0, 0, 0))
            out_shape[i] = jax.ShapeDtypeStruct((n // tb, NA_HEADS, HEAD_DIM, tb), F32)
    full = lambda a: pl.BlockSpec(a.shape, lambda i: (0,) * a.ndim)
    return pl.pallas_call(
        functools.partial(_in_kernel, kv_channel_major=kv_results),
        grid=(n // tb,),
        in_specs=[pl.BlockSpec((tb, d), lambda i: (i, 0)),
                  pl.BlockSpec((1, N_MOD, d), lambda i: (cond_of_seq(i), 0, 0)),
                  full(norm_g), pl.BlockSpec(w_in_bf16.shape, lambda i: (0, 0), pipeline_mode=pl.Buffered(1)),
                  full(q_gain), full(k_gain),
                  pl.BlockSpec((HEAD_AVG_WIDTH, HEAD_AVG_WIDTH), lambda i: (0, 0))],
        out_specs=out_specs,
        out_shape=out_shape,
        compiler_params=_params(("arbitrary",)),
        name="in_proj",
    )(x2d, mods, norm_g, w_in_bf16, q_gain, k_gain, _head_avg_matrix())


def _lat_windows(rows):
    kr = min(WIN_ROWS, rows)
    rb = min(LAT_ROW_BLOCK, rows)
    wr = min(kr + rb, rows)
    assert rows % rb == 0
    starts = [min(max(i * rb - kr // 2, 0), rows - wr) for i in range(rows // rb)]
    for i, ws in enumerate(starts):
        for r in range(i * rb, (i + 1) * rb):
            rs = min(max(r - kr // 2, 0), rows - kr)
            assert ws <= rs and rs + kr <= ws + wr
    return kr, rb, wr, starts


def _attn_lat_kernel(q_ref, k_ref, v_ref, ckt_ref, cvt_ref, bias_ref, att_ref,
                     kn_s, vb_s, ckb_s, cvb_s, *, rows, kr, rb, wr):
    i = pl.program_id(1)

    @pl.when(i == 0)
    def _():
        lc = ckt_ref.shape[3]
        ones_row = jnp.where(lax.broadcasted_iota(I32, (HEAD_DIM, lc), 0) == 0, 1.0, 0.0)
        for h in range(NA_HEADS):
            sl = slice(h * HEAD_DIM, (h + 1) * HEAD_DIM)
            kn_s[h] = k_ref[0, :, sl]
            vb_s[h] = _with_ones(v_ref[0, :, sl])
            ckb_s[h] = ckt_ref[0, h].astype(BF16)
            cvb_s[h] = jnp.concatenate([cvt_ref[0, h], ones_row], axis=0).astype(BF16)

    win_start = jnp.clip(i * rb - kr // 2, 0, rows - wr)
    key_rows = pl.ds(pl.multiple_of(win_start * GRID_W, GRID_W), wr * GRID_W)
    qn_all = q_ref[0]
    for h in range(NA_HEADS):
        sl = slice(h * HEAD_DIM, (h + 1) * HEAD_DIM)
        qn = qn_all[:, sl]
        bias = bias_ref[h, pl.ds(i * rb, rb)].reshape(rb * GRID_W, wr * GRID_W)
        s_loc = _dot_nt(qn, kn_s[h, key_rows, :]) + bias
        s_ctx = _dot(qn, ckb_s[h])
        m = jnp.maximum(jnp.max(s_loc, axis=-1, keepdims=True), jnp.max(s_ctx, axis=-1, keepdims=True))
        o = (_dot(jnp.exp(s_loc - m).astype(BF16), vb_s[h, key_rows, :])
             + _dot_nt(jnp.exp(s_ctx - m).astype(BF16), cvb_s[h]))
        att_ref[0, :, sl] = (o[:, :HEAD_DIM] / o[:, HEAD_DIM:HEAD_DIM + 1]).astype(att_ref.dtype)


def _attn_lat(q, k, v, ckt, cvt, bias):
    b, l, w = q.shape
    lc = ckt.shape[3]
    rows = l // GRID_W
    kr, rb, wr, _ = _lat_windows(rows)
    seq = pl.BlockSpec((1, l, w), lambda i, r: (i, 0, 0))
    ctx = pl.BlockSpec((1, NA_HEADS, HEAD_DIM, lc), lambda i, r: (i, 0, 0, 0))
    rowblk = pl.BlockSpec((1, rb * GRID_W, w), lambda i, r: (i, r, 0))
    return pl.pallas_call(
        functools.partial(_attn_lat_kernel, rows=rows, kr=kr, rb=rb, wr=wr),
        grid=(b, rows // rb),
        in_specs=[rowblk, seq, seq, ctx, ctx,
                  pl.BlockSpec(bias.shape, lambda i, r: (0, 0, 0, 0), pipeline_mode=pl.Buffered(1))],
        out_specs=rowblk,
        out_shape=jax.ShapeDtypeStruct((b, l, w), BF16),
        scratch_shapes=[pltpu.VMEM((NA_HEADS, l, HEAD_DIM), BF16), pltpu.VMEM((NA_HEADS, l, 2 * HEAD_DIM), BF16),
                        pltpu.VMEM((NA_HEADS, HEAD_DIM, lc), BF16), pltpu.VMEM((NA_HEADS, 2 * HEAD_DIM, lc), BF16)],
        compiler_params=_params(("arbitrary", "arbitrary")),
        name="attn_lat",
    )(q, k, v, ckt, cvt, bias)


def _bias_kernel(rpb_ref, o_ref, *, rows):
    kr, rb, wr, starts = _lat_windows(rows)
    h = pl.program_id(0)
    n_ro, n_co = 2 * WIN_ROWS - 1, 2 * WIN_COLS - 1
    wq = lax.broadcasted_iota(I32, (GRID_W, GRID_W), 0)
    wk = lax.broadcasted_iota(I32, (GRID_W, GRID_W), 1)
    co = jnp.clip(wk - wq, -(WIN_COLS - 1), WIN_COLS - 1) + (WIN_COLS - 1)
    col_start = jnp.clip(wq - WIN_COLS // 2, 0, GRID_W - WIN_COLS)
    in_win = (wk >= col_start) & (wk < col_start + WIN_COLS)
    co = jnp.where(in_win, co, -1)
    masked = jnp.full((GRID_W, GRID_W), NEG_INF, F32)
    tables = {}
    for r in range(rows):
        row_start = min(max(r - kr // 2, 0), rows - kr)
        for j in range(wr):
            key_row = starts[r // rb] + j
            blk = masked
            if row_start <= key_row < row_start + kr:
                ro = key_row - r + (WIN_ROWS - 1)
                if ro not in tables:
                    t = masked
                    for c in range(n_co):
                        t = jnp.where(co == c, rpb_ref[(h * n_ro + ro) * n_co + c], t)
                    tables[ro] = t
                blk = tables[ro]
            o_ref[0, r, :, j * GRID_W:(j + 1) * GRID_W] = blk


def _window_bias(rpb, rows):
    _, _, wr, _ = _lat_windows(rows)
    nh = rpb.shape[0]
    return pl.pallas_call(
        functools.partial(_bias_kernel, rows=rows),
        grid=(nh,),
        in_specs=[pl.BlockSpec(memory_space=pltpu.SMEM)],
        out_specs=pl.BlockSpec((1, rows, GRID_W, wr * GRID_W), lambda h: (h, 0, 0, 0)),
        out_shape=jax.ShapeDtypeStruct((nh, rows, GRID_W, wr * GRID_W), F32),
        compiler_params=_params(("arbitrary",)),
        name="window_bias",
    )(rpb.reshape(-1))


def _merge_kernel(xc_ref, xl_ref, attc_ref, attl_ref, plc_ref, pll_ref, gac_ref, gal_ref, gbc_ref, gbl_ref,
                  mod_ref, watt_ref, wpool_ref, ps_ref, wout_ref, g2_ref, wr_ref, x1_ref, h2_ref, aff_ref, *, ctx_blocks):
    is_ctx = pl.program_id(0) < ctx_blocks

    def block(x_ref, att_ref, pl_ref, ga_ref, gb_ref):
        for r0 in range(0, x_ref.shape[0], MERGE_ROWS):
            rows = slice(r0, r0 + MERGE_ROWS)
            o_a = _dot(att_ref[rows, :], watt_ref[...])
            o_b = jnp.concatenate(
                [_dot(pl_ref[rows, g * POOL_GROUP_DIM:(g + 1) * POOL_GROUP_DIM], wpool_ref[g])
                 for g in range(len(POOL_WINDOWS))], axis=-1) * ps_ref[...]
            merged = ga_ref[rows, :] * o_a + gb_ref[rows, :] * o_b
            x1 = x_ref[rows, :] + mod_ref[0, 2:3, :] * _dot(merged.astype(BF16), wout_ref[...])
            x1_ref[rows, :] = x1
            h2 = _rms(x1, g2_ref[...]) * (1.0 + mod_ref[0, 4:5, :]) + mod_ref[0, 3:4, :]
            h2_ref[rows, :] = h2
            logits = _dot3(h2, wr_ref[...])
            e = jnp.exp(logits - jnp.max(logits, axis=-1, keepdims=True))
            aff_ref[rows, :] = e / jnp.sum(e, axis=-1, keepdims=True)

    @pl.when(is_ctx)
    def _():
        block(xc_ref, attc_ref, plc_ref, gac_ref, gbc_ref)

    @pl.when(jnp.logical_not(is_ctx))
    def _():
        block(xl_ref, attl_ref, pll_ref, gal_ref, gbl_ref)


def _merge(ctx, lat, mods, w_att, w_pool, pool_scale, w_out, norm2_g, w_router, lat_blocks_per_seq):
    n_c, d = ctx[0].shape
    n = n_c + lat[0].shape[0]
    tb = MERGE_BLOCK
    cb = n_c // tb
    crow = lambda w: pl.BlockSpec((tb, w), lambda i: (jnp.minimum(i, cb - 1), 0))
    lrow = lambda w: pl.BlockSpec((tb, w), lambda i: (jnp.maximum(i - cb, 0), 0))
    orow = lambda w: pl.BlockSpec((tb, w), lambda i: (i, 0))
    full = lambda a: pl.BlockSpec(a.shape, lambda i: (0,) * a.ndim)
    cond = lambda i: jnp.where(i < cb, 0, 1 + (i - cb) // lat_blocks_per_seq)
    pairs, specs = [], []
    for a_c, a_l in zip(ctx, lat):
        pairs += [a_c, a_l]
        specs += [crow(a_c.shape[1]), lrow(a_l.shape[1])]
    return pl.pallas_call(
        functools.partial(_merge_kernel, ctx_blocks=cb),
        grid=(n // tb,),
        in_specs=specs + [pl.BlockSpec((1, N_MOD, d), lambda i: (cond(i), 0, 0)),
                          full(w_att), full(w_pool), full(pool_scale), full(w_out), full(norm2_g), full(w_router)],
        out_specs=[orow(d), orow(d), orow(N_EXPERTS)],
        out_shape=[jax.ShapeDtypeStruct((n, d), F32), jax.ShapeDtypeStruct((n, d), F32),
                   jax.ShapeDtypeStruct((n, N_EXPERTS), F32)],
        compiler_params=_params(("arbitrary",)),
        name="merge",
    )(*pairs, mods, w_att, w_pool, pool_scale, w_out, norm2_g, w_router)


def _prefix_incl(mask_f32, out_ref):
    e, n = mask_f32.shape
    i = lax.broadcasted_iota(I32, (LANES, LANES), 0)
    j = lax.broadcasted_iota(I32, (LANES, LANES), 1)
    tri = jnp.where(i <= j, 1.0, 0.0).astype(BF16)
    carry = jnp.zeros((e, 1), F32)
    for c in range(n // LANES):
        lanes = slice(c * LANES, (c + 1) * LANES)
        inc = _dot(mask_f32[:, lanes].astype(BF16), tri) + carry
        out_ref[:, lanes] = inc.astype(I32)
        carry = inc[:, LANES - 1:LANES]


def _route_kernel(aff_ref, sel_ref, rinc_ref, tmp_ref, *, cap):
    aff = aff_ref[...]

    def bit_step(i, thr):
        cand = thr | jnp.left_shift(jnp.int32(1), 30 - i)
        cnt = jnp.sum(jnp.where(aff >= pltpu.bitcast(cand, F32), 1, 0), axis=1, keepdims=True)
        return jnp.where(cnt >= cap, cand, thr)

    thr = lax.fori_loop(0, 31, bit_step, jnp.zeros((aff.shape[0], 1), I32))
    gt = aff >= pltpu.bitcast(thr + 1, F32)
    eq = (aff >= pltpu.bitcast(thr, F32)) & jnp.logical_not(gt)
    need = cap - jnp.sum(jnp.where(gt, 1, 0), axis=1, keepdims=True)
    _prefix_incl(jnp.where(eq, 1.0, 0.0), tmp_ref)
    sel = gt | (eq & (tmp_ref[...] <= need))
    sel_ref[...] = jnp.where(sel, 1, 0)
    _prefix_incl(jnp.where(sel, 1.0, 0.0), rinc_ref)


def _route(aff_t, cap):
    e, n = aff_t.shape
    full = pl.BlockSpec((e, n), lambda: (0, 0))
    return pl.pallas_call(
        functools.partial(_route_kernel, cap=cap),
        in_specs=[full], out_specs=[full, full],
        out_shape=[jax.ShapeDtypeStruct((e, n), I32)] * 2,
        scratch_shapes=[pltpu.VMEM((e, n), I32)],
        compiler_params=pltpu.CompilerParams(vmem_limit_bytes=VMEM_LIMIT),
        name="route",
    )(aff_t)


def _moe_kernel(gidx_ref, qpos_ref, h_ref, gate_ref, wg_ref, wu_ref, wd_ref, z_ref,
                xf_ref, xb_ref, acc_ref, y_ref, sem_g, sem_s, *, m_tot, n_f, step_rows, z_zero_rows):
    e = pl.program_id(0)
    j = pl.program_id(1)
    n_e = pl.num_programs(0)
    m_iss = step_rows * n_f
    tf = wg_ref.shape[1]
    slot = e % 2

    def gather_row(lst, s, dst_slot):
        t = gidx_ref[lst * m_iss + s]
        return pltpu.make_async_copy(h_ref.at[pl.ds(t, 1)], xf_ref.at[dst_slot, pl.ds(s, 1)], sem_g.at[dst_slot])

    def scatter_row(lst, s):
        q = qpos_ref[lst * m_iss + s]
        return pltpu.make_async_copy(y_ref.at[pl.ds(s, 1)], z_ref.at[pl.ds(q, 1)], sem_s)

    def wait_gathers(dst_slot):
        pltpu.make_async_copy(h_ref.at[pl.ds(0, m_iss)], xf_ref.at[dst_slot], sem_g.at[dst_slot]).wait()

    def wait_scatters():
        pltpu.make_async_copy(y_ref, z_ref.at[pl.ds(0, m_iss)], sem_s).wait()

    @pl.when((e == 0) & (j == 0))
    def _():
        y_ref[...] = jnp.zeros_like(y_ref)
        cp = pltpu.make_async_copy(y_ref.at[pl.ds(0, z_zero_rows)], z_ref.at[pl.ds(n_e * m_tot, z_zero_rows)], sem_s)
        cp.start()
        cp.wait()

        def first(g, c):
            for u in range(ROW_ALIGN):
                gather_row(0, g * ROW_ALIGN + u, 0).start()
            return c

        lax.fori_loop(0, m_iss // ROW_ALIGN, first, 0)

    @pl.when(j == 0)
    def _():
        wait_gathers(slot)
        xb_ref[...] = xf_ref[slot, 0:m_tot, :].astype(BF16)
        acc_ref[...] = jnp.zeros_like(acc_ref)

    nxt = lax.rem(e + 1, n_e)
    for u in range(step_rows):
        s = j * step_rows + u
        gather_row(nxt, s, 1 - slot).start()
        scatter_row(e, s).start()

    valid = D_EXPERT - j * tf
    cmask = lax.broadcasted_iota(I32, (1, tf), 1) < valid
    rmask = lax.broadcasted_iota(I32, (tf, 1), 0) < valid
    wg = wg_ref[0].astype(BF16)
    wu = wu_ref[0].astype(BF16)
    wd = jnp.where(rmask, wd_ref[0], 0.0).astype(BF16)
    for m in range(m_tot // EXPERT_M_BLOCK):
        rs = slice(m * EXPERT_M_BLOCK, (m + 1) * EXPERT_M_BLOCK)
        xs = xb_ref[rs, :]
        a = _dot_nt(xs, wg)
        b = _dot_nt(xs, wu)
        hm = jnp.where(cmask, a * _sigmoid(a) * b, 0.0).astype(BF16)
        acc_ref[rs, :] += _dot(hm, wd)

    @pl.when(j == n_f - 1)
    def _():
        wait_scatters()
        y_ref[0:m_tot, :] = _pack_halves(acc_ref[...] * gate_ref[0])

        @pl.when(e == n_e - 1)
        def _():
            def last(g, c):
                for u in range(ROW_ALIGN):
                    scatter_row(n_e, g * ROW_ALIGN + u).start()
                return c

            lax.fori_loop(0, m_iss // ROW_ALIGN, last, 0)
            wait_scatters()
            wait_gathers(1 - slot)


def _moe_step_rows(m_tot, n_f):
    return pl.cdiv(pl.cdiv(m_tot, n_f), ROW_ALIGN) * ROW_ALIGN


def _moe(gidx, qpos, h, gate, w_gate, w_up, w_down, m_tot, z_rows, z_zero_rows):
    d = h.shape[1]
    n_e = w_down.shape[0]
    tf = EXPERT_F_BLOCK
    n_f = pl.cdiv(D_EXPERT, tf)
    step_rows = _moe_step_rows(m_tot, n_f)
    m_iss = step_rows * n_f
    weights = pl.BlockSpec((1, tf, d), lambda e, j, *_: (e, j, 0))
    grid_spec = pltpu.PrefetchScalarGridSpec(
        num_scalar_prefetch=2,
        grid=(n_e, n_f),
        in_specs=[pl.BlockSpec(memory_space=pl.ANY),
                  pl.BlockSpec((1, m_tot, 1), lambda e, j, *_: (e, 0, 0)),
                  weights, weights, weights],
        out_specs=pl.BlockSpec(memory_space=pl.ANY),
        scratch_shapes=[pltpu.VMEM((2, m_iss, d), F32), pltpu.VMEM((m_tot, d), BF16), pltpu.VMEM((m_tot, d), F32),
                        pltpu.VMEM((m_iss, d // 2), jnp.uint32), pltpu.SemaphoreType.DMA((2,)),
                        pltpu.SemaphoreType.DMA],
    )
    return pl.pallas_call(
        functools.partial(_moe_kernel, m_tot=m_tot, n_f=n_f, step_rows=step_rows, z_zero_rows=z_zero_rows),
        grid_spec=grid_spec,
        out_shape=jax.ShapeDtypeStruct((z_rows, d // 2), jnp.uint32),
        compiler_params=_params(("arbitrary", "arbitrary")),
        name="moe",
    )(gidx, qpos, h, gate, w_gate, w_up, w_down)


def _combine_kernel(nc_ref, first_ref, start_ref, total_ref, z_ref, x1_ref, mod_ref, qt_ref, kt_ref, o_ref,
                    zbuf, sem, acc_ref):
    blk = pl.program_id(0)
    n_chunks = nc_ref[blk]
    first = first_ref[blk]
    total = total_ref[0]
    ahead = Z_BUFFERS - 1
    qt = qt_ref[...]
    kt = kt_ref[...]
    acc_ref[...] = jnp.zeros_like(acc_ref)

    def z_copy(g):
        rows = pl.ds(pl.multiple_of(start_ref[g], ROW_ALIGN), Z_CHUNK)
        slot = g % Z_BUFFERS
        return pltpu.make_async_copy(z_ref.at[rows], zbuf.at[slot], sem.at[slot])

    @pl.when(blk == 0)
    def _():
        for g in range(ahead):
            @pl.when(g < total)
            def _():
                z_copy(g).start()

    def chunk(c, carry):
        g = first + c
        slot = g % Z_BUFFERS

        @pl.when(g + ahead < total)
        def _():
            z_copy(g + ahead).start()

        z_copy(g).wait()
        r = start_ref[g] + lax.broadcasted_iota(I32, (qt.shape[0], Z_CHUNK), 1)
        onehot = jnp.where((r >= qt) & (r < qt + kt), 1.0, 0.0).astype(BF16)
        left, right = _unpack_halves(zbuf[slot])
        half = left.shape[1]
        acc_ref[:, :half] += _dot(onehot, left)
        acc_ref[:, half:] += _dot(onehot, right)
        return carry

    lax.fori_loop(0, n_chunks, chunk, 0)
    o_ref[...] = x1_ref[...] + mod_ref[0, 5:6, :] * acc_ref[...]


def _combine(chunks, z, x1, first_block, mods, qt, kt, cond_of_block):
    n = qt.shape[0]
    d = x1.shape[1]
    tb = COMBINE_BLOCK
    grid_spec = pltpu.PrefetchScalarGridSpec(
        num_scalar_prefetch=4,
        grid=(n // tb,),
        in_specs=[pl.BlockSpec(memory_space=pl.ANY),
                  pl.BlockSpec((tb, d), lambda i, *_: (i + first_block, 0)),
                  pl.BlockSpec((1, N_MOD, d), lambda i, *_: (cond_of_block(i), 0, 0)),
                  pl.BlockSpec((tb, 1), lambda i, *_: (i, 0)),
                  pl.BlockSpec((tb, 1), lambda i, *_: (i, 0))],
        out_specs=pl.BlockSpec((tb, d), lambda i, *_: (i, 0)),
        scratch_shapes=[pltpu.VMEM((Z_BUFFERS, Z_CHUNK, z.shape[1]), z.dtype), pltpu.SemaphoreType.DMA((Z_BUFFERS,)),
                        pltpu.VMEM((tb, d), F32)],
    )
    return pl.pallas_call(
        _combine_kernel,
        grid_spec=grid_spec,
        out_shape=jax.ShapeDtypeStruct((n, d), F32),
        compiler_params=_params(("arbitrary",)),
        name="combine",
    )(*chunks, z, x1, mods, qt, kt)


def _routing_tables(aff, cap, z_off):
    n = aff.shape[0]
    aff_t = aff.T
    sel, rinc = _route(aff_t, cap)
    slots = jnp.arange(cap, dtype=I32)
    chunk = LANES
    rinc3 = rinc.reshape(rinc.shape[0], n // chunk, chunk)
    n_full = jnp.sum((rinc3[:, None, :, chunk - 1] <= slots[None, :, None]).astype(I32), axis=-1)
    pick = (n_full[:, :, None] == jnp.arange(n // chunk, dtype=I32)[None, None, :]).astype(F32)
    inside = jnp.einsum("eca,eab->ecb", pick, rinc3.astype(F32), precision=lax.Precision.HIGHEST)
    idx = n_full * chunk + jnp.sum((inside <= slots[None, :, None].astype(F32)).astype(I32), axis=-1)
    gate = jnp.take_along_axis(aff_t, idx, axis=1)
    k_tok = jnp.sum(sel, axis=0)
    q_tok = jnp.cumsum(k_tok) - k_tok + z_off
    before = jnp.cumsum(sel, axis=0) - sel
    qpos = jnp.take_along_axis(q_tok[None, :] + before, idx, axis=1)
    nblk = n // COMBINE_BLOCK
    q_blk = q_tok[::COMBINE_BLOCK]
    end_blk = q_blk + jnp.sum(k_tok.reshape(nblk, COMBINE_BLOCK), axis=1)
    w0 = (q_blk // ROW_ALIGN) * ROW_ALIGN
    nc = jnp.maximum((end_blk - w0 + Z_CHUNK - 1) // Z_CHUNK, 1).astype(I32)
    first = jnp.cumsum(nc) - nc
    g = jnp.arange(nblk + (N_EXPERTS * cap) // Z_CHUNK + nblk, dtype=I32)
    blk_of = jnp.sum((first[None, 1:] <= g[:, None]).astype(I32), axis=1)
    start = jnp.where(g < jnp.sum(nc), w0[blk_of] + (g - first[blk_of]) * Z_CHUNK, 0)
    chunks = (nc, first.astype(I32), start.astype(I32), jnp.sum(nc).astype(I32)[None])
    return idx, gate, qpos.astype(I32), chunks, q_tok.astype(I32)[:, None], k_tok.astype(I32)[:, None]


def kernel(x_prompt, x_sample, cache_k, cache_v, c, c_ctx, norm1_g, norm2_g, w_ada, b_ada, w_in, q_norm_g, k_norm_g,
           rpb, w_att_proj, w_pool, pool_scale, w_out, w_router, w_gate_e, w_up_e, w_down_e):
    assert w_ada.shape[0] == 1, "single trunk layer"
    nb, ls, d = x_prompt.shape
    db, ll, _ = x_sample.shape
    n_c, n_l = nb * ls, db * ll
    assert ls == TOKEN_BLOCK, "context keys / values are written one sequence per projection block"

    cond_rows = pl.cdiv(1 + db, ROW_ALIGN) * ROW_ALIGN
    cond = jnp.concatenate([c_ctx[None, :], c, jnp.zeros((cond_rows - 1 - db, d), F32)], axis=0)
    mods = _ada(cond, w_ada[0], b_ada).reshape(cond_rows, N_MOD, d)

    w_in_b = w_in[0].astype(BF16)
    w_att_b = w_att_proj[0].astype(BF16)
    w_pool_b = w_pool[0].astype(BF16)
    w_out_b = w_out[0].astype(BF16)


    xc = x_prompt.reshape(n_c, d)
    xl = x_sample.reshape(n_l, d)
    q_gain = jnp.tile(q_norm_g, (1, NA_HEADS))
    k_gain = jnp.tile(k_norm_g, (1, NA_HEADS))
    att_c, kc, vc, pool_c, gac, gbc = _in_proj(xc, ls, mods, norm1_g, w_in_b, q_gain, k_gain, lambda i: 0, True)
    ql, kl, vl, pool_l, gal, gbl = _in_proj(xl, ll, mods, norm1_g, w_in_b, q_gain, k_gain, lambda i: 1 + i, False)

    seq_l = lambda a: a.reshape(db, ll, a.shape[-1])
    bias = _window_bias(rpb[0], ll // GRID_W)
    att_l = _attn_lat(seq_l(ql), seq_l(kl), seq_l(vl), jnp.transpose(cache_k[:, 0], (0, 2, 3, 1)),
                      jnp.transpose(cache_v[:, 0], (0, 2, 3, 1)), bias)

    x1, h2, aff = _merge((xc, att_c, pool_c, gac, gbc),
                         (xl, att_l.reshape(n_l, NA_WIDTH), pool_l, gal, gbl),
                         mods, w_att_b, w_pool_b, pool_scale, w_out_b, norm2_g, w_router[0], ll // MERGE_BLOCK)

    cap_c = max(1, (CAPACITY_FACTOR * n_c) // N_EXPERTS)
    cap_l = max(1, (CAPACITY_FACTOR * n_l) // N_EXPERTS)
    m_tot = cap_c + cap_l
    m_iss = _moe_step_rows(m_tot, pl.cdiv(D_EXPERT, EXPERT_F_BLOCK)) * pl.cdiv(D_EXPERT, EXPERT_F_BLOCK)
    z_valid = N_EXPERTS * m_tot
    z_zero_rows = Z_CHUNK + ROW_ALIGN
    z_spare = z_valid + z_zero_rows
    z_rows = z_spare + m_iss
    idx_c, gate_c, qpos_c, chunks_c, qt_c, kt_c = _routing_tables(aff[:n_c], cap_c, 0)
    idx_l, gate_l, qpos_l, chunks_l, qt_l, kt_l = _routing_tables(aff[n_c:], cap_l, N_EXPERTS * cap_c)
    spare = jnp.broadcast_to(z_spare + jnp.arange(m_iss, dtype=I32), (N_EXPERTS + 1, m_iss))
    gidx = jnp.concatenate([idx_c, idx_l + n_c, jnp.zeros((N_EXPERTS, m_iss - m_tot), I32)], axis=1).reshape(-1)
    qpos = jnp.concatenate([spare[:1], jnp.concatenate([qpos_c, qpos_l, spare[1:, m_tot:]], axis=1)], axis=0).reshape(-1)
    gate = jnp.concatenate([gate_c, gate_l], axis=1)[:, :, None]

    z = _moe(gidx, qpos, h2, gate, jnp.swapaxes(w_gate_e[0], 1, 2), jnp.swapaxes(w_up_e[0], 1, 2), w_down_e[0],
             m_tot, z_rows, z_zero_rows)

    y_c = _combine(chunks_c, z, x1, 0, mods, qt_c, kt_c, lambda i: 0)
    y_l = _combine(chunks_l, z, x1, n_c // COMBINE_BLOCK, mods, qt_l, kt_l, lambda i: 1 + i // (ll // COMBINE_BLOCK))

    state_k = jnp.transpose(kc, (0, 3, 1, 2))[:, None]
    state_v = jnp.transpose(vc, (0, 3, 1, 2))[:, None]
    return (y_c.reshape(nb, ls, d), y_l.reshape(db, ll, d), state_k, state_v)
```
